```python
import math
import jax, jax.numpy as jnp
from jax import lax
import numpy as np

D_MODEL = 1024
BATCH = 8
SEQ = 16384
DEPTH = 2

HEAD_DIM = 64
DILATED_GROUPS = ((128, 1), (512, 4), (2048, 16))
N_DIL = len(DILATED_GROUPS)
HEADS_PER_GROUP = 8
N_ATTN_HEADS = N_DIL * HEADS_PER_GROUP
ATTN_WIDTH = N_ATTN_HEADS * HEAD_DIM
ATTN_OUT_WIDTH = HEADS_PER_GROUP * HEAD_DIM
ATTN_BLOCK = 128
N_REL_BUCKETS = 32
REL_MAX_DISTANCE = 2048

D_INNER = 2 * D_MODEL
SSM_HEAD_DIM = 64
N_SSM_HEADS = D_INNER // SSM_HEAD_DIM
N_SSM_GROUPS = 4
HEADS_PER_SSM_GROUP = N_SSM_HEADS // N_SSM_GROUPS
D_STATE = 128
CONV_WIDTH = 4
SSD_CHUNK = 128
XBC_WIDTH = D_INNER + 2 * N_SSM_GROUPS * D_STATE

N_BRANCHES = 2
D_FF = -(-8 * D_MODEL // (3 * 256)) * 256
IN_PROJ_WIDTH = 3 * ATTN_WIDTH + D_INNER + XBC_WIDTH + N_SSM_HEADS + N_BRANCHES * D_MODEL
EPS = 1e-6

kernel_name = "hybrid_dilated_attn_ssd_gated_block"


def rmsnorm(x, w):
    xf = x.astype(jnp.float32)
    y = xf * lax.rsqrt(jnp.mean(xf * xf, axis=-1, keepdims=True) + EPS)
    return (y * w.astype(jnp.float32)).astype(x.dtype)


def t5_causal_bucket(dist):
    max_exact = N_REL_BUCKETS // 2
    d_f = jnp.maximum(dist, 1).astype(jnp.float32)
    large = max_exact + (jnp.log(d_f / max_exact) / math.log(REL_MAX_DISTANCE / max_exact)
                         * (N_REL_BUCKETS - max_exact)).astype(jnp.int32)
    large = jnp.minimum(large, N_REL_BUCKETS - 1)
    return jnp.where(dist < max_exact, dist, large)


def rel_bias_block(rel_bias_g, dilation, n_steps):
    qi = jnp.arange(ATTN_BLOCK)[:, None]
    kj = jnp.arange(2 * ATTN_BLOCK)[None, :]
    steps = jnp.clip(qi + ATTN_BLOCK - kj, 0, n_steps)
    bucket = t5_causal_bucket(steps * dilation)
    return jnp.transpose(rel_bias_g[bucket], (2, 0, 1)).astype(jnp.float32)


def dilated_window_attention(q, k, v, bias, dilation, n_steps):
    b, s, h, dh = q.shape
    seg = s // dilation
    nb = -(-seg // ATTN_BLOCK)
    segp = nb * ATTN_BLOCK

    def to_blocks(t):
        t = t.reshape(b, seg, dilation, h, dh).transpose(0, 2, 1, 3, 4).reshape(b * dilation, seg, h, dh)
        t = jnp.pad(t, ((0, 0), (0, segp - seg), (0, 0), (0, 0)))
        return t.reshape(b * dilation, nb, ATTN_BLOCK, h, dh)

    def with_prev(t):
        prev = jnp.pad(t[:, :-1], ((0, 0), (1, 0), (0, 0), (0, 0), (0, 0)))
        return jnp.concatenate([prev, t], axis=2)

    qb = to_blocks(q)
    kw = with_prev(to_blocks(k))
    vw = with_prev(to_blocks(v))

    qi = jnp.arange(ATTN_BLOCK)[:, None]
    kj = jnp.arange(2 * ATTN_BLOCK)[None, :]
    steps = qi + ATTN_BLOCK - kj
    blk = jnp.arange(nb)[:, None, None]
    valid = (steps >= 0) & (steps <= n_steps) & (blk * ATTN_BLOCK - ATTN_BLOCK + kj >= 0)

    logits = jnp.einsum('znqhd,znkhd->znhqk', qb, kw).astype(jnp.float32) * (HEAD_DIM ** -0.5) + bias
    logits = jnp.where(valid[None, :, None], logits, -jnp.inf)
    m = jnp.max(logits, axis=-1, keepdims=True)
    p = jnp.exp(logits - m)
    den = jnp.sum(p, axis=-1, keepdims=True)
    o = jnp.einsum('znhqk,znkhd->znqhd', p, vw.astype(jnp.float32)) / jnp.swapaxes(den, 2, 3)
    lse = jnp.swapaxes((m + jnp.log(den))[..., 0], 2, 3)

    o = o.reshape(b, dilation, segp, h, dh)[:, :, :seg]
    o = jnp.swapaxes(o, 1, 2).reshape(b, s, h, dh)
    lse = lse.reshape(b, dilation, segp, h)[:, :, :seg]
    lse = jnp.swapaxes(lse, 1, 2).reshape(b, s, h)
    return o, lse


def causal_depthwise_conv(x, w, bias):
    c = x.shape[-1]
    y = lax.conv_general_dilated(x, w[:, None, :], window_strides=(1,),
                                 padding=((CONV_WIDTH - 1, 0),),
                                 dimension_numbers=('NWC', 'WIO', 'NWC'),
                                 feature_group_count=c)
    return y + bias


def ssd_chunked_scan(xh, dt, a, bm, cm):
    b, s = xh.shape[:2]
    nc = s // SSD_CHUNK

    def chunks(t):
        return jnp.swapaxes(t.reshape(b, nc, SSD_CHUNK, *t.shape[2:]), 0, 1)

    causal = jnp.tril(jnp.ones((SSD_CHUNK, SSD_CHUNK), dtype=bool))[None, :, :, None, None]

    def step(state, inp):
        x_, dt_, b_, c_ = inp
        la = jnp.cumsum(dt_ * a, axis=1)
        seg = la[:, :, None] - la[:, None, :]
        decay = jnp.exp(jnp.where(causal, seg, -jnp.inf))
        cb = jnp.einsum('bign,bjgn->bijg', c_, b_)
        xdt = x_ * dt_[..., None]
        y_intra = jnp.einsum('bijg,bijgh,bjghp->bighp', cb, decay, xdt)
        y_inter = jnp.einsum('bign,bghpn->bighp', c_, state) * jnp.exp(la)[..., None]
        to_end = jnp.exp(la[:, -1:] - la)
        new_state = (state * jnp.exp(la[:, -1])[..., None, None]
                     + jnp.einsum('bjgn,bjgh,bjghp->bghpn', b_, to_end, xdt))
        return new_state, y_intra + y_inter

    state0 = jnp.zeros((b, N_SSM_GROUPS, HEADS_PER_SSM_GROUP, SSM_HEAD_DIM, D_STATE), jnp.float32)
    _, ys = lax.scan(step, state0, (chunks(xh), chunks(dt), chunks(bm), chunks(cm)))
    return jnp.swapaxes(ys, 0, 1).reshape(b, s, N_SSM_GROUPS, HEADS_PER_SSM_GROUP, SSM_HEAD_DIM)


def hybrid_mixer(xn, w_in, conv_w, conv_b, dt_bias, a_log, d_skip, ssm_norm_w,
                 w_attn_branch, w_ssm_branch, w_out, attn_biases):
    b, s, _ = xn.shape
    proj = xn @ w_in
    q, k, v, z, xbc, dt_raw, gate_logits = jnp.split(
        proj, np.cumsum([ATTN_WIDTH, ATTN_WIDTH, ATTN_WIDTH, D_INNER, XBC_WIDTH, N_SSM_HEADS]).tolist(), axis=-1)

    q = q.reshape(b, s, N_DIL, HEADS_PER_GROUP, HEAD_DIM)
    k = k.reshape(b, s, N_DIL, HEADS_PER_GROUP, HEAD_DIM)
    v = v.reshape(b, s, N_DIL, HEADS_PER_GROUP, HEAD_DIM)
    outs, lses = [], []
    for g, (window, dil) in enumerate(DILATED_GROUPS):
        o_g, lse_g = dilated_window_attention(q[:, :, g], k[:, :, g], v[:, :, g],
                                              attn_biases[g], dil, window // dil)
        outs.append(o_g)
        lses.append(lse_g)
    o = jnp.stack(outs, axis=2)
    alpha = jax.nn.softmax(jnp.stack(lses, axis=2), axis=2)
    attn = jnp.sum(alpha[..., None] * o, axis=2).reshape(b, s, ATTN_OUT_WIDTH).astype(xn.dtype)

    xbc = jax.nn.silu(causal_depthwise_conv(xbc, conv_w, conv_b))
    xs, bm, cm = jnp.split(xbc, [D_INNER, D_INNER + N_SSM_GROUPS * D_STATE], axis=-1)
    xh = xs.reshape(b, s, N_SSM_GROUPS, HEADS_PER_SSM_GROUP, SSM_HEAD_DIM).astype(jnp.float32)
    bm = bm.reshape(b, s, N_SSM_GROUPS, D_STATE).astype(jnp.float32)
    cm = cm.reshape(b, s, N_SSM_GROUPS, D_STATE).astype(jnp.float32)
    dt = jax.nn.softplus(dt_raw.astype(jnp.float32) + dt_bias.astype(jnp.float32))
    dt = dt.reshape(b, s, N_SSM_GROUPS, HEADS_PER_SSM_GROUP)
    a = -jnp.exp(a_log.astype(jnp.float32)).reshape(N_SSM_GROUPS, HEADS_PER_SSM_GROUP)
    y = ssd_chunked_scan(xh, dt, a, bm, cm)
    y = y + xh * d_skip.astype(jnp.float32).reshape(N_SSM_GROUPS, HEADS_PER_SSM_GROUP)[..., None]
    yg = (y.reshape(b, s, D_INNER) * jax.nn.silu(z.astype(jnp.float32))).reshape(b, s, N_SSM_GROUPS, D_INNER // N_SSM_GROUPS)
    yg = yg * lax.rsqrt(jnp.mean(yg * yg, axis=-1, keepdims=True) + EPS)
    ssm = (yg.reshape(b, s, D_INNER) * ssm_norm_w.astype(jnp.float32)).astype(xn.dtype)

    gates = jax.nn.sigmoid(gate_logits.astype(jnp.float32)).reshape(b, s, N_BRANCHES, D_MODEL)
    merged = (gates[:, :, 0] * (attn @ w_attn_branch).astype(jnp.float32)
              + gates[:, :, 1] * (ssm @ w_ssm_branch).astype(jnp.float32)).astype(xn.dtype)
    return merged @ w_out


def swiglu(xn, w_ffn_in, w_ffn_out):
    gate, up = jnp.split(xn @ w_ffn_in, 2, axis=-1)
    return (jax.nn.silu(gate) * up) @ w_ffn_out


def _fwd_setup_inputs(seed: int = 0) -> dict:
    key = jax.random.key(seed)
    ks = jax.random.split(key, 20)
    f32 = jnp.float32
    nrm = lambda k, shape, scale: jax.random.normal(k, shape, f32) * scale
    dt0 = jnp.exp(jax.random.uniform(ks[5], (DEPTH, N_SSM_HEADS), f32, math.log(1e-3), math.log(1e-1)))
    return {
        "x": nrm(ks[0], (BATCH, SEQ, D_MODEL), 1.0),
        "norm1_w": 1.0 + nrm(ks[1], (DEPTH, D_MODEL), 0.02),
        "w_in": nrm(ks[2], (DEPTH, D_MODEL, IN_PROJ_WIDTH), D_MODEL ** -0.5),
        "conv_w": nrm(ks[3], (DEPTH, CONV_WIDTH, XBC_WIDTH), CONV_WIDTH ** -0.5),
        "conv_b": nrm(ks[4], (DEPTH, XBC_WIDTH), 0.02),
        "dt_bias": dt0 + jnp.log(-jnp.expm1(-dt0)),
        "a_log": jnp.log(jax.random.uniform(ks[6], (DEPTH, N_SSM_HEADS), f32, 1.0, 16.0)),
        "d_skip": 1.0 + nrm(ks[7], (DEPTH, N_SSM_HEADS), 0.1),
        "ssm_norm_w": 1.0 + nrm(ks[8], (DEPTH, D_INNER), 0.02),
        "w_attn_branch": nrm(ks[9], (DEPTH, ATTN_OUT_WIDTH, D_MODEL), ATTN_OUT_WIDTH ** -0.5),
        "w_ssm_branch": nrm(ks[10], (DEPTH, D_INNER, D_MODEL), D_INNER ** -0.5),
        "w_out": nrm(ks[11], (DEPTH, D_MODEL, D_MODEL), D_MODEL ** -0.5),
        "norm2_w": 1.0 + nrm(ks[12], (DEPTH, D_MODEL), 0.02),
        "w_ffn_in": nrm(ks[13], (DEPTH, D_MODEL, 2 * D_FF), D_MODEL ** -0.5),
        "w_ffn_out": nrm(ks[14], (DEPTH, D_FF, D_MODEL), D_FF ** -0.5),
        "rel_bias": nrm(ks[15], (N_REL_BUCKETS, N_ATTN_HEADS), 0.5),
        "final_norm_w": 1.0 + nrm(ks[16], (D_MODEL,), 0.02),
    }


def _fwd_reference(x, norm1_w, w_in, conv_w, conv_b, dt_bias, a_log, d_skip, ssm_norm_w,
              w_attn_branch, w_ssm_branch, w_out, norm2_w, w_ffn_in, w_ffn_out,
              rel_bias, final_norm_w):
    attn_biases = [rel_bias_block(rel_bias[:, g * HEADS_PER_GROUP:(g + 1) * HEADS_PER_GROUP], dil, window // dil)
                   for g, (window, dil) in enumerate(DILATED_GROUPS)]
    h = x
    for layer in range(DEPTH):
        h = h + hybrid_mixer(rmsnorm(h, norm1_w[layer]), w_in[layer], conv_w[layer], conv_b[layer],
                             dt_bias[layer], a_log[layer], d_skip[layer], ssm_norm_w[layer],
                             w_attn_branch[layer], w_ssm_branch[layer], w_out[layer], attn_biases)
        h = h + swiglu(rmsnorm(h, norm2_w[layer]), w_ffn_in[layer], w_ffn_out[layer])
    return rmsnorm(h, final_norm_w)


import jax as _jax
import jax.numpy as _jnp

TWIN_FORMAT = 'train_step'
FWD_PARAMS = ['x', 'norm1_w', 'w_in', 'conv_w', 'conv_b', 'dt_bias', 'a_log', 'd_skip', 'ssm_norm_w', 'w_attn_branch', 'w_ssm_branch', 'w_out', 'norm2_w', 'w_ffn_in', 'w_ffn_out', 'rel_bias', 'final_norm_w']
TWIN_WEIGHTS = ['norm1_w', 'w_in', 'conv_w', 'conv_b', 'dt_bias', 'a_log', 'd_skip', 'ssm_norm_w', 'w_attn_branch', 'w_ssm_branch', 'w_out', 'norm2_w', 'w_ffn_in', 'w_ffn_out', 'rel_bias', 'final_norm_w']
TWIN_DIFF_INPUT = 'x'
TWIN_INPUTS = ['x', 'norm1_w', 'w_in', 'conv_w', 'conv_b', 'dt_bias', 'a_log', 'd_skip', 'ssm_norm_w', 'w_attn_branch', 'w_ssm_branch', 'w_out', 'norm2_w', 'w_ffn_in', 'w_ffn_out', 'rel_bias', 'final_norm_w', 'loss_target', 'm_norm1_w', 'm_w_in', 'm_conv_w', 'm_conv_b', 'm_dt_bias', 'm_a_log', 'm_d_skip', 'm_ssm_norm_w', 'm_w_attn_branch', 'm_w_ssm_branch', 'm_w_out', 'm_norm2_w', 'm_w_ffn_in', 'm_w_ffn_out', 'm_rel_bias', 'm_final_norm_w', 'v_norm1_w', 'v_w_in', 'v_conv_w', 'v_conv_b', 'v_dt_bias', 'v_a_log', 'v_d_skip', 'v_ssm_norm_w', 'v_w_attn_branch', 'v_w_ssm_branch', 'v_w_out', 'v_norm2_w', 'v_w_ffn_in', 'v_w_ffn_out', 'v_rel_bias', 'v_final_norm_w']
TWIN_OUTPUTS = ['loss', 'grad_x', 'grad_norm1_w', 'grad_w_in', 'grad_conv_w', 'grad_conv_b', 'grad_dt_bias', 'grad_a_log', 'grad_d_skip', 'grad_ssm_norm_w', 'grad_w_attn_branch', 'grad_w_ssm_branch', 'grad_w_out', 'grad_norm2_w', 'grad_w_ffn_in', 'grad_w_ffn_out', 'grad_rel_bias', 'grad_final_norm_w', 'delta_norm1_w', 'delta_w_in', 'delta_conv_w', 'delta_conv_b', 'delta_dt_bias', 'delta_a_log', 'delta_d_skip', 'delta_ssm_norm_w', 'delta_w_attn_branch', 'delta_w_ssm_branch', 'delta_w_out', 'delta_norm2_w', 'delta_w_ffn_in', 'delta_w_ffn_out', 'delta_rel_bias', 'delta_final_norm_w', 'new_m_norm1_w', 'new_m_w_in', 'new_m_conv_w', 'new_m_conv_b', 'new_m_dt_bias', 'new_m_a_log', 'new_m_d_skip', 'new_m_ssm_norm_w', 'new_m_w_attn_branch', 'new_m_w_ssm_branch', 'new_m_w_out', 'new_m_norm2_w', 'new_m_w_ffn_in', 'new_m_w_ffn_out', 'new_m_rel_bias', 'new_m_final_norm_w', 'new_v_norm1_w', 'new_v_w_in', 'new_v_conv_w', 'new_v_conv_b', 'new_v_dt_bias', 'new_v_a_log', 'new_v_d_skip', 'new_v_ssm_norm_w', 'new_v_w_attn_branch', 'new_v_w_ssm_branch', 'new_v_w_out', 'new_v_norm2_w', 'new_v_w_ffn_in', 'new_v_w_ffn_out', 'new_v_rel_bias', 'new_v_final_norm_w']
TWIN_LEAF_KINDS = {'loss': 'loss', 'grad_x': 'grad_x', 'grad_norm1_w': 'grad_w', 'grad_w_in': 'grad_w', 'grad_conv_w': 'grad_w', 'grad_conv_b': 'grad_w', 'grad_dt_bias': 'grad_w', 'grad_a_log': 'grad_w', 'grad_d_skip': 'grad_w', 'grad_ssm_norm_w': 'grad_w', 'grad_w_attn_branch': 'grad_w', 'grad_w_ssm_branch': 'grad_w', 'grad_w_out': 'grad_w', 'grad_norm2_w': 'grad_w', 'grad_w_ffn_in': 'grad_w', 'grad_w_ffn_out': 'grad_w', 'grad_rel_bias': 'grad_w', 'grad_final_norm_w': 'grad_w', 'delta_norm1_w': 'delta_w', 'delta_w_in': 'delta_w', 'delta_conv_w': 'delta_w', 'delta_conv_b': 'delta_w', 'delta_dt_bias': 'delta_w', 'delta_a_log': 'delta_w', 'delta_d_skip': 'delta_w', 'delta_ssm_norm_w': 'delta_w', 'delta_w_attn_branch': 'delta_w', 'delta_w_ssm_branch': 'delta_w', 'delta_w_out': 'delta_w', 'delta_norm2_w': 'delta_w', 'delta_w_ffn_in': 'delta_w', 'delta_w_ffn_out': 'delta_w', 'delta_rel_bias': 'delta_w', 'delta_final_norm_w': 'delta_w', 'new_m_norm1_w': 'new_m', 'new_m_w_in': 'new_m', 'new_m_conv_w': 'new_m', 'new_m_conv_b': 'new_m', 'new_m_dt_bias': 'new_m', 'new_m_a_log': 'new_m', 'new_m_d_skip': 'new_m', 'new_m_ssm_norm_w': 'new_m', 'new_m_w_attn_branch': 'new_m', 'new_m_w_ssm_branch': 'new_m', 'new_m_w_out': 'new_m', 'new_m_norm2_w': 'new_m', 'new_m_w_ffn_in': 'new_m', 'new_m_w_ffn_out': 'new_m', 'new_m_rel_bias': 'new_m', 'new_m_final_norm_w': 'new_m', 'new_v_norm1_w': 'new_v', 'new_v_w_in': 'new_v', 'new_v_conv_w': 'new_v', 'new_v_conv_b': 'new_v', 'new_v_dt_bias': 'new_v', 'new_v_a_log': 'new_v', 'new_v_d_skip': 'new_v', 'new_v_ssm_norm_w': 'new_v', 'new_v_w_attn_branch': 'new_v', 'new_v_w_ssm_branch': 'new_v', 'new_v_w_out': 'new_v', 'new_v_norm2_w': 'new_v', 'new_v_w_ffn_in': 'new_v', 'new_v_w_ffn_out': 'new_v', 'new_v_rel_bias': 'new_v', 'new_v_final_norm_w': 'new_v'}


def _forward(args):
    return _fwd_reference(*[args[k] for k in FWD_PARAMS])


def _output_shape():
    def fwd():
        inp = _fwd_setup_inputs(0)
        return _fwd_reference(*[inp[k] for k in FWD_PARAMS])
    out = _jax.eval_shape(fwd)
    return out.shape, out.dtype

N_MICROBATCH = 1
ADAM_LR = 0.001
ADAM_B1 = 0.9
ADAM_B2 = 0.999
ADAM_EPS = 1e-08
ADAM_WD = 0.01
ADAM_STEP = 10
PER_EXAMPLE_BATCH_AXIS = {'x': 0, 'loss_target': 0}
SHARED_INPUTS = []
_WEIGHT_DTYPES = {'norm1_w': _jnp.float32, 'w_in': _jnp.float32, 'conv_w': _jnp.float32, 'conv_b': _jnp.float32, 'dt_bias': _jnp.float32, 'a_log': _jnp.float32, 'd_skip': _jnp.float32, 'ssm_norm_w': _jnp.float32, 'w_attn_branch': _jnp.float32, 'w_ssm_branch': _jnp.float32, 'w_out': _jnp.float32, 'norm2_w': _jnp.float32, 'w_ffn_in': _jnp.float32, 'w_ffn_out': _jnp.float32, 'rel_bias': _jnp.float32, 'final_norm_w': _jnp.float32}
MOMENT_SCALE = {'norm1_w': 3.060895e-01, 'w_in': 8.961709e-02, 'conv_w': 1.177553e-01, 'conv_b': 1.791379e-01, 'dt_bias': 4.724946e-01, 'a_log': 3.742767e-01, 'd_skip': 7.148398e-01, 'ssm_norm_w': 1.455228e-01, 'w_attn_branch': 3.904214e-02, 'w_ssm_branch': 1.929223e-01, 'w_out': 1.932320e-01, 'norm2_w': 2.576042e-01, 'w_ffn_in': 1.048609e-01, 'w_ffn_out': 1.711346e-01, 'rel_bias': 4.904285e-02, 'final_norm_w': 1.279480e+02}


def _to_microbatches(a, axis):
    t = _jnp.moveaxis(a, axis, 0)
    t = t.reshape((N_MICROBATCH, t.shape[0] // N_MICROBATCH) + t.shape[1:])
    return _jnp.moveaxis(t, 1, axis + 1)


def setup_inputs(seed: int = 0) -> dict:
    inp = _fwd_setup_inputs(seed)
    key = _jax.random.fold_in(_jax.random.key(seed), 7919)
    shape, _ = _output_shape()
    out = dict(inp)
    out["loss_target"] = _jax.random.normal(_jax.random.fold_in(key, 0), shape, _jnp.float32)
    for i, name in enumerate(TWIN_WEIGHTS):
        w = inp[name].astype(_jnp.float32)
        if MOMENT_SCALE is None:
            s = _jnp.sqrt(_jnp.mean(_jnp.square(w)) + 1e-30)
        else:
            s = MOMENT_SCALE[name]
        km, kv = _jax.random.split(_jax.random.fold_in(key, i + 1))
        out[name] = w
        out["m_" + name] = s * _jax.random.normal(km, w.shape, _jnp.float32)
        out["v_" + name] = (s * s) * _jax.random.uniform(kv, w.shape, _jnp.float32, 0.5, 1.5)
    if N_MICROBATCH > 1:
        for name, axis in PER_EXAMPLE_BATCH_AXIS.items():
            out[name] = _to_microbatches(out[name], axis)
    return {'x': out['x'], 'norm1_w': out['norm1_w'], 'w_in': out['w_in'], 'conv_w': out['conv_w'], 'conv_b': out['conv_b'], 'dt_bias': out['dt_bias'], 'a_log': out['a_log'], 'd_skip': out['d_skip'], 'ssm_norm_w': out['ssm_norm_w'], 'w_attn_branch': out['w_attn_branch'], 'w_ssm_branch': out['w_ssm_branch'], 'w_out': out['w_out'], 'norm2_w': out['norm2_w'], 'w_ffn_in': out['w_ffn_in'], 'w_ffn_out': out['w_ffn_out'], 'rel_bias': out['rel_bias'], 'final_norm_w': out['final_norm_w'], 'loss_target': out['loss_target'], 'm_norm1_w': out['m_norm1_w'], 'm_w_in': out['m_w_in'], 'm_conv_w': out['m_conv_w'], 'm_conv_b': out['m_conv_b'], 'm_dt_bias': out['m_dt_bias'], 'm_a_log': out['m_a_log'], 'm_d_skip': out['m_d_skip'], 'm_ssm_norm_w': out['m_ssm_norm_w'], 'm_w_attn_branch': out['m_w_attn_branch'], 'm_w_ssm_branch': out['m_w_ssm_branch'], 'm_w_out': out['m_w_out'], 'm_norm2_w': out['m_norm2_w'], 'm_w_ffn_in': out['m_w_ffn_in'], 'm_w_ffn_out': out['m_w_ffn_out'], 'm_rel_bias': out['m_rel_bias'], 'm_final_norm_w': out['m_final_norm_w'], 'v_norm1_w': out['v_norm1_w'], 'v_w_in': out['v_w_in'], 'v_conv_w': out['v_conv_w'], 'v_conv_b': out['v_conv_b'], 'v_dt_bias': out['v_dt_bias'], 'v_a_log': out['v_a_log'], 'v_d_skip': out['v_d_skip'], 'v_ssm_norm_w': out['v_ssm_norm_w'], 'v_w_attn_branch': out['v_w_attn_branch'], 'v_w_ssm_branch': out['v_w_ssm_branch'], 'v_w_out': out['v_w_out'], 'v_norm2_w': out['v_norm2_w'], 'v_w_ffn_in': out['v_w_ffn_in'], 'v_w_ffn_out': out['v_w_ffn_out'], 'v_rel_bias': out['v_rel_bias'], 'v_final_norm_w': out['v_final_norm_w']}


def _loss(weights, diff, rest, loss_target):
    with _jax.named_scope("forward"):
        args = {**rest, TWIN_DIFF_INPUT: diff, **{k: w.astype(_WEIGHT_DTYPES[k]) for k, w in weights.items()}}
        y = _forward(args)
    with _jax.named_scope("loss_head"):
        err = _jnp.square(y.astype(_jnp.float32) - loss_target)
        return 0.5 * _jnp.sum(_jnp.mean(err, axis=-1)) if err.ndim else 0.5 * err


def _adamw(w, g, m, v):
    m = ADAM_B1 * m + (1.0 - ADAM_B1) * g
    v = ADAM_B2 * v + (1.0 - ADAM_B2) * _jnp.square(g)
    m_hat = m / (1.0 - ADAM_B1 ** ADAM_STEP)
    v_hat = v / (1.0 - ADAM_B2 ** ADAM_STEP)
    delta = -ADAM_LR * (m_hat / (_jnp.sqrt(v_hat) + ADAM_EPS) + ADAM_WD * w)
    return delta, m, v


def reference(x, norm1_w, w_in, conv_w, conv_b, dt_bias, a_log, d_skip, ssm_norm_w, w_attn_branch, w_ssm_branch, w_out, norm2_w, w_ffn_in, w_ffn_out, rel_bias, final_norm_w, loss_target, m_norm1_w, m_w_in, m_conv_w, m_conv_b, m_dt_bias, m_a_log, m_d_skip, m_ssm_norm_w, m_w_attn_branch, m_w_ssm_branch, m_w_out, m_norm2_w, m_w_ffn_in, m_w_ffn_out, m_rel_bias, m_final_norm_w, v_norm1_w, v_w_in, v_conv_w, v_conv_b, v_dt_bias, v_a_log, v_d_skip, v_ssm_norm_w, v_w_attn_branch, v_w_ssm_branch, v_w_out, v_norm2_w, v_w_ffn_in, v_w_ffn_out, v_rel_bias, v_final_norm_w):
    given = dict(x=x, norm1_w=norm1_w, w_in=w_in, conv_w=conv_w, conv_b=conv_b, dt_bias=dt_bias, a_log=a_log, d_skip=d_skip, ssm_norm_w=ssm_norm_w, w_attn_branch=w_attn_branch, w_ssm_branch=w_ssm_branch, w_out=w_out, norm2_w=norm2_w, w_ffn_in=w_ffn_in, w_ffn_out=w_ffn_out, rel_bias=rel_bias, final_norm_w=final_norm_w, loss_target=loss_target, m_norm1_w=m_norm1_w, m_w_in=m_w_in, m_conv_w=m_conv_w, m_conv_b=m_conv_b, m_dt_bias=m_dt_bias, m_a_log=m_a_log, m_d_skip=m_d_skip, m_ssm_norm_w=m_ssm_norm_w, m_w_attn_branch=m_w_attn_branch, m_w_ssm_branch=m_w_ssm_branch, m_w_out=m_w_out, m_norm2_w=m_norm2_w, m_w_ffn_in=m_w_ffn_in, m_w_ffn_out=m_w_ffn_out, m_rel_bias=m_rel_bias, m_final_norm_w=m_final_norm_w, v_norm1_w=v_norm1_w, v_w_in=v_w_in, v_conv_w=v_conv_w, v_conv_b=v_conv_b, v_dt_bias=v_dt_bias, v_a_log=v_a_log, v_d_skip=v_d_skip, v_ssm_norm_w=v_ssm_norm_w, v_w_attn_branch=v_w_attn_branch, v_w_ssm_branch=v_w_ssm_branch, v_w_out=v_w_out, v_norm2_w=v_norm2_w, v_w_ffn_in=v_w_ffn_in, v_w_ffn_out=v_w_ffn_out, v_rel_bias=v_rel_bias, v_final_norm_w=v_final_norm_w)
    weights = {n: given[n] for n in TWIN_WEIGHTS}
    shared = {n: given[n] for n in SHARED_INPUTS}
    per_example = {n: given[n] for n in ['x']}
    grad_fn = _jax.value_and_grad(_loss, argnums=(0, 1))

    def one_microbatch(ex, loss_target):
        ex = dict(ex)
        diff = ex.pop(TWIN_DIFF_INPUT)
        return grad_fn(weights, diff, {**shared, **ex}, loss_target)

    if N_MICROBATCH == 1:
        loss, (grad_w, grad_x) = one_microbatch(per_example, given["loss_target"])
    else:
        def body(carry, xs):
            loss_sum, grad_sum = carry
            l_k, (gw_k, gx_k) = one_microbatch(xs[0], xs[1])
            with _jax.named_scope("update"):
                return (loss_sum + l_k, _jax.tree.map(_jnp.add, grad_sum, gw_k)), gx_k

        init = (_jnp.zeros((), _jnp.float32), _jax.tree.map(_jnp.zeros_like, weights))
        (loss, grad_w), grad_x = _jax.lax.scan(body, init, (per_example, given["loss_target"]))
    with _jax.named_scope("update"):
        delta_w, new_m, new_v = {}, {}, {}
        for n in TWIN_WEIGHTS:
            delta_w[n], new_m[n], new_v[n] = _adamw(weights[n], grad_w[n], given["m_" + n], given["v_" + n])
    return (loss, grad_x, *[grad_w[n] for n in TWIN_WEIGHTS], *[delta_w[n] for n in TWIN_WEIGHTS],
            *[new_m[n] for n in TWIN_WEIGHTS], *[new_v[n] for n in TWIN_WEIGHTS])
```

```python
import functools
import math

import jax
import jax.numpy as jnp
from jax import lax
from jax.experimental import pallas as pl
from jax.experimental.pallas import tpu as pltpu

F32, BF16 = jnp.float32, jnp.bfloat16
SDS = jax.ShapeDtypeStruct
BS = pl.BlockSpec
MESH = pl.DeviceIdType.MESH
HIGHEST = lax.Precision.HIGHEST

EPS = 1e-6
HEAD_DIM = 64
ATTN_BLOCK = 128
DILATIONS = (1, 4, 16)
N_GROUPS_ATTN = len(DILATIONS)
N_STEPS = 128
N_REL_BUCKETS = 32
REL_MAX_DISTANCE = 2048
SSM_GROUPS = 4
D_STATE = 128
CHUNK = 128
CONV_WIDTH = 4
HALO = 16
LANES = 128
N_CHIPS = 4
N_DEV = 8
VMEM_LIMIT_BYTES = 48 * 1024 * 1024

ADAM_LR, ADAM_B1, ADAM_B2, ADAM_EPS, ADAM_WD, ADAM_STEP = 0.001, 0.9, 0.999, 1e-08, 0.01, 10

NT = (((1,), (1,)), ((), ()))
TN = (((0,), (0,)), ((), ()))
NN = (((1,), (0,)), ((), ()))


def _cp(*sem):
    return pltpu.CompilerParams(dimension_semantics=sem if sem else None, vmem_limit_bytes=VMEM_LIMIT_BYTES)


def _pick(n, cands):
    for c in cands:
        if n % c == 0:
            return c
    raise ValueError(f"no block size of {cands} divides {n}")


def _dot(a, b, dims=NN, precision=None):
    return lax.dot_general(a, b, dims, precision=precision, preferred_element_type=F32)


def _silu(x):
    return x / (1.0 + jnp.exp(-x))


def _sigmoid(x):
    return 1.0 / (1.0 + jnp.exp(-x))


def _dsilu(x):
    s = _sigmoid(x)
    return s * (1.0 + x * (1.0 - s))


def _matmul(a, b, *, name, ta=False, tb=False, out_dtype=F32, res=None):
    (kdim, m) = a.shape if ta else a.shape[::-1]
    (n, k2) = b.shape if tb else b.shape[::-1]
    assert kdim == k2, (a.shape, b.shape, ta, tb)
    tm = _pick(m, (1024, 512, 256, 128))
    tn = _pick(n, (512, 256, 128))
    tk = _pick(kdim, (1024, 512, 256, 128))
    nk = kdim // tk
    a_spec = BS((tk, tm), lambda i, j, k: (k, i)) if ta else BS((tm, tk), lambda i, j, k: (i, k))
    b_spec = BS((tn, tk), lambda i, j, k: (j, k)) if tb else BS((tk, tn), lambda i, j, k: (k, j))
    dims = (((0 if ta else 1,), (1 if tb else 0,)), ((), ()))
    has_res = res is not None

    def body(*refs):
        if has_res:
            a_ref, b_ref, r_ref, o_ref, acc = refs
        else:
            a_ref, b_ref, o_ref, acc = refs
        k = pl.program_id(2)

        @pl.when(k == 0)
        def _():
            acc[...] = jnp.zeros_like(acc)

        acc[...] += _dot(a_ref[...].astype(BF16), b_ref[...].astype(BF16), dims)

        @pl.when(k == nk - 1)
        def _():
            r = acc[...]
            if has_res:
                r = r + r_ref[...]
            o_ref[...] = r.astype(o_ref.dtype)

    in_specs = [a_spec, b_spec]
    args = [a, b]
    if has_res:
        in_specs.append(BS((tm, tn), lambda i, j, k: (i, j)))
        args.append(res)
    return pl.pallas_call(
        body, name=name, grid=(m // tm, n // tn, nk), in_specs=in_specs,
        out_specs=BS((tm, tn), lambda i, j, k: (i, j)), out_shape=SDS((m, n), out_dtype),
        scratch_shapes=[pltpu.VMEM((tm, tn), F32)], compiler_params=_cp("parallel", "parallel", "arbitrary"),
    )(*args)


def _rms_fwd(h, w, *, name):
    t, d = h.shape
    tm = _pick(t, (512, 256, 128))

    def body(h_ref, w_ref, o_ref):
        x = h_ref[...]
        r = lax.rsqrt(jnp.mean(x * x, axis=-1, keepdims=True) + EPS)
        o_ref[...] = (x * r * w_ref[...]).astype(BF16)

    return pl.pallas_call(
        body, name=name, grid=(t // tm,), in_specs=[BS((tm, d), lambda i: (i, 0)), BS((1, d), lambda i: (0, 0))],
        out_specs=BS((tm, d), lambda i: (i, 0)), out_shape=SDS((t, d), BF16), compiler_params=_cp("parallel"),
    )(h, w.reshape(1, d))


def _rms_bwd(h, w, dys, dres, *, name):
    t, d = h.shape
    tm = _pick(t, (512, 256, 128))
    n_dy = len(dys)

    def body(*refs):
        h_ref, w_ref = refs[:2]
        dy_refs = refs[2:2 + n_dy]
        dres_ref, dh_ref, dw_ref = refs[2 + n_dy:]
        x = h_ref[...]
        dy = dy_refs[0][...]
        for r_ in dy_refs[1:]:
            dy = dy + r_[...]
        r = lax.rsqrt(jnp.mean(x * x, axis=-1, keepdims=True) + EPS)
        g = dy * w_ref[...]
        proj = jnp.sum(g * x, axis=-1, keepdims=True) * (1.0 / d)
        dh_ref[...] = dres_ref[...] + r * g - x * (r * r * r) * proj

        @pl.when(pl.program_id(0) == 0)
        def _():
            dw_ref[...] = jnp.zeros_like(dw_ref)

        dw_ref[...] += jnp.sum(dy * x * r, axis=0, keepdims=True)

    row = BS((tm, d), lambda i: (i, 0))
    vec = BS((1, d), lambda i: (0, 0))
    dh, dw = pl.pallas_call(
        body, name=name, grid=(t // tm,), in_specs=[row, vec] + [row] * n_dy + [row],
        out_specs=[row, vec], out_shape=[SDS((t, d), F32), SDS((1, d), F32)], compiler_params=_cp("arbitrary"),
    )(h, w.reshape(1, d), *dys, dres)
    return dh, dw[0]


def _loss_head(h, w, tgt, *, name):
    t, d = h.shape
    tm = _pick(t, (512, 256, 128))

    def body(h_ref, w_ref, t_ref, loss_ref, dh_ref, dw_ref):
        x = h_ref[...]
        r = lax.rsqrt(jnp.mean(x * x, axis=-1, keepdims=True) + EPS)
        err = x * r * w_ref[...] - t_ref[...]
        loss_ref[...] = jnp.zeros(loss_ref.shape, F32) + 0.5 * jnp.sum(err * err) * (1.0 / d)
        dy = err * (1.0 / d)
        g = dy * w_ref[...]
        proj = jnp.sum(g * x, axis=-1, keepdims=True) * (1.0 / d)
        dh_ref[...] = r * g - x * (r * r * r) * proj

        @pl.when(pl.program_id(0) == 0)
        def _():
            dw_ref[...] = jnp.zeros_like(dw_ref)

        dw_ref[...] += jnp.sum(dy * x * r, axis=0, keepdims=True)

    row = BS((tm, d), lambda i: (i, 0))
    vec = BS((1, d), lambda i: (0, 0))
    loss, dh, dw = pl.pallas_call(
        body, name=name, grid=(t // tm,), in_specs=[row, vec, row],
        out_specs=[BS((1, 8, LANES), lambda i: (i, 0, 0)), row, vec],
        out_shape=[SDS((t // tm, 8, LANES), F32), SDS((t, d), F32), SDS((1, d), F32)], compiler_params=_cp("arbitrary"),
    )(h, w.reshape(1, d), tgt)
    return jnp.sum(loss[:, 0, 0]), dh, dw[0]


def _attn_masks(mb):
    qi = lax.broadcasted_iota(jnp.int32, (ATTN_BLOCK, 2 * ATTN_BLOCK), 0)
    kj = lax.broadcasted_iota(jnp.int32, (ATTN_BLOCK, 2 * ATTN_BLOCK), 1)
    steps = qi + ATTN_BLOCK - kj
    valid = (steps >= 0) & (steps <= N_STEPS) & ((kj >= ATTN_BLOCK) | (mb > 0))
    low = lax.broadcasted_iota(jnp.int32, (ATTN_BLOCK, LANES), 1) < HEAD_DIM
    return valid, low


def _attn_specs(g, d, nbp, awg, clamp):
    def spec(which, prev):
        def index(r, mb):
            blk = clamp(mb)
            if prev:
                blk = jnp.maximum(blk - 1, 0)
            return (blk, r * nbp + which * N_GROUPS_ATTN + g)
        return BS((ATTN_BLOCK, awg), index)
    return [spec(0, False), spec(1, False), spec(1, True), spec(2, False), spec(2, True)]


def _attn_fwd(proj, bias, g, *, name):
    d = DILATIONS[g]
    t, pw = proj.shape
    hg = bias.shape[0]
    awg = hg * HEAD_DIM
    nbp = pw // awg
    td = t // d
    nb = td // ATTN_BLOCK
    scale = HEAD_DIM ** -0.5

    def body(q_ref, kc_ref, kp_ref, vc_ref, vp_ref, b_ref, o_ref, l_ref):
        valid, low = _attn_masks(pl.program_id(1))
        for pi in range(awg // LANES):
            sl = slice(pi * LANES, (pi + 1) * LANES)
            q2 = q_ref[:, sl]
            k2 = jnp.concatenate([kp_ref[:, sl], kc_ref[:, sl]], axis=0)
            v2 = jnp.concatenate([vp_ref[:, sl], vc_ref[:, sl]], axis=0)
            outs, lses = [], []
            for hh in range(2):
                mh = low if hh == 0 else jnp.logical_not(low)
                qm = jnp.where(mh, q2, jnp.zeros_like(q2))
                s = _dot(qm, k2, NT) * scale + b_ref[pi * 2 + hh]
                s = jnp.where(valid, s, -jnp.inf)
                m = jnp.max(s, axis=-1, keepdims=True)
                p = jnp.exp(s - m)
                den = jnp.sum(p, axis=-1, keepdims=True)
                outs.append(_dot(p.astype(BF16), v2) / den)
                lses.append(jnp.broadcast_to(m + jnp.log(den), (ATTN_BLOCK, LANES)))
            o_ref[:, sl] = jnp.where(low, outs[0], outs[1])
            l_ref[:, sl] = jnp.where(low, lses[0], lses[1])

    out_spec = BS((ATTN_BLOCK, awg), lambda r, mb: (mb, r))
    o, lse = pl.pallas_call(
        body, name=name, grid=(d, nb),
        in_specs=_attn_specs(g, d, nbp, awg, lambda mb: mb) + [BS(bias.shape, lambda r, mb: (0, 0, 0))],
        out_specs=[out_spec, out_spec], out_shape=[SDS((td, d * awg), F32)] * 2, compiler_params=_cp("parallel", "parallel"),
    )(*([proj.reshape(td, d * pw)] * 5), bias)
    return o.reshape(t, awg), lse.reshape(t, awg)


def _attn_bwd(proj, bias, lse, do, corr, g, *, name):
    d = DILATIONS[g]
    t, pw = proj.shape
    hg = bias.shape[0]
    awg = hg * HEAD_DIM
    nbp = pw // awg
    td = t // d
    nb = td // ATTN_BLOCK
    scale = HEAD_DIM ** -0.5

    def body(q_ref, kc_ref, kp_ref, vc_ref, vp_ref, b_ref, l_ref, do_ref, c_ref, dq_ref, dk_ref, dv_ref, db_ref, ck, cv):
        r, mb = pl.program_id(0), pl.program_id(1)

        @pl.when((r == 0) & (mb == 0))
        def _():
            db_ref[...] = jnp.zeros_like(db_ref)

        @pl.when(mb == 0)
        def _():
            ck[...] = jnp.zeros_like(ck)
            cv[...] = jnp.zeros_like(cv)

        @pl.when(mb < nb)
        def _():
            valid, low = _attn_masks(mb)
            for pi in range(awg // LANES):
                sl = slice(pi * LANES, (pi + 1) * LANES)
                q2 = q_ref[:, sl]
                k2 = jnp.concatenate([kp_ref[:, sl], kc_ref[:, sl]], axis=0)
                v2 = jnp.concatenate([vp_ref[:, sl], vc_ref[:, sl]], axis=0)
                do2 = do_ref[:, sl]
                lse2 = l_ref[:, sl]
                corr2 = c_ref[:, sl]
                dk2 = jnp.zeros((2 * ATTN_BLOCK, LANES), F32)
                dv2 = jnp.zeros((2 * ATTN_BLOCK, LANES), F32)
                dqs = []
                for hh in range(2):
                    mh = low if hh == 0 else jnp.logical_not(low)
                    qm = jnp.where(mh, q2, jnp.zeros_like(q2))
                    dom = jnp.where(mh, do2, jnp.zeros_like(do2))
                    lse_c = jnp.max(jnp.where(mh, lse2, -jnp.inf), axis=-1, keepdims=True)
                    corr_c = jnp.max(jnp.where(mh, corr2, -jnp.inf), axis=-1, keepdims=True)
                    s = _dot(qm, k2, NT) * scale + b_ref[pi * 2 + hh]
                    p = jnp.exp(jnp.where(valid, s, -jnp.inf) - lse_c)
                    ds = p * (_dot(dom, v2, NT) + corr_c)
                    db_ref[pi * 2 + hh] += ds
                    dsb = ds.astype(BF16)
                    dqs.append(_dot(dsb, k2) * scale)
                    dk2 = dk2 + _dot(dsb, qm, TN) * scale
                    dv2 = dv2 + _dot(p.astype(BF16), dom, TN)
                dq_ref[:, sl] = jnp.where(low, dqs[0], dqs[1]).astype(BF16)
                dk_ref[:, sl] = (ck[:, sl] + dk2[:ATTN_BLOCK]).astype(BF16)
                dv_ref[:, sl] = (cv[:, sl] + dv2[:ATTN_BLOCK]).astype(BF16)
                ck[:, sl] = dk2[ATTN_BLOCK:]
                cv[:, sl] = dv2[ATTN_BLOCK:]

        @pl.when(mb == nb)
        def _():
            dk_ref[...] = ck[...].astype(BF16)
            dv_ref[...] = cv[...].astype(BF16)

    clamp = lambda mb: jnp.minimum(mb, nb - 1)
    cur = BS((ATTN_BLOCK, awg), lambda r, mb: (clamp(mb), r))
    prev = BS((ATTN_BLOCK, awg), lambda r, mb: (jnp.maximum(mb - 1, 0), r))
    bias_spec = BS(bias.shape, lambda r, mb: (0, 0, 0))
    view = lambda a: a.reshape(td, d * awg)
    dq, dk, dv, dbias = pl.pallas_call(
        body, name=name, grid=(d, nb + 1),
        in_specs=_attn_specs(g, d, nbp, awg, clamp) + [bias_spec, cur, cur, cur],
        out_specs=[cur, prev, prev, bias_spec],
        out_shape=[SDS((td, d * awg), BF16)] * 3 + [SDS(bias.shape, F32)],
        scratch_shapes=[pltpu.VMEM((ATTN_BLOCK, awg), F32)] * 2, compiler_params=_cp("arbitrary", "arbitrary"),
    )(*([proj.reshape(td, d * pw)] * 5), bias, view(lse), view(do), view(corr))
    return dq.reshape(t, awg), dk.reshape(t, awg), dv.reshape(t, awg), dbias


def _head_sum(x, low):
    a = jnp.sum(jnp.where(low, x, 0.0), axis=-1, keepdims=True)
    b = jnp.sum(jnp.where(low, 0.0, x), axis=-1, keepdims=True)
    return jnp.where(low, a, b)


def _combine_weights(lses):
    mx = jnp.maximum(jnp.maximum(lses[0], lses[1]), lses[2])
    es = [jnp.exp(l - mx) for l in lses]
    tot = es[0] + es[1] + es[2]
    return [e / tot for e in es]


def _combine_fwd(os_, lses, *, name):
    t, awg = os_[0].shape
    tm = _pick(t, (512, 256, 128))

    def body(o0, o1, o2, l0, l1, l2, out_ref):
        al = _combine_weights([l0[...], l1[...], l2[...]])
        out_ref[...] = (al[0] * o0[...] + al[1] * o1[...] + al[2] * o2[...]).astype(BF16)

    blk = BS((tm, LANES), lambda i, j: (i, j))
    return pl.pallas_call(
        body, name=name, grid=(t // tm, awg // LANES), in_specs=[blk] * 6, out_specs=blk,
        out_shape=SDS((t, awg), BF16), compiler_params=_cp("parallel", "parallel"),
    )(*os_, *lses)


def _combine_bwd(dattn, os_, lses, *, name):
    t, awg = dattn.shape
    tm = _pick(t, (512, 256, 128))

    def body(da_ref, o0, o1, o2, l0, l1, l2, d0, d1, d2, c0, c1, c2):
        low = lax.broadcasted_iota(jnp.int32, (tm, LANES), 1) < HEAD_DIM
        da = da_ref[...]
        al = _combine_weights([l0[...], l1[...], l2[...]])
        tot = jnp.zeros((tm, LANES), F32)
        for a, o in zip(al, (o0, o1, o2)):
            tot = tot + a * _head_sum(da * o[...], low)
        for a, d_ref, c_ref in zip(al, (d0, d1, d2), (c0, c1, c2)):
            d_ref[...] = (a * da).astype(BF16)
            c_ref[...] = -a * tot

    blk = BS((tm, LANES), lambda i, j: (i, j))
    outs = pl.pallas_call(
        body, name=name, grid=(t // tm, awg // LANES), in_specs=[blk] * 7, out_specs=[blk] * 6,
        out_shape=[SDS((t, awg), BF16)] * 3 + [SDS((t, awg), F32)] * 3, compiler_params=_cp("parallel", "parallel"),
    )(dattn, *os_, *lses)
    return outs[:3], outs[3:]


def _conv_block(width, *offsets):
    for c in (512, 256, 128):
        if width % c == 0 and all(o % c == 0 for o in offsets):
            return c
    raise ValueError((width, offsets))


def _conv_pre(x_ref, halo_ref, w_ref, b_ref, ext, i, tm):
    ext[pl.ds(0, HALO), :] = jnp.where(i > 0, halo_ref[...].astype(F32), 0.0)
    ext[pl.ds(HALO, tm), :] = x_ref[...].astype(F32)
    pre = b_ref[...] + w_ref[pl.ds(CONV_WIDTH - 1, 1), :] * ext[pl.ds(HALO, tm), :]
    for k in range(CONV_WIDTH - 1):
        pre = pre + w_ref[pl.ds(k, 1), :] * ext[pl.ds(HALO - (CONV_WIDTH - 1) + k, tm), :]
    return pre


def _conv_fwd(proj, off, w, b, *, name):
    t = proj.shape[0]
    c = w.shape[1]
    cw = _conv_block(c, off)
    tm = _pick(t, (512, 256, 128))
    ob = off // cw

    def body(x_ref, halo_ref, w_ref, b_ref, o_ref, ext):
        pre = _conv_pre(x_ref, halo_ref, w_ref, b_ref, ext, pl.program_id(1), tm)
        o_ref[...] = _silu(pre).astype(BF16)

    return pl.pallas_call(
        body, name=name, grid=(c // cw, t // tm),
        in_specs=[BS((tm, cw), lambda j, i: (i, ob + j)),
                  BS((HALO, cw), lambda j, i: (jnp.maximum(i * (tm // HALO) - 1, 0), ob + j)),
                  BS((CONV_WIDTH, cw), lambda j, i: (0, j)), BS((1, cw), lambda j, i: (0, j))],
        out_specs=BS((tm, cw), lambda j, i: (i, j)), out_shape=SDS((t, c), BF16),
        scratch_shapes=[pltpu.VMEM((HALO + tm, cw), F32)], compiler_params=_cp("parallel", "arbitrary"),
    )(proj, proj, w, b.reshape(1, c))


def _conv_bwd(proj, off, w, b, dxc, *, name):
    t = proj.shape[0]
    c = w.shape[1]
    cw = _conv_block(c, off)
    tm = _pick(t, (512, 256, 128))
    ob = off // cw
    nt = t // tm

    def body_pre(x_ref, halo_ref, w_ref, b_ref, d_ref, dp_ref, dw_ref, db_ref, ext):
        i = pl.program_id(1)
        pre = _conv_pre(x_ref, halo_ref, w_ref, b_ref, ext, i, tm)
        dpre = d_ref[...] * _dsilu(pre)
        dp_ref[...] = dpre

        @pl.when(i == 0)
        def _():
            dw_ref[...] = jnp.zeros_like(dw_ref)
            db_ref[...] = jnp.zeros_like(db_ref)

        db_ref[...] += jnp.sum(dpre, axis=0, keepdims=True)
        for k in range(CONV_WIDTH):
            dw_ref[pl.ds(k, 1), :] += jnp.sum(dpre * ext[pl.ds(HALO - (CONV_WIDTH - 1) + k, tm), :], axis=0, keepdims=True)

    dpre, dw, db = pl.pallas_call(
        body_pre, name=name + "_pre", grid=(c // cw, nt),
        in_specs=[BS((tm, cw), lambda j, i: (i, ob + j)),
                  BS((HALO, cw), lambda j, i: (jnp.maximum(i * (tm // HALO) - 1, 0), ob + j)),
                  BS((CONV_WIDTH, cw), lambda j, i: (0, j)), BS((1, cw), lambda j, i: (0, j)),
                  BS((tm, cw), lambda j, i: (i, j))],
        out_specs=[BS((tm, cw), lambda j, i: (i, j)), BS((CONV_WIDTH, cw), lambda j, i: (0, j)), BS((1, cw), lambda j, i: (0, j))],
        out_shape=[SDS((t, c), F32), SDS((CONV_WIDTH, c), F32), SDS((1, c), F32)],
        scratch_shapes=[pltpu.VMEM((HALO + tm, cw), F32)], compiler_params=_cp("parallel", "arbitrary"),
    )(proj, proj, w, b.reshape(1, c), dxc)

    def body_in(dp_ref, nxt_ref, w_ref, dx_ref, ext):
        i = pl.program_id(1)
        ext[pl.ds(0, tm), :] = dp_ref[...]
        ext[pl.ds(tm, 8), :] = jnp.where(i < nt - 1, nxt_ref[...], 0.0)
        dx = w_ref[pl.ds(CONV_WIDTH - 1, 1), :] * ext[pl.ds(0, tm), :]
        for k in range(CONV_WIDTH - 1):
            dx = dx + w_ref[pl.ds(k, 1), :] * ext[pl.ds(CONV_WIDTH - 1 - k, tm), :]
        dx_ref[...] = dx.astype(BF16)

    dx = pl.pallas_call(
        body_in, name=name + "_in", grid=(c // cw, nt),
        in_specs=[BS((tm, cw), lambda j, i: (i, j)),
                  BS((8, cw), lambda j, i: (jnp.minimum((i + 1) * (tm // 8), t // 8 - 1), j)),
                  BS((CONV_WIDTH, cw), lambda j, i: (0, j))],
        out_specs=BS((tm, cw), lambda j, i: (i, j)), out_shape=SDS((t, c), BF16),
        scratch_shapes=[pltpu.VMEM((tm + 8, cw), F32)], compiler_params=_cp("parallel", "arbitrary"),
    )(dpre, dpre, w)
    return dx, dw, db[0]


def _softplus(x):
    return jnp.maximum(x, 0.0) + jnp.log(1.0 + jnp.exp(-jnp.abs(x)))


def _ssd_common(dtr_ref, bias_ref, a_ref):
    pre = dtr_ref[...] + bias_ref[...]
    dt = _softplus(pre)
    ri = lax.broadcasted_iota(jnp.int32, (CHUNK, CHUNK), 0)
    ci = lax.broadcasted_iota(jnp.int32, (CHUNK, CHUNK), 1)
    tril = ri >= ci
    la = _dot(tril.astype(F32), dt * a_ref[...], precision=HIGHEST)
    return pre, dt, la, la.T, tril


def _lane_col(x, lane, h):
    return jnp.sum(jnp.where(lane == h, x, 0.0), axis=-1, keepdims=True)


def _ssd_specs(xc, di, gw, cidx):
    nbx = di // LANES
    return [BS((CHUNK, gw), lambda g, c: (cidx(c), g)),
            BS((CHUNK, D_STATE), lambda g, c: (cidx(c), nbx + g)),
            BS((CHUNK, D_STATE), lambda g, c: (cidx(c), nbx + SSM_GROUPS + g)),
            BS((CHUNK, LANES), lambda g, c: (cidx(c), 0)),
            BS((1, LANES), lambda g, c: (0, 0)), BS((1, LANES), lambda g, c: (0, 0)),
            BS((1, gw), lambda g, c: (0, g))]


def _ssd_fwd(xc, dtr, dt_bias, a, dskip, *, name):
    t = xc.shape[0]
    di = xc.shape[1] - 2 * SSM_GROUPS * D_STATE
    gw = di // SSM_GROUPS
    hpg = gw // HEAD_DIM
    npair = gw // LANES
    nc = t // CHUNK

    def body(x_ref, b_ref, c_ref, dtr_ref, bias_ref, a_ref, dsk_ref, y_ref, st_ref, state):
        g, c = pl.program_id(0), pl.program_id(1)

        @pl.when(c == 0)
        def _():
            state[...] = jnp.zeros_like(state)

        st_ref[0, 0] = state[...]
        _, dt, la, la_t, tril = _ssd_common(dtr_ref, bias_ref, a_ref)
        lane = lax.broadcasted_iota(jnp.int32, (CHUNK, LANES), 1)
        sub = lax.broadcasted_iota(jnp.int32, (LANES, CHUNK), 0)
        lane1 = lax.broadcasted_iota(jnp.int32, (1, LANES), 1)
        low, low1 = lane < HEAD_DIM, lane1 < HEAD_DIM
        last = lax.broadcasted_iota(jnp.int32, (CHUNK, LANES), 0) == CHUNK - 1
        lend = jnp.sum(jnp.where(last, la, 0.0), axis=0, keepdims=True)
        bm, cm = b_ref[...], c_ref[...]
        gmat = _dot(cm, bm, NT)
        for p in range(npair):
            sl = slice(p * LANES, (p + 1) * LANES)
            ps = slice(p * D_STATE, (p + 1) * D_STATE)
            x2 = x_ref[:, sl].astype(F32)
            cols, ms = [], []
            for hh in range(2):
                h = g * hpg + p * 2 + hh
                col_la = _lane_col(la, lane, h)
                row_la = jnp.sum(jnp.where(sub == h, la_t, 0.0), axis=0, keepdims=True)
                lend_h = _lane_col(lend, lane1, h)
                decay = jnp.exp(jnp.where(tril, col_la - row_la, -jnp.inf))
                ms.append((gmat * decay).astype(BF16))
                cols.append((_lane_col(dt, lane, h), jnp.exp(col_la), jnp.exp(lend_h - col_la), jnp.exp(lend_h)))
            pair = lambda k: jnp.where(low, cols[0][k], cols[1][k])
            xdt = x2 * pair(0)
            xdtb = xdt.astype(BF16)
            s2 = state[ps, :]
            y = jnp.where(low, _dot(ms[0], xdtb), _dot(ms[1], xdtb))
            y = y + pair(1) * _dot(cm, s2.astype(BF16)) + x2 * dsk_ref[:, sl]
            y_ref[:, sl] = y
            state[ps, :] = s2 * jnp.where(low1, cols[0][3], cols[1][3]) + _dot(bm, (xdt * pair(2)).astype(BF16), TN)

    y, st = pl.pallas_call(
        body, name=name, grid=(SSM_GROUPS, nc), in_specs=_ssd_specs(xc, di, gw, lambda c: c),
        out_specs=[BS((CHUNK, gw), lambda g, c: (c, g)), BS((1, 1, npair * D_STATE, LANES), lambda g, c: (c, g, 0, 0))],
        out_shape=[SDS((t, di), F32), SDS((nc, SSM_GROUPS, npair * D_STATE, LANES), F32)],
        scratch_shapes=[pltpu.VMEM((npair * D_STATE, LANES), F32)], compiler_params=_cp("parallel", "arbitrary"),
    )(xc, xc, xc, dtr, dt_bias, a, dskip)
    return y, st


def _ssd_bwd(xc, dtr, dt_bias, a, dskip, st, dy, *, name):
    t = xc.shape[0]
    di = xc.shape[1] - 2 * SSM_GROUPS * D_STATE
    gw = di // SSM_GROUPS
    hpg = gw // HEAD_DIM
    npair = gw // LANES
    nc = t // CHUNK
    rev = lambda c: nc - 1 - c

    def body(x_ref, b_ref, c_ref, dtr_ref, bias_ref, a_ref, dsk_ref, st_ref, dy_ref,
             dx_ref, db_ref, dc_ref, ddtr_ref, da_ref, dbias_ref, ddsk_ref, dstate):
        g, c = pl.program_id(0), pl.program_id(1)

        @pl.when(c == 0)
        def _():
            dstate[...] = jnp.zeros_like(dstate)
            da_ref[...] = jnp.zeros_like(da_ref)
            dbias_ref[...] = jnp.zeros_like(dbias_ref)
            ddsk_ref[...] = jnp.zeros_like(ddsk_ref)

        pre, dt, la, la_t, tril = _ssd_common(dtr_ref, bias_ref, a_ref)
        lane = lax.broadcasted_iota(jnp.int32, (CHUNK, LANES), 1)
        sub = lax.broadcasted_iota(jnp.int32, (LANES, CHUNK), 0)
        lane1 = lax.broadcasted_iota(jnp.int32, (1, LANES), 1)
        low, low1 = lane < HEAD_DIM, lane1 < HEAD_DIM
        last = lax.broadcasted_iota(jnp.int32, (CHUNK, LANES), 0) == CHUNK - 1
        lend = jnp.sum(jnp.where(last, la, 0.0), axis=0, keepdims=True)
        bm, cm = b_ref[...], c_ref[...]
        gmat = _dot(cm, bm, NT)
        dg = jnp.zeros((CHUNK, CHUNK), F32)
        dla_cols = jnp.zeros((CHUNK, LANES), F32)
        dla_rows = jnp.zeros((LANES, CHUNK), F32)
        dtsum = jnp.zeros((CHUNK, LANES), F32)
        dbm = jnp.zeros((CHUNK, D_STATE), F32)
        dcm = jnp.zeros((CHUNK, D_STATE), F32)
        for p in range(npair):
            sl = slice(p * LANES, (p + 1) * LANES)
            ps = slice(p * D_STATE, (p + 1) * D_STATE)
            x2 = x_ref[:, sl].astype(F32)
            dy2 = dy_ref[:, sl]
            s2 = st_ref[0, 0, ps, :]
            ds2 = dstate[ps, :]
            hs, cols, ms, decays = [], [], [], []
            for hh in range(2):
                h = g * hpg + p * 2 + hh
                col_la = _lane_col(la, lane, h)
                row_la = jnp.sum(jnp.where(sub == h, la_t, 0.0), axis=0, keepdims=True)
                lend_h = _lane_col(lend, lane1, h)
                decay = jnp.exp(jnp.where(tril, col_la - row_la, -jnp.inf))
                hs.append(h)
                decays.append(decay)
                ms.append(gmat * decay)
                cols.append((_lane_col(dt, lane, h), jnp.exp(col_la), jnp.exp(lend_h - col_la), jnp.exp(lend_h)))
            pair = lambda k: jnp.where(low, cols[0][k], cols[1][k])
            dtc, ec, eend = pair(0), pair(1), pair(2)
            eend_s = jnp.where(low1, cols[0][3], cols[1][3])
            xdt = x2 * dtc
            xdtb = xdt.astype(BF16)
            dys = dy2 * ec
            dysb = dys.astype(BF16)
            dxdt_state = eend * _dot(bm, ds2.astype(BF16))
            inter = dys * _dot(cm, s2.astype(BF16))
            u = dxdt_state * xdt
            sds = s2 * ds2
            dxdt = dxdt_state
            for hh in range(2):
                h = hs[hh]
                mh = low if hh == 0 else jnp.logical_not(low)
                dym = jnp.where(mh, dy2, 0.0).astype(BF16)
                dxdt = dxdt + _dot(ms[hh].astype(BF16), dym, TN)
                dm = _dot(dym, xdtb, NT)
                w = dm * ms[hh]
                dg = dg + dm * decays[hh]
                u_col = jnp.sum(jnp.where(mh, u, 0.0), axis=-1, keepdims=True)
                dlend = jnp.sum(u_col, axis=0, keepdims=True) + cols[hh][3] * jnp.sum(jnp.where(low1 if hh == 0 else jnp.logical_not(low1), jnp.sum(sds, axis=0, keepdims=True), 0.0), axis=-1, keepdims=True)
                col = jnp.sum(w, axis=-1, keepdims=True) + jnp.sum(jnp.where(mh, inter, 0.0), axis=-1, keepdims=True) - u_col
                dla_cols = dla_cols + jnp.where(lane == h, col + jnp.where(last, dlend, 0.0), 0.0)
                dla_rows = dla_rows - jnp.where(sub == h, jnp.sum(w, axis=0, keepdims=True), 0.0)
            for hh in range(2):
                mh = low if hh == 0 else jnp.logical_not(low)
                dtsum = dtsum + jnp.where(lane == hs[hh], jnp.sum(jnp.where(mh, dxdt * x2, 0.0), axis=-1, keepdims=True), 0.0)
            dcm = dcm + _dot(dysb, s2.astype(BF16), NT)
            dbm = dbm + _dot((xdt * eend).astype(BF16), ds2.astype(BF16), NT)
            dstate[ps, :] = ds2 * eend_s + _dot(cm, dysb, TN)
            dx_ref[:, sl] = dxdt * dtc + dy2 * dsk_ref[:, sl]
            ddsk_ref[:, sl] += jnp.sum(dy2 * x2, axis=0, keepdims=True)
        dgb = dg.astype(BF16)
        dc_ref[...] = dcm + _dot(dgb, bm)
        db_ref[...] = dbm + _dot(dgb, cm, TN)
        dla = dla_cols + dla_rows.T
        triu = lax.broadcasted_iota(jnp.int32, (CHUNK, CHUNK), 0) <= lax.broadcasted_iota(jnp.int32, (CHUNK, CHUNK), 1)
        ddta = _dot(triu.astype(F32), dla, precision=HIGHEST)
        ddt = ddta * a_ref[...] + dtsum
        da_ref[0] += jnp.sum(ddta * dt, axis=0, keepdims=True)
        ddtr = ddt * _sigmoid(pre)
        ddtr_ref[0] = ddtr
        dbias_ref[0] += jnp.sum(ddtr, axis=0, keepdims=True)

    vec = BS((1, 1, LANES), lambda g, c: (g, 0, 0))
    outs = pl.pallas_call(
        body, name=name, grid=(SSM_GROUPS, nc),
        in_specs=_ssd_specs(xc, di, gw, rev) + [BS((1, 1, npair * D_STATE, LANES), lambda g, c: (rev(c), g, 0, 0)),
                                                BS((CHUNK, gw), lambda g, c: (rev(c), g))],
        out_specs=[BS((CHUNK, gw), lambda g, c: (rev(c), g)), BS((CHUNK, D_STATE), lambda g, c: (rev(c), g)),
                   BS((CHUNK, D_STATE), lambda g, c: (rev(c), g)), BS((1, CHUNK, LANES), lambda g, c: (g, rev(c), 0)),
                   vec, vec, BS((1, gw), lambda g, c: (0, g))],
        out_shape=[SDS((t, di), F32), SDS((t, SSM_GROUPS * D_STATE), F32), SDS((t, SSM_GROUPS * D_STATE), F32),
                   SDS((SSM_GROUPS, t, LANES), F32), SDS((SSM_GROUPS, 1, LANES), F32), SDS((SSM_GROUPS, 1, LANES), F32),
                   SDS((1, di), F32)],
        scratch_shapes=[pltpu.VMEM((npair * D_STATE, LANES), F32)], compiler_params=_cp("parallel", "arbitrary"),
    )(xc, xc, xc, dtr, dt_bias, a, dskip, st, dy)
    return outs


def _gate_norm_fwd(y, proj, zoff, w, *, name):
    t, di = y.shape
    gw = di // SSM_GROUPS
    tm = _pick(t, (512, 256, 128))
    zb = zoff // gw

    def body(y_ref, z_ref, w_ref, o_ref):
        yg = y_ref[...] * _silu(z_ref[...].astype(F32))
        r = lax.rsqrt(jnp.mean(yg * yg, axis=-1, keepdims=True) + EPS)
        o_ref[...] = (yg * r * w_ref[...]).astype(BF16)

    return pl.pallas_call(
        body, name=name, grid=(t // tm, SSM_GROUPS),
        in_specs=[BS((tm, gw), lambda i, g: (i, g)), BS((tm, gw), lambda i, g: (i, zb + g)), BS((1, gw), lambda i, g: (0, g))],
        out_specs=BS((tm, gw), lambda i, g: (i, g)), out_shape=SDS((t, di), BF16), compiler_params=_cp("parallel", "parallel"),
    )(y, proj, w.reshape(1, di))


def _gate_norm_bwd(dssm, y, proj, zoff, w, *, name):
    t, di = y.shape
    gw = di // SSM_GROUPS
    tm = _pick(t, (512, 256, 128))
    zb = zoff // gw

    def body(d_ref, y_ref, z_ref, w_ref, dy_ref, dz_ref, dw_ref):
        z = z_ref[...].astype(F32)
        yv = y_ref[...]
        sz = _silu(z)
        yg = yv * sz
        r = lax.rsqrt(jnp.mean(yg * yg, axis=-1, keepdims=True) + EPS)
        n = yg * r
        d = d_ref[...]
        dn = d * w_ref[...]
        dyg = r * (dn - n * jnp.mean(dn * n, axis=-1, keepdims=True))
        dy_ref[...] = dyg * sz
        dz_ref[...] = (dyg * yv * _dsilu(z)).astype(BF16)

        @pl.when(pl.program_id(1) == 0)
        def _():
            dw_ref[...] = jnp.zeros_like(dw_ref)

        dw_ref[...] += jnp.sum(d * n, axis=0, keepdims=True)

    blk = BS((tm, gw), lambda g, i: (i, g))
    dy, dz, dw = pl.pallas_call(
        body, name=name, grid=(SSM_GROUPS, t // tm),
        in_specs=[blk, blk, BS((tm, gw), lambda g, i: (i, zb + g)), BS((1, gw), lambda g, i: (0, g))],
        out_specs=[blk, blk, BS((1, gw), lambda g, i: (0, g))],
        out_shape=[SDS((t, di), F32), SDS((t, di), BF16), SDS((1, di), F32)], compiler_params=_cp("parallel", "arbitrary"),
    )(dssm, y, proj, w.reshape(1, di))
    return dy, dz, dw[0]


def _merge_fwd(proj, goff, ga, gs, *, name):
    t, d = ga.shape
    cw = _conv_block(d, goff)
    tm = _pick(t, (512, 256, 128))
    gb = goff // cw

    def body(g0, g1, a_ref, s_ref, o_ref):
        o_ref[...] = (_sigmoid(g0[...].astype(F32)) * a_ref[...] + _sigmoid(g1[...].astype(F32)) * s_ref[...]).astype(BF16)

    blk = BS((tm, cw), lambda i, j: (i, j))
    return pl.pallas_call(
        body, name=name, grid=(t // tm, d // cw),
        in_specs=[BS((tm, cw), lambda i, j: (i, gb + j)), BS((tm, cw), lambda i, j: (i, gb + d // cw + j)), blk, blk],
        out_specs=blk, out_shape=SDS((t, d), BF16), compiler_params=_cp("parallel", "parallel"),
    )(proj, proj, ga, gs)


def _merge_bwd(proj, goff, ga, gs, dm, *, name):
    t, d = ga.shape
    cw = _conv_block(d, goff)
    tm = _pick(t, (512, 256, 128))
    gb = goff // cw

    def body(g0, g1, a_ref, s_ref, dm_ref, da_ref, ds_ref, dg0_ref, dg1_ref):
        dmv = dm_ref[...]
        s0 = _sigmoid(g0[...].astype(F32))
        s1 = _sigmoid(g1[...].astype(F32))
        da_ref[...] = (s0 * dmv).astype(BF16)
        ds_ref[...] = (s1 * dmv).astype(BF16)
        dg0_ref[...] = (dmv * a_ref[...] * s0 * (1.0 - s0)).astype(BF16)
        dg1_ref[...] = (dmv * s_ref[...] * s1 * (1.0 - s1)).astype(BF16)

    blk = BS((tm, cw), lambda i, j: (i, j))
    return pl.pallas_call(
        body, name=name, grid=(t // tm, d // cw),
        in_specs=[BS((tm, cw), lambda i, j: (i, gb + j)), BS((tm, cw), lambda i, j: (i, gb + d // cw + j)), blk, blk, blk],
        out_specs=[blk] * 4, out_shape=[SDS((t, d), BF16)] * 4, compiler_params=_cp("parallel", "parallel"),
    )(proj, proj, ga, gs, dm)


def _swiglu_fwd(u, *, name):
    t, two_f = u.shape
    f = two_f // 2
    cw = _pick(f, (512, 256, 128))
    tm = _pick(t, (512, 256, 128))

    def body(g_ref, u_ref, o_ref):
        o_ref[...] = (_silu(g_ref[...].astype(F32)) * u_ref[...].astype(F32)).astype(BF16)

    return pl.pallas_call(
        body, name=name, grid=(t // tm, f // cw),
        in_specs=[BS((tm, cw), lambda i, j: (i, j)), BS((tm, cw), lambda i, j: (i, f // cw + j))],
        out_specs=BS((tm, cw), lambda i, j: (i, j)), out_shape=SDS((t, f), BF16), compiler_params=_cp("parallel", "parallel"),
    )(u, u)


def _swiglu_bwd(u, df, *, name):
    t, two_f = u.shape
    f = two_f // 2
    cw = _pick(f, (512, 256, 128))
    tm = _pick(t, (512, 256, 128))

    def body(g_ref, u_ref, d_ref, dg_ref, du_ref):
        gt = g_ref[...].astype(F32)
        d = d_ref[...].astype(F32)
        dg_ref[...] = (d * u_ref[...].astype(F32) * _dsilu(gt)).astype(BF16)
        du_ref[...] = (d * _silu(gt)).astype(BF16)

    blk = BS((tm, cw), lambda i, j: (i, j))
    return pl.pallas_call(
        body, name=name, grid=(t // tm, f // cw),
        in_specs=[blk, BS((tm, cw), lambda i, j: (i, f // cw + j)), blk],
        out_specs=[blk, blk], out_shape=[SDS((t, f), BF16)] * 2, compiler_params=_cp("parallel", "parallel"),
    )(u, u, df)


def _row_block(rows, cols, n_arrays):
    budget = VMEM_LIMIT_BYTES // 3
    for tr in (512, 256, 128, 64, 32, 16, 8):
        if rows % tr == 0 and tr * cols * 4 * n_arrays * 2 <= budget:
            return tr
    raise ValueError((rows, cols))


def _add2(a, b, *, name):
    rows, cols = a.shape
    tr = _row_block(rows, cols, 3)

    def body(a_ref, b_ref, o_ref):
        o_ref[...] = a_ref[...] + b_ref[...]

    blk = BS((tr, cols), lambda i: (i, 0))
    return pl.pallas_call(body, name=name, grid=(rows // tr,), in_specs=[blk, blk], out_specs=blk,
                          out_shape=SDS((rows, cols), F32), compiler_params=_cp("parallel"))(a, b)


def _sum_chips(a, *, name):
    _, rows, cols = a.shape
    tr = _row_block(rows, cols, 5)

    def body(a_ref, o_ref):
        o_ref[...] = ((a_ref[0] + a_ref[1]) + a_ref[2]) + a_ref[3]

    return pl.pallas_call(body, name=name, grid=(rows // tr,), in_specs=[BS((N_CHIPS, tr, cols), lambda i: (0, i, 0))],
                          out_specs=BS((tr, cols), lambda i: (i, 0)), out_shape=SDS((rows, cols), F32),
                          compiler_params=_cp("parallel"))(a)


def _adamw(w, g, m, v, *, name):
    rows, cols = w.shape
    tr = _row_block(rows, cols, 7) if rows % 8 == 0 else rows
    c1 = 1.0 - ADAM_B1 ** ADAM_STEP
    c2 = 1.0 - ADAM_B2 ** ADAM_STEP

    def body(w_ref, g_ref, m_ref, v_ref, d_ref, nm_ref, nv_ref):
        gv = g_ref[...]
        nm = ADAM_B1 * m_ref[...] + (1.0 - ADAM_B1) * gv
        nv = ADAM_B2 * v_ref[...] + (1.0 - ADAM_B2) * (gv * gv)
        d_ref[...] = -ADAM_LR * ((nm / c1) / (jnp.sqrt(nv / c2) + ADAM_EPS) + ADAM_WD * w_ref[...])
        nm_ref[...] = nm
        nv_ref[...] = nv

    blk = BS((tr, cols), lambda i: (i, 0))
    return pl.pallas_call(body, name=name, grid=(rows // tr,), in_specs=[blk] * 4, out_specs=[blk] * 3,
                          out_shape=[SDS((rows, cols), F32)] * 3, compiler_params=_cp("parallel"))(w, g, m, v)


ANY = BS(memory_space=pl.ANY)


def _place():
    x, y, c = lax.axis_index("x"), lax.axis_index("y"), lax.axis_index("c")
    return x, y, c, [(1 - x, y), (x, 1 - y), (1 - x, 1 - y)]


def _gather_shards(arrs, *, name):
    n = len(arrs)

    def body(*refs):
        ins, outs = refs[:n], refs[n:2 * n]
        send_sems, recv_sems, local_sems = refs[2 * n:]
        x, y, c, chips = _place()
        s = 2 * x + y
        copies = []
        for i in range(n):
            own = pltpu.make_async_copy(ins[i], outs[i].at[s], local_sems.at[i])
            own.start()
            copies.append(own)
            for j, (px, py) in enumerate(chips):
                cp = pltpu.make_async_remote_copy(
                    src_ref=ins[i], dst_ref=outs[i].at[s], send_sem=send_sems.at[i * 3 + j], recv_sem=recv_sems.at[i * 3 + j],
                    device_id=(px, py, c), device_id_type=MESH)
                cp.start()
                copies.append(cp)
        for cp in copies:
            cp.wait()

    return pl.pallas_call(
        body, name=name, in_specs=[ANY] * n, out_specs=[ANY] * n,
        out_shape=[SDS((N_CHIPS,) + a.shape, a.dtype) for a in arrs],
        scratch_shapes=[pltpu.SemaphoreType.DMA((3 * n,)), pltpu.SemaphoreType.DMA((3 * n,)), pltpu.SemaphoreType.DMA((n,))],
    )(*arrs)


def _pair_swap_layers(arrs, *, name):
    n = len(arrs)

    def body(*refs):
        ins, outs = refs[:n], refs[n:2 * n]
        send_sems, recv_sems = refs[2 * n:]
        x, y, c, _ = _place()
        copies = []
        for i in range(n):
            cp = pltpu.make_async_remote_copy(
                src_ref=ins[i].at[1 - c], dst_ref=outs[i], send_sem=send_sems.at[i], recv_sem=recv_sems.at[i],
                device_id=(x, y, 1 - c), device_id_type=MESH)
            cp.start()
            copies.append(cp)
        for cp in copies:
            cp.wait()

    return pl.pallas_call(
        body, name=name, in_specs=[ANY] * n, out_specs=[ANY] * n,
        out_shape=[SDS(a.shape[1:], a.dtype) for a in arrs],
        scratch_shapes=[pltpu.SemaphoreType.DMA((n,)), pltpu.SemaphoreType.DMA((n,))],
    )(*arrs)


def _scatter_to_chips(arrs, *, name):
    n = len(arrs)

    def body(*refs):
        ins, outs = refs[:n], refs[n:2 * n]
        send_sems, recv_sems, local_sems = refs[2 * n:]
        x, y, c, chips = _place()
        s = 2 * x + y
        copies = []
        for i in range(n):
            own = pltpu.make_async_copy(ins[i].at[s], outs[i].at[s], local_sems.at[i])
            own.start()
            copies.append(own)
            for j, (px, py) in enumerate(chips):
                cp = pltpu.make_async_remote_copy(
                    src_ref=ins[i].at[2 * px + py], dst_ref=outs[i].at[s], send_sem=send_sems.at[i * 3 + j],
                    recv_sem=recv_sems.at[i * 3 + j], device_id=(px, py, c), device_id_type=MESH)
                cp.start()
                copies.append(cp)
        for cp in copies:
            cp.wait()

    return pl.pallas_call(
        body, name=name, in_specs=[ANY] * n, out_specs=[ANY] * n, out_shape=[SDS(a.shape, a.dtype) for a in arrs],
        scratch_shapes=[pltpu.SemaphoreType.DMA((3 * n,)), pltpu.SemaphoreType.DMA((3 * n,)), pltpu.SemaphoreType.DMA((n,))],
    )(*arrs)


def _pair_join(arrs, *, name):
    n = len(arrs)

    def body(*refs):
        ins, outs = refs[:n], refs[n:2 * n]
        send_sems, recv_sems, local_sems = refs[2 * n:]
        x, y, c, _ = _place()
        copies = []
        for i in range(n):
            own = pltpu.make_async_copy(ins[i], outs[i].at[c], local_sems.at[i])
            own.start()
            copies.append(own)
            cp = pltpu.make_async_remote_copy(
                src_ref=ins[i], dst_ref=outs[i].at[c], send_sem=send_sems.at[i], recv_sem=recv_sems.at[i],
                device_id=(x, y, 1 - c), device_id_type=MESH)
            cp.start()
            copies.append(cp)
        for cp in copies:
            cp.wait()

    return pl.pallas_call(
        body, name=name, in_specs=[ANY] * n, out_specs=[ANY] * n, out_shape=[SDS((2,) + a.shape, a.dtype) for a in arrs],
        scratch_shapes=[pltpu.SemaphoreType.DMA((n,)), pltpu.SemaphoreType.DMA((n,)), pltpu.SemaphoreType.DMA((n,))],
    )(*arrs)


def _allreduce_small(v, *, name):
    rows, cols = v.shape

    def body(v_ref, o_ref, gath, send_sems, recv_sems):
        x, y, c, _ = _place()
        me = 4 * x + 2 * y + c
        gath[me] = v_ref[...]
        copies = []
        for k in range(1, N_DEV):
            fx, fy, fc = (k >> 2) & 1, (k >> 1) & 1, k & 1
            peer = (1 - x if fx else x, 1 - y if fy else y, 1 - c if fc else c)
            cp = pltpu.make_async_remote_copy(
                src_ref=v_ref, dst_ref=gath.at[me], send_sem=send_sems.at[k - 1], recv_sem=recv_sems.at[k - 1],
                device_id=peer, device_id_type=MESH)
            cp.start()
            copies.append(cp)
        for cp in copies:
            cp.wait()
        acc = gath[0]
        for k in range(1, N_DEV):
            acc = acc + gath[k]
        o_ref[...] = acc

    vm = BS(memory_space=pltpu.VMEM)
    return pl.pallas_call(
        body, name=name, in_specs=[vm], out_specs=vm, out_shape=SDS((rows, cols), F32),
        scratch_shapes=[pltpu.VMEM((N_DEV, rows, cols), F32), pltpu.SemaphoreType.DMA((N_DEV - 1,)), pltpu.SemaphoreType.DMA((N_DEV - 1,))],
    )(v)


def _t5_bucket(dist):
    max_exact = N_REL_BUCKETS // 2
    d_f = jnp.maximum(dist, 1).astype(F32)
    large = max_exact + (jnp.log(d_f / max_exact) / math.log(REL_MAX_DISTANCE / max_exact) * (N_REL_BUCKETS - max_exact)).astype(jnp.int32)
    return jnp.where(dist < max_exact, dist, jnp.minimum(large, N_REL_BUCKETS - 1))


def _rel_buckets(dilation):
    qi = jnp.arange(ATTN_BLOCK)[:, None]
    kj = jnp.arange(2 * ATTN_BLOCK)[None, :]
    return _t5_bucket(jnp.clip(qi + ATTN_BLOCK - kj, 0, N_STEPS) * dilation)


def _layer_fwd(h, p, biases, lname):
    sv = {"h": h}
    xn1 = _rms_fwd(h, p["norm1_w"], name=lname + "norm1")
    proj = _matmul(xn1, p["w_main"], out_dtype=BF16, name=lname + "in_proj")
    dtr = _matmul(xn1, p["w_dt"], out_dtype=F32, name=lname + "in_proj_dt")
    os_, lses = [], []
    for g in range(N_GROUPS_ATTN):
        o, lse = _attn_fwd(proj, biases[g], g, name=f"{lname}attn{g}")
        os_.append(o)
        lses.append(lse)
    attn = _combine_fwd(os_, lses, name=lname + "attn_combine")
    xc = _conv_fwd(proj, p["off_xbc"], p["conv_w"], p["conv_b"], name=lname + "conv")
    y, st = _ssd_fwd(xc, dtr, p["dt_bias"], p["a"], p["dskip"], name=lname + "ssd")
    ssm = _gate_norm_fwd(y, proj, p["off_z"], p["ssm_norm_w"], name=lname + "gate_norm")
    ga = _matmul(attn, p["w_attn_branch"], name=lname + "attn_branch")
    gs = _matmul(ssm, p["w_ssm_branch"], name=lname + "ssm_branch")
    merged = _merge_fwd(proj, p["off_gate"], ga, gs, name=lname + "merge")
    h1 = _matmul(merged, p["w_out"], res=h, name=lname + "out_proj")
    xn2 = _rms_fwd(h1, p["norm2_w"], name=lname + "norm2")
    u = _matmul(xn2, p["w_ffn_in"], out_dtype=BF16, name=lname + "ffn_in")
    f = _swiglu_fwd(u, name=lname + "swiglu")
    h2 = _matmul(f, p["w_ffn_out"], res=h1, name=lname + "ffn_out")
    sv.update(xn1=xn1, proj=proj, dtr=dtr, os=os_, lses=lses, attn=attn, xc=xc, y=y, st=st, ssm=ssm, ga=ga, gs=gs,
              merged=merged, h1=h1, xn2=xn2, u=u, f=f)
    return h2, sv


def _layer_bwd(dh2, p, sv, biases, lname):
    gr = {}
    lname = lname + "bwd_"
    df = _matmul(dh2, p["w_ffn_out"], tb=True, out_dtype=BF16, name=lname + "ffn_out_dx")
    gr["w_ffn_out"] = _matmul(sv["f"], dh2, ta=True, name=lname + "ffn_out_dw")
    dgate, dup = _swiglu_bwd(sv["u"], df, name=lname + "swiglu")
    du = jnp.concatenate([dgate, dup], axis=1)
    dxn2 = _matmul(du, p["w_ffn_in"], tb=True, name=lname + "ffn_in_dx")
    gr["w_ffn_in"] = _matmul(sv["xn2"], du, ta=True, name=lname + "ffn_in_dw")
    dh1, gr["norm2_w"] = _rms_bwd(sv["h1"], p["norm2_w"], [dxn2], dh2, name=lname + "norm2")
    dmerged = _matmul(dh1, p["w_out"], tb=True, name=lname + "out_proj_dx")
    gr["w_out"] = _matmul(sv["merged"], dh1, ta=True, name=lname + "out_proj_dw")
    dga, dgs, dg0, dg1 = _merge_bwd(sv["proj"], p["off_gate"], sv["ga"], sv["gs"], dmerged, name=lname + "merge")
    dattn = _matmul(dga, p["w_attn_branch"], tb=True, name=lname + "attn_branch_dx")
    gr["w_attn_branch"] = _matmul(sv["attn"], dga, ta=True, name=lname + "attn_branch_dw")
    dssm = _matmul(dgs, p["w_ssm_branch"], tb=True, name=lname + "ssm_branch_dx")
    gr["w_ssm_branch"] = _matmul(sv["ssm"], dgs, ta=True, name=lname + "ssm_branch_dw")
    dy, dz, gr["ssm_norm_w"] = _gate_norm_bwd(dssm, sv["y"], sv["proj"], p["off_z"], p["ssm_norm_w"], name=lname + "gate_norm")
    dxs, dbm, dcm, ddtr4, da4, dbias4, ddsk = _ssd_bwd(sv["xc"], sv["dtr"], p["dt_bias"], p["a"], p["dskip"], sv["st"], dy,
                                                       name=lname + "ssd")
    nsh = p["n_ssm_heads"]
    ddtr = jnp.sum(ddtr4, axis=0)
    gr["a_log"] = jnp.sum(da4, axis=(0, 1))[:nsh] * p["a"][0, :nsh]
    gr["dt_bias"] = jnp.sum(dbias4, axis=(0, 1))[:nsh]
    gr["d_skip"] = jnp.sum(ddsk.reshape(nsh, HEAD_DIM), axis=1)
    di = dxs.shape[1]
    dxbc, dcw, dcb = [], [], []
    for part, (lo, hi) in zip((dxs, dbm, dcm), ((0, di), (di, di + dbm.shape[1]), (di + dbm.shape[1], di + 2 * dbm.shape[1]))):
        dx_, dw_, db_ = _conv_bwd(sv["proj"], p["off_xbc"] + lo, p["conv_w"][:, lo:hi], p["conv_b"][lo:hi], part,
                                  name=f"{lname}conv{lo}")
        dxbc.append(dx_)
        dcw.append(dw_)
        dcb.append(db_)
    gr["conv_w"] = jnp.concatenate(dcw, axis=1)
    gr["conv_b"] = jnp.concatenate(dcb, axis=0)
    dos, corrs = _combine_bwd(dattn, sv["os"], sv["lses"], name=lname + "attn_combine")
    dqs, dks, dvs, dbiases = [], [], [], []
    for g in range(N_GROUPS_ATTN):
        dq, dk, dv, dbias = _attn_bwd(sv["proj"], biases[g], sv["lses"][g], dos[g], corrs[g], g, name=f"{lname}attn{g}")
        dqs.append(dq)
        dks.append(dk)
        dvs.append(dv)
        dbiases.append(dbias)
    dmain = jnp.concatenate(dqs + dks + dvs + [dz] + dxbc + [dg0, dg1], axis=1)
    dxn1a = _matmul(dmain, p["w_main"], tb=True, name=lname + "in_proj_dx")
    dxn1b = _matmul(ddtr, p["w_dt"], tb=True, name=lname + "in_proj_dt_dx")
    dw_main = _matmul(sv["xn1"], dmain, ta=True, name=lname + "in_proj_dw")
    dw_dt = _matmul(sv["xn1"], ddtr, ta=True, name=lname + "in_proj_dt_dw")
    od = p["off_dt"]
    gr["w_in"] = jnp.concatenate([dw_main[:, :od], dw_dt[:, :nsh], dw_main[:, od:]], axis=1)
    dh, gr["norm1_w"] = _rms_bwd(sv["h"], p["norm1_w"], [dxn1a, dxn1b], dh1, name=lname + "norm1")
    return dh, gr, dbiases


def _layer_params(l, w, n_ssm_heads, hg):
    d = w["norm1_w"].shape[1]
    aw = N_GROUPS_ATTN * hg * HEAD_DIM
    di = n_ssm_heads * HEAD_DIM
    xbc = di + 2 * SSM_GROUPS * D_STATE
    off_z, off_xbc = 3 * aw, 3 * aw + di
    off_dt = off_xbc + xbc
    w_in = w["w_in"][l]
    pad = lambda v: jnp.pad(v.astype(F32), (0, LANES - n_ssm_heads)).reshape(1, LANES)
    return dict(
        n_ssm_heads=n_ssm_heads, off_z=off_z, off_xbc=off_xbc, off_dt=off_dt, off_gate=off_dt,
        w_main=jnp.concatenate([w_in[:, :off_dt], w_in[:, off_dt + n_ssm_heads:]], axis=1),
        w_dt=jnp.pad(w_in[:, off_dt:off_dt + n_ssm_heads], ((0, 0), (0, LANES - n_ssm_heads))),
        norm1_w=w["norm1_w"][l], norm2_w=w["norm2_w"][l], conv_w=w["conv_w"][l], conv_b=w["conv_b"][l],
        dt_bias=pad(w["dt_bias"][l]), a=pad(-jnp.exp(w["a_log"][l])),
        dskip=jnp.repeat(w["d_skip"][l], HEAD_DIM).reshape(1, di), ssm_norm_w=w["ssm_norm_w"][l],
        w_attn_branch=w["w_attn_branch"][l], w_ssm_branch=w["w_ssm_branch"][l], w_out=w["w_out"][l],
        w_ffn_in=w["w_ffn_in"][l], w_ffn_out=w["w_ffn_out"][l],
    )


def _local_step(x, tgt, w):
    depth = w["norm1_w"].shape[0]
    n_ssm_heads = w["dt_bias"].shape[1]
    hg = w["rel_bias"].shape[1] // N_GROUPS_ATTN
    onehots = [(_rel_buckets(dil)[:, :, None] == jnp.arange(N_REL_BUCKETS)[None, None, :]).astype(F32) for dil in DILATIONS]
    biases = [jnp.einsum("qkb,bh->hqk", oh, w["rel_bias"][:, g * hg:(g + 1) * hg].astype(F32), precision=HIGHEST)
              for g, oh in enumerate(onehots)]
    params = [_layer_params(l, w, n_ssm_heads, hg) for l in range(depth)]
    h = x
    saved = []
    for l in range(depth):
        h, sv = _layer_fwd(h, params[l], biases, f"l{l}_")
        saved.append(sv)
    loss, dh, g_final = _loss_head(h, w["final_norm_w"], tgt, name="loss_head")
    grads = [None] * depth
    dbias_tot = [jnp.zeros(b.shape, F32) for b in biases]
    for l in reversed(range(depth)):
        dh, grads[l], dbiases = _layer_bwd(dh, params[l], saved[l], biases, f"l{l}_")
        dbias_tot = [a + b for a, b in zip(dbias_tot, dbiases)]
    out = {k: jnp.stack([gl[k] for gl in grads]) for k in grads[0]}
    out["final_norm_w"] = g_final
    drel = []
    for g, (oh, db) in enumerate(zip(onehots, dbias_tot)):
        oh_t = jnp.pad(oh.reshape(-1, N_REL_BUCKETS).T, ((0, LANES - N_REL_BUCKETS), (0, 0)))
        db_rows = jnp.pad(db.reshape(hg, -1), ((0, LANES - hg), (0, 0)))
        drel.append(_matmul(oh_t, db_rows, tb=True, name=f"rel_bias_fold{g}")[:N_REL_BUCKETS, :hg])
    out["rel_bias"] = jnp.concatenate(drel, axis=1)
    return loss, dh, out


MATRICES = ("w_in", "w_attn_branch", "w_ssm_branch", "w_out", "w_ffn_in", "w_ffn_out")
COL_SHARDED = ("w_in", "w_attn_branch", "w_ffn_in")
SMALL = ("norm1_w", "conv_b", "dt_bias", "a_log", "d_skip", "ssm_norm_w", "norm2_w", "rel_bias", "final_norm_w")
WEIGHTS = ("norm1_w", "w_in", "conv_w", "conv_b", "dt_bias", "a_log", "d_skip", "ssm_norm_w", "w_attn_branch",
           "w_ssm_branch", "w_out", "norm2_w", "w_ffn_in", "w_ffn_out", "rel_bias", "final_norm_w")
SMALL_COLS = 1024


def _unshard(name, g):
    _, depth, r, c = g.shape
    if name in COL_SHARDED or name == "conv_w":
        return jnp.transpose(g, (1, 2, 0, 3)).reshape(depth, r, N_CHIPS * c)
    return jnp.transpose(g, (1, 0, 2, 3)).reshape(depth, N_CHIPS * r, c)


def _to_shards(name, g):
    r, c = g.shape
    if name in COL_SHARDED:
        return jnp.transpose(g.reshape(r, N_CHIPS, c // N_CHIPS), (1, 0, 2))
    return g.reshape(N_CHIPS, r // N_CHIPS, c)


def kernel(x, norm1_w, w_in, conv_w, conv_b, dt_bias, a_log, d_skip, ssm_norm_w, w_attn_branch, w_ssm_branch, w_out, norm2_w, w_ffn_in, w_ffn_out, rel_bias, final_norm_w, loss_target, m_norm1_w, m_w_in, m_conv_w, m_conv_b, m_dt_bias, m_a_log, m_d_skip, m_ssm_norm_w, m_w_attn_branch, m_w_ssm_branch, m_w_out, m_norm2_w, m_w_ffn_in, m_w_ffn_out, m_rel_bias, m_final_norm_w, v_norm1_w, v_w_in, v_conv_w, v_conv_b, v_dt_bias, v_a_log, v_d_skip, v_ssm_norm_w, v_w_attn_branch, v_w_ssm_branch, v_w_out, v_norm2_w, v_w_ffn_in, v_w_ffn_out, v_rel_bias, v_final_norm_w):
    env = dict(locals())
    wts = {k: env[k] for k in WEIGHTS}
    mom = {k: env["m_" + k] for k in WEIGHTS}
    var = {k: env["v_" + k] for k in WEIGHTS}
    chip = 2 * lax.axis_index("x") + lax.axis_index("y")
    core = lax.axis_index("c")

    gathered = _gather_shards([wts[k].astype(BF16) for k in MATRICES] + [conv_w], name="gather_weights")
    full = {k: wts[k] for k in SMALL}
    for k, g in zip(MATRICES + ("conv_w",), gathered):
        full[k] = _unshard(k, g)

    loss, dx, grads = _local_step(x[0], loss_target[0], full)
    loss = lax.psum(loss, ("x", "y", "c"))

    from_pair = _pair_swap_layers([grads[k] for k in MATRICES], name="reduce_pair_swap")
    scattered_in = []
    for k, got in zip(MATRICES, from_pair):
        mine = lax.dynamic_index_in_dim(grads[k], core, axis=0, keepdims=False)
        scattered_in.append(_to_shards(k, _add2(mine, got, name="reduce_pair_add_" + k)))
    scattered = _scatter_to_chips(scattered_in, name="reduce_scatter")
    halves = [_sum_chips(a, name="reduce_sum_" + k) for k, a in zip(MATRICES, scattered)]
    reduced = dict(zip(MATRICES, _pair_join(halves, name="reduce_pair_join")))

    small_names = SMALL + ("conv_w",)
    flat = jnp.concatenate([grads[k].reshape(-1) for k in small_names])
    n_small = flat.shape[0]
    rows = -(-n_small // SMALL_COLS)
    rows = -(-rows // 8) * 8
    flat = jnp.pad(flat, (0, rows * SMALL_COLS - n_small)).reshape(rows, SMALL_COLS)
    flat = _allreduce_small(flat, name="allreduce_small").reshape(-1)
    pos = 0
    for k in small_names:
        size = math.prod(grads[k].shape)
        reduced[k] = flat[pos:pos + size].reshape(grads[k].shape)
        pos += size
    cs = conv_w.shape[2]
    reduced["conv_w"] = lax.dynamic_slice_in_dim(reduced["conv_w"], chip * cs, cs, axis=2)

    delta, new_m, new_v = {}, {}, {}
    for k in MATRICES:
        shape = wts[k].shape
        two_d = lambda a: a.reshape(shape[0] * shape[1], shape[2])
        d_, m_, v_ = _adamw(two_d(wts[k]), two_d(reduced[k]), two_d(mom[k]), two_d(var[k]), name="adamw_" + k)
        delta[k], new_m[k], new_v[k] = d_.reshape(shape), m_.reshape(shape), v_.reshape(shape)
    pack = lambda src: jnp.pad(jnp.concatenate([src[k].reshape(-1) for k in small_names]),
                               (0, rows * SMALL_COLS - n_shard)).reshape(rows, SMALL_COLS)
    n_shard = sum(math.prod(wts[k].shape) for k in small_names)
    d_, m_, v_ = _adamw(pack(wts), pack(reduced), pack(mom), pack(var), name="adamw_small")
    pos = 0
    for k in small_names:
        size = math.prod(wts[k].shape)
        for dst, src in ((delta, d_), (new_m, m_), (new_v, v_)):
            dst[k] = src.reshape(-1)[pos:pos + size].reshape(wts[k].shape)
        pos += size

    return (loss, dx[None], *[reduced[k] for k in WEIGHTS], *[delta[k] for k in WEIGHTS],
            *[new_m[k] for k in WEIGHTS], *[new_v[k] for k in WEIGHTS])
```

```python
import functools
import math

import jax
import jax.numpy as jnp
from jax import lax
from jax.experimental import pallas as pl
from jax.experimental.pallas import tpu as pltpu

F32, BF16 = jnp.float32, jnp.bfloat16
SDS = jax.ShapeDtypeStruct
BS = pl.BlockSpec
MESH = pl.DeviceIdType.MESH
HIGHEST = lax.Precision.HIGHEST

EPS = 1e-6
HEAD_DIM = 64
ATTN_BLOCK = 128
DILATIONS = (1, 4, 16)
N_GROUPS_ATTN = len(DILATIONS)
N_STEPS = 128
N_REL_BUCKETS = 32
REL_MAX_DISTANCE = 2048
SSM_GROUPS = 4
D_STATE = 128
CHUNK = 128
CONV_WIDTH = 4
HALO = 16
LANES = 128
N_CHIPS = 4
N_DEV = 8
VMEM_LIMIT_BYTES = 48 * 1024 * 1024

ADAM_LR, ADAM_B1, ADAM_B2, ADAM_EPS, ADAM_WD, ADAM_STEP = 0.001, 0.9, 0.999, 1e-08, 0.01, 10

NT = (((1,), (1,)), ((), ()))
TN = (((0,), (0,)), ((), ()))
NN = (((1,), (0,)), ((), ()))


def _cp(*sem):
    return pltpu.CompilerParams(dimension_semantics=sem if sem else None, vmem_limit_bytes=VMEM_LIMIT_BYTES)


def _pick(n, cands):
    for c in cands:
        if n % c == 0:
            return c
    raise ValueError(f"no block size of {cands} divides {n}")


def _divisors(n, cap):
    out = [c for c in range(LANES, min(n, cap) + 1, LANES) if n % c == 0]
    return out or [n]


def _wide(n, cap=2048):
    return _divisors(n, cap)[-1]


MXU_FLOPS = 9.0e14
HBM_BYTES_PER_S = 3.0e12
ACC_BYTES_PER_S = 4.0e12
GRID_STEP_S = 0.4e-6
TILE_VMEM_BYTES = 36 * 1024 * 1024


def _matmul_tiles(m, n, k, a_bytes, b_bytes, o_bytes, has_res):
    best = None
    for tm in _divisors(m, 2048):
        for tn in _divisors(n, 2048):
            for tk in _divisors(k, 4096):
                ni, nj, nk = m // tm, n // tn, k // tk
                vmem = 2 * (tm * tk * a_bytes + tk * tn * b_bytes + tm * tn * (o_bytes + (4 if has_res else 0)))
                vmem += tm * tn * 4 * (2 if nk > 1 else 1) + (tm * tk + tk * tn) * 2
                if vmem > TILE_VMEM_BYTES:
                    continue
                hbm = m * k * a_bytes * (nj if nk > 1 else 1) + k * n * b_bytes * (ni if nj * nk > 1 else 1)
                hbm += m * n * (o_bytes + (4 if has_res else 0))
                t = max(2.0 * m * n * k / MXU_FLOPS, hbm / HBM_BYTES_PER_S) + ni * nj * nk * GRID_STEP_S
                if nk > 1:
                    t += m * n * 8.0 * nk / ACC_BYTES_PER_S
                if best is None or t < best[0]:
                    best = (t, tm, tn, tk)
    assert best is not None, (m, n, k)
    return best[1:]


def _dot(a, b, dims=NN, precision=None):
    return lax.dot_general(a, b, dims, precision=precision, preferred_element_type=F32)


def _silu(x):
    return x / (1.0 + jnp.exp(-x))


def _sigmoid(x):
    return 1.0 / (1.0 + jnp.exp(-x))


def _dsilu(x):
    s = _sigmoid(x)
    return s * (1.0 + x * (1.0 - s))


def _matmul(a, b, *, name, ta=False, tb=False, out_dtype=F32, res=None):
    (kdim, m) = a.shape if ta else a.shape[::-1]
    (n, k2) = b.shape if tb else b.shape[::-1]
    assert kdim == k2, (a.shape, b.shape, ta, tb)
    tm, tn, tk = _matmul_tiles(m, n, kdim, a.dtype.itemsize, b.dtype.itemsize, jnp.dtype(out_dtype).itemsize, res is not None)
    nk = kdim // tk
    a_spec = BS((tk, tm), lambda i, j, k: (k, i)) if ta else BS((tm, tk), lambda i, j, k: (i, k))
    b_spec = BS((tn, tk), lambda i, j, k: (j, k)) if tb else BS((tk, tn), lambda i, j, k: (k, j))
    dims = (((0 if ta else 1,), (1 if tb else 0,)), ((), ()))
    has_res = res is not None

    def body(*refs):
        a_ref, b_ref = refs[:2]
        r_ref = refs[2] if has_res else None
        o_ref = refs[3] if has_res else refs[2]
        prod = _dot(a_ref[...].astype(BF16), b_ref[...].astype(BF16), dims)
        if nk == 1:
            o_ref[...] = (prod + r_ref[...] if has_res else prod).astype(o_ref.dtype)
            return
        acc = refs[-1]
        k = pl.program_id(2)

        @pl.when(k == 0)
        def _():
            acc[...] = prod

        @pl.when(k > 0)
        def _():
            acc[...] += prod

        @pl.when(k == nk - 1)
        def _():
            r = acc[...]
            if has_res:
                r = r + r_ref[...]
            o_ref[...] = r.astype(o_ref.dtype)

    in_specs = [a_spec, b_spec]
    args = [a, b]
    if has_res:
        in_specs.append(BS((tm, tn), lambda i, j, k: (i, j)))
        args.append(res)
    return pl.pallas_call(
        body, name=name, grid=(m // tm, n // tn, nk), in_specs=in_specs,
        out_specs=BS((tm, tn), lambda i, j, k: (i, j)), out_shape=SDS((m, n), out_dtype),
        scratch_shapes=[pltpu.VMEM((tm, tn), F32)] if nk > 1 else [],
        compiler_params=_cp("parallel", "parallel", "arbitrary"),
    )(*args)


def _rms_fwd(h, w, *, name):
    t, d = h.shape
    tm = _pick(t, (512, 256, 128))

    def body(h_ref, w_ref, o_ref):
        x = h_ref[...]
        r = lax.rsqrt(jnp.mean(x * x, axis=-1, keepdims=True) + EPS)
        o_ref[...] = (x * r * w_ref[...]).astype(BF16)

    return pl.pallas_call(
        body, name=name, grid=(t // tm,), in_specs=[BS((tm, d), lambda i: (i, 0)), BS((1, d), lambda i: (0, 0))],
        out_specs=BS((tm, d), lambda i: (i, 0)), out_shape=SDS((t, d), BF16), compiler_params=_cp("parallel"),
    )(h, w.reshape(1, d))


def _rms_bwd(h, w, dys, dres, *, name):
    t, d = h.shape
    tm = _pick(t, (512, 256, 128))
    n_dy = len(dys)

    def body(*refs):
        h_ref, w_ref = refs[:2]
        dy_refs = refs[2:2 + n_dy]
        dres_ref, dh_ref, dw_ref = refs[2 + n_dy:]
        x = h_ref[...]
        dy = dy_refs[0][...]
        for r_ in dy_refs[1:]:
            dy = dy + r_[...]
        r = lax.rsqrt(jnp.mean(x * x, axis=-1, keepdims=True) + EPS)
        g = dy * w_ref[...]
        proj = jnp.sum(g * x, axis=-1, keepdims=True) * (1.0 / d)
        dh_ref[...] = dres_ref[...] + r * g - x * (r * r * r) * proj

        @pl.when(pl.program_id(0) == 0)
        def _():
            dw_ref[...] = jnp.zeros_like(dw_ref)

        dw_ref[...] += jnp.sum(dy * x * r, axis=0, keepdims=True)

    row = BS((tm, d), lambda i: (i, 0))
    vec = BS((1, d), lambda i: (0, 0))
    dh, dw = pl.pallas_call(
        body, name=name, grid=(t // tm,), in_specs=[row, vec] + [row] * n_dy + [row],
        out_specs=[row, vec], out_shape=[SDS((t, d), F32), SDS((1, d), F32)], compiler_params=_cp("arbitrary"),
    )(h, w.reshape(1, d), *dys, dres)
    return dh, dw[0]


def _loss_head(h, w, tgt, *, name):
    t, d = h.shape
    tm = _pick(t, (512, 256, 128))

    def body(h_ref, w_ref, t_ref, loss_ref, dh_ref, dw_ref):
        x = h_ref[...]
        r = lax.rsqrt(jnp.mean(x * x, axis=-1, keepdims=True) + EPS)
        err = x * r * w_ref[...] - t_ref[...]
        loss_ref[...] = jnp.zeros(loss_ref.shape, F32) + 0.5 * jnp.sum(err * err) * (1.0 / d)
        dy = err * (1.0 / d)
        g = dy * w_ref[...]
        proj = jnp.sum(g * x, axis=-1, keepdims=True) * (1.0 / d)
        dh_ref[...] = r * g - x * (r * r * r) * proj

        @pl.when(pl.program_id(0) == 0)
        def _():
            dw_ref[...] = jnp.zeros_like(dw_ref)

        dw_ref[...] += jnp.sum(dy * x * r, axis=0, keepdims=True)

    row = BS((tm, d), lambda i: (i, 0))
    vec = BS((1, d), lambda i: (0, 0))
    loss, dh, dw = pl.pallas_call(
        body, name=name, grid=(t // tm,), in_specs=[row, vec, row],
        out_specs=[BS((1, 8, LANES), lambda i: (i, 0, 0)), row, vec],
        out_shape=[SDS((t // tm, 8, LANES), F32), SDS((t, d), F32), SDS((1, d), F32)], compiler_params=_cp("arbitrary"),
    )(h, w.reshape(1, d), tgt)
    return jnp.sum(loss[:, 0, 0]), dh, dw[0]


def _attn_masks(mb):
    qi = lax.broadcasted_iota(jnp.int32, (ATTN_BLOCK, 2 * ATTN_BLOCK), 0)
    kj = lax.broadcasted_iota(jnp.int32, (ATTN_BLOCK, 2 * ATTN_BLOCK), 1)
    steps = qi + ATTN_BLOCK - kj
    valid = (steps >= 0) & (steps <= N_STEPS) & ((kj >= ATTN_BLOCK) | (mb > 0))
    low = lax.broadcasted_iota(jnp.int32, (ATTN_BLOCK, LANES), 1) < HEAD_DIM
    return valid, low


def _to_residue_major(a, d):
    t, c = a.shape
    return a if d == 1 else a.reshape(t // d, d, c).transpose(1, 0, 2).reshape(t, c)


def _to_token_major(a, d):
    t, c = a.shape
    return a if d == 1 else a.reshape(d, t // d, c).transpose(1, 0, 2).reshape(t, c)


def _attn_specs(cols, nb, awg, clamp):
    def spec(col, prev):
        def index(r, mb):
            blk = clamp(mb)
            if prev:
                blk = jnp.maximum(blk - 1, 0)
            return (r * nb + blk, col)
        return BS((ATTN_BLOCK, awg), index)
    return [spec(cols[0], False), spec(cols[1], False), spec(cols[1], True), spec(cols[2], False), spec(cols[2], True)]


def _attn_fwd(qkv, cols, bias, d, *, name):
    t = qkv.shape[0]
    hg = bias.shape[0]
    awg = hg * HEAD_DIM
    nb = t // d // ATTN_BLOCK
    scale = HEAD_DIM ** -0.5

    def body(q_ref, kc_ref, kp_ref, vc_ref, vp_ref, b_ref, o_ref, l_ref):
        valid, low = _attn_masks(pl.program_id(1))
        for pi in range(awg // LANES):
            sl = slice(pi * LANES, (pi + 1) * LANES)
            q2 = q_ref[:, sl]
            k2 = jnp.concatenate([kp_ref[:, sl], kc_ref[:, sl]], axis=0)
            v2 = jnp.concatenate([vp_ref[:, sl], vc_ref[:, sl]], axis=0)
            outs, lses = [], []
            for hh in range(2):
                mh = low if hh == 0 else jnp.logical_not(low)
                qm = jnp.where(mh, q2, jnp.zeros_like(q2))
                s = _dot(qm, k2, NT) * scale + b_ref[pi * 2 + hh]
                s = jnp.where(valid, s, -jnp.inf)
                m = jnp.max(s, axis=-1, keepdims=True)
                p = jnp.exp(s - m)
                den = jnp.sum(p, axis=-1, keepdims=True)
                outs.append(_dot(p.astype(BF16), v2) / den)
                lses.append(jnp.broadcast_to(m + jnp.log(den), (ATTN_BLOCK, LANES)))
            o_ref[:, sl] = jnp.where(low, outs[0], outs[1])
            l_ref[:, sl] = jnp.where(low, lses[0], lses[1])

    out_spec = BS((ATTN_BLOCK, awg), lambda r, mb: (r * nb + mb, 0))
    return pl.pallas_call(
        body, name=name, grid=(d, nb),
        in_specs=_attn_specs(cols, nb, awg, lambda mb: mb) + [BS(bias.shape, lambda r, mb: (0, 0, 0))],
        out_specs=[out_spec, out_spec], out_shape=[SDS((t, awg), F32)] * 2, compiler_params=_cp("parallel", "parallel"),
    )(*([qkv] * 5), bias)


def _attn_bwd(qkv, cols, bias, lse, do, corr, d, *, name):
    t = qkv.shape[0]
    hg = bias.shape[0]
    awg = hg * HEAD_DIM
    nb = t // d // ATTN_BLOCK
    scale = HEAD_DIM ** -0.5

    def body(q_ref, kc_ref, kp_ref, vc_ref, vp_ref, b_ref, l_ref, do_ref, c_ref, dq_ref, dk_ref, dv_ref, db_ref, ck, cv):
        r, mb = pl.program_id(0), pl.program_id(1)

        @pl.when((r == 0) & (mb == 0))
        def _():
            db_ref[...] = jnp.zeros_like(db_ref)

        @pl.when(mb == 0)
        def _():
            ck[...] = jnp.zeros_like(ck)
            cv[...] = jnp.zeros_like(cv)

        @pl.when(mb < nb)
        def _():
            valid, low = _attn_masks(mb)
            for pi in range(awg // LANES):
                sl = slice(pi * LANES, (pi + 1) * LANES)
                q2 = q_ref[:, sl]
                k2 = jnp.concatenate([kp_ref[:, sl], kc_ref[:, sl]], axis=0)
                v2 = jnp.concatenate([vp_ref[:, sl], vc_ref[:, sl]], axis=0)
                do2 = do_ref[:, sl]
                lse2 = l_ref[:, sl]
                corr2 = c_ref[:, sl]
                dk2 = jnp.zeros((2 * ATTN_BLOCK, LANES), F32)
                dv2 = jnp.zeros((2 * ATTN_BLOCK, LANES), F32)
                dqs = []
                for hh in range(2):
                    mh = low if hh == 0 else jnp.logical_not(low)
                    qm = jnp.where(mh, q2, jnp.zeros_like(q2))
                    dom = jnp.where(mh, do2, jnp.zeros_like(do2))
                    lse_c = jnp.max(jnp.where(mh, lse2, -jnp.inf), axis=-1, keepdims=True)
                    corr_c = jnp.max(jnp.where(mh, corr2, -jnp.inf), axis=-1, keepdims=True)
                    s = _dot(qm, k2, NT) * scale + b_ref[pi * 2 + hh]
                    p = jnp.exp(jnp.where(valid, s, -jnp.inf) - lse_c)
                    ds = p * (_dot(dom, v2, NT) + corr_c)
                    db_ref[pi * 2 + hh] += ds
                    dsb = ds.astype(BF16)
                    dqs.append(_dot(dsb, k2) * scale)
                    dk2 = dk2 + _dot(dsb, qm, TN) * scale
                    dv2 = dv2 + _dot(p.astype(BF16), dom, TN)
                dq_ref[:, sl] = jnp.where(low, dqs[0], dqs[1]).astype(BF16)
                dk_ref[:, sl] = (ck[:, sl] + dk2[:ATTN_BLOCK]).astype(BF16)
                dv_ref[:, sl] = (cv[:, sl] + dv2[:ATTN_BLOCK]).astype(BF16)
                ck[:, sl] = dk2[ATTN_BLOCK:]
                cv[:, sl] = dv2[ATTN_BLOCK:]

        @pl.when(mb == nb)
        def _():
            dk_ref[...] = ck[...].astype(BF16)
            dv_ref[...] = cv[...].astype(BF16)

    clamp = lambda mb: jnp.minimum(mb, nb - 1)
    cur = BS((ATTN_BLOCK, awg), lambda r, mb: (r * nb + clamp(mb), 0))
    prev = BS((ATTN_BLOCK, awg), lambda r, mb: (r * nb + jnp.maximum(mb - 1, 0), 0))
    bias_spec = BS(bias.shape, lambda r, mb: (0, 0, 0))
    return pl.pallas_call(
        body, name=name, grid=(d, nb + 1),
        in_specs=_attn_specs(cols, nb, awg, clamp) + [bias_spec, cur, cur, cur],
        out_specs=[cur, prev, prev, bias_spec],
        out_shape=[SDS((t, awg), BF16)] * 3 + [SDS(bias.shape, F32)],
        scratch_shapes=[pltpu.VMEM((ATTN_BLOCK, awg), F32)] * 2, compiler_params=_cp("arbitrary", "arbitrary"),
    )(*([qkv] * 5), bias, lse, do, corr)


def _head_sum(x, low):
    a = jnp.sum(jnp.where(low, x, 0.0), axis=-1, keepdims=True)
    b = jnp.sum(jnp.where(low, 0.0, x), axis=-1, keepdims=True)
    return jnp.where(low, a, b)


def _combine_weights(lses):
    mx = jnp.maximum(jnp.maximum(lses[0], lses[1]), lses[2])
    es = [jnp.exp(l - mx) for l in lses]
    tot = es[0] + es[1] + es[2]
    return [e / tot for e in es]


def _combine_fwd(os_, lses, *, name):
    t, awg = os_[0].shape
    tm = _pick(t, (512, 256, 128))

    def body(o0, o1, o2, l0, l1, l2, out_ref):
        al = _combine_weights([l0[...], l1[...], l2[...]])
        out_ref[...] = (al[0] * o0[...] + al[1] * o1[...] + al[2] * o2[...]).astype(BF16)

    blk = BS((tm, awg), lambda i: (i, 0))
    return pl.pallas_call(
        body, name=name, grid=(t // tm,), in_specs=[blk] * 6, out_specs=blk,
        out_shape=SDS((t, awg), BF16), compiler_params=_cp("parallel"),
    )(*os_, *lses)


def _combine_bwd(dattn, os_, lses, *, name):
    t, awg = dattn.shape
    tm = _pick(t, (512, 256, 128))

    def body(da_ref, o0, o1, o2, l0, l1, l2, d0, d1, d2, c0, c1, c2):
        low = lax.broadcasted_iota(jnp.int32, (tm, LANES), 1) < HEAD_DIM
        for pi in range(awg // LANES):
            sl = slice(pi * LANES, (pi + 1) * LANES)
            da = da_ref[:, sl]
            al = _combine_weights([l0[:, sl], l1[:, sl], l2[:, sl]])
            tot = jnp.zeros((tm, LANES), F32)
            for a, o in zip(al, (o0, o1, o2)):
                tot = tot + a * _head_sum(da * o[:, sl], low)
            for a, d_ref, c_ref in zip(al, (d0, d1, d2), (c0, c1, c2)):
                d_ref[:, sl] = (a * da).astype(BF16)
                c_ref[:, sl] = -a * tot

    blk = BS((tm, awg), lambda i: (i, 0))
    outs = pl.pallas_call(
        body, name=name, grid=(t // tm,), in_specs=[blk] * 7, out_specs=[blk] * 6,
        out_shape=[SDS((t, awg), BF16)] * 3 + [SDS((t, awg), F32)] * 3, compiler_params=_cp("parallel"),
    )(dattn, *os_, *lses)
    return outs[:3], outs[3:]


def _conv_block(width, *offsets):
    for c in (512, 256, 128):
        if width % c == 0 and all(o % c == 0 for o in offsets):
            return c
    raise ValueError((width, offsets))


def _conv_pre(x_ref, halo_ref, w_ref, b_ref, ext, i, tm):
    ext[pl.ds(0, HALO), :] = jnp.where(i > 0, halo_ref[...].astype(F32), 0.0)
    ext[pl.ds(HALO, tm), :] = x_ref[...].astype(F32)
    pre = b_ref[...] + w_ref[pl.ds(CONV_WIDTH - 1, 1), :] * ext[pl.ds(HALO, tm), :]
    for k in range(CONV_WIDTH - 1):
        pre = pre + w_ref[pl.ds(k, 1), :] * ext[pl.ds(HALO - (CONV_WIDTH - 1) + k, tm), :]
    return pre


def _conv_fwd(proj, off, w, b, *, name):
    t = proj.shape[0]
    c = w.shape[1]
    cw = _conv_block(c, off)
    tm = _pick(t, (512, 256, 128))
    ob = off // cw

    def body(x_ref, halo_ref, w_ref, b_ref, o_ref, ext):
        pre = _conv_pre(x_ref, halo_ref, w_ref, b_ref, ext, pl.program_id(1), tm)
        o_ref[...] = _silu(pre).astype(BF16)

    return pl.pallas_call(
        body, name=name, grid=(c // cw, t // tm),
        in_specs=[BS((tm, cw), lambda j, i: (i, ob + j)),
                  BS((HALO, cw), lambda j, i: (jnp.maximum(i * (tm // HALO) - 1, 0), ob + j)),
                  BS((CONV_WIDTH, cw), lambda j, i: (0, j)), BS((1, cw), lambda j, i: (0, j))],
        out_specs=BS((tm, cw), lambda j, i: (i, j)), out_shape=SDS((t, c), BF16),
        scratch_shapes=[pltpu.VMEM((HALO + tm, cw), F32)], compiler_params=_cp("parallel", "arbitrary"),
    )(proj, proj, w, b.reshape(1, c))


def _conv_bwd(proj, off, w, b, dxc, *, name):
    t = proj.shape[0]
    c = w.shape[1]
    cw = _conv_block(c, off)
    tm = _pick(t, (512, 256, 128))
    ob = off // cw
    nt = t // tm

    def body_pre(x_ref, halo_ref, w_ref, b_ref, d_ref, dp_ref, dw_ref, db_ref, ext):
        i = pl.program_id(1)
        pre = _conv_pre(x_ref, halo_ref, w_ref, b_ref, ext, i, tm)
        dpre = d_ref[...] * _dsilu(pre)
        dp_ref[...] = dpre

        @pl.when(i == 0)
        def _():
            dw_ref[...] = jnp.zeros_like(dw_ref)
            db_ref[...] = jnp.zeros_like(db_ref)

        db_ref[...] += jnp.sum(dpre, axis=0, keepdims=True)
        for k in range(CONV_WIDTH):
            dw_ref[pl.ds(k, 1), :] += jnp.sum(dpre * ext[pl.ds(HALO - (CONV_WIDTH - 1) + k, tm), :], axis=0, keepdims=True)

    dpre, dw, db = pl.pallas_call(
        body_pre, name=name + "_pre", grid=(c // cw, nt),
        in_specs=[BS((tm, cw), lambda j, i: (i, ob + j)),
                  BS((HALO, cw), lambda j, i: (jnp.maximum(i * (tm // HALO) - 1, 0), ob + j)),
                  BS((CONV_WIDTH, cw), lambda j, i: (0, j)), BS((1, cw), lambda j, i: (0, j)),
                  BS((tm, cw), lambda j, i: (i, j))],
        out_specs=[BS((tm, cw), lambda j, i: (i, j)), BS((CONV_WIDTH, cw), lambda j, i: (0, j)), BS((1, cw), lambda j, i: (0, j))],
        out_shape=[SDS((t, c), F32), SDS((CONV_WIDTH, c), F32), SDS((1, c), F32)],
        scratch_shapes=[pltpu.VMEM((HALO + tm, cw), F32)], compiler_params=_cp("parallel", "arbitrary"),
    )(proj, proj, w, b.reshape(1, c), dxc)

    def body_in(dp_ref, nxt_ref, w_ref, dx_ref, ext):
        i = pl.program_id(1)
        ext[pl.ds(0, tm), :] = dp_ref[...]
        ext[pl.ds(tm, 8), :] = jnp.where(i < nt - 1, nxt_ref[...], 0.0)
        dx = w_ref[pl.ds(CONV_WIDTH - 1, 1), :] * ext[pl.ds(0, tm), :]
        for k in range(CONV_WIDTH - 1):
            dx = dx + w_ref[pl.ds(k, 1), :] * ext[pl.ds(CONV_WIDTH - 1 - k, tm), :]
        dx_ref[...] = dx.astype(BF16)

    dx = pl.pallas_call(
        body_in, name=name + "_in", grid=(c // cw, nt),
        in_specs=[BS((tm, cw), lambda j, i: (i, j)),
                  BS((8, cw), lambda j, i: (jnp.minimum((i + 1) * (tm // 8), t // 8 - 1), j)),
                  BS((CONV_WIDTH, cw), lambda j, i: (0, j))],
        out_specs=BS((tm, cw), lambda j, i: (i, j)), out_shape=SDS((t, c), BF16),
        scratch_shapes=[pltpu.VMEM((tm + 8, cw), F32)], compiler_params=_cp("parallel", "arbitrary"),
    )(dpre, dpre, w)
    return dx, dw, db[0]


def _softplus(x):
    return jnp.maximum(x, 0.0) + jnp.log(1.0 + jnp.exp(-jnp.abs(x)))


def _ssd_common(dtr_ref, bias_ref, a_ref):
    pre = dtr_ref[...] + bias_ref[...]
    dt = _softplus(pre)
    ri = lax.broadcasted_iota(jnp.int32, (CHUNK, CHUNK), 0)
    ci = lax.broadcasted_iota(jnp.int32, (CHUNK, CHUNK), 1)
    tril = ri >= ci
    la = _dot(tril.astype(F32), dt * a_ref[...], precision=HIGHEST)
    return pre, dt, la, la.T, tril


def _lane_col(x, lane, h):
    return jnp.sum(jnp.where(lane == h, x, 0.0), axis=-1, keepdims=True)


def _ssd_specs(xc, di, gw, cidx):
    nbx = di // LANES
    return [BS((CHUNK, gw), lambda g, c: (cidx(c), g)),
            BS((CHUNK, D_STATE), lambda g, c: (cidx(c), nbx + g)),
            BS((CHUNK, D_STATE), lambda g, c: (cidx(c), nbx + SSM_GROUPS + g)),
            BS((CHUNK, LANES), lambda g, c: (cidx(c), 0)),
            BS((1, LANES), lambda g, c: (0, 0)), BS((1, LANES), lambda g, c: (0, 0)),
            BS((1, gw), lambda g, c: (0, g))]


def _ssd_fwd(xc, dtr, dt_bias, a, dskip, *, name):
    t = xc.shape[0]
    di = xc.shape[1] - 2 * SSM_GROUPS * D_STATE
    gw = di // SSM_GROUPS
    hpg = gw // HEAD_DIM
    npair = gw // LANES
    nc = t // CHUNK

    def body(x_ref, b_ref, c_ref, dtr_ref, bias_ref, a_ref, dsk_ref, y_ref, st_ref, state):
        g, c = pl.program_id(0), pl.program_id(1)

        @pl.when(c == 0)
        def _():
            state[...] = jnp.zeros_like(state)

        st_ref[0, 0] = state[...]
        _, dt, la, la_t, tril = _ssd_common(dtr_ref, bias_ref, a_ref)
        lane = lax.broadcasted_iota(jnp.int32, (CHUNK, LANES), 1)
        sub = lax.broadcasted_iota(jnp.int32, (LANES, CHUNK), 0)
        lane1 = lax.broadcasted_iota(jnp.int32, (1, LANES), 1)
        low, low1 = lane < HEAD_DIM, lane1 < HEAD_DIM
        last = lax.broadcasted_iota(jnp.int32, (CHUNK, LANES), 0) == CHUNK - 1
        lend = jnp.sum(jnp.where(last, la, 0.0), axis=0, keepdims=True)
        bm, cm = b_ref[...], c_ref[...]
        gmat = _dot(cm, bm, NT)
        for p in range(npair):
            sl = slice(p * LANES, (p + 1) * LANES)
            ps = slice(p * D_STATE, (p + 1) * D_STATE)
            x2 = x_ref[:, sl].astype(F32)
            cols, ms = [], []
            for hh in range(2):
                h = g * hpg + p * 2 + hh
                col_la = _lane_col(la, lane, h)
                row_la = jnp.sum(jnp.where(sub == h, la_t, 0.0), axis=0, keepdims=True)
                lend_h = _lane_col(lend, lane1, h)
                decay = jnp.exp(jnp.where(tril, col_la - row_la, -jnp.inf))
                ms.append((gmat * decay).astype(BF16))
                cols.append((_lane_col(dt, lane, h), jnp.exp(col_la), jnp.exp(lend_h - col_la), jnp.exp(lend_h)))
            pair = lambda k: jnp.where(low, cols[0][k], cols[1][k])
            xdt = x2 * pair(0)
            xdtb = xdt.astype(BF16)
            s2 = state[ps, :]
            y = jnp.where(low, _dot(ms[0], xdtb), _dot(ms[1], xdtb))
            y = y + pair(1) * _dot(cm, s2.astype(BF16)) + x2 * dsk_ref[:, sl]
            y_ref[:, sl] = y
            state[ps, :] = s2 * jnp.where(low1, cols[0][3], cols[1][3]) + _dot(bm, (xdt * pair(2)).astype(BF16), TN)

    y, st = pl.pallas_call(
        body, name=name, grid=(SSM_GROUPS, nc), in_specs=_ssd_specs(xc, di, gw, lambda c: c),
        out_specs=[BS((CHUNK, gw), lambda g, c: (c, g)), BS((1, 1, npair * D_STATE, LANES), lambda g, c: (c, g, 0, 0))],
        out_shape=[SDS((t, di), F32), SDS((nc, SSM_GROUPS, npair * D_STATE, LANES), F32)],
        scratch_shapes=[pltpu.VMEM((npair * D_STATE, LANES), F32)], compiler_params=_cp("parallel", "arbitrary"),
    )(xc, xc, xc, dtr, dt_bias, a, dskip)
    return y, st


def _ssd_bwd(xc, dtr, dt_bias, a, dskip, st, dy, *, name):
    t = xc.shape[0]
    di = xc.shape[1] - 2 * SSM_GROUPS * D_STATE
    gw = di // SSM_GROUPS
    hpg = gw // HEAD_DIM
    npair = gw // LANES
    nc = t // CHUNK
    rev = lambda c: nc - 1 - c

    def body(x_ref, b_ref, c_ref, dtr_ref, bias_ref, a_ref, dsk_ref, st_ref, dy_ref,
             dx_ref, db_ref, dc_ref, ddtr_ref, da_ref, dbias_ref, ddsk_ref, dstate):
        g, c = pl.program_id(0), pl.program_id(1)

        @pl.when(c == 0)
        def _():
            dstate[...] = jnp.zeros_like(dstate)
            da_ref[...] = jnp.zeros_like(da_ref)
            dbias_ref[...] = jnp.zeros_like(dbias_ref)
            ddsk_ref[...] = jnp.zeros_like(ddsk_ref)

        pre, dt, la, la_t, tril = _ssd_common(dtr_ref, bias_ref, a_ref)
        lane = lax.broadcasted_iota(jnp.int32, (CHUNK, LANES), 1)
        sub = lax.broadcasted_iota(jnp.int32, (LANES, CHUNK), 0)
        lane1 = lax.broadcasted_iota(jnp.int32, (1, LANES), 1)
        low, low1 = lane < HEAD_DIM, lane1 < HEAD_DIM
        last = lax.broadcasted_iota(jnp.int32, (CHUNK, LANES), 0) == CHUNK - 1
        lend = jnp.sum(jnp.where(last, la, 0.0), axis=0, keepdims=True)
        bm, cm = b_ref[...], c_ref[...]
        gmat = _dot(cm, bm, NT)
        dg = jnp.zeros((CHUNK, CHUNK), F32)
        dla_cols = jnp.zeros((CHUNK, LANES), F32)
        dla_rows = jnp.zeros((LANES, CHUNK), F32)
        dtsum = jnp.zeros((CHUNK, LANES), F32)
        dbm = jnp.zeros((CHUNK, D_STATE), F32)
        dcm = jnp.zeros((CHUNK, D_STATE), F32)
        for p in range(npair):
            sl = slice(p * LANES, (p + 1) * LANES)
            ps = slice(p * D_STATE, (p + 1) * D_STATE)
            x2 = x_ref[:, sl].astype(F32)
            dy2 = dy_ref[:, sl]
            s2 = st_ref[0, 0, ps, :]
            ds2 = dstate[ps, :]
            hs, cols, ms, decays = [], [], [], []
            for hh in range(2):
                h = g * hpg + p * 2 + hh
                col_la = _lane_col(la, lane, h)
                row_la = jnp.sum(jnp.where(sub == h, la_t, 0.0), axis=0, keepdims=True)
                lend_h = _lane_col(lend, lane1, h)
                decay = jnp.exp(jnp.where(tril, col_la - row_la, -jnp.inf))
                hs.append(h)
                decays.append(decay)
                ms.append(gmat * decay)
                cols.append((_lane_col(dt, lane, h), jnp.exp(col_la), jnp.exp(lend_h - col_la), jnp.exp(lend_h)))
            pair = lambda k: jnp.where(low, cols[0][k], cols[1][k])
            dtc, ec, eend = pair(0), pair(1), pair(2)
            eend_s = jnp.where(low1, cols[0][3], cols[1][3])
            xdt = x2 * dtc
            xdtb = xdt.astype(BF16)
            dys = dy2 * ec
            dysb = dys.astype(BF16)
            dxdt_state = eend * _dot(bm, ds2.astype(BF16))
            inter = dys * _dot(cm, s2.astype(BF16))
            u = dxdt_state * xdt
            sds = s2 * ds2
            dxdt = dxdt_state
            for hh in range(2):
                h = hs[hh]
                mh = low if hh == 0 else jnp.logical_not(low)
                dym = jnp.where(mh, dy2, 0.0).astype(BF16)
                dxdt = dxdt + _dot(ms[hh].astype(BF16), dym, TN)
                dm = _dot(dym, xdtb, NT)
                w = dm * ms[hh]
                dg = dg + dm * decays[hh]
                u_col = jnp.sum(jnp.where(mh, u, 0.0), axis=-1, keepdims=True)
                dlend = jnp.sum(u_col, axis=0, keepdims=True) + cols[hh][3] * jnp.sum(jnp.where(low1 if hh == 0 else jnp.logical_not(low1), jnp.sum(sds, axis=0, keepdims=True), 0.0), axis=-1, keepdims=True)
                col = jnp.sum(w, axis=-1, keepdims=True) + jnp.sum(jnp.where(mh, inter, 0.0), axis=-1, keepdims=True) - u_col
                dla_cols = dla_cols + jnp.where(lane == h, col + jnp.where(last, dlend, 0.0), 0.0)
                dla_rows = dla_rows - jnp.where(sub == h, jnp.sum(w, axis=0, keepdims=True), 0.0)
            for hh in range(2):
                mh = low if hh == 0 else jnp.logical_not(low)
                dtsum = dtsum + jnp.where(lane == hs[hh], jnp.sum(jnp.where(mh, dxdt * x2, 0.0), axis=-1, keepdims=True), 0.0)
            dcm = dcm + _dot(dysb, s2.astype(BF16), NT)
            dbm = dbm + _dot((xdt * eend).astype(BF16), ds2.astype(BF16), NT)
            dstate[ps, :] = ds2 * eend_s + _dot(cm, dysb, TN)
            dx_ref[:, sl] = dxdt * dtc + dy2 * dsk_ref[:, sl]
            ddsk_ref[:, sl] += jnp.sum(dy2 * x2, axis=0, keepdims=True)
        dgb = dg.astype(BF16)
        dc_ref[...] = dcm + _dot(dgb, bm)
        db_ref[...] = dbm + _dot(dgb, cm, TN)
        dla = dla_cols + dla_rows.T
        triu = lax.broadcasted_iota(jnp.int32, (CHUNK, CHUNK), 0) <= lax.broadcasted_iota(jnp.int32, (CHUNK, CHUNK), 1)
        ddta = _dot(triu.astype(F32), dla, precision=HIGHEST)
        ddt = ddta * a_ref[...] + dtsum
        da_ref[0] += jnp.sum(ddta * dt, axis=0, keepdims=True)
        ddtr = ddt * _sigmoid(pre)
        ddtr_ref[0] = ddtr
        dbias_ref[0] += jnp.sum(ddtr, axis=0, keepdims=True)

    vec = BS((1, 1, LANES), lambda g, c: (g, 0, 0))
    outs = pl.pallas_call(
        body, name=name, grid=(SSM_GROUPS, nc),
        in_specs=_ssd_specs(xc, di, gw, rev) + [BS((1, 1, npair * D_STATE, LANES), lambda g, c: (rev(c), g, 0, 0)),
                                                BS((CHUNK, gw), lambda g, c: (rev(c), g))],
        out_specs=[BS((CHUNK, gw), lambda g, c: (rev(c), g)), BS((CHUNK, D_STATE), lambda g, c: (rev(c), g)),
                   BS((CHUNK, D_STATE), lambda g, c: (rev(c), g)), BS((1, CHUNK, LANES), lambda g, c: (g, rev(c), 0)),
                   vec, vec, BS((1, gw), lambda g, c: (0, g))],
        out_shape=[SDS((t, di), F32), SDS((t, SSM_GROUPS * D_STATE), F32), SDS((t, SSM_GROUPS * D_STATE), F32),
                   SDS((SSM_GROUPS, t, LANES), F32), SDS((SSM_GROUPS, 1, LANES), F32), SDS((SSM_GROUPS, 1, LANES), F32),
                   SDS((1, di), F32)],
        scratch_shapes=[pltpu.VMEM((npair * D_STATE, LANES), F32)], compiler_params=_cp("parallel", "arbitrary"),
    )(xc, xc, xc, dtr, dt_bias, a, dskip, st, dy)
    return outs


def _gate_norm_fwd(y, proj, zoff, w, *, name):
    t, di = y.shape
    gw = di // SSM_GROUPS
    tm = _pick(t, (512, 256, 128))
    zb = zoff // gw

    def body(y_ref, z_ref, w_ref, o_ref):
        yg = y_ref[...] * _silu(z_ref[...].astype(F32))
        r = lax.rsqrt(jnp.mean(yg * yg, axis=-1, keepdims=True) + EPS)
        o_ref[...] = (yg * r * w_ref[...]).astype(BF16)

    return pl.pallas_call(
        body, name=name, grid=(t // tm, SSM_GROUPS),
        in_specs=[BS((tm, gw), lambda i, g: (i, g)), BS((tm, gw), lambda i, g: (i, zb + g)), BS((1, gw), lambda i, g: (0, g))],
        out_specs=BS((tm, gw), lambda i, g: (i, g)), out_shape=SDS((t, di), BF16), compiler_params=_cp("parallel", "parallel"),
    )(y, proj, w.reshape(1, di))


def _gate_norm_bwd(dssm, y, proj, zoff, w, *, name):
    t, di = y.shape
    gw = di // SSM_GROUPS
    tm = _pick(t, (512, 256, 128))
    zb = zoff // gw

    def body(d_ref, y_ref, z_ref, w_ref, dy_ref, dz_ref, dw_ref):
        z = z_ref[...].astype(F32)
        yv = y_ref[...]
        sz = _silu(z)
        yg = yv * sz
        r = lax.rsqrt(jnp.mean(yg * yg, axis=-1, keepdims=True) + EPS)
        n = yg * r
        d = d_ref[...]
        dn = d * w_ref[...]
        dyg = r * (dn - n * jnp.mean(dn * n, axis=-1, keepdims=True))
        dy_ref[...] = dyg * sz
        dz_ref[...] = (dyg * yv * _dsilu(z)).astype(BF16)

        @pl.when(pl.program_id(1) == 0)
        def _():
            dw_ref[...] = jnp.zeros_like(dw_ref)

        dw_ref[...] += jnp.sum(d * n, axis=0, keepdims=True)

    blk = BS((tm, gw), lambda g, i: (i, g))
    dy, dz, dw = pl.pallas_call(
        body, name=name, grid=(SSM_GROUPS, t // tm),
        in_specs=[blk, blk, BS((tm, gw), lambda g, i: (i, zb + g)), BS((1, gw), lambda g, i: (0, g))],
        out_specs=[blk, blk, BS((1, gw), lambda g, i: (0, g))],
        out_shape=[SDS((t, di), F32), SDS((t, di), BF16), SDS((1, di), F32)], compiler_params=_cp("parallel", "arbitrary"),
    )(dssm, y, proj, w.reshape(1, di))
    return dy, dz, dw[0]


def _merge_fwd(proj, goff, ga, gs, *, name):
    t, d = ga.shape
    cw = _conv_block(d, goff)
    tm = _pick(t, (512, 256, 128))
    gb = goff // cw

    def body(g0, g1, a_ref, s_ref, o_ref):
        o_ref[...] = (_sigmoid(g0[...].astype(F32)) * a_ref[...] + _sigmoid(g1[...].astype(F32)) * s_ref[...]).astype(BF16)

    blk = BS((tm, cw), lambda i, j: (i, j))
    return pl.pallas_call(
        body, name=name, grid=(t // tm, d // cw),
        in_specs=[BS((tm, cw), lambda i, j: (i, gb + j)), BS((tm, cw), lambda i, j: (i, gb + d // cw + j)), blk, blk],
        out_specs=blk, out_shape=SDS((t, d), BF16), compiler_params=_cp("parallel", "parallel"),
    )(proj, proj, ga, gs)


def _merge_bwd(proj, goff, ga, gs, dm, *, name):
    t, d = ga.shape
    cw = _conv_block(d, goff)
    tm = _pick(t, (512, 256, 128))
    gb = goff // cw

    def body(g0, g1, a_ref, s_ref, dm_ref, da_ref, ds_ref, dg0_ref, dg1_ref):
        dmv = dm_ref[...]
        s0 = _sigmoid(g0[...].astype(F32))
        s1 = _sigmoid(g1[...].astype(F32))
        da_ref[...] = (s0 * dmv).astype(BF16)
        ds_ref[...] = (s1 * dmv).astype(BF16)
        dg0_ref[...] = (dmv * a_ref[...] * s0 * (1.0 - s0)).astype(BF16)
        dg1_ref[...] = (dmv * s_ref[...] * s1 * (1.0 - s1)).astype(BF16)

    blk = BS((tm, cw), lambda i, j: (i, j))
    return pl.pallas_call(
        body, name=name, grid=(t // tm, d // cw),
        in_specs=[BS((tm, cw), lambda i, j: (i, gb + j)), BS((tm, cw), lambda i, j: (i, gb + d // cw + j)), blk, blk, blk],
        out_specs=[blk] * 4, out_shape=[SDS((t, d), BF16)] * 4, compiler_params=_cp("parallel", "parallel"),
    )(proj, proj, ga, gs, dm)


def _swiglu_fwd(u, *, name):
    t, two_f = u.shape
    f = two_f // 2
    cw = _wide(f)
    tm = _pick(t, (512, 256, 128))

    def body(g_ref, u_ref, o_ref):
        o_ref[...] = (_silu(g_ref[...].astype(F32)) * u_ref[...].astype(F32)).astype(BF16)

    return pl.pallas_call(
        body, name=name, grid=(t // tm, f // cw),
        in_specs=[BS((tm, cw), lambda i, j: (i, j)), BS((tm, cw), lambda i, j: (i, f // cw + j))],
        out_specs=BS((tm, cw), lambda i, j: (i, j)), out_shape=SDS((t, f), BF16), compiler_params=_cp("parallel", "parallel"),
    )(u, u)


def _swiglu_bwd(u, df, *, name):
    t, two_f = u.shape
    f = two_f // 2
    cw = _wide(f)
    tm = _pick(t, (512, 256, 128))

    def body(g_ref, u_ref, d_ref, dg_ref, du_ref):
        gt = g_ref[...].astype(F32)
        d = d_ref[...].astype(F32)
        dg_ref[...] = (d * u_ref[...].astype(F32) * _dsilu(gt)).astype(BF16)
        du_ref[...] = (d * _silu(gt)).astype(BF16)

    blk = BS((tm, cw), lambda i, j: (i, j))
    return pl.pallas_call(
        body, name=name, grid=(t // tm, f // cw),
        in_specs=[blk, BS((tm, cw), lambda i, j: (i, f // cw + j)), blk],
        out_specs=[blk, blk], out_shape=[SDS((t, f), BF16)] * 2, compiler_params=_cp("parallel", "parallel"),
    )(u, u, df)


def _row_block(rows, cols, n_arrays):
    budget = VMEM_LIMIT_BYTES // 3
    for tr in (512, 256, 128, 64, 32, 16, 8):
        if rows % tr == 0 and tr * cols * 4 * n_arrays * 2 <= budget:
            return tr
    raise ValueError((rows, cols))


def _add2(a, b, *, name):
    rows, cols = a.shape
    tr = _row_block(rows, cols, 3)

    def body(a_ref, b_ref, o_ref):
        o_ref[...] = a_ref[...] + b_ref[...]

    blk = BS((tr, cols), lambda i: (i, 0))
    return pl.pallas_call(body, name=name, grid=(rows // tr,), in_specs=[blk, blk], out_specs=blk,
                          out_shape=SDS((rows, cols), F32), compiler_params=_cp("parallel"))(a, b)


def _sum_chips(a, *, name):
    _, rows, cols = a.shape
    tr = _row_block(rows, cols, 5)

    def body(a_ref, o_ref):
        o_ref[...] = ((a_ref[0] + a_ref[1]) + a_ref[2]) + a_ref[3]

    return pl.pallas_call(body, name=name, grid=(rows // tr,), in_specs=[BS((N_CHIPS, tr, cols), lambda i: (0, i, 0))],
                          out_specs=BS((tr, cols), lambda i: (i, 0)), out_shape=SDS((rows, cols), F32),
                          compiler_params=_cp("parallel"))(a)


def _adamw(w, g, m, v, *, name):
    rows, cols = w.shape
    tr = _row_block(rows, cols, 7) if rows % 8 == 0 else rows
    c1 = 1.0 - ADAM_B1 ** ADAM_STEP
    c2 = 1.0 - ADAM_B2 ** ADAM_STEP

    def body(w_ref, g_ref, m_ref, v_ref, d_ref, nm_ref, nv_ref):
        gv = g_ref[...]
        nm = ADAM_B1 * m_ref[...] + (1.0 - ADAM_B1) * gv
        nv = ADAM_B2 * v_ref[...] + (1.0 - ADAM_B2) * (gv * gv)
        d_ref[...] = -ADAM_LR * ((nm / c1) / (jnp.sqrt(nv / c2) + ADAM_EPS) + ADAM_WD * w_ref[...])
        nm_ref[...] = nm
        nv_ref[...] = nv

    blk = BS((tr, cols), lambda i: (i, 0))
    return pl.pallas_call(body, name=name, grid=(rows // tr,), in_specs=[blk] * 4, out_specs=[blk] * 3,
                          out_shape=[SDS((rows, cols), F32)] * 3, compiler_params=_cp("parallel"))(w, g, m, v)


ANY = BS(memory_space=pl.ANY)


def _place():
    x, y, c = lax.axis_index("x"), lax.axis_index("y"), lax.axis_index("c")
    return x, y, c, [(1 - x, y), (x, 1 - y), (1 - x, 1 - y)]


def _gather_shards(arrs, *, name):
    n = len(arrs)

    def body(*refs):
        ins, outs = refs[:n], refs[n:2 * n]
        send_sems, recv_sems, local_sems = refs[2 * n:]
        x, y, c, chips = _place()
        s = 2 * x + y
        copies = []
        for i in range(n):
            own = pltpu.make_async_copy(ins[i], outs[i].at[s], local_sems.at[i])
            own.start()
            copies.append(own)
            for j, (px, py) in enumerate(chips):
                cp = pltpu.make_async_remote_copy(
                    src_ref=ins[i], dst_ref=outs[i].at[s], send_sem=send_sems.at[i * 3 + j], recv_sem=recv_sems.at[i * 3 + j],
                    device_id=(px, py, c), device_id_type=MESH)
                cp.start()
                copies.append(cp)
        for cp in copies:
            cp.wait()

    return pl.pallas_call(
        body, name=name, in_specs=[ANY] * n, out_specs=[ANY] * n,
        out_shape=[SDS((N_CHIPS,) + a.shape, a.dtype) for a in arrs],
        scratch_shapes=[pltpu.SemaphoreType.DMA((3 * n,)), pltpu.SemaphoreType.DMA((3 * n,)), pltpu.SemaphoreType.DMA((n,))],
    )(*arrs)


def _pair_swap_layers(arrs, *, name):
    n = len(arrs)

    def body(*refs):
        ins, outs = refs[:n], refs[n:2 * n]
        send_sems, recv_sems = refs[2 * n:]
        x, y, c, _ = _place()
        copies = []
        for i in range(n):
            cp = pltpu.make_async_remote_copy(
                src_ref=ins[i].at[1 - c], dst_ref=outs[i], send_sem=send_sems.at[i], recv_sem=recv_sems.at[i],
                device_id=(x, y, 1 - c), device_id_type=MESH)
            cp.start()
            copies.append(cp)
        for cp in copies:
            cp.wait()

    return pl.pallas_call(
        body, name=name, in_specs=[ANY] * n, out_specs=[ANY] * n,
        out_shape=[SDS(a.shape[1:], a.dtype) for a in arrs],
        scratch_shapes=[pltpu.SemaphoreType.DMA((n,)), pltpu.SemaphoreType.DMA((n,))],
    )(*arrs)


def _scatter_to_chips(arrs, *, name):
    n = len(arrs)

    def body(*refs):
        ins, outs = refs[:n], refs[n:2 * n]
        send_sems, recv_sems, local_sems = refs[2 * n:]
        x, y, c, chips = _place()
        s = 2 * x + y
        copies = []
        for i in range(n):
            own = pltpu.make_async_copy(ins[i].at[s], outs[i].at[s], local_sems.at[i])
            own.start()
            copies.append(own)
            for j, (px, py) in enumerate(chips):
                cp = pltpu.make_async_remote_copy(
                    src_ref=ins[i].at[2 * px + py], dst_ref=outs[i].at[s], send_sem=send_sems.at[i * 3 + j],
                    recv_sem=recv_sems.at[i * 3 + j], device_id=(px, py, c), device_id_type=MESH)
                cp.start()
                copies.append(cp)
        for cp in copies:
            cp.wait()

    return pl.pallas_call(
        body, name=name, in_specs=[ANY] * n, out_specs=[ANY] * n, out_shape=[SDS(a.shape, a.dtype) for a in arrs],
        scratch_shapes=[pltpu.SemaphoreType.DMA((3 * n,)), pltpu.SemaphoreType.DMA((3 * n,)), pltpu.SemaphoreType.DMA((n,))],
    )(*arrs)


def _pair_join(arrs, *, name):
    n = len(arrs)

    def body(*refs):
        ins, outs = refs[:n], refs[n:2 * n]
        send_sems, recv_sems, local_sems = refs[2 * n:]
        x, y, c, _ = _place()
        copies = []
        for i in range(n):
            own = pltpu.make_async_copy(ins[i], outs[i].at[c], local_sems.at[i])
            own.start()
            copies.append(own)
            cp = pltpu.make_async_remote_copy(
                src_ref=ins[i], dst_ref=outs[i].at[c], send_sem=send_sems.at[i], recv_sem=recv_sems.at[i],
                device_id=(x, y, 1 - c), device_id_type=MESH)
            cp.start()
            copies.append(cp)
        for cp in copies:
            cp.wait()

    return pl.pallas_call(
        body, name=name, in_specs=[ANY] * n, out_specs=[ANY] * n, out_shape=[SDS((2,) + a.shape, a.dtype) for a in arrs],
        scratch_shapes=[pltpu.SemaphoreType.DMA((n,)), pltpu.SemaphoreType.DMA((n,)), pltpu.SemaphoreType.DMA((n,))],
    )(*arrs)


def _allreduce_small(v, *, name):
    rows, cols = v.shape

    def body(v_ref, o_ref, gath, send_sems, recv_sems):
        x, y, c, _ = _place()
        me = 4 * x + 2 * y + c
        gath[me] = v_ref[...]
        copies = []
        for k in range(1, N_DEV):
            fx, fy, fc = (k >> 2) & 1, (k >> 1) & 1, k & 1
            peer = (1 - x if fx else x, 1 - y if fy else y, 1 - c if fc else c)
            cp = pltpu.make_async_remote_copy(
                src_ref=v_ref, dst_ref=gath.at[me], send_sem=send_sems.at[k - 1], recv_sem=recv_sems.at[k - 1],
                device_id=peer, device_id_type=MESH)
            cp.start()
            copies.append(cp)
        for cp in copies:
            cp.wait()
        acc = gath[0]
        for k in range(1, N_DEV):
            acc = acc + gath[k]
        o_ref[...] = acc

    vm = BS(memory_space=pltpu.VMEM)
    return pl.pallas_call(
        body, name=name, in_specs=[vm], out_specs=vm, out_shape=SDS((rows, cols), F32),
        scratch_shapes=[pltpu.VMEM((N_DEV, rows, cols), F32), pltpu.SemaphoreType.DMA((N_DEV - 1,)), pltpu.SemaphoreType.DMA((N_DEV - 1,))],
    )(v)


def _t5_bucket(dist):
    max_exact = N_REL_BUCKETS // 2
    d_f = jnp.maximum(dist, 1).astype(F32)
    large = max_exact + (jnp.log(d_f / max_exact) / math.log(REL_MAX_DISTANCE / max_exact) * (N_REL_BUCKETS - max_exact)).astype(jnp.int32)
    return jnp.where(dist < max_exact, dist, jnp.minimum(large, N_REL_BUCKETS - 1))


def _rel_buckets(dilation):
    qi = jnp.arange(ATTN_BLOCK)[:, None]
    kj = jnp.arange(2 * ATTN_BLOCK)[None, :]
    return _t5_bucket(jnp.clip(qi + ATTN_BLOCK - kj, 0, N_STEPS) * dilation)


def _layer_fwd(h, p, biases, lname):
    sv = {"h": h}
    xn1 = _rms_fwd(h, p["norm1_w"], name=lname + "norm1")
    proj = _matmul(xn1, p["w_main"], out_dtype=BF16, name=lname + "in_proj")
    dtr = _matmul(xn1, p["w_dt"], out_dtype=F32, name=lname + "in_proj_dt")
    xn1_rm, qkvs = [xn1], [proj]
    for g in range(1, N_GROUPS_ATTN):
        xn1_rm.append(_to_residue_major(xn1, DILATIONS[g]))
        qkvs.append(_matmul(xn1_rm[g], p["w_qkv"][g], out_dtype=BF16, name=f"{lname}in_proj_qkv{g}"))
    os_, lses, lses_rm = [], [], []
    for g, d in enumerate(DILATIONS):
        o, lse = _attn_fwd(qkvs[g], (0, 1, 2), biases[g], d, name=f"{lname}attn{g}")
        os_.append(_to_token_major(o, d))
        lses.append(_to_token_major(lse, d))
        lses_rm.append(lse)
    sv.update(xn1_rm=xn1_rm, qkvs=qkvs, lses_rm=lses_rm)
    attn = _combine_fwd(os_, lses, name=lname + "attn_combine")
    xc = _conv_fwd(proj, p["off_xbc"], p["conv_w"], p["conv_b"], name=lname + "conv")
    y, st = _ssd_fwd(xc, dtr, p["dt_bias"], p["a"], p["dskip"], name=lname + "ssd")
    ssm = _gate_norm_fwd(y, proj, p["off_z"], p["ssm_norm_w"], name=lname + "gate_norm")
    ga = _matmul(attn, p["w_attn_branch"], name=lname + "attn_branch")
    gs = _matmul(ssm, p["w_ssm_branch"], name=lname + "ssm_branch")
    merged = _merge_fwd(proj, p["off_gate"], ga, gs, name=lname + "merge")
    h1 = _matmul(merged, p["w_out"], res=h, name=lname + "out_proj")
    xn2 = _rms_fwd(h1, p["norm2_w"], name=lname + "norm2")
    u = _matmul(xn2, p["w_ffn_in"], out_dtype=BF16, name=lname + "ffn_in")
    f = _swiglu_fwd(u, name=lname + "swiglu")
    h2 = _matmul(f, p["w_ffn_out"], res=h1, name=lname + "ffn_out")
    sv.update(xn1=xn1, proj=proj, dtr=dtr, os=os_, lses=lses, attn=attn, xc=xc, y=y, st=st, ssm=ssm, ga=ga, gs=gs,
              merged=merged, h1=h1, xn2=xn2, u=u, f=f)
    return h2, sv


def _layer_bwd(dh2, p, sv, biases, lname):
    gr = {}
    lname = lname + "bwd_"
    df = _matmul(dh2, p["w_ffn_out"], tb=True, out_dtype=BF16, name=lname + "ffn_out_dx")
    gr["w_ffn_out"] = _matmul(sv["f"], dh2, ta=True, name=lname + "ffn_out_dw")
    dgate, dup = _swiglu_bwd(sv["u"], df, name=lname + "swiglu")
    du = jnp.concatenate([dgate, dup], axis=1)
    dxn2 = _matmul(du, p["w_ffn_in"], tb=True, name=lname + "ffn_in_dx")
    gr["w_ffn_in"] = _matmul(sv["xn2"], du, ta=True, name=lname + "ffn_in_dw")
    dh1, gr["norm2_w"] = _rms_bwd(sv["h1"], p["norm2_w"], [dxn2], dh2, name=lname + "norm2")
    dmerged = _matmul(dh1, p["w_out"], tb=True, name=lname + "out_proj_dx")
    gr["w_out"] = _matmul(sv["merged"], dh1, ta=True, name=lname + "out_proj_dw")
    dga, dgs, dg0, dg1 = _merge_bwd(sv["proj"], p["off_gate"], sv["ga"], sv["gs"], dmerged, name=lname + "merge")
    dattn = _matmul(dga, p["w_attn_branch"], tb=True, name=lname + "attn_branch_dx")
    gr["w_attn_branch"] = _matmul(sv["attn"], dga, ta=True, name=lname + "attn_branch_dw")
    dssm = _matmul(dgs, p["w_ssm_branch"], tb=True, name=lname + "ssm_branch_dx")
    gr["w_ssm_branch"] = _matmul(sv["ssm"], dgs, ta=True, name=lname + "ssm_branch_dw")
    dy, dz, gr["ssm_norm_w"] = _gate_norm_bwd(dssm, sv["y"], sv["proj"], p["off_z"], p["ssm_norm_w"], name=lname + "gate_norm")
    dxs, dbm, dcm, ddtr4, da4, dbias4, ddsk = _ssd_bwd(sv["xc"], sv["dtr"], p["dt_bias"], p["a"], p["dskip"], sv["st"], dy,
                                                       name=lname + "ssd")
    nsh = p["n_ssm_heads"]
    ddtr = jnp.sum(ddtr4, axis=0)
    gr["a_log"] = jnp.sum(da4, axis=(0, 1))[:nsh] * p["a"][0, :nsh]
    gr["dt_bias"] = jnp.sum(dbias4, axis=(0, 1))[:nsh]
    gr["d_skip"] = jnp.sum(ddsk.reshape(nsh, HEAD_DIM), axis=1)
    di = dxs.shape[1]
    dxbc, dcw, dcb = [], [], []
    for part, (lo, hi) in zip((dxs, dbm, dcm), ((0, di), (di, di + dbm.shape[1]), (di + dbm.shape[1], di + 2 * dbm.shape[1]))):
        dx_, dw_, db_ = _conv_bwd(sv["proj"], p["off_xbc"] + lo, p["conv_w"][:, lo:hi], p["conv_b"][lo:hi], part,
                                  name=f"{lname}conv{lo}")
        dxbc.append(dx_)
        dcw.append(dw_)
        dcb.append(db_)
    gr["conv_w"] = jnp.concatenate(dcw, axis=1)
    gr["conv_b"] = jnp.concatenate(dcb, axis=0)
    dos, corrs = _combine_bwd(dattn, sv["os"], sv["lses"], name=lname + "attn_combine")
    dqkvs, dbiases = [], []
    for g, d in enumerate(DILATIONS):
        dq, dk, dv, dbias = _attn_bwd(sv["qkvs"][g], (0, 1, 2), biases[g], sv["lses_rm"][g], _to_residue_major(dos[g], d),
                                      _to_residue_major(corrs[g], d), d, name=f"{lname}attn{g}")
        dqkvs.append([dq, dk, dv])
        dbiases.append(dbias)
    dmain = jnp.concatenate(dqkvs[0] + [dz] + dxbc + [dg0, dg1], axis=1)
    dxn1 = [_matmul(dmain, p["w_main"], tb=True, name=lname + "in_proj_dx"),
            _matmul(ddtr, p["w_dt"], tb=True, name=lname + "in_proj_dt_dx")]
    dw_main = _matmul(sv["xn1"], dmain, ta=True, name=lname + "in_proj_dw")
    dw_dt = _matmul(sv["xn1"], ddtr, ta=True, name=lname + "in_proj_dt_dw")
    dw_qkv = [dw_main]
    for g in range(1, N_GROUPS_ATTN):
        dqkv = jnp.concatenate(dqkvs[g], axis=1)
        dxn1.append(_to_token_major(_matmul(dqkv, p["w_qkv"][g], tb=True, name=f"{lname}in_proj_qkv{g}_dx"), DILATIONS[g]))
        dw_qkv.append(_matmul(sv["xn1_rm"][g], dqkv, ta=True, name=f"{lname}in_proj_qkv{g}_dw"))
    awg, og = dqkvs[0][0].shape[1], p["off_gate"]
    cols = [dw[:, i * awg:(i + 1) * awg] for i in range(3) for dw in dw_qkv]
    gr["w_in"] = jnp.concatenate(cols + [dw_main[:, 3 * awg:og], dw_dt[:, :nsh], dw_main[:, og:]], axis=1)
    dh, gr["norm1_w"] = _rms_bwd(sv["h"], p["norm1_w"], dxn1, dh1, name=lname + "norm1")
    return dh, gr, dbiases


def _layer_params(l, w, n_ssm_heads, hg):
    awg = hg * HEAD_DIM
    aw = N_GROUPS_ATTN * awg
    di = n_ssm_heads * HEAD_DIM
    xbc = di + 2 * SSM_GROUPS * D_STATE
    in_dt = 3 * aw + di + xbc
    w_in = w["w_in"][l]
    qkv_cols = lambda g: [w_in[:, (i * N_GROUPS_ATTN + g) * awg:(i * N_GROUPS_ATTN + g + 1) * awg] for i in range(3)]
    pad = lambda v: jnp.pad(v.astype(F32), (0, LANES - n_ssm_heads)).reshape(1, LANES)
    return dict(
        n_ssm_heads=n_ssm_heads, off_z=3 * awg, off_xbc=3 * awg + di, off_gate=3 * awg + di + xbc,
        w_main=jnp.concatenate(qkv_cols(0) + [w_in[:, 3 * aw:in_dt], w_in[:, in_dt + n_ssm_heads:]], axis=1),
        w_qkv=[None] + [jnp.concatenate(qkv_cols(g), axis=1) for g in range(1, N_GROUPS_ATTN)],
        w_dt=jnp.pad(w_in[:, in_dt:in_dt + n_ssm_heads], ((0, 0), (0, LANES - n_ssm_heads))),
        norm1_w=w["norm1_w"][l], norm2_w=w["norm2_w"][l], conv_w=w["conv_w"][l], conv_b=w["conv_b"][l],
        dt_bias=pad(w["dt_bias"][l]), a=pad(-jnp.exp(w["a_log"][l])),
        dskip=jnp.repeat(w["d_skip"][l], HEAD_DIM).reshape(1, di), ssm_norm_w=w["ssm_norm_w"][l],
        w_attn_branch=w["w_attn_branch"][l], w_ssm_branch=w["w_ssm_branch"][l], w_out=w["w_out"][l],
        w_ffn_in=w["w_ffn_in"][l], w_ffn_out=w["w_ffn_out"][l],
    )


def _local_step(x, tgt, w):
    depth = w["norm1_w"].shape[0]
    n_ssm_heads = w["dt_bias"].shape[1]
    hg = w["rel_bias"].shape[1] // N_GROUPS_ATTN
    onehots = [(_rel_buckets(dil)[:, :, None] == jnp.arange(N_REL_BUCKETS)[None, None, :]).astype(F32) for dil in DILATIONS]
    biases = [jnp.einsum("qkb,bh->hqk", oh, w["rel_bias"][:, g * hg:(g + 1) * hg].astype(F32), precision=HIGHEST)
              for g, oh in enumerate(onehots)]
    params = [_layer_params(l, w, n_ssm_heads, hg) for l in range(depth)]
    h = x
    saved = []
    for l in range(depth):
        h, sv = _layer_fwd(h, params[l], biases, f"l{l}_")
        saved.append(sv)
    loss, dh, g_final = _loss_head(h, w["final_norm_w"], tgt, name="loss_head")
    grads = [None] * depth
    dbias_tot = [jnp.zeros(b.shape, F32) for b in biases]
    for l in reversed(range(depth)):
        dh, grads[l], dbiases = _layer_bwd(dh, params[l], saved[l], biases, f"l{l}_")
        dbias_tot = [a + b for a, b in zip(dbias_tot, dbiases)]
    out = {k: jnp.stack([gl[k] for gl in grads]) for k in grads[0]}
    out["final_norm_w"] = g_final
    drel = []
    for g, (oh, db) in enumerate(zip(onehots, dbias_tot)):
        oh_t = jnp.pad(oh.reshape(-1, N_REL_BUCKETS).T, ((0, LANES - N_REL_BUCKETS), (0, 0)))
        db_rows = jnp.pad(db.reshape(hg, -1), ((0, LANES - hg), (0, 0)))
        drel.append(_matmul(oh_t, db_rows, tb=True, name=f"rel_bias_fold{g}")[:N_REL_BUCKETS, :hg])
    out["rel_bias"] = jnp.concatenate(drel, axis=1)
    return loss, dh, out


MATRICES = ("w_in", "w_attn_branch", "w_ssm_branch", "w_out", "w_ffn_in", "w_ffn_out")
COL_SHARDED = ("w_in", "w_attn_branch", "w_ffn_in")
SMALL = ("norm1_w", "conv_b", "dt_bias", "a_log", "d_skip", "ssm_norm_w", "norm2_w", "rel_bias", "final_norm_w")
WEIGHTS = ("norm1_w", "w_in", "conv_w", "conv_b", "dt_bias", "a_log", "d_skip", "ssm_norm_w", "w_attn_branch",
           "w_ssm_branch", "w_out", "norm2_w", "w_ffn_in", "w_ffn_out", "rel_bias", "final_norm_w")
SMALL_COLS = 1024


def _unshard(name, g):
    _, depth, r, c = g.shape
    if name in COL_SHARDED or name == "conv_w":
        return jnp.transpose(g, (1, 2, 0, 3)).reshape(depth, r, N_CHIPS * c)
    return jnp.transpose(g, (1, 0, 2, 3)).reshape(depth, N_CHIPS * r, c)


def _to_shards(name, g):
    r, c = g.shape
    if name in COL_SHARDED:
        return jnp.transpose(g.reshape(r, N_CHIPS, c // N_CHIPS), (1, 0, 2))
    return g.reshape(N_CHIPS, r // N_CHIPS, c)


def kernel(x, norm1_w, w_in, conv_w, conv_b, dt_bias, a_log, d_skip, ssm_norm_w, w_attn_branch, w_ssm_branch, w_out, norm2_w, w_ffn_in, w_ffn_out, rel_bias, final_norm_w, loss_target, m_norm1_w, m_w_in, m_conv_w, m_conv_b, m_dt_bias, m_a_log, m_d_skip, m_ssm_norm_w, m_w_attn_branch, m_w_ssm_branch, m_w_out, m_norm2_w, m_w_ffn_in, m_w_ffn_out, m_rel_bias, m_final_norm_w, v_norm1_w, v_w_in, v_conv_w, v_conv_b, v_dt_bias, v_a_log, v_d_skip, v_ssm_norm_w, v_w_attn_branch, v_w_ssm_branch, v_w_out, v_norm2_w, v_w_ffn_in, v_w_ffn_out, v_rel_bias, v_final_norm_w):
    env = dict(locals())
    wts = {k: env[k] for k in WEIGHTS}
    mom = {k: env["m_" + k] for k in WEIGHTS}
    var = {k: env["v_" + k] for k in WEIGHTS}
    chip = 2 * lax.axis_index("x") + lax.axis_index("y")
    core = lax.axis_index("c")

    gathered = _gather_shards([wts[k].astype(BF16) for k in MATRICES] + [conv_w], name="gather_weights")
    full = {k: wts[k] for k in SMALL}
    for k, g in zip(MATRICES + ("conv_w",), gathered):
        full[k] = _unshard(k, g)

    loss, dx, grads = _local_step(x[0], loss_target[0], full)
    loss = lax.psum(loss, ("x", "y", "c"))

    from_pair = _pair_swap_layers([grads[k] for k in MATRICES], name="reduce_pair_swap")
    scattered_in = []
    for k, got in zip(MATRICES, from_pair):
        mine = lax.dynamic_index_in_dim(grads[k], core, axis=0, keepdims=False)
        scattered_in.append(_to_shards(k, _add2(mine, got, name="reduce_pair_add_" + k)))
    scattered = _scatter_to_chips(scattered_in, name="reduce_scatter")
    halves = [_sum_chips(a, name="reduce_sum_" + k) for k, a in zip(MATRICES, scattered)]
    reduced = dict(zip(MATRICES, _pair_join(halves, name="reduce_pair_join")))

    small_names = SMALL + ("conv_w",)
    flat = jnp.concatenate([grads[k].reshape(-1) for k in small_names])
    n_small = flat.shape[0]
    rows = -(-n_small // SMALL_COLS)
    rows = -(-rows // 8) * 8
    flat = jnp.pad(flat, (0, rows * SMALL_COLS - n_small)).reshape(rows, SMALL_COLS)
    flat = _allreduce_small(flat, name="allreduce_small").reshape(-1)
    pos = 0
    for k in small_names:
        size = math.prod(grads[k].shape)
        reduced[k] = flat[pos:pos + size].reshape(grads[k].shape)
        pos += size
    cs = conv_w.shape[2]
    reduced["conv_w"] = lax.dynamic_slice_in_dim(reduced["conv_w"], chip * cs, cs, axis=2)

    delta, new_m, new_v = {}, {}, {}
    for k in MATRICES:
        shape = wts[k].shape
        two_d = lambda a: a.reshape(shape[0] * shape[1], shape[2])
        d_, m_, v_ = _adamw(two_d(wts[k]), two_d(reduced[k]), two_d(mom[k]), two_d(var[k]), name="adamw_" + k)
        delta[k], new_m[k], new_v[k] = d_.reshape(shape), m_.reshape(shape), v_.reshape(shape)
    pack = lambda src: jnp.pad(jnp.concatenate([src[k].reshape(-1) for k in small_names]),
                               (0, rows * SMALL_COLS - n_shard)).reshape(rows, SMALL_COLS)
    n_shard = sum(math.prod(wts[k].shape) for k in small_names)
    d_, m_, v_ = _adamw(pack(wts), pack(reduced), pack(mom), pack(var), name="adamw_small")
    pos = 0
    for k in small_names:
        size = math.prod(wts[k].shape)
        for dst, src in ((delta, d_), (new_m, m_), (new_v, v_)):
            dst[k] = src.reshape(-1)[pos:pos + size].reshape(wts[k].shape)
        pos += size

    return (loss, dx[None], *[reduced[k] for k in WEIGHTS], *[delta[k] for k in WEIGHTS],
            *[new_m[k] for k in WEIGHTS], *[new_v[k] for k in WEIGHTS])
```

```python
import functools
import math

import jax
import jax.numpy as jnp
from jax import lax
from jax.experimental import pallas as pl
from jax.experimental.pallas import tpu as pltpu

F32, BF16 = jnp.float32, jnp.bfloat16
SDS = jax.ShapeDtypeStruct
BS = pl.BlockSpec
MESH = pl.DeviceIdType.MESH
HIGHEST = lax.Precision.HIGHEST

EPS = 1e-6
HEAD_DIM = 64
ATTN_BLOCK = 128
DILATIONS = (1, 4, 16)
N_GROUPS_ATTN = len(DILATIONS)
N_STEPS = 128
N_REL_BUCKETS = 32
REL_MAX_DISTANCE = 2048
SSM_GROUPS = 4
D_STATE = 128
CHUNK = 128
CONV_WIDTH = 4
HALO = 16
LANES = 128
N_CHIPS = 4
N_DEV = 8
VMEM_LIMIT_BYTES = 48 * 1024 * 1024

ADAM_LR, ADAM_B1, ADAM_B2, ADAM_EPS, ADAM_WD, ADAM_STEP = 0.001, 0.9, 0.999, 1e-08, 0.01, 10

NT = (((1,), (1,)), ((), ()))
TN = (((0,), (0,)), ((), ()))
NN = (((1,), (0,)), ((), ()))


def _cp(*sem):
    return pltpu.CompilerParams(dimension_semantics=sem if sem else None, vmem_limit_bytes=VMEM_LIMIT_BYTES)


def _pick(n, cands):
    for c in cands:
        if n % c == 0:
            return c
    raise ValueError(f"no block size of {cands} divides {n}")


def _divisors(n, cap):
    out = [c for c in range(LANES, min(n, cap) + 1, LANES) if n % c == 0]
    return out or [n]


def _wide(n, cap=2048):
    return _divisors(n, cap)[-1]


MXU_FLOPS = 9.0e14
HBM_BYTES_PER_S = 3.0e12
ACC_BYTES_PER_S = 4.0e12
GRID_STEP_S = 0.4e-6
TILE_VMEM_BYTES = 36 * 1024 * 1024


def _matmul_tiles(m, n, k, a_bytes, b_bytes, o_bytes, has_res):
    best = None
    for tm in _divisors(m, 2048):
        for tn in _divisors(n, 2048):
            for tk in _divisors(k, 4096):
                ni, nj, nk = m // tm, n // tn, k // tk
                vmem = 2 * (tm * tk * a_bytes + tk * tn * b_bytes + tm * tn * (o_bytes + (4 if has_res else 0)))
                vmem += tm * tn * 4 * (2 if nk > 1 else 1) + (tm * tk + tk * tn) * 2
                if vmem > TILE_VMEM_BYTES:
                    continue
                hbm = m * k * a_bytes * (nj if nk > 1 else 1) + k * n * b_bytes * (ni if nj * nk > 1 else 1)
                hbm += m * n * (o_bytes + (4 if has_res else 0))
                t = max(2.0 * m * n * k / MXU_FLOPS, hbm / HBM_BYTES_PER_S) + ni * nj * nk * GRID_STEP_S
                if nk > 1:
                    t += m * n * 8.0 * nk / ACC_BYTES_PER_S
                if best is None or t < best[0]:
                    best = (t, tm, tn, tk)
    assert best is not None, (m, n, k)
    return best[1:]


def _dot(a, b, dims=NN, precision=None):
    return lax.dot_general(a, b, dims, precision=precision, preferred_element_type=F32)


def _silu(x):
    return x / (1.0 + jnp.exp(-x))


def _sigmoid(x):
    return 1.0 / (1.0 + jnp.exp(-x))


def _dsilu(x):
    s = _sigmoid(x)
    return s * (1.0 + x * (1.0 - s))


def _matmul(a, b, *, name, ta=False, tb=False, out_dtype=F32, res=None):
    (kdim, m) = a.shape if ta else a.shape[::-1]
    (n, k2) = b.shape if tb else b.shape[::-1]
    assert kdim == k2, (a.shape, b.shape, ta, tb)
    tm, tn, tk = _matmul_tiles(m, n, kdim, a.dtype.itemsize, b.dtype.itemsize, jnp.dtype(out_dtype).itemsize, res is not None)
    nk = kdim // tk
    a_spec = BS((tk, tm), lambda i, j, k: (k, i)) if ta else BS((tm, tk), lambda i, j, k: (i, k))
    b_spec = BS((tn, tk), lambda i, j, k: (j, k)) if tb else BS((tk, tn), lambda i, j, k: (k, j))
    dims = (((0 if ta else 1,), (1 if tb else 0,)), ((), ()))
    has_res = res is not None

    def body(*refs):
        a_ref, b_ref = refs[:2]
        r_ref = refs[2] if has_res else None
        o_ref = refs[3] if has_res else refs[2]
        prod = _dot(a_ref[...].astype(BF16), b_ref[...].astype(BF16), dims)
        if nk == 1:
            o_ref[...] = (prod + r_ref[...] if has_res else prod).astype(o_ref.dtype)
            return
        acc = refs[-1]
        k = pl.program_id(2)

        @pl.when(k == 0)
        def _():
            acc[...] = prod

        @pl.when(k > 0)
        def _():
            acc[...] += prod

        @pl.when(k == nk - 1)
        def _():
            r = acc[...]
            if has_res:
                r = r + r_ref[...]
            o_ref[...] = r.astype(o_ref.dtype)

    in_specs = [a_spec, b_spec]
    args = [a, b]
    if has_res:
        in_specs.append(BS((tm, tn), lambda i, j, k: (i, j)))
        args.append(res)
    return pl.pallas_call(
        body, name=name, grid=(m // tm, n // tn, nk), in_specs=in_specs,
        out_specs=BS((tm, tn), lambda i, j, k: (i, j)), out_shape=SDS((m, n), out_dtype),
        scratch_shapes=[pltpu.VMEM((tm, tn), F32)] if nk > 1 else [],
        compiler_params=_cp("parallel", "parallel", "arbitrary"),
    )(*args)


def _rms_fwd(h, w, *, name):
    t, d = h.shape
    tm = _pick(t, (512, 256, 128))

    def body(h_ref, w_ref, o_ref):
        x = h_ref[...]
        r = lax.rsqrt(jnp.mean(x * x, axis=-1, keepdims=True) + EPS)
        o_ref[...] = (x * r * w_ref[...]).astype(BF16)

    return pl.pallas_call(
        body, name=name, grid=(t // tm,), in_specs=[BS((tm, d), lambda i: (i, 0)), BS((1, d), lambda i: (0, 0))],
        out_specs=BS((tm, d), lambda i: (i, 0)), out_shape=SDS((t, d), BF16), compiler_params=_cp("parallel"),
    )(h, w.reshape(1, d))


def _rms_bwd(h, w, dys, dres, *, name):
    t, d = h.shape
    tm = _pick(t, (512, 256, 128))
    n_dy = len(dys)

    def body(*refs):
        h_ref, w_ref = refs[:2]
        dy_refs = refs[2:2 + n_dy]
        dres_ref, dh_ref, dw_ref = refs[2 + n_dy:]
        x = h_ref[...]
        dy = dy_refs[0][...]
        for r_ in dy_refs[1:]:
            dy = dy + r_[...]
        r = lax.rsqrt(jnp.mean(x * x, axis=-1, keepdims=True) + EPS)
        g = dy * w_ref[...]
        proj = jnp.sum(g * x, axis=-1, keepdims=True) * (1.0 / d)
        dh_ref[...] = dres_ref[...] + r * g - x * (r * r * r) * proj

        @pl.when(pl.program_id(0) == 0)
        def _():
            dw_ref[...] = jnp.zeros_like(dw_ref)

        dw_ref[...] += jnp.sum(dy * x * r, axis=0, keepdims=True)

    row = BS((tm, d), lambda i: (i, 0))
    vec = BS((1, d), lambda i: (0, 0))
    dh, dw = pl.pallas_call(
        body, name=name, grid=(t // tm,), in_specs=[row, vec] + [row] * n_dy + [row],
        out_specs=[row, vec], out_shape=[SDS((t, d), F32), SDS((1, d), F32)], compiler_params=_cp("arbitrary"),
    )(h, w.reshape(1, d), *dys, dres)
    return dh, dw[0]


def _loss_head(h, w, tgt, *, name):
    t, d = h.shape
    tm = _pick(t, (512, 256, 128))

    def body(h_ref, w_ref, t_ref, loss_ref, dh_ref, dw_ref):
        x = h_ref[...]
        r = lax.rsqrt(jnp.mean(x * x, axis=-1, keepdims=True) + EPS)
        err = x * r * w_ref[...] - t_ref[...]
        loss_ref[...] = jnp.zeros(loss_ref.shape, F32) + 0.5 * jnp.sum(err * err) * (1.0 / d)
        dy = err * (1.0 / d)
        g = dy * w_ref[...]
        proj = jnp.sum(g * x, axis=-1, keepdims=True) * (1.0 / d)
        dh_ref[...] = r * g - x * (r * r * r) * proj

        @pl.when(pl.program_id(0) == 0)
        def _():
            dw_ref[...] = jnp.zeros_like(dw_ref)

        dw_ref[...] += jnp.sum(dy * x * r, axis=0, keepdims=True)

    row = BS((tm, d), lambda i: (i, 0))
    vec = BS((1, d), lambda i: (0, 0))
    loss, dh, dw = pl.pallas_call(
        body, name=name, grid=(t // tm,), in_specs=[row, vec, row],
        out_specs=[BS((1, 8, LANES), lambda i: (i, 0, 0)), row, vec],
        out_shape=[SDS((t // tm, 8, LANES), F32), SDS((t, d), F32), SDS((1, d), F32)], compiler_params=_cp("arbitrary"),
    )(h, w.reshape(1, d), tgt)
    return jnp.sum(loss[:, 0, 0]), dh, dw[0]


def _attn_masks(mb):
    qi = lax.broadcasted_iota(jnp.int32, (ATTN_BLOCK, 2 * ATTN_BLOCK), 0)
    kj = lax.broadcasted_iota(jnp.int32, (ATTN_BLOCK, 2 * ATTN_BLOCK), 1)
    steps = qi + ATTN_BLOCK - kj
    valid = (steps >= 0) & (steps <= N_STEPS) & ((kj >= ATTN_BLOCK) | (mb > 0))
    low = lax.broadcasted_iota(jnp.int32, (ATTN_BLOCK, LANES), 1) < HEAD_DIM
    return valid, low


def _to_residue_major(a, d):
    t, c = a.shape
    return a if d == 1 else a.reshape(t // d, d, c).transpose(1, 0, 2).reshape(t, c)


def _to_token_major(a, d):
    t, c = a.shape
    return a if d == 1 else a.reshape(d, t // d, c).transpose(1, 0, 2).reshape(t, c)


def _attn_specs(cols, nb, awg, clamp):
    def spec(col, prev):
        def index(r, mb):
            blk = clamp(mb)
            if prev:
                blk = jnp.maximum(blk - 1, 0)
            return (r * nb + blk, col)
        return BS((ATTN_BLOCK, awg), index)
    return [spec(cols[0], False), spec(cols[1], False), spec(cols[1], True), spec(cols[2], False), spec(cols[2], True)]


def _attn_fwd(qkv, cols, bias, d, *, name):
    t = qkv.shape[0]
    hg = bias.shape[0]
    awg = hg * HEAD_DIM
    nb = t // d // ATTN_BLOCK
    scale = HEAD_DIM ** -0.5

    def body(q_ref, kc_ref, kp_ref, vc_ref, vp_ref, b_ref, o_ref, l_ref):
        valid, low = _attn_masks(pl.program_id(1))
        for pi in range(awg // LANES):
            sl = slice(pi * LANES, (pi + 1) * LANES)
            q2 = q_ref[:, sl]
            k2 = jnp.concatenate([kp_ref[:, sl], kc_ref[:, sl]], axis=0)
            v2 = jnp.concatenate([vp_ref[:, sl], vc_ref[:, sl]], axis=0)
            outs, lses = [], []
            for hh in range(2):
                mh = low if hh == 0 else jnp.logical_not(low)
                qm = jnp.where(mh, q2, jnp.zeros_like(q2))
                s = _dot(qm, k2, NT) * scale + b_ref[pi * 2 + hh]
                s = jnp.where(valid, s, -jnp.inf)
                m = jnp.max(s, axis=-1, keepdims=True)
                p = jnp.exp(s - m)
                den = jnp.sum(p, axis=-1, keepdims=True)
                outs.append(_dot(p.astype(BF16), v2) / den)
                lses.append(jnp.broadcast_to(m + jnp.log(den), (ATTN_BLOCK, LANES)))
            o_ref[:, sl] = jnp.where(low, outs[0], outs[1])
            l_ref[:, sl] = jnp.where(low, lses[0], lses[1])

    out_spec = BS((ATTN_BLOCK, awg), lambda r, mb: (r * nb + mb, 0))
    return pl.pallas_call(
        body, name=name, grid=(d, nb),
        in_specs=_attn_specs(cols, nb, awg, lambda mb: mb) + [BS(bias.shape, lambda r, mb: (0, 0, 0))],
        out_specs=[out_spec, out_spec], out_shape=[SDS((t, awg), F32)] * 2, compiler_params=_cp("parallel", "parallel"),
    )(*([qkv] * 5), bias)


def _attn_bwd(qkv, cols, bias, lse, do, corr, d, *, name):
    t = qkv.shape[0]
    hg = bias.shape[0]
    awg = hg * HEAD_DIM
    nb = t // d // ATTN_BLOCK
    scale = HEAD_DIM ** -0.5

    def body(q_ref, kc_ref, kp_ref, vc_ref, vp_ref, b_ref, l_ref, do_ref, c_ref, dq_ref, dk_ref, dv_ref, db_ref, ck, cv):
        r, mb = pl.program_id(0), pl.program_id(1)

        @pl.when((r == 0) & (mb == 0))
        def _():
            db_ref[...] = jnp.zeros_like(db_ref)

        @pl.when(mb == 0)
        def _():
            ck[...] = jnp.zeros_like(ck)
            cv[...] = jnp.zeros_like(cv)

        @pl.when(mb < nb)
        def _():
            valid, low = _attn_masks(mb)
            for pi in range(awg // LANES):
                sl = slice(pi * LANES, (pi + 1) * LANES)
                q2 = q_ref[:, sl]
                k2 = jnp.concatenate([kp_ref[:, sl], kc_ref[:, sl]], axis=0)
                v2 = jnp.concatenate([vp_ref[:, sl], vc_ref[:, sl]], axis=0)
                do2 = do_ref[:, sl]
                lse2 = l_ref[:, sl]
                corr2 = c_ref[:, sl]
                dk2 = jnp.zeros((2 * ATTN_BLOCK, LANES), F32)
                dv2 = jnp.zeros((2 * ATTN_BLOCK, LANES), F32)
                dqs = []
                for hh in range(2):
                    mh = low if hh == 0 else jnp.logical_not(low)
                    qm = jnp.where(mh, q2, jnp.zeros_like(q2))
                    dom = jnp.where(mh, do2, jnp.zeros_like(do2))
                    lse_c = jnp.max(jnp.where(mh, lse2, -jnp.inf), axis=-1, keepdims=True)
                    corr_c = jnp.max(jnp.where(mh, corr2, -jnp.inf), axis=-1, keepdims=True)
                    s = _dot(qm, k2, NT) * scale + b_ref[pi * 2 + hh]
                    p = jnp.exp(jnp.where(valid, s, -jnp.inf) - lse_c)
                    ds = p * (_dot(dom, v2, NT) + corr_c)
                    db_ref[pi * 2 + hh] += ds
                    dsb = ds.astype(BF16)
                    dqs.append(_dot(dsb, k2) * scale)
                    dk2 = dk2 + _dot(dsb, qm, TN) * scale
                    dv2 = dv2 + _dot(p.astype(BF16), dom, TN)
                dq_ref[:, sl] = jnp.where(low, dqs[0], dqs[1]).astype(BF16)
                dk_ref[:, sl] = (ck[:, sl] + dk2[:ATTN_BLOCK]).astype(BF16)
                dv_ref[:, sl] = (cv[:, sl] + dv2[:ATTN_BLOCK]).astype(BF16)
                ck[:, sl] = dk2[ATTN_BLOCK:]
                cv[:, sl] = dv2[ATTN_BLOCK:]

        @pl.when(mb == nb)
        def _():
            dk_ref[...] = ck[...].astype(BF16)
            dv_ref[...] = cv[...].astype(BF16)

    clamp = lambda mb: jnp.minimum(mb, nb - 1)
    cur = BS((ATTN_BLOCK, awg), lambda r, mb: (r * nb + clamp(mb), 0))
    prev = BS((ATTN_BLOCK, awg), lambda r, mb: (r * nb + jnp.maximum(mb - 1, 0), 0))
    bias_spec = BS(bias.shape, lambda r, mb: (0, 0, 0))
    return pl.pallas_call(
        body, name=name, grid=(d, nb + 1),
        in_specs=_attn_specs(cols, nb, awg, clamp) + [bias_spec, cur, cur, cur],
        out_specs=[cur, prev, prev, bias_spec],
        out_shape=[SDS((t, awg), BF16)] * 3 + [SDS(bias.shape, F32)],
        scratch_shapes=[pltpu.VMEM((ATTN_BLOCK, awg), F32)] * 2, compiler_params=_cp("arbitrary", "arbitrary"),
    )(*([qkv] * 5), bias, lse, do, corr)


def _head_sum(x, low):
    a = jnp.sum(jnp.where(low, x, 0.0), axis=-1, keepdims=True)
    b = jnp.sum(jnp.where(low, 0.0, x), axis=-1, keepdims=True)
    return jnp.where(low, a, b)


def _combine_weights(lses):
    mx = jnp.maximum(jnp.maximum(lses[0], lses[1]), lses[2])
    es = [jnp.exp(l - mx) for l in lses]
    tot = es[0] + es[1] + es[2]
    return [e / tot for e in es]


def _combine_fwd(os_, lses, *, name):
    t, awg = os_[0].shape
    tm = _pick(t, (512, 256, 128))

    def body(o0, o1, o2, l0, l1, l2, out_ref):
        al = _combine_weights([l0[...], l1[...], l2[...]])
        out_ref[...] = (al[0] * o0[...] + al[1] * o1[...] + al[2] * o2[...]).astype(BF16)

    blk = BS((tm, awg), lambda i: (i, 0))
    return pl.pallas_call(
        body, name=name, grid=(t // tm,), in_specs=[blk] * 6, out_specs=blk,
        out_shape=SDS((t, awg), BF16), compiler_params=_cp("parallel"),
    )(*os_, *lses)


def _combine_bwd(dattn, os_, lses, *, name):
    t, awg = dattn.shape
    tm = _pick(t, (512, 256, 128))

    def body(da_ref, o0, o1, o2, l0, l1, l2, d0, d1, d2, c0, c1, c2):
        low = lax.broadcasted_iota(jnp.int32, (tm, LANES), 1) < HEAD_DIM
        for pi in range(awg // LANES):
            sl = slice(pi * LANES, (pi + 1) * LANES)
            da = da_ref[:, sl]
            al = _combine_weights([l0[:, sl], l1[:, sl], l2[:, sl]])
            tot = jnp.zeros((tm, LANES), F32)
            for a, o in zip(al, (o0, o1, o2)):
                tot = tot + a * _head_sum(da * o[:, sl], low)
            for a, d_ref, c_ref in zip(al, (d0, d1, d2), (c0, c1, c2)):
                d_ref[:, sl] = (a * da).astype(BF16)
                c_ref[:, sl] = -a * tot

    blk = BS((tm, awg), lambda i: (i, 0))
    outs = pl.pallas_call(
        body, name=name, grid=(t // tm,), in_specs=[blk] * 7, out_specs=[blk] * 6,
        out_shape=[SDS((t, awg), BF16)] * 3 + [SDS((t, awg), F32)] * 3, compiler_params=_cp("parallel"),
    )(dattn, *os_, *lses)
    return outs[:3], outs[3:]


def _conv_block(width, *offsets):
    for c in (512, 256, 128):
        if width % c == 0 and all(o % c == 0 for o in offsets):
            return c
    raise ValueError((width, offsets))


CONV_ROWS = 32


def _conv_pre(x_ref, halo_ref, w_ref, b_ref, ext, i, tm):
    ext[pl.ds(0, HALO), :] = jnp.where(i > 0, halo_ref[...].astype(F32), 0.0)
    ext[pl.ds(HALO, tm), :] = x_ref[...].astype(F32)
    taps = [w_ref[pl.ds(k, 1), :] for k in range(CONV_WIDTH)]
    bias = b_ref[...]
    for r0 in range(0, tm, CONV_ROWS):
        xs = [ext[pl.ds(HALO + r0 - (CONV_WIDTH - 1) + k, CONV_ROWS), :] for k in range(CONV_WIDTH)]
        pre = bias + taps[0] * xs[0]
        for k in range(1, CONV_WIDTH):
            pre = pre + taps[k] * xs[k]
        yield r0, pre, xs


def _fold8(v):
    return jnp.sum(v.reshape(v.shape[0] // 8, 8, v.shape[1]), axis=0)


def _conv_fwd(proj, off, w, b, *, name):
    t = proj.shape[0]
    c = w.shape[1]
    cw = _conv_block(c, off)
    tm = _pick(t, (512, 256, 128))
    ob = off // cw

    def body(x_ref, halo_ref, w_ref, b_ref, o_ref, ext):
        for r0, pre, _ in _conv_pre(x_ref, halo_ref, w_ref, b_ref, ext, pl.program_id(1), tm):
            o_ref[pl.ds(r0, CONV_ROWS), :] = _silu(pre).astype(BF16)

    return pl.pallas_call(
        body, name=name, grid=(c // cw, t // tm),
        in_specs=[BS((tm, cw), lambda j, i: (i, ob + j)),
                  BS((HALO, cw), lambda j, i: (jnp.maximum(i * (tm // HALO) - 1, 0), ob + j)),
                  BS((CONV_WIDTH, cw), lambda j, i: (0, j)), BS((1, cw), lambda j, i: (0, j))],
        out_specs=BS((tm, cw), lambda j, i: (i, j)), out_shape=SDS((t, c), BF16),
        scratch_shapes=[pltpu.VMEM((HALO + tm, cw), F32)], compiler_params=_cp("parallel", "arbitrary"),
    )(proj, proj, w, b.reshape(1, c))


def _conv_bwd(proj, off, w, b, dxc, *, name):
    t = proj.shape[0]
    c = w.shape[1]
    cw = _conv_block(c, off)
    tm = _pick(t, (512, 256, 128))
    ob = off // cw
    nt = t // tm

    def body_pre(x_ref, halo_ref, w_ref, b_ref, d_ref, dp_ref, dw_ref, db_ref, ext):
        i = pl.program_id(1)

        @pl.when(i == 0)
        def _():
            dw_ref[...] = jnp.zeros_like(dw_ref)
            db_ref[...] = jnp.zeros_like(db_ref)

        db_acc = jnp.zeros((8, cw), F32)
        dw_acc = [jnp.zeros((8, cw), F32) for _ in range(CONV_WIDTH)]
        for r0, pre, xs in _conv_pre(x_ref, halo_ref, w_ref, b_ref, ext, i, tm):
            dpre = d_ref[pl.ds(r0, CONV_ROWS), :] * _dsilu(pre)
            dp_ref[pl.ds(r0, CONV_ROWS), :] = dpre
            db_acc = db_acc + _fold8(dpre)
            dw_acc = [acc + _fold8(dpre * x) for acc, x in zip(dw_acc, xs)]
        db_ref[...] += jnp.sum(db_acc, axis=0, keepdims=True)
        for k in range(CONV_WIDTH):
            dw_ref[pl.ds(k, 1), :] += jnp.sum(dw_acc[k], axis=0, keepdims=True)

    dpre, dw, db = pl.pallas_call(
        body_pre, name=name + "_pre", grid=(c // cw, nt),
        in_specs=[BS((tm, cw), lambda j, i: (i, ob + j)),
                  BS((HALO, cw), lambda j, i: (jnp.maximum(i * (tm // HALO) - 1, 0), ob + j)),
                  BS((CONV_WIDTH, cw), lambda j, i: (0, j)), BS((1, cw), lambda j, i: (0, j)),
                  BS((tm, cw), lambda j, i: (i, j))],
        out_specs=[BS((tm, cw), lambda j, i: (i, j)), BS((CONV_WIDTH, cw), lambda j, i: (0, j)), BS((1, cw), lambda j, i: (0, j))],
        out_shape=[SDS((t, c), F32), SDS((CONV_WIDTH, c), F32), SDS((1, c), F32)],
        scratch_shapes=[pltpu.VMEM((HALO + tm, cw), F32)], compiler_params=_cp("parallel", "arbitrary"),
    )(proj, proj, w, b.reshape(1, c), dxc)

    def body_in(dp_ref, nxt_ref, w_ref, dx_ref, ext):
        i = pl.program_id(1)
        ext[pl.ds(0, tm), :] = dp_ref[...]
        ext[pl.ds(tm, 8), :] = jnp.where(i < nt - 1, nxt_ref[...], 0.0)
        taps = [w_ref[pl.ds(k, 1), :] for k in range(CONV_WIDTH)]
        for r0 in range(0, tm, CONV_ROWS):
            dx = taps[CONV_WIDTH - 1] * ext[pl.ds(r0, CONV_ROWS), :]
            for k in range(CONV_WIDTH - 1):
                dx = dx + taps[k] * ext[pl.ds(r0 + CONV_WIDTH - 1 - k, CONV_ROWS), :]
            dx_ref[pl.ds(r0, CONV_ROWS), :] = dx.astype(BF16)

    dx = pl.pallas_call(
        body_in, name=name + "_in", grid=(c // cw, nt),
        in_specs=[BS((tm, cw), lambda j, i: (i, j)),
                  BS((8, cw), lambda j, i: (jnp.minimum((i + 1) * (tm // 8), t // 8 - 1), j)),
                  BS((CONV_WIDTH, cw), lambda j, i: (0, j))],
        out_specs=BS((tm, cw), lambda j, i: (i, j)), out_shape=SDS((t, c), BF16),
        scratch_shapes=[pltpu.VMEM((tm + 8, cw), F32)], compiler_params=_cp("parallel", "arbitrary"),
    )(dpre, dpre, w)
    return dx, dw, db[0]


def _softplus(x):
    return jnp.maximum(x, 0.0) + jnp.log(1.0 + jnp.exp(-jnp.abs(x)))


def _ssd_common(dtr_ref, bias_ref, a_ref):
    pre = dtr_ref[...] + bias_ref[...]
    dt = _softplus(pre)
    ri = lax.broadcasted_iota(jnp.int32, (CHUNK, CHUNK), 0)
    ci = lax.broadcasted_iota(jnp.int32, (CHUNK, CHUNK), 1)
    tril = ri >= ci
    la = _dot(tril.astype(F32), dt * a_ref[...], precision=HIGHEST)
    return pre, dt, la, la.T, tril


def _lane_col(x, lane, h):
    return jnp.sum(jnp.where(lane == h, x, 0.0), axis=-1, keepdims=True)


def _ssd_specs(xc, di, gw, cidx):
    nbx = di // LANES
    return [BS((CHUNK, gw), lambda g, c: (cidx(c), g)),
            BS((CHUNK, D_STATE), lambda g, c: (cidx(c), nbx + g)),
            BS((CHUNK, D_STATE), lambda g, c: (cidx(c), nbx + SSM_GROUPS + g)),
            BS((CHUNK, LANES), lambda g, c: (cidx(c), 0)),
            BS((1, LANES), lambda g, c: (0, 0)), BS((1, LANES), lambda g, c: (0, 0)),
            BS((1, gw), lambda g, c: (0, g))]


def _ssd_fwd(xc, dtr, dt_bias, a, dskip, *, name):
    t = xc.shape[0]
    di = xc.shape[1] - 2 * SSM_GROUPS * D_STATE
    gw = di // SSM_GROUPS
    hpg = gw // HEAD_DIM
    npair = gw // LANES
    nc = t // CHUNK

    def body(x_ref, b_ref, c_ref, dtr_ref, bias_ref, a_ref, dsk_ref, y_ref, st_ref, state):
        g, c = pl.program_id(0), pl.program_id(1)

        @pl.when(c == 0)
        def _():
            state[...] = jnp.zeros_like(state)

        st_ref[0, 0] = state[...]
        _, dt, la, la_t, tril = _ssd_common(dtr_ref, bias_ref, a_ref)
        lane = lax.broadcasted_iota(jnp.int32, (CHUNK, LANES), 1)
        sub = lax.broadcasted_iota(jnp.int32, (LANES, CHUNK), 0)
        lane1 = lax.broadcasted_iota(jnp.int32, (1, LANES), 1)
        low, low1 = lane < HEAD_DIM, lane1 < HEAD_DIM
        last = lax.broadcasted_iota(jnp.int32, (CHUNK, LANES), 0) == CHUNK - 1
        lend = jnp.sum(jnp.where(last, la, 0.0), axis=0, keepdims=True)
        bm, cm = b_ref[...], c_ref[...]
        gmat = _dot(cm, bm, NT)
        for p in range(npair):
            sl = slice(p * LANES, (p + 1) * LANES)
            ps = slice(p * D_STATE, (p + 1) * D_STATE)
            x2 = x_ref[:, sl].astype(F32)
            cols, ms = [], []
            for hh in range(2):
                h = g * hpg + p * 2 + hh
                col_la = _lane_col(la, lane, h)
                row_la = jnp.sum(jnp.where(sub == h, la_t, 0.0), axis=0, keepdims=True)
                lend_h = _lane_col(lend, lane1, h)
                decay = jnp.exp(jnp.where(tril, col_la - row_la, -jnp.inf))
                ms.append((gmat * decay).astype(BF16))
                cols.append((_lane_col(dt, lane, h), jnp.exp(col_la), jnp.exp(lend_h - col_la), jnp.exp(lend_h)))
            pair = lambda k: jnp.where(low, cols[0][k], cols[1][k])
            xdt = x2 * pair(0)
            xdtb = xdt.astype(BF16)
            s2 = state[ps, :]
            y = jnp.where(low, _dot(ms[0], xdtb), _dot(ms[1], xdtb))
            y = y + pair(1) * _dot(cm, s2.astype(BF16)) + x2 * dsk_ref[:, sl]
            y_ref[:, sl] = y
            state[ps, :] = s2 * jnp.where(low1, cols[0][3], cols[1][3]) + _dot(bm, (xdt * pair(2)).astype(BF16), TN)

    y, st = pl.pallas_call(
        body, name=name, grid=(SSM_GROUPS, nc), in_specs=_ssd_specs(xc, di, gw, lambda c: c),
        out_specs=[BS((CHUNK, gw), lambda g, c: (c, g)), BS((1, 1, npair * D_STATE, LANES), lambda g, c: (c, g, 0, 0))],
        out_shape=[SDS((t, di), F32), SDS((nc, SSM_GROUPS, npair * D_STATE, LANES), F32)],
        scratch_shapes=[pltpu.VMEM((npair * D_STATE, LANES), F32)], compiler_params=_cp("parallel", "arbitrary"),
    )(xc, xc, xc, dtr, dt_bias, a, dskip)
    return y, st


def _ssd_bwd(xc, dtr, dt_bias, a, dskip, st, dy, *, name):
    t = xc.shape[0]
    di = xc.shape[1] - 2 * SSM_GROUPS * D_STATE
    gw = di // SSM_GROUPS
    hpg = gw // HEAD_DIM
    npair = gw // LANES
    nc = t // CHUNK
    rev = lambda c: nc - 1 - c

    def body(x_ref, b_ref, c_ref, dtr_ref, bias_ref, a_ref, dsk_ref, st_ref, dy_ref,
             dx_ref, db_ref, dc_ref, ddtr_ref, da_ref, dbias_ref, ddsk_ref, dstate):
        g, c = pl.program_id(0), pl.program_id(1)

        @pl.when(c == 0)
        def _():
            dstate[...] = jnp.zeros_like(dstate)
            da_ref[...] = jnp.zeros_like(da_ref)
            dbias_ref[...] = jnp.zeros_like(dbias_ref)
            ddsk_ref[...] = jnp.zeros_like(ddsk_ref)

        pre, dt, la, la_t, tril = _ssd_common(dtr_ref, bias_ref, a_ref)
        lane = lax.broadcasted_iota(jnp.int32, (CHUNK, LANES), 1)
        sub = lax.broadcasted_iota(jnp.int32, (LANES, CHUNK), 0)
        lane1 = lax.broadcasted_iota(jnp.int32, (1, LANES), 1)
        low, low1 = lane < HEAD_DIM, lane1 < HEAD_DIM
        last = lax.broadcasted_iota(jnp.int32, (CHUNK, LANES), 0) == CHUNK - 1
        lend = jnp.sum(jnp.where(last, la, 0.0), axis=0, keepdims=True)
        bm, cm = b_ref[...], c_ref[...]
        gmat = _dot(cm, bm, NT)
        dg = jnp.zeros((CHUNK, CHUNK), F32)
        dla_cols = jnp.zeros((CHUNK, LANES), F32)
        dla_rows = jnp.zeros((LANES, CHUNK), F32)
        dtsum = jnp.zeros((CHUNK, LANES), F32)
        dbm = jnp.zeros((CHUNK, D_STATE), F32)
        dcm = jnp.zeros((CHUNK, D_STATE), F32)
        for p in range(npair):
            sl = slice(p * LANES, (p + 1) * LANES)
            ps = slice(p * D_STATE, (p + 1) * D_STATE)
            x2 = x_ref[:, sl].astype(F32)
            dy2 = dy_ref[:, sl]
            s2 = st_ref[0, 0, ps, :]
            ds2 = dstate[ps, :]
            hs, cols, ms, decays = [], [], [], []
            for hh in range(2):
                h = g * hpg + p * 2 + hh
                col_la = _lane_col(la, lane, h)
                row_la = jnp.sum(jnp.where(sub == h, la_t, 0.0), axis=0, keepdims=True)
                lend_h = _lane_col(lend, lane1, h)
                decay = jnp.exp(jnp.where(tril, col_la - row_la, -jnp.inf))
                hs.append(h)
                decays.append(decay)
                ms.append(gmat * decay)
                cols.append((_lane_col(dt, lane, h), jnp.exp(col_la), jnp.exp(lend_h - col_la), jnp.exp(lend_h)))
            pair = lambda k: jnp.where(low, cols[0][k], cols[1][k])
            dtc, ec, eend = pair(0), pair(1), pair(2)
            eend_s = jnp.where(low1, cols[0][3], cols[1][3])
            xdt = x2 * dtc
            xdtb = xdt.astype(BF16)
            dys = dy2 * ec
            dysb = dys.astype(BF16)
            dxdt_state = eend * _dot(bm, ds2.astype(BF16))
            inter = dys * _dot(cm, s2.astype(BF16))
            u = dxdt_state * xdt
            sds = s2 * ds2
            dxdt = dxdt_state
            for hh in range(2):
                h = hs[hh]
                mh = low if hh == 0 else jnp.logical_not(low)
                dym = jnp.where(mh, dy2, 0.0).astype(BF16)
                dxdt = dxdt + _dot(ms[hh].astype(BF16), dym, TN)
                dm = _dot(dym, xdtb, NT)
                w = dm * ms[hh]
                dg = dg + dm * decays[hh]
                u_col = jnp.sum(jnp.where(mh, u, 0.0), axis=-1, keepdims=True)
                dlend = jnp.sum(u_col, axis=0, keepdims=True) + cols[hh][3] * jnp.sum(jnp.where(low1 if hh == 0 else jnp.logical_not(low1), jnp.sum(sds, axis=0, keepdims=True), 0.0), axis=-1, keepdims=True)
                col = jnp.sum(w, axis=-1, keepdims=True) + jnp.sum(jnp.where(mh, inter, 0.0), axis=-1, keepdims=True) - u_col
                dla_cols = dla_cols + jnp.where(lane == h, col + jnp.where(last, dlend, 0.0), 0.0)
                dla_rows = dla_rows - jnp.where(sub == h, jnp.sum(w, axis=0, keepdims=True), 0.0)
            for hh in range(2):
                mh = low if hh == 0 else jnp.logical_not(low)
                dtsum = dtsum + jnp.where(lane == hs[hh], jnp.sum(jnp.where(mh, dxdt * x2, 0.0), axis=-1, keepdims=True), 0.0)
            dcm = dcm + _dot(dysb, s2.astype(BF16), NT)
            dbm = dbm + _dot((xdt * eend).astype(BF16), ds2.astype(BF16), NT)
            dstate[ps, :] = ds2 * eend_s + _dot(cm, dysb, TN)
            dx_ref[:, sl] = dxdt * dtc + dy2 * dsk_ref[:, sl]
            ddsk_ref[:, sl] += jnp.sum(dy2 * x2, axis=0, keepdims=True)
        dgb = dg.astype(BF16)
        dc_ref[...] = dcm + _dot(dgb, bm)
        db_ref[...] = dbm + _dot(dgb, cm, TN)
        dla = dla_cols + dla_rows.T
        triu = lax.broadcasted_iota(jnp.int32, (CHUNK, CHUNK), 0) <= lax.broadcasted_iota(jnp.int32, (CHUNK, CHUNK), 1)
        ddta = _dot(triu.astype(F32), dla, precision=HIGHEST)
        ddt = ddta * a_ref[...] + dtsum
        da_ref[0] += jnp.sum(ddta * dt, axis=0, keepdims=True)
        ddtr = ddt * _sigmoid(pre)
        ddtr_ref[0] = ddtr
        dbias_ref[0] += jnp.sum(ddtr, axis=0, keepdims=True)

    vec = BS((1, 1, LANES), lambda g, c: (g, 0, 0))
    outs = pl.pallas_call(
        body, name=name, grid=(SSM_GROUPS, nc),
        in_specs=_ssd_specs(xc, di, gw, rev) + [BS((1, 1, npair * D_STATE, LANES), lambda g, c: (rev(c), g, 0, 0)),
                                                BS((CHUNK, gw), lambda g, c: (rev(c), g))],
        out_specs=[BS((CHUNK, gw), lambda g, c: (rev(c), g)), BS((CHUNK, D_STATE), lambda g, c: (rev(c), g)),
                   BS((CHUNK, D_STATE), lambda g, c: (rev(c), g)), BS((1, CHUNK, LANES), lambda g, c: (g, rev(c), 0)),
                   vec, vec, BS((1, gw), lambda g, c: (0, g))],
        out_shape=[SDS((t, di), F32), SDS((t, SSM_GROUPS * D_STATE), F32), SDS((t, SSM_GROUPS * D_STATE), F32),
                   SDS((SSM_GROUPS, t, LANES), F32), SDS((SSM_GROUPS, 1, LANES), F32), SDS((SSM_GROUPS, 1, LANES), F32),
                   SDS((1, di), F32)],
        scratch_shapes=[pltpu.VMEM((npair * D_STATE, LANES), F32)], compiler_params=_cp("parallel", "arbitrary"),
    )(xc, xc, xc, dtr, dt_bias, a, dskip, st, dy)
    return outs


def _gate_norm_fwd(y, proj, zoff, w, *, name):
    t, di = y.shape
    gw = di // SSM_GROUPS
    tm = _pick(t, (512, 256, 128))
    zb = zoff // gw

    def body(y_ref, z_ref, w_ref, o_ref):
        yg = y_ref[...] * _silu(z_ref[...].astype(F32))
        r = lax.rsqrt(jnp.mean(yg * yg, axis=-1, keepdims=True) + EPS)
        o_ref[...] = (yg * r * w_ref[...]).astype(BF16)

    return pl.pallas_call(
        body, name=name, grid=(t // tm, SSM_GROUPS),
        in_specs=[BS((tm, gw), lambda i, g: (i, g)), BS((tm, gw), lambda i, g: (i, zb + g)), BS((1, gw), lambda i, g: (0, g))],
        out_specs=BS((tm, gw), lambda i, g: (i, g)), out_shape=SDS((t, di), BF16), compiler_params=_cp("parallel", "parallel"),
    )(y, proj, w.reshape(1, di))


def _gate_norm_bwd(dssm, y, proj, zoff, w, *, name):
    t, di = y.shape
    gw = di // SSM_GROUPS
    tm = _pick(t, (512, 256, 128))
    zb = zoff // gw

    def body(d_ref, y_ref, z_ref, w_ref, dy_ref, dz_ref, dw_ref):
        z = z_ref[...].astype(F32)
        yv = y_ref[...]
        sz = _silu(z)
        yg = yv * sz
        r = lax.rsqrt(jnp.mean(yg * yg, axis=-1, keepdims=True) + EPS)
        n = yg * r
        d = d_ref[...]
        dn = d * w_ref[...]
        dyg = r * (dn - n * jnp.mean(dn * n, axis=-1, keepdims=True))
        dy_ref[...] = dyg * sz
        dz_ref[...] = (dyg * yv * _dsilu(z)).astype(BF16)

        @pl.when(pl.program_id(1) == 0)
        def _():
            dw_ref[...] = jnp.zeros_like(dw_ref)

        dw_ref[...] += jnp.sum(d * n, axis=0, keepdims=True)

    blk = BS((tm, gw), lambda g, i: (i, g))
    dy, dz, dw = pl.pallas_call(
        body, name=name, grid=(SSM_GROUPS, t // tm),
        in_specs=[blk, blk, BS((tm, gw), lambda g, i: (i, zb + g)), BS((1, gw), lambda g, i: (0, g))],
        out_specs=[blk, blk, BS((1, gw), lambda g, i: (0, g))],
        out_shape=[SDS((t, di), F32), SDS((t, di), BF16), SDS((1, di), F32)], compiler_params=_cp("parallel", "arbitrary"),
    )(dssm, y, proj, w.reshape(1, di))
    return dy, dz, dw[0]


def _merge_fwd(proj, goff, ga, gs, *, name):
    t, d = ga.shape
    cw = _conv_block(d, goff)
    tm = _pick(t, (512, 256, 128))
    gb = goff // cw

    def body(g0, g1, a_ref, s_ref, o_ref):
        o_ref[...] = (_sigmoid(g0[...].astype(F32)) * a_ref[...] + _sigmoid(g1[...].astype(F32)) * s_ref[...]).astype(BF16)

    blk = BS((tm, cw), lambda i, j: (i, j))
    return pl.pallas_call(
        body, name=name, grid=(t // tm, d // cw),
        in_specs=[BS((tm, cw), lambda i, j: (i, gb + j)), BS((tm, cw), lambda i, j: (i, gb + d // cw + j)), blk, blk],
        out_specs=blk, out_shape=SDS((t, d), BF16), compiler_params=_cp("parallel", "parallel"),
    )(proj, proj, ga, gs)


def _merge_bwd(proj, goff, ga, gs, dm, *, name):
    t, d = ga.shape
    cw = _conv_block(d, goff)
    tm = _pick(t, (512, 256, 128))
    gb = goff // cw

    def body(g0, g1, a_ref, s_ref, dm_ref, da_ref, ds_ref, dg0_ref, dg1_ref):
        dmv = dm_ref[...]
        s0 = _sigmoid(g0[...].astype(F32))
        s1 = _sigmoid(g1[...].astype(F32))
        da_ref[...] = (s0 * dmv).astype(BF16)
        ds_ref[...] = (s1 * dmv).astype(BF16)
        dg0_ref[...] = (dmv * a_ref[...] * s0 * (1.0 - s0)).astype(BF16)
        dg1_ref[...] = (dmv * s_ref[...] * s1 * (1.0 - s1)).astype(BF16)

    blk = BS((tm, cw), lambda i, j: (i, j))
    return pl.pallas_call(
        body, name=name, grid=(t // tm, d // cw),
        in_specs=[BS((tm, cw), lambda i, j: (i, gb + j)), BS((tm, cw), lambda i, j: (i, gb + d // cw + j)), blk, blk, blk],
        out_specs=[blk] * 4, out_shape=[SDS((t, d), BF16)] * 4, compiler_params=_cp("parallel", "parallel"),
    )(proj, proj, ga, gs, dm)


def _swiglu_fwd(u, *, name):
    t, two_f = u.shape
    f = two_f // 2
    cw = _wide(f)
    tm = _pick(t, (512, 256, 128))

    def body(g_ref, u_ref, o_ref):
        o_ref[...] = (_silu(g_ref[...].astype(F32)) * u_ref[...].astype(F32)).astype(BF16)

    return pl.pallas_call(
        body, name=name, grid=(t // tm, f // cw),
        in_specs=[BS((tm, cw), lambda i, j: (i, j)), BS((tm, cw), lambda i, j: (i, f // cw + j))],
        out_specs=BS((tm, cw), lambda i, j: (i, j)), out_shape=SDS((t, f), BF16), compiler_params=_cp("parallel", "parallel"),
    )(u, u)


def _swiglu_bwd(u, df, *, name):
    t, two_f = u.shape
    f = two_f // 2
    cw = _wide(f)
    tm = _pick(t, (512, 256, 128))

    def body(g_ref, u_ref, d_ref, dg_ref, du_ref):
        gt = g_ref[...].astype(F32)
        d = d_ref[...].astype(F32)
        dg_ref[...] = (d * u_ref[...].astype(F32) * _dsilu(gt)).astype(BF16)
        du_ref[...] = (d * _silu(gt)).astype(BF16)

    blk = BS((tm, cw), lambda i, j: (i, j))
    return pl.pallas_call(
        body, name=name, grid=(t // tm, f // cw),
        in_specs=[blk, BS((tm, cw), lambda i, j: (i, f // cw + j)), blk],
        out_specs=[blk, blk], out_shape=[SDS((t, f), BF16)] * 2, compiler_params=_cp("parallel", "parallel"),
    )(u, u, df)


def _row_block(rows, cols, n_arrays):
    budget = VMEM_LIMIT_BYTES // 3
    for tr in (512, 256, 128, 64, 32, 16, 8):
        if rows % tr == 0 and tr * cols * 4 * n_arrays * 2 <= budget:
            return tr
    raise ValueError((rows, cols))


def _add_own_layer(g0, g1, got, core, *, name):
    rows, cols = got.shape
    tr = _row_block(rows, cols, 4)

    def body(core_ref, g0_ref, g1_ref, got_ref, o_ref):
        o_ref[...] = jnp.where(core_ref[0] == 0, g0_ref[...], g1_ref[...]) + got_ref[...]

    blk = BS((tr, cols), lambda i, cr: (i, 0))
    grid_spec = pltpu.PrefetchScalarGridSpec(
        num_scalar_prefetch=1, grid=(rows // tr,),
        in_specs=[BS((tr, cols), lambda i, cr: (i * (1 - cr[0]), 0)), BS((tr, cols), lambda i, cr: (i * cr[0], 0)), blk],
        out_specs=blk)
    return pl.pallas_call(body, name=name, grid_spec=grid_spec, out_shape=SDS((rows, cols), F32),
                          compiler_params=_cp("arbitrary"))(core, g0, g1, got)


def _sum_chips(a, *, name):
    _, rows, cols = a.shape
    tr = _row_block(rows, cols, 5)

    def body(a_ref, o_ref):
        o_ref[...] = ((a_ref[0] + a_ref[1]) + a_ref[2]) + a_ref[3]

    return pl.pallas_call(body, name=name, grid=(rows // tr,), in_specs=[BS((N_CHIPS, tr, cols), lambda i: (0, i, 0))],
                          out_specs=BS((tr, cols), lambda i: (i, 0)), out_shape=SDS((rows, cols), F32),
                          compiler_params=_cp("parallel"))(a)


def _adamw(w, g, m, v, *, name):
    rows, cols = w.shape
    tr = _row_block(rows, cols, 7) if rows % 8 == 0 else rows
    c1 = 1.0 - ADAM_B1 ** ADAM_STEP
    c2 = 1.0 - ADAM_B2 ** ADAM_STEP

    def body(w_ref, g_ref, m_ref, v_ref, d_ref, nm_ref, nv_ref):
        gv = g_ref[...]
        nm = ADAM_B1 * m_ref[...] + (1.0 - ADAM_B1) * gv
        nv = ADAM_B2 * v_ref[...] + (1.0 - ADAM_B2) * (gv * gv)
        d_ref[...] = -ADAM_LR * ((nm / c1) / (jnp.sqrt(nv / c2) + ADAM_EPS) + ADAM_WD * w_ref[...])
        nm_ref[...] = nm
        nv_ref[...] = nv

    blk = BS((tr, cols), lambda i: (i, 0))
    return pl.pallas_call(body, name=name, grid=(rows // tr,), in_specs=[blk] * 4, out_specs=[blk] * 3,
                          out_shape=[SDS((rows, cols), F32)] * 3, compiler_params=_cp("parallel"))(w, g, m, v)


def _adamw_layers(w, g_own, g_other, m, v, core, *, name):
    _, rows, cols = w.shape
    tr = _row_block(rows, cols, 9)
    c1 = 1.0 - ADAM_B1 ** ADAM_STEP
    c2 = 1.0 - ADAM_B2 ** ADAM_STEP

    def body(core_ref, w_ref, own_ref, oth_ref, m_ref, v_ref, g_ref, d_ref, nm_ref, nv_ref):
        gv = jnp.where(pl.program_id(0) == core_ref[0], own_ref[...], oth_ref[...])
        nm = ADAM_B1 * m_ref[0] + (1.0 - ADAM_B1) * gv
        nv = ADAM_B2 * v_ref[0] + (1.0 - ADAM_B2) * (gv * gv)
        g_ref[0] = gv
        d_ref[0] = -ADAM_LR * ((nm / c1) / (jnp.sqrt(nv / c2) + ADAM_EPS) + ADAM_WD * w_ref[0])
        nm_ref[0] = nm
        nv_ref[0] = nv

    own_here = lambda l, cr: 1 - (l - cr[0]) * (l - cr[0])
    slab = BS((1, tr, cols), lambda l, i, cr: (l, i, 0))
    grid_spec = pltpu.PrefetchScalarGridSpec(
        num_scalar_prefetch=1, grid=(2, rows // tr),
        in_specs=[slab, BS((tr, cols), lambda l, i, cr: (i * own_here(l, cr), 0)),
                  BS((tr, cols), lambda l, i, cr: (i * (1 - own_here(l, cr)), 0)), slab, slab],
        out_specs=[slab] * 4)
    return pl.pallas_call(body, name=name, grid_spec=grid_spec, out_shape=[SDS(w.shape, F32)] * 4,
                          compiler_params=_cp("arbitrary", "arbitrary"))(core, w, g_own, g_other, m, v)


ANY = BS(memory_space=pl.ANY)


def _place():
    x, y, c = lax.axis_index("x"), lax.axis_index("y"), lax.axis_index("c")
    return x, y, c, [(1 - x, y), (x, 1 - y), (1 - x, 1 - y)]


def _gather_shards(arrs, *, name):
    n = len(arrs)

    def body(*refs):
        ins, outs = refs[:n], refs[n:2 * n]
        send_sems, recv_sems, pass_send_sems, pass_recv_sems = refs[2 * n:]
        x, y, c, chips = _place()
        s = 2 * x + y

        def ici(i, j, src_chip, to):
            src = ins[i].at[c] if src_chip is None else outs[i].at[src_chip, c]
            return pltpu.make_async_remote_copy(
                src_ref=src, dst_ref=outs[i].at[s if src_chip is None else src_chip, c], send_sem=send_sems.at[i * 3 + j],
                recv_sem=recv_sems.at[i * 3 + j], device_id=to, device_id_type=MESH)

        def d2d(i, j, src_chip, layer):
            slab = outs[i].at[src_chip, layer]
            return pltpu.make_async_remote_copy(
                src_ref=slab, dst_ref=slab, send_sem=pass_send_sems.at[i * 3 + j], recv_sem=pass_recv_sems.at[i * 3 + j],
                device_id=(x, y, 1 - c), device_id_type=MESH)

        sent = []
        for i in range(n):
            for j, (px, py) in enumerate(chips):
                cp = ici(i, j, None, (px, py, c))
                cp.start()
                sent.append(cp)
        passed = []
        for i in range(n):
            for j, (px, py) in enumerate(chips):
                ici(i, j, 2 * px + py, (x, y, c)).wait_recv()
                cp = d2d(i, j, 2 * px + py, c)
                cp.start()
                passed.append(cp)
        for i in range(n):
            for j, (px, py) in enumerate(chips):
                d2d(i, j, 2 * px + py, 1 - c).wait_recv()
        for cp in sent + passed:
            cp.wait_send()

    return pl.pallas_call(
        body, name=name, in_specs=[ANY] * n, out_specs=[ANY] * n,
        out_shape=[SDS((N_CHIPS,) + a.shape, a.dtype) for a in arrs],
        scratch_shapes=[pltpu.SemaphoreType.DMA((3 * n,))] * 4,
    )(*arrs)


def _pair_swap_layers(layer0, layer1, *, name):
    n = len(layer0)

    def body(*refs):
        in0, in1, outs = refs[:n], refs[n:2 * n], refs[2 * n:3 * n]
        send_sems, recv_sems = refs[3 * n:]
        x, y, c, _ = _place()

        def copy(src, i):
            return pltpu.make_async_remote_copy(
                src_ref=src[i], dst_ref=outs[i], send_sem=send_sems.at[i], recv_sem=recv_sems.at[i],
                device_id=(x, y, 1 - c), device_id_type=MESH)

        @pl.when(c == 0)
        def _():
            for i in range(n):
                copy(in1, i).start()

        @pl.when(c == 1)
        def _():
            for i in range(n):
                copy(in0, i).start()

        for i in range(n):
            copy(in0, i).wait()

    return pl.pallas_call(
        body, name=name, in_specs=[ANY] * (2 * n), out_specs=[ANY] * n, out_shape=[SDS(a.shape, a.dtype) for a in layer0],
        scratch_shapes=[pltpu.SemaphoreType.DMA((n,)), pltpu.SemaphoreType.DMA((n,))],
    )(*layer0, *layer1)


def _scatter_to_chips(arrs, *, name):
    n = len(arrs)

    def body(*refs):
        ins, outs = refs[:n], refs[n:2 * n]
        send_sems, recv_sems = refs[2 * n:]
        x, y, c, chips = _place()
        s = 2 * x + y
        copies = []
        for i in range(n):
            for j, (px, py) in enumerate(chips):
                cp = pltpu.make_async_remote_copy(
                    src_ref=ins[i].at[2 * px + py], dst_ref=outs[i].at[s], send_sem=send_sems.at[i * 3 + j],
                    recv_sem=recv_sems.at[i * 3 + j], device_id=(px, py, c), device_id_type=MESH)
                cp.start()
                copies.append(cp)
        for cp in copies:
            cp.wait()

    return pl.pallas_call(
        body, name=name, in_specs=[ANY] * n, out_specs=[ANY] * n, out_shape=[SDS(a.shape, a.dtype) for a in arrs],
        scratch_shapes=[pltpu.SemaphoreType.DMA((3 * n,)), pltpu.SemaphoreType.DMA((3 * n,))],
    )(*arrs)


def _pair_swap(arrs, *, name):
    n = len(arrs)

    def body(*refs):
        ins, outs = refs[:n], refs[n:2 * n]
        send_sems, recv_sems = refs[2 * n:]
        x, y, c, _ = _place()
        copies = []
        for i in range(n):
            cp = pltpu.make_async_remote_copy(
                src_ref=ins[i], dst_ref=outs[i], send_sem=send_sems.at[i], recv_sem=recv_sems.at[i],
                device_id=(x, y, 1 - c), device_id_type=MESH)
            cp.start()
            copies.append(cp)
        for cp in copies:
            cp.wait()

    return pl.pallas_call(
        body, name=name, in_specs=[ANY] * n, out_specs=[ANY] * n, out_shape=[SDS(a.shape, a.dtype) for a in arrs],
        scratch_shapes=[pltpu.SemaphoreType.DMA((n,)), pltpu.SemaphoreType.DMA((n,))],
    )(*arrs)


def _allreduce_small(v, *, name):
    rows, cols = v.shape

    def body(v_ref, o_ref, gath, send_sems, recv_sems):
        x, y, c, _ = _place()
        me = 4 * x + 2 * y + c
        gath[me] = v_ref[...]
        copies = []
        for k in range(1, N_DEV):
            fx, fy, fc = (k >> 2) & 1, (k >> 1) & 1, k & 1
            peer = (1 - x if fx else x, 1 - y if fy else y, 1 - c if fc else c)
            cp = pltpu.make_async_remote_copy(
                src_ref=v_ref, dst_ref=gath.at[me], send_sem=send_sems.at[k - 1], recv_sem=recv_sems.at[k - 1],
                device_id=peer, device_id_type=MESH)
            cp.start()
            copies.append(cp)
        for cp in copies:
            cp.wait()
        acc = gath[0]
        for k in range(1, N_DEV):
            acc = acc + gath[k]
        o_ref[...] = acc

    vm = BS(memory_space=pltpu.VMEM)
    return pl.pallas_call(
        body, name=name, in_specs=[vm], out_specs=vm, out_shape=SDS((rows, cols), F32),
        scratch_shapes=[pltpu.VMEM((N_DEV, rows, cols), F32), pltpu.SemaphoreType.DMA((N_DEV - 1,)), pltpu.SemaphoreType.DMA((N_DEV - 1,))],
    )(v)


def _t5_bucket(dist):
    max_exact = N_REL_BUCKETS // 2
    d_f = jnp.maximum(dist, 1).astype(F32)
    large = max_exact + (jnp.log(d_f / max_exact) / math.log(REL_MAX_DISTANCE / max_exact) * (N_REL_BUCKETS - max_exact)).astype(jnp.int32)
    return jnp.where(dist < max_exact, dist, jnp.minimum(large, N_REL_BUCKETS - 1))


def _rel_buckets(dilation):
    qi = jnp.arange(ATTN_BLOCK)[:, None]
    kj = jnp.arange(2 * ATTN_BLOCK)[None, :]
    return _t5_bucket(jnp.clip(qi + ATTN_BLOCK - kj, 0, N_STEPS) * dilation)


def _layer_fwd(h, p, biases, lname):
    sv = {"h": h}
    xn1 = _rms_fwd(h, p["norm1_w"], name=lname + "norm1")
    proj = _matmul(xn1, p["w_main"], out_dtype=BF16, name=lname + "in_proj")
    dtr = _matmul(xn1, p["w_dt"], out_dtype=F32, name=lname + "in_proj_dt")
    xn1_rm, qkvs = [xn1], [proj]
    for g in range(1, N_GROUPS_ATTN):
        xn1_rm.append(_to_residue_major(xn1, DILATIONS[g]))
        qkvs.append(_matmul(xn1_rm[g], p["w_qkv"][g], out_dtype=BF16, name=f"{lname}in_proj_qkv{g}"))
    os_, lses, lses_rm = [], [], []
    for g, d in enumerate(DILATIONS):
        o, lse = _attn_fwd(qkvs[g], (0, 1, 2), biases[g], d, name=f"{lname}attn{g}")
        os_.append(_to_token_major(o, d))
        lses.append(_to_token_major(lse, d))
        lses_rm.append(lse)
    sv.update(xn1_rm=xn1_rm, qkvs=qkvs, lses_rm=lses_rm)
    attn = _combine_fwd(os_, lses, name=lname + "attn_combine")
    xc = _conv_fwd(proj, p["off_xbc"], p["conv_w"], p["conv_b"], name=lname + "conv")
    y, st = _ssd_fwd(xc, dtr, p["dt_bias"], p["a"], p["dskip"], name=lname + "ssd")
    ssm = _gate_norm_fwd(y, proj, p["off_z"], p["ssm_norm_w"], name=lname + "gate_norm")
    ga = _matmul(attn, p["w_attn_branch"], name=lname + "attn_branch")
    gs = _matmul(ssm, p["w_ssm_branch"], name=lname + "ssm_branch")
    merged = _merge_fwd(proj, p["off_gate"], ga, gs, name=lname + "merge")
    h1 = _matmul(merged, p["w_out"], res=h, name=lname + "out_proj")
    xn2 = _rms_fwd(h1, p["norm2_w"], name=lname + "norm2")
    u = _matmul(xn2, p["w_ffn_in"], out_dtype=BF16, name=lname + "ffn_in")
    f = _swiglu_fwd(u, name=lname + "swiglu")
    h2 = _matmul(f, p["w_ffn_out"], res=h1, name=lname + "ffn_out")
    sv.update(xn1=xn1, proj=proj, dtr=dtr, os=os_, lses=lses, attn=attn, xc=xc, y=y, st=st, ssm=ssm, ga=ga, gs=gs,
              merged=merged, h1=h1, xn2=xn2, u=u, f=f)
    return h2, sv


def _layer_bwd(dh2, p, sv, biases, lname):
    gr = {}
    lname = lname + "bwd_"
    df = _matmul(dh2, p["w_ffn_out"], tb=True, out_dtype=BF16, name=lname + "ffn_out_dx")
    gr["w_ffn_out"] = _matmul(sv["f"], dh2, ta=True, name=lname + "ffn_out_dw")
    dgate, dup = _swiglu_bwd(sv["u"], df, name=lname + "swiglu")
    du = jnp.concatenate([dgate, dup], axis=1)
    dxn2 = _matmul(du, p["w_ffn_in"], tb=True, name=lname + "ffn_in_dx")
    gr["w_ffn_in"] = _matmul(sv["xn2"], du, ta=True, name=lname + "ffn_in_dw")
    dh1, gr["norm2_w"] = _rms_bwd(sv["h1"], p["norm2_w"], [dxn2], dh2, name=lname + "norm2")
    dmerged = _matmul(dh1, p["w_out"], tb=True, name=lname + "out_proj_dx")
    gr["w_out"] = _matmul(sv["merged"], dh1, ta=True, name=lname + "out_proj_dw")
    dga, dgs, dg0, dg1 = _merge_bwd(sv["proj"], p["off_gate"], sv["ga"], sv["gs"], dmerged, name=lname + "merge")
    dattn = _matmul(dga, p["w_attn_branch"], tb=True, name=lname + "attn_branch_dx")
    gr["w_attn_branch"] = _matmul(sv["attn"], dga, ta=True, name=lname + "attn_branch_dw")
    dssm = _matmul(dgs, p["w_ssm_branch"], tb=True, name=lname + "ssm_branch_dx")
    gr["w_ssm_branch"] = _matmul(sv["ssm"], dgs, ta=True, name=lname + "ssm_branch_dw")
    dy, dz, gr["ssm_norm_w"] = _gate_norm_bwd(dssm, sv["y"], sv["proj"], p["off_z"], p["ssm_norm_w"], name=lname + "gate_norm")
    dxs, dbm, dcm, ddtr4, da4, dbias4, ddsk = _ssd_bwd(sv["xc"], sv["dtr"], p["dt_bias"], p["a"], p["dskip"], sv["st"], dy,
                                                       name=lname + "ssd")
    nsh = p["n_ssm_heads"]
    ddtr = jnp.sum(ddtr4, axis=0)
    gr["a_log"] = jnp.sum(da4, axis=(0, 1))[:nsh] * p["a"][0, :nsh]
    gr["dt_bias"] = jnp.sum(dbias4, axis=(0, 1))[:nsh]
    gr["d_skip"] = jnp.sum(ddsk.reshape(nsh, HEAD_DIM), axis=1)
    di = dxs.shape[1]
    dxbc, dcw, dcb = [], [], []
    for part, (lo, hi) in zip((dxs, dbm, dcm), ((0, di), (di, di + dbm.shape[1]), (di + dbm.shape[1], di + 2 * dbm.shape[1]))):
        dx_, dw_, db_ = _conv_bwd(sv["proj"], p["off_xbc"] + lo, p["conv_w"][:, lo:hi], p["conv_b"][lo:hi], part,
                                  name=f"{lname}conv{lo}")
        dxbc.append(dx_)
        dcw.append(dw_)
        dcb.append(db_)
    gr["conv_w"] = jnp.concatenate(dcw, axis=1)
    gr["conv_b"] = jnp.concatenate(dcb, axis=0)
    dos, corrs = _combine_bwd(dattn, sv["os"], sv["lses"], name=lname + "attn_combine")
    dqkvs, dbiases = [], []
    for g, d in enumerate(DILATIONS):
        dq, dk, dv, dbias = _attn_bwd(sv["qkvs"][g], (0, 1, 2), biases[g], sv["lses_rm"][g], _to_residue_major(dos[g], d),
                                      _to_residue_major(corrs[g], d), d, name=f"{lname}attn{g}")
        dqkvs.append([dq, dk, dv])
        dbiases.append(dbias)
    dmain = jnp.concatenate(dqkvs[0] + [dz] + dxbc + [dg0, dg1], axis=1)
    dxn1 = [_matmul(dmain, p["w_main"], tb=True, name=lname + "in_proj_dx"),
            _matmul(ddtr, p["w_dt"], tb=True, name=lname + "in_proj_dt_dx")]
    dw_main = _matmul(sv["xn1"], dmain, ta=True, name=lname + "in_proj_dw")
    dw_dt = _matmul(sv["xn1"], ddtr, ta=True, name=lname + "in_proj_dt_dw")
    dw_qkv = [dw_main]
    for g in range(1, N_GROUPS_ATTN):
        dqkv = jnp.concatenate(dqkvs[g], axis=1)
        dxn1.append(_to_token_major(_matmul(dqkv, p["w_qkv"][g], tb=True, name=f"{lname}in_proj_qkv{g}_dx"), DILATIONS[g]))
        dw_qkv.append(_matmul(sv["xn1_rm"][g], dqkv, ta=True, name=f"{lname}in_proj_qkv{g}_dw"))
    awg, og = dqkvs[0][0].shape[1], p["off_gate"]
    cols = [dw[:, i * awg:(i + 1) * awg] for i in range(3) for dw in dw_qkv]
    gr["w_in"] = jnp.concatenate(cols + [dw_main[:, 3 * awg:og], dw_dt[:, :nsh], dw_main[:, og:]], axis=1)
    dh, gr["norm1_w"] = _rms_bwd(sv["h"], p["norm1_w"], dxn1, dh1, name=lname + "norm1")
    return dh, gr, dbiases


def _layer_params(l, w, n_ssm_heads, hg):
    awg = hg * HEAD_DIM
    aw = N_GROUPS_ATTN * awg
    di = n_ssm_heads * HEAD_DIM
    xbc = di + 2 * SSM_GROUPS * D_STATE
    in_dt = 3 * aw + di + xbc
    w_in = w["w_in"][l]
    qkv_cols = lambda g: [w_in[:, (i * N_GROUPS_ATTN + g) * awg:(i * N_GROUPS_ATTN + g + 1) * awg] for i in range(3)]
    pad = lambda v: jnp.pad(v.astype(F32), (0, LANES - n_ssm_heads)).reshape(1, LANES)
    return dict(
        n_ssm_heads=n_ssm_heads, off_z=3 * awg, off_xbc=3 * awg + di, off_gate=3 * awg + di + xbc,
        w_main=jnp.concatenate(qkv_cols(0) + [w_in[:, 3 * aw:in_dt], w_in[:, in_dt + n_ssm_heads:]], axis=1),
        w_qkv=[None] + [jnp.concatenate(qkv_cols(g), axis=1) for g in range(1, N_GROUPS_ATTN)],
        w_dt=jnp.pad(w_in[:, in_dt:in_dt + n_ssm_heads], ((0, 0), (0, LANES - n_ssm_heads))),
        norm1_w=w["norm1_w"][l], norm2_w=w["norm2_w"][l], conv_w=w["conv_w"][l], conv_b=w["conv_b"][l],
        dt_bias=pad(w["dt_bias"][l]), a=pad(-jnp.exp(w["a_log"][l])),
        dskip=jnp.repeat(w["d_skip"][l], HEAD_DIM).reshape(1, di), ssm_norm_w=w["ssm_norm_w"][l],
        w_attn_branch=w["w_attn_branch"][l], w_ssm_branch=w["w_ssm_branch"][l], w_out=w["w_out"][l],
        w_ffn_in=w["w_ffn_in"][l], w_ffn_out=w["w_ffn_out"][l],
    )


def _local_step(x, tgt, w):
    depth = w["norm1_w"].shape[0]
    n_ssm_heads = w["dt_bias"].shape[1]
    hg = w["rel_bias"].shape[1] // N_GROUPS_ATTN
    onehots = [(_rel_buckets(dil)[:, :, None] == jnp.arange(N_REL_BUCKETS)[None, None, :]).astype(F32) for dil in DILATIONS]
    biases = [jnp.einsum("qkb,bh->hqk", oh, w["rel_bias"][:, g * hg:(g + 1) * hg].astype(F32), precision=HIGHEST)
              for g, oh in enumerate(onehots)]
    params = [_layer_params(l, w, n_ssm_heads, hg) for l in range(depth)]
    h = x
    saved = []
    for l in range(depth):
        h, sv = _layer_fwd(h, params[l], biases, f"l{l}_")
        saved.append(sv)
    loss, dh, g_final = _loss_head(h, w["final_norm_w"], tgt, name="loss_head")
    grads = [None] * depth
    dbias_tot = [jnp.zeros(b.shape, F32) for b in biases]
    for l in reversed(range(depth)):
        dh, grads[l], dbiases = _layer_bwd(dh, params[l], saved[l], biases, f"l{l}_")
        dbias_tot = [a + b for a, b in zip(dbias_tot, dbiases)]
    out = {k: [gl[k] for gl in grads] if k in MATRICES else jnp.stack([gl[k] for gl in grads]) for k in grads[0]}
    out["final_norm_w"] = g_final
    drel = []
    for g, (oh, db) in enumerate(zip(onehots, dbias_tot)):
        oh_t = jnp.pad(oh.reshape(-1, N_REL_BUCKETS).T, ((0, LANES - N_REL_BUCKETS), (0, 0)))
        db_rows = jnp.pad(db.reshape(hg, -1), ((0, LANES - hg), (0, 0)))
        drel.append(_matmul(oh_t, db_rows, tb=True, name=f"rel_bias_fold{g}")[:N_REL_BUCKETS, :hg])
    out["rel_bias"] = jnp.concatenate(drel, axis=1)
    return loss, dh, out


MATRICES = ("w_in", "w_attn_branch", "w_ssm_branch", "w_out", "w_ffn_in", "w_ffn_out")
COL_SHARDED = ("w_in", "w_attn_branch", "w_ffn_in")
SMALL = ("norm1_w", "conv_b", "dt_bias", "a_log", "d_skip", "ssm_norm_w", "norm2_w", "rel_bias", "final_norm_w")
WEIGHTS = ("norm1_w", "w_in", "conv_w", "conv_b", "dt_bias", "a_log", "d_skip", "ssm_norm_w", "w_attn_branch",
           "w_ssm_branch", "w_out", "norm2_w", "w_ffn_in", "w_ffn_out", "rel_bias", "final_norm_w")
SMALL_COLS = 1024


def _unshard(name, g):
    _, depth, r, c = g.shape
    if name in COL_SHARDED or name == "conv_w":
        return jnp.transpose(g, (1, 2, 0, 3)).reshape(depth, r, N_CHIPS * c)
    return jnp.transpose(g, (1, 0, 2, 3)).reshape(depth, N_CHIPS * r, c)


def _to_shards(name, g):
    r, c = g.shape
    if name in COL_SHARDED:
        return jnp.transpose(g.reshape(r, N_CHIPS, c // N_CHIPS), (1, 0, 2))
    return g.reshape(N_CHIPS, r // N_CHIPS, c)


def kernel(x, norm1_w, w_in, conv_w, conv_b, dt_bias, a_log, d_skip, ssm_norm_w, w_attn_branch, w_ssm_branch, w_out, norm2_w, w_ffn_in, w_ffn_out, rel_bias, final_norm_w, loss_target, m_norm1_w, m_w_in, m_conv_w, m_conv_b, m_dt_bias, m_a_log, m_d_skip, m_ssm_norm_w, m_w_attn_branch, m_w_ssm_branch, m_w_out, m_norm2_w, m_w_ffn_in, m_w_ffn_out, m_rel_bias, m_final_norm_w, v_norm1_w, v_w_in, v_conv_w, v_conv_b, v_dt_bias, v_a_log, v_d_skip, v_ssm_norm_w, v_w_attn_branch, v_w_ssm_branch, v_w_out, v_norm2_w, v_w_ffn_in, v_w_ffn_out, v_rel_bias, v_final_norm_w):
    env = dict(locals())
    wts = {k: env[k] for k in WEIGHTS}
    mom = {k: env["m_" + k] for k in WEIGHTS}
    var = {k: env["v_" + k] for k in WEIGHTS}
    chip = 2 * lax.axis_index("x") + lax.axis_index("y")
    core = lax.axis_index("c")

    shards = [wts[k].astype(BF16) for k in MATRICES] + [conv_w]
    gathered = _gather_shards(shards, name="gather_weights")
    full = {k: wts[k] for k in SMALL}
    for k, own, g in zip(MATRICES + ("conv_w",), shards, gathered):
        full[k] = _unshard(k, lax.dynamic_update_index_in_dim(g, own, chip, axis=0))

    loss, dx, grads = _local_step(x[0], loss_target[0], full)
    loss = lax.psum(loss, ("x", "y", "c"))

    core1 = core.reshape(1).astype(jnp.int32)
    from_pair = _pair_swap_layers([grads[k][0] for k in MATRICES], [grads[k][1] for k in MATRICES], name="reduce_pair_swap")
    scatter_in = [_to_shards(k, _add_own_layer(grads[k][0], grads[k][1], got, core1, name="reduce_pair_add_" + k))
                  for k, got in zip(MATRICES, from_pair)]
    scattered = _scatter_to_chips(scatter_in, name="reduce_scatter")
    own_layer = []
    for k, sent, got in zip(MATRICES, scatter_in, scattered):
        got = lax.dynamic_update_index_in_dim(got, lax.dynamic_index_in_dim(sent, chip, axis=0, keepdims=False), chip, axis=0)
        own_layer.append(_sum_chips(got, name="reduce_sum_" + k))
    other_layer = _pair_swap(own_layer, name="reduce_pair_exchange")
    reduced = {}

    small_names = SMALL + ("conv_w",)
    flat = jnp.concatenate([grads[k].reshape(-1) for k in small_names])
    n_small = flat.shape[0]
    rows = -(-n_small // SMALL_COLS)
    rows = -(-rows // 8) * 8
    flat = jnp.pad(flat, (0, rows * SMALL_COLS - n_small)).reshape(rows, SMALL_COLS)
    flat = _allreduce_small(flat, name="allreduce_small").reshape(-1)
    pos = 0
    for k in small_names:
        size = math.prod(grads[k].shape)
        reduced[k] = flat[pos:pos + size].reshape(grads[k].shape)
        pos += size
    cs = conv_w.shape[2]
    reduced["conv_w"] = lax.dynamic_slice_in_dim(reduced["conv_w"], chip * cs, cs, axis=2)

    delta, new_m, new_v = {}, {}, {}
    for k, own, other in zip(MATRICES, own_layer, other_layer):
        reduced[k], delta[k], new_m[k], new_v[k] = _adamw_layers(wts[k], own, other, mom[k], var[k], core1, name="adamw_" + k)
    pack = lambda src: jnp.pad(jnp.concatenate([src[k].reshape(-1) for k in small_names]),
                               (0, rows * SMALL_COLS - n_shard)).reshape(rows, SMALL_COLS)
    n_shard = sum(math.prod(wts[k].shape) for k in small_names)
    d_, m_, v_ = _adamw(pack(wts), pack(reduced), pack(mom), pack(var), name="adamw_small")
    pos = 0
    for k in small_names:
        size = math.prod(wts[k].shape)
        for dst, src in ((delta, d_), (new_m, m_), (new_v, v_)):
            dst[k] = src.reshape(-1)[pos:pos + size].reshape(wts[k].shape)
        pos += size

    return (loss, dx[None], *[reduced[k] for k in WEIGHTS], *[delta[k] for k in WEIGHTS],
            *[new_m[k] for k in WEIGHTS], *[new_v[k] for k in WEIGHTS])
```

```python
import functools
import math

import jax
import jax.numpy as jnp
from jax import lax
from jax.experimental import pallas as pl
from jax.experimental.pallas import tpu as pltpu

F32, BF16 = jnp.float32, jnp.bfloat16
SDS = jax.ShapeDtypeStruct
BS = pl.BlockSpec
MESH = pl.DeviceIdType.MESH
HIGHEST = lax.Precision.HIGHEST

EPS = 1e-6
HEAD_DIM = 64
ATTN_BLOCK = 128
DILATIONS = (1, 4, 16)
N_GROUPS_ATTN = len(DILATIONS)
N_STEPS = 128
N_REL_BUCKETS = 32
REL_MAX_DISTANCE = 2048
SSM_GROUPS = 4
D_STATE = 128
CHUNK = 128
CONV_WIDTH = 4
HALO = 16
LANES = 128
N_CHIPS = 4
N_DEV = 8
VMEM_LIMIT_BYTES = 48 * 1024 * 1024

ADAM_LR, ADAM_B1, ADAM_B2, ADAM_EPS, ADAM_WD, ADAM_STEP = 0.001, 0.9, 0.999, 1e-08, 0.01, 10

NT = (((1,), (1,)), ((), ()))
TN = (((0,), (0,)), ((), ()))
NN = (((1,), (0,)), ((), ()))


def _cp(*sem):
    return pltpu.CompilerParams(dimension_semantics=sem if sem else None, vmem_limit_bytes=VMEM_LIMIT_BYTES)


def _pick(n, cands):
    for c in cands:
        if n % c == 0:
            return c
    raise ValueError(f"no block size of {cands} divides {n}")


def _divisors(n, cap):
    out = [c for c in range(LANES, min(n, cap) + 1, LANES) if n % c == 0]
    return out or [n]


def _wide(n, cap=2048):
    return _divisors(n, cap)[-1]


MXU_FLOPS = 9.0e14
HBM_BYTES_PER_S = 3.0e12
ACC_BYTES_PER_S = 4.0e12
GRID_STEP_S = 0.4e-6
TILE_VMEM_BYTES = 36 * 1024 * 1024


def _matmul_tiles(m, n, k, a_bytes, b_bytes, o_bytes, has_res):
    best = None
    for tm in _divisors(m, 2048):
        for tn in _divisors(n, 2048):
            for tk in _divisors(k, 4096):
                ni, nj, nk = m // tm, n // tn, k // tk
                vmem = 2 * (tm * tk * a_bytes + tk * tn * b_bytes + tm * tn * (o_bytes + (4 if has_res else 0)))
                vmem += tm * tn * 4 * (2 if nk > 1 else 1) + (tm * tk + tk * tn) * 2
                if vmem > TILE_VMEM_BYTES:
                    continue
                hbm = m * k * a_bytes * (nj if nk > 1 else 1) + k * n * b_bytes * (ni if nj * nk > 1 else 1)
                hbm += m * n * (o_bytes + (4 if has_res else 0))
                t = max(2.0 * m * n * k / MXU_FLOPS, hbm / HBM_BYTES_PER_S) + ni * nj * nk * GRID_STEP_S
                if nk > 1:
                    t += m * n * 8.0 * nk / ACC_BYTES_PER_S
                if best is None or t < best[0]:
                    best = (t, tm, tn, tk)
    assert best is not None, (m, n, k)
    return best[1:]


def _dot(a, b, dims=NN, precision=None):
    return lax.dot_general(a, b, dims, precision=precision, preferred_element_type=F32)


def _silu(x):
    return x / (1.0 + jnp.exp(-x))


def _sigmoid(x):
    return 1.0 / (1.0 + jnp.exp(-x))


def _dsilu(x):
    s = _sigmoid(x)
    return s * (1.0 + x * (1.0 - s))


def _matmul(a, b, *, name, ta=False, tb=False, out_dtype=F32, res=None):
    (kdim, m) = a.shape if ta else a.shape[::-1]
    (n, k2) = b.shape if tb else b.shape[::-1]
    assert kdim == k2, (a.shape, b.shape, ta, tb)
    tm, tn, tk = _matmul_tiles(m, n, kdim, a.dtype.itemsize, b.dtype.itemsize, jnp.dtype(out_dtype).itemsize, res is not None)
    nk = kdim // tk
    a_spec = BS((tk, tm), lambda i, j, k: (k, i)) if ta else BS((tm, tk), lambda i, j, k: (i, k))
    b_spec = BS((tn, tk), lambda i, j, k: (j, k)) if tb else BS((tk, tn), lambda i, j, k: (k, j))
    dims = (((0 if ta else 1,), (1 if tb else 0,)), ((), ()))
    has_res = res is not None

    def body(*refs):
        a_ref, b_ref = refs[:2]
        r_ref = refs[2] if has_res else None
        o_ref = refs[3] if has_res else refs[2]
        prod = _dot(a_ref[...].astype(BF16), b_ref[...].astype(BF16), dims)
        if nk == 1:
            o_ref[...] = (prod + r_ref[...] if has_res else prod).astype(o_ref.dtype)
            return
        acc = refs[-1]
        k = pl.program_id(2)

        @pl.when(k == 0)
        def _():
            acc[...] = prod

        @pl.when(k > 0)
        def _():
            acc[...] += prod

        @pl.when(k == nk - 1)
        def _():
            r = acc[...]
            if has_res:
                r = r + r_ref[...]
            o_ref[...] = r.astype(o_ref.dtype)

    in_specs = [a_spec, b_spec]
    args = [a, b]
    if has_res:
        in_specs.append(BS((tm, tn), lambda i, j, k: (i, j)))
        args.append(res)
    return pl.pallas_call(
        body, name=name, grid=(m // tm, n // tn, nk), in_specs=in_specs,
        out_specs=BS((tm, tn), lambda i, j, k: (i, j)), out_shape=SDS((m, n), out_dtype),
        scratch_shapes=[pltpu.VMEM((tm, tn), F32)] if nk > 1 else [],
        compiler_params=_cp("parallel", "parallel", "arbitrary"),
    )(*args)


def _rms_fwd(h, w, *, name):
    t, d = h.shape
    tm = _pick(t, (512, 256, 128))

    def body(h_ref, w_ref, o_ref):
        x = h_ref[...]
        r = lax.rsqrt(jnp.mean(x * x, axis=-1, keepdims=True) + EPS)
        o_ref[...] = (x * r * w_ref[...]).astype(BF16)

    return pl.pallas_call(
        body, name=name, grid=(t // tm,), in_specs=[BS((tm, d), lambda i: (i, 0)), BS((1, d), lambda i: (0, 0))],
        out_specs=BS((tm, d), lambda i: (i, 0)), out_shape=SDS((t, d), BF16), compiler_params=_cp("parallel"),
    )(h, w.reshape(1, d))


def _rms_bwd(h, w, dys, dres, *, name):
    t, d = h.shape
    tm = _pick(t, (512, 256, 128))
    n_dy = len(dys)

    def body(*refs):
        h_ref, w_ref = refs[:2]
        dy_refs = refs[2:2 + n_dy]
        dres_ref, dh_ref, dw_ref = refs[2 + n_dy:]
        x = h_ref[...]
        dy = dy_refs[0][...]
        for r_ in dy_refs[1:]:
            dy = dy + r_[...]
        r = lax.rsqrt(jnp.mean(x * x, axis=-1, keepdims=True) + EPS)
        g = dy * w_ref[...]
        proj = jnp.sum(g * x, axis=-1, keepdims=True) * (1.0 / d)
        dh_ref[...] = dres_ref[...] + r * g - x * (r * r * r) * proj

        @pl.when(pl.program_id(0) == 0)
        def _():
            dw_ref[...] = jnp.zeros_like(dw_ref)

        dw_ref[...] += jnp.sum(dy * x * r, axis=0, keepdims=True)

    row = BS((tm, d), lambda i: (i, 0))
    vec = BS((1, d), lambda i: (0, 0))
    dh, dw = pl.pallas_call(
        body, name=name, grid=(t // tm,), in_specs=[row, vec] + [row] * n_dy + [row],
        out_specs=[row, vec], out_shape=[SDS((t, d), F32), SDS((1, d), F32)], compiler_params=_cp("arbitrary"),
    )(h, w.reshape(1, d), *dys, dres)
    return dh, dw[0]


def _loss_head(h, w, tgt, *, name):
    t, d = h.shape
    tm = _pick(t, (512, 256, 128))

    def body(h_ref, w_ref, t_ref, loss_ref, dh_ref, dw_ref):
        x = h_ref[...]
        r = lax.rsqrt(jnp.mean(x * x, axis=-1, keepdims=True) + EPS)
        err = x * r * w_ref[...] - t_ref[...]
        loss_ref[...] = jnp.zeros(loss_ref.shape, F32) + 0.5 * jnp.sum(err * err) * (1.0 / d)
        dy = err * (1.0 / d)
        g = dy * w_ref[...]
        proj = jnp.sum(g * x, axis=-1, keepdims=True) * (1.0 / d)
        dh_ref[...] = r * g - x * (r * r * r) * proj

        @pl.when(pl.program_id(0) == 0)
        def _():
            dw_ref[...] = jnp.zeros_like(dw_ref)

        dw_ref[...] += jnp.sum(dy * x * r, axis=0, keepdims=True)

    row = BS((tm, d), lambda i: (i, 0))
    vec = BS((1, d), lambda i: (0, 0))
    loss, dh, dw = pl.pallas_call(
        body, name=name, grid=(t // tm,), in_specs=[row, vec, row],
        out_specs=[BS((1, 8, LANES), lambda i: (i, 0, 0)), row, vec],
        out_shape=[SDS((t // tm, 8, LANES), F32), SDS((t, d), F32), SDS((1, d), F32)], compiler_params=_cp("arbitrary"),
    )(h, w.reshape(1, d), tgt)
    return jnp.sum(loss[:, 0, 0]), dh, dw[0]


def _attn_masks(mb):
    qi = lax.broadcasted_iota(jnp.int32, (ATTN_BLOCK, 2 * ATTN_BLOCK), 0)
    kj = lax.broadcasted_iota(jnp.int32, (ATTN_BLOCK, 2 * ATTN_BLOCK), 1)
    steps = qi + ATTN_BLOCK - kj
    valid = (steps >= 0) & (steps <= N_STEPS) & ((kj >= ATTN_BLOCK) | (mb > 0))
    low = lax.broadcasted_iota(jnp.int32, (ATTN_BLOCK, LANES), 1) < HEAD_DIM
    return valid, low


def _to_residue_major(a, d):
    t, c = a.shape
    return a if d == 1 else a.reshape(t // d, d, c).transpose(1, 0, 2).reshape(t, c)


def _to_token_major(a, d):
    t, c = a.shape
    return a if d == 1 else a.reshape(d, t // d, c).transpose(1, 0, 2).reshape(t, c)


def _attn_specs(cols, nb, sub, awg, clamp):
    nsb = nb // sub

    def cur(col):
        return BS((sub * ATTN_BLOCK, awg), lambda r, sb: (r * nsb + clamp(sb), col))

    def prev(col):
        return BS((ATTN_BLOCK, awg), lambda r, sb: (r * nb + jnp.maximum(clamp(sb) * sub - 1, 0), col))

    return [cur(cols[0]), cur(cols[1]), prev(cols[1]), cur(cols[2]), prev(cols[2])]


def _sub_rows(s, n=1):
    return pl.ds(pl.multiple_of(s * ATTN_BLOCK, ATTN_BLOCK), n * ATTN_BLOCK)


def _attn_fwd(qkv, cols, bias, d, *, name):
    t = qkv.shape[0]
    hg = bias.shape[0]
    awg = hg * HEAD_DIM
    nb = t // d // ATTN_BLOCK
    sub = _pick(nb, (4, 2, 1))
    rows = sub * ATTN_BLOCK
    scale = HEAD_DIM ** -0.5

    def body(q_ref, kc_ref, kp_ref, vc_ref, vp_ref, b_ref, o_ref, l_ref, kext, vext):
        sb = pl.program_id(1)
        kext[pl.ds(0, ATTN_BLOCK), :] = kp_ref[...]
        kext[pl.ds(ATTN_BLOCK, rows), :] = kc_ref[...]
        vext[pl.ds(0, ATTN_BLOCK), :] = vp_ref[...]
        vext[pl.ds(ATTN_BLOCK, rows), :] = vc_ref[...]

        def one_block(s, carry):
            valid, low = _attn_masks(sb * sub + s)
            for pi in range(awg // LANES):
                sl = slice(pi * LANES, (pi + 1) * LANES)
                q2 = q_ref[_sub_rows(s), sl]
                k2 = kext[_sub_rows(s, 2), sl]
                v2 = vext[_sub_rows(s, 2), sl]
                outs, lses = [], []
                for hh in range(2):
                    mh = low if hh == 0 else jnp.logical_not(low)
                    qm = jnp.where(mh, q2, jnp.zeros_like(q2))
                    sc = _dot(qm, k2, NT) * scale + b_ref[pi * 2 + hh]
                    sc = jnp.where(valid, sc, -jnp.inf)
                    m = jnp.max(sc, axis=-1, keepdims=True)
                    p = jnp.exp(sc - m)
                    den = jnp.sum(p, axis=-1, keepdims=True)
                    outs.append(_dot(p.astype(BF16), v2) / den)
                    lses.append(jnp.broadcast_to(m + jnp.log(den), (ATTN_BLOCK, LANES)))
                o_ref[_sub_rows(s), sl] = jnp.where(low, outs[0], outs[1])
                l_ref[_sub_rows(s), sl] = jnp.where(low, lses[0], lses[1])
            return carry

        lax.fori_loop(0, sub, one_block, 0)

    out_spec = BS((rows, awg), lambda r, sb: (r * (nb // sub) + sb, 0))
    return pl.pallas_call(
        body, name=name, grid=(d, nb // sub),
        in_specs=_attn_specs(cols, nb, sub, awg, lambda sb: sb) + [BS(bias.shape, lambda r, sb: (0, 0, 0))],
        out_specs=[out_spec, out_spec], out_shape=[SDS((t, awg), F32)] * 2,
        scratch_shapes=[pltpu.VMEM((rows + ATTN_BLOCK, awg), BF16)] * 2, compiler_params=_cp("parallel", "parallel"),
    )(*([qkv] * 5), bias)


def _attn_bwd(qkv, cols, bias, lse, do, corr, d, *, name):
    t = qkv.shape[0]
    hg = bias.shape[0]
    awg = hg * HEAD_DIM
    nb = t // d // ATTN_BLOCK
    sub = _pick(nb, (4, 2, 1))
    nsb = nb // sub
    rows = sub * ATTN_BLOCK
    scale = HEAD_DIM ** -0.5

    def body(q_ref, kc_ref, kp_ref, vc_ref, vp_ref, b_ref, l_ref, do_ref, c_ref, dq_ref, dk_ref, dv_ref, db_ref,
             kext, vext, dkext, dvext, ck, cv):
        r, sb = pl.program_id(0), pl.program_id(1)

        @pl.when((r == 0) & (sb == 0))
        def _():
            db_ref[...] = jnp.zeros_like(db_ref)

        @pl.when(sb == 0)
        def _():
            ck[...] = jnp.zeros_like(ck)
            cv[...] = jnp.zeros_like(cv)

        @pl.when(sb < nsb)
        def _():
            kext[pl.ds(0, ATTN_BLOCK), :] = kp_ref[...]
            kext[pl.ds(ATTN_BLOCK, rows), :] = kc_ref[...]
            vext[pl.ds(0, ATTN_BLOCK), :] = vp_ref[...]
            vext[pl.ds(ATTN_BLOCK, rows), :] = vc_ref[...]
            dkext[...] = jnp.zeros_like(dkext)
            dvext[...] = jnp.zeros_like(dvext)

            def one_block(s, carry):
                valid, low = _attn_masks(sb * sub + s)
                for pi in range(awg // LANES):
                    sl = slice(pi * LANES, (pi + 1) * LANES)
                    q2 = q_ref[_sub_rows(s), sl]
                    k2 = kext[_sub_rows(s, 2), sl]
                    v2 = vext[_sub_rows(s, 2), sl]
                    do2 = do_ref[_sub_rows(s), sl]
                    lse2 = l_ref[_sub_rows(s), sl]
                    corr2 = c_ref[_sub_rows(s), sl]
                    dk2 = jnp.zeros((2 * ATTN_BLOCK, LANES), F32)
                    dv2 = jnp.zeros((2 * ATTN_BLOCK, LANES), F32)
                    dqs = []
                    for hh in range(2):
                        mh = low if hh == 0 else jnp.logical_not(low)
                        qm = jnp.where(mh, q2, jnp.zeros_like(q2))
                        dom = jnp.where(mh, do2, jnp.zeros_like(do2))
                        lse_c = jnp.max(jnp.where(mh, lse2, -jnp.inf), axis=-1, keepdims=True)
                        corr_c = jnp.max(jnp.where(mh, corr2, -jnp.inf), axis=-1, keepdims=True)
                        sc = _dot(qm, k2, NT) * scale + b_ref[pi * 2 + hh]
                        p = jnp.exp(jnp.where(valid, sc, -jnp.inf) - lse_c)
                        ds = p * (_dot(dom, v2, NT) + corr_c)
                        db_ref[pi * 2 + hh] += ds
                        dsb = ds.astype(BF16)
                        dqs.append(_dot(dsb, k2) * scale)
                        dk2 = dk2 + _dot(dsb, qm, TN) * scale
                        dv2 = dv2 + _dot(p.astype(BF16), dom, TN)
                    dq_ref[_sub_rows(s), sl] = jnp.where(low, dqs[0], dqs[1]).astype(BF16)
                    dkext[_sub_rows(s, 2), sl] += dk2
                    dvext[_sub_rows(s, 2), sl] += dv2
                return carry

            lax.fori_loop(0, sub, one_block, 0)
            head, tail = pl.ds(0, rows - ATTN_BLOCK), pl.ds(rows - ATTN_BLOCK, ATTN_BLOCK)
            for out_ref, carry_ref, ext in ((dk_ref, ck, dkext), (dv_ref, cv, dvext)):
                if sub > 1:
                    out_ref[head, :] = carry_ref[head, :].astype(BF16)
                out_ref[tail, :] = (carry_ref[tail, :] + ext[pl.ds(0, ATTN_BLOCK), :]).astype(BF16)
                carry_ref[...] = ext[pl.ds(ATTN_BLOCK, rows), :]

        @pl.when(sb == nsb)
        def _():
            dk_ref[...] = ck[...].astype(BF16)
            dv_ref[...] = cv[...].astype(BF16)

    clamp = lambda sb: jnp.minimum(sb, nsb - 1)
    cur = BS((rows, awg), lambda r, sb: (r * nsb + clamp(sb), 0))
    prev = BS((rows, awg), lambda r, sb: (r * nsb + jnp.maximum(sb - 1, 0), 0))
    bias_spec = BS(bias.shape, lambda r, sb: (0, 0, 0))
    return pl.pallas_call(
        body, name=name, grid=(d, nsb + 1),
        in_specs=_attn_specs(cols, nb, sub, awg, clamp) + [bias_spec, cur, cur, cur],
        out_specs=[cur, prev, prev, bias_spec],
        out_shape=[SDS((t, awg), BF16)] * 3 + [SDS(bias.shape, F32)],
        scratch_shapes=[pltpu.VMEM((rows + ATTN_BLOCK, awg), BF16)] * 2 + [pltpu.VMEM((rows + ATTN_BLOCK, awg), F32)] * 2
        + [pltpu.VMEM((rows, awg), F32)] * 2,
        compiler_params=_cp("arbitrary", "arbitrary"),
    )(*([qkv] * 5), bias, lse, do, corr)


def _head_sum(x, low):
    a = jnp.sum(jnp.where(low, x, 0.0), axis=-1, keepdims=True)
    b = jnp.sum(jnp.where(low, 0.0, x), axis=-1, keepdims=True)
    return jnp.where(low, a, b)


def _combine_weights(lses):
    mx = jnp.maximum(jnp.maximum(lses[0], lses[1]), lses[2])
    es = [jnp.exp(l - mx) for l in lses]
    tot = es[0] + es[1] + es[2]
    return [e / tot for e in es]


def _combine_fwd(os_, lses, *, name):
    t, awg = os_[0].shape
    tm = _pick(t, (512, 256, 128))

    def body(o0, o1, o2, l0, l1, l2, out_ref):
        al = _combine_weights([l0[...], l1[...], l2[...]])
        out_ref[...] = (al[0] * o0[...] + al[1] * o1[...] + al[2] * o2[...]).astype(BF16)

    blk = BS((tm, awg), lambda i: (i, 0))
    return pl.pallas_call(
        body, name=name, grid=(t // tm,), in_specs=[blk] * 6, out_specs=blk,
        out_shape=SDS((t, awg), BF16), compiler_params=_cp("parallel"),
    )(*os_, *lses)


def _combine_bwd(dattn, os_, lses, *, name):
    t, awg = dattn.shape
    tm = _pick(t, (512, 256, 128))

    def body(da_ref, o0, o1, o2, l0, l1, l2, d0, d1, d2, c0, c1, c2):
        low = lax.broadcasted_iota(jnp.int32, (tm, LANES), 1) < HEAD_DIM
        for pi in range(awg // LANES):
            sl = slice(pi * LANES, (pi + 1) * LANES)
            da = da_ref[:, sl]
            al = _combine_weights([l0[:, sl], l1[:, sl], l2[:, sl]])
            tot = jnp.zeros((tm, LANES), F32)
            for a, o in zip(al, (o0, o1, o2)):
                tot = tot + a * _head_sum(da * o[:, sl], low)
            for a, d_ref, c_ref in zip(al, (d0, d1, d2), (c0, c1, c2)):
                d_ref[:, sl] = (a * da).astype(BF16)
                c_ref[:, sl] = -a * tot

    blk = BS((tm, awg), lambda i: (i, 0))
    outs = pl.pallas_call(
        body, name=name, grid=(t // tm,), in_specs=[blk] * 7, out_specs=[blk] * 6,
        out_shape=[SDS((t, awg), BF16)] * 3 + [SDS((t, awg), F32)] * 3, compiler_params=_cp("parallel"),
    )(dattn, *os_, *lses)
    return outs[:3], outs[3:]


def _conv_block(width, *offsets):
    for c in (512, 256, 128):
        if width % c == 0 and all(o % c == 0 for o in offsets):
            return c
    raise ValueError((width, offsets))


CONV_ROWS = 32


def _conv_pre(x_ref, halo_ref, w_ref, b_ref, ext, i, tm):
    ext[pl.ds(0, HALO), :] = jnp.where(i > 0, halo_ref[...].astype(F32), 0.0)
    ext[pl.ds(HALO, tm), :] = x_ref[...].astype(F32)
    taps = [w_ref[pl.ds(k, 1), :] for k in range(CONV_WIDTH)]
    bias = b_ref[...]
    for r0 in range(0, tm, CONV_ROWS):
        xs = [ext[pl.ds(HALO + r0 - (CONV_WIDTH - 1) + k, CONV_ROWS), :] for k in range(CONV_WIDTH)]
        pre = bias + taps[0] * xs[0]
        for k in range(1, CONV_WIDTH):
            pre = pre + taps[k] * xs[k]
        yield r0, pre, xs


def _fold8(v):
    return jnp.sum(v.reshape(v.shape[0] // 8, 8, v.shape[1]), axis=0)


def _conv_fwd(proj, off, w, b, *, name):
    t = proj.shape[0]
    c = w.shape[1]
    cw = _conv_block(c, off)
    tm = _pick(t, (512, 256, 128))
    ob = off // cw

    def body(x_ref, halo_ref, w_ref, b_ref, o_ref, ext):
        for r0, pre, _ in _conv_pre(x_ref, halo_ref, w_ref, b_ref, ext, pl.program_id(1), tm):
            o_ref[pl.ds(r0, CONV_ROWS), :] = _silu(pre).astype(BF16)

    return pl.pallas_call(
        body, name=name, grid=(c // cw, t // tm),
        in_specs=[BS((tm, cw), lambda j, i: (i, ob + j)),
                  BS((HALO, cw), lambda j, i: (jnp.maximum(i * (tm // HALO) - 1, 0), ob + j)),
                  BS((CONV_WIDTH, cw), lambda j, i: (0, j)), BS((1, cw), lambda j, i: (0, j))],
        out_specs=BS((tm, cw), lambda j, i: (i, j)), out_shape=SDS((t, c), BF16),
        scratch_shapes=[pltpu.VMEM((HALO + tm, cw), F32)], compiler_params=_cp("parallel", "arbitrary"),
    )(proj, proj, w, b.reshape(1, c))


def _conv_bwd(proj, off, w, b, dxc, *, name):
    t = proj.shape[0]
    c = w.shape[1]
    cw = _conv_block(c, off)
    tm = _pick(t, (512, 256, 128))
    ob = off // cw
    nt = t // tm

    def body_pre(x_ref, halo_ref, w_ref, b_ref, d_ref, dp_ref, dw_ref, db_ref, ext):
        i = pl.program_id(1)

        @pl.when(i == 0)
        def _():
            dw_ref[...] = jnp.zeros_like(dw_ref)
            db_ref[...] = jnp.zeros_like(db_ref)

        db_acc = jnp.zeros((8, cw), F32)
        dw_acc = [jnp.zeros((8, cw), F32) for _ in range(CONV_WIDTH)]
        for r0, pre, xs in _conv_pre(x_ref, halo_ref, w_ref, b_ref, ext, i, tm):
            dpre = d_ref[pl.ds(r0, CONV_ROWS), :] * _dsilu(pre)
            dp_ref[pl.ds(r0, CONV_ROWS), :] = dpre
            db_acc = db_acc + _fold8(dpre)
            dw_acc = [acc + _fold8(dpre * x) for acc, x in zip(dw_acc, xs)]
        db_ref[...] += jnp.sum(db_acc, axis=0, keepdims=True)
        for k in range(CONV_WIDTH):
            dw_ref[pl.ds(k, 1), :] += jnp.sum(dw_acc[k], axis=0, keepdims=True)

    dpre, dw, db = pl.pallas_call(
        body_pre, name=name + "_pre", grid=(c // cw, nt),
        in_specs=[BS((tm, cw), lambda j, i: (i, ob + j)),
                  BS((HALO, cw), lambda j, i: (jnp.maximum(i * (tm // HALO) - 1, 0), ob + j)),
                  BS((CONV_WIDTH, cw), lambda j, i: (0, j)), BS((1, cw), lambda j, i: (0, j)),
                  BS((tm, cw), lambda j, i: (i, j))],
        out_specs=[BS((tm, cw), lambda j, i: (i, j)), BS((CONV_WIDTH, cw), lambda j, i: (0, j)), BS((1, cw), lambda j, i: (0, j))],
        out_shape=[SDS((t, c), F32), SDS((CONV_WIDTH, c), F32), SDS((1, c), F32)],
        scratch_shapes=[pltpu.VMEM((HALO + tm, cw), F32)], compiler_params=_cp("parallel", "arbitrary"),
    )(proj, proj, w, b.reshape(1, c), dxc)

    def body_in(dp_ref, nxt_ref, w_ref, dx_ref, ext):
        i = pl.program_id(1)
        ext[pl.ds(0, tm), :] = dp_ref[...]
        ext[pl.ds(tm, 8), :] = jnp.where(i < nt - 1, nxt_ref[...], 0.0)
        taps = [w_ref[pl.ds(k, 1), :] for k in range(CONV_WIDTH)]
        for r0 in range(0, tm, CONV_ROWS):
            dx = taps[CONV_WIDTH - 1] * ext[pl.ds(r0, CONV_ROWS), :]
            for k in range(CONV_WIDTH - 1):
                dx = dx + taps[k] * ext[pl.ds(r0 + CONV_WIDTH - 1 - k, CONV_ROWS), :]
            dx_ref[pl.ds(r0, CONV_ROWS), :] = dx.astype(BF16)

    dx = pl.pallas_call(
        body_in, name=name + "_in", grid=(c // cw, nt),
        in_specs=[BS((tm, cw), lambda j, i: (i, j)),
                  BS((8, cw), lambda j, i: (jnp.minimum((i + 1) * (tm // 8), t // 8 - 1), j)),
                  BS((CONV_WIDTH, cw), lambda j, i: (0, j))],
        out_specs=BS((tm, cw), lambda j, i: (i, j)), out_shape=SDS((t, c), BF16),
        scratch_shapes=[pltpu.VMEM((tm + 8, cw), F32)], compiler_params=_cp("parallel", "arbitrary"),
    )(dpre, dpre, w)
    return dx, dw, db[0]


def _softplus(x):
    return jnp.maximum(x, 0.0) + jnp.log(1.0 + jnp.exp(-jnp.abs(x)))


def _ssd_common(dtr_ref, bias_ref, a_ref):
    pre = dtr_ref[...] + bias_ref[...]
    dt = _softplus(pre)
    ri = lax.broadcasted_iota(jnp.int32, (CHUNK, CHUNK), 0)
    ci = lax.broadcasted_iota(jnp.int32, (CHUNK, CHUNK), 1)
    tril = ri >= ci
    la = _dot(tril.astype(F32), dt * a_ref[...], precision=HIGHEST)
    return pre, dt, la, la.T, tril


def _lane_col(x, lane, h):
    return jnp.sum(jnp.where(lane == h, x, 0.0), axis=-1, keepdims=True)


def _ssd_specs(rows, di, gw, cidx):
    nbx = di // LANES
    return [BS((rows, gw), lambda g, c: (cidx(c), g)),
            BS((rows, D_STATE), lambda g, c: (cidx(c), nbx + g)),
            BS((rows, D_STATE), lambda g, c: (cidx(c), nbx + SSM_GROUPS + g)),
            BS((rows, LANES), lambda g, c: (cidx(c), 0)),
            BS((1, LANES), lambda g, c: (0, 0)), BS((1, LANES), lambda g, c: (0, 0)),
            BS((1, gw), lambda g, c: (0, g))]


def _chunk_rows(ci):
    return pl.ds(pl.multiple_of(ci * CHUNK, CHUNK), CHUNK)


def _ssd_fwd(xc, dtr, dt_bias, a, dskip, *, name):
    t = xc.shape[0]
    di = xc.shape[1] - 2 * SSM_GROUPS * D_STATE
    gw = di // SSM_GROUPS
    hpg = gw // HEAD_DIM
    npair = gw // LANES
    nc = t // CHUNK
    cb = _pick(nc, (4, 2, 1))

    def body(xb_ref, bb_ref, cb_ref, dtrb_ref, bias_ref, a_ref, dsk_ref, yb_ref, st_ref, state):
        @pl.when(pl.program_id(1) == 0)
        def _():
            state[...] = jnp.zeros_like(state)

        g = pl.program_id(0)

        def one_chunk(ci, carry):
            rows = _chunk_rows(ci)
            ssd_chunk(xb_ref.at[rows], bb_ref.at[rows], cb_ref.at[rows], dtrb_ref.at[rows], bias_ref, a_ref, dsk_ref,
                      yb_ref.at[rows], st_ref, state, ci, g)
            return carry

        lax.fori_loop(0, cb, one_chunk, 0)

    def ssd_chunk(x_ref, b_ref, c_ref, dtr_ref, bias_ref, a_ref, dsk_ref, y_ref, st_ref, state, ci, g):
        st_ref[ci, 0] = state[...]
        _, dt, la, la_t, tril = _ssd_common(dtr_ref, bias_ref, a_ref)
        lane = lax.broadcasted_iota(jnp.int32, (CHUNK, LANES), 1)
        sub = lax.broadcasted_iota(jnp.int32, (LANES, CHUNK), 0)
        lane1 = lax.broadcasted_iota(jnp.int32, (1, LANES), 1)
        low, low1 = lane < HEAD_DIM, lane1 < HEAD_DIM
        last = lax.broadcasted_iota(jnp.int32, (CHUNK, LANES), 0) == CHUNK - 1
        lend = jnp.sum(jnp.where(last, la, 0.0), axis=0, keepdims=True)
        bm, cm = b_ref[...], c_ref[...]
        gmat = _dot(cm, bm, NT)
        for p in range(npair):
            sl = slice(p * LANES, (p + 1) * LANES)
            ps = slice(p * D_STATE, (p + 1) * D_STATE)
            x2 = x_ref[:, sl].astype(F32)
            cols, ms = [], []
            for hh in range(2):
                h = g * hpg + p * 2 + hh
                col_la = _lane_col(la, lane, h)
                row_la = jnp.sum(jnp.where(sub == h, la_t, 0.0), axis=0, keepdims=True)
                lend_h = _lane_col(lend, lane1, h)
                decay = jnp.exp(jnp.where(tril, col_la - row_la, -jnp.inf))
                ms.append((gmat * decay).astype(BF16))
                cols.append((_lane_col(dt, lane, h), jnp.exp(col_la), jnp.exp(lend_h - col_la), jnp.exp(lend_h)))
            pair = lambda k: jnp.where(low, cols[0][k], cols[1][k])
            xdt = x2 * pair(0)
            xdtb = xdt.astype(BF16)
            s2 = state[ps, :]
            y = jnp.where(low, _dot(ms[0], xdtb), _dot(ms[1], xdtb))
            y = y + pair(1) * _dot(cm, s2.astype(BF16)) + x2 * dsk_ref[:, sl]
            y_ref[:, sl] = y
            state[ps, :] = s2 * jnp.where(low1, cols[0][3], cols[1][3]) + _dot(bm, (xdt * pair(2)).astype(BF16), TN)

    y, st = pl.pallas_call(
        body, name=name, grid=(SSM_GROUPS, nc // cb), in_specs=_ssd_specs(cb * CHUNK, di, gw, lambda c: c),
        out_specs=[BS((cb * CHUNK, gw), lambda g, c: (c, g)), BS((cb, 1, npair * D_STATE, LANES), lambda g, c: (c, g, 0, 0))],
        out_shape=[SDS((t, di), F32), SDS((nc, SSM_GROUPS, npair * D_STATE, LANES), F32)],
        scratch_shapes=[pltpu.VMEM((npair * D_STATE, LANES), F32)], compiler_params=_cp("parallel", "arbitrary"),
    )(xc, xc, xc, dtr, dt_bias, a, dskip)
    return y, st


def _ssd_bwd(xc, dtr, dt_bias, a, dskip, st, dy, *, name):
    t = xc.shape[0]
    di = xc.shape[1] - 2 * SSM_GROUPS * D_STATE
    gw = di // SSM_GROUPS
    hpg = gw // HEAD_DIM
    npair = gw // LANES
    nc = t // CHUNK
    cb = _pick(nc, (4, 2, 1))
    rev = lambda c: nc // cb - 1 - c

    def body(xb_ref, bb_ref, cb_ref, dtrb_ref, bias_ref, a_ref, dsk_ref, st_ref, dyb_ref,
             dxb_ref, dbb_ref, dcb_ref, ddtrb_ref, da_ref, dbias_ref, ddsk_ref, dstate):
        @pl.when(pl.program_id(1) == 0)
        def _():
            dstate[...] = jnp.zeros_like(dstate)
            da_ref[...] = jnp.zeros_like(da_ref)
            dbias_ref[...] = jnp.zeros_like(dbias_ref)
            ddsk_ref[...] = jnp.zeros_like(ddsk_ref)

        g = pl.program_id(0)

        def one_chunk(j, carry):
            ci = cb - 1 - j
            rows = _chunk_rows(ci)
            ssd_chunk(xb_ref.at[rows], bb_ref.at[rows], cb_ref.at[rows], dtrb_ref.at[rows], bias_ref, a_ref, dsk_ref,
                      st_ref.at[ci], dyb_ref.at[rows], dxb_ref.at[rows], dbb_ref.at[rows], dcb_ref.at[rows],
                      ddtrb_ref.at[:, rows], da_ref, dbias_ref, ddsk_ref, dstate, g)
            return carry

        lax.fori_loop(0, cb, one_chunk, 0)

    def ssd_chunk(x_ref, b_ref, c_ref, dtr_ref, bias_ref, a_ref, dsk_ref, st_ref, dy_ref,
                  dx_ref, db_ref, dc_ref, ddtr_ref, da_ref, dbias_ref, ddsk_ref, dstate, g):
        pre, dt, la, la_t, tril = _ssd_common(dtr_ref, bias_ref, a_ref)
        lane = lax.broadcasted_iota(jnp.int32, (CHUNK, LANES), 1)
        sub = lax.broadcasted_iota(jnp.int32, (LANES, CHUNK), 0)
        lane1 = lax.broadcasted_iota(jnp.int32, (1, LANES), 1)
        low, low1 = lane < HEAD_DIM, lane1 < HEAD_DIM
        last = lax.broadcasted_iota(jnp.int32, (CHUNK, LANES), 0) == CHUNK - 1
        lend = jnp.sum(jnp.where(last, la, 0.0), axis=0, keepdims=True)
        bm, cm = b_ref[...], c_ref[...]
        gmat = _dot(cm, bm, NT)
        dg = jnp.zeros((CHUNK, CHUNK), F32)
        dla_cols = jnp.zeros((CHUNK, LANES), F32)
        dla_rows = jnp.zeros((LANES, CHUNK), F32)
        dtsum = jnp.zeros((CHUNK, LANES), F32)
        dbm = jnp.zeros((CHUNK, D_STATE), F32)
        dcm = jnp.zeros((CHUNK, D_STATE), F32)
        for p in range(npair):
            sl = slice(p * LANES, (p + 1) * LANES)
            ps = slice(p * D_STATE, (p + 1) * D_STATE)
            x2 = x_ref[:, sl].astype(F32)
            dy2 = dy_ref[:, sl]
            s2 = st_ref[0, ps, :]
            ds2 = dstate[ps, :]
            hs, cols, ms, decays = [], [], [], []
            for hh in range(2):
                h = g * hpg + p * 2 + hh
                col_la = _lane_col(la, lane, h)
                row_la = jnp.sum(jnp.where(sub == h, la_t, 0.0), axis=0, keepdims=True)
                lend_h = _lane_col(lend, lane1, h)
                decay = jnp.exp(jnp.where(tril, col_la - row_la, -jnp.inf))
                hs.append(h)
                decays.append(decay)
                ms.append(gmat * decay)
                cols.append((_lane_col(dt, lane, h), jnp.exp(col_la), jnp.exp(lend_h - col_la), jnp.exp(lend_h)))
            pair = lambda k: jnp.where(low, cols[0][k], cols[1][k])
            dtc, ec, eend = pair(0), pair(1), pair(2)
            eend_s = jnp.where(low1, cols[0][3], cols[1][3])
            xdt = x2 * dtc
            xdtb = xdt.astype(BF16)
            dys = dy2 * ec
            dysb = dys.astype(BF16)
            dxdt_state = eend * _dot(bm, ds2.astype(BF16))
            inter = dys * _dot(cm, s2.astype(BF16))
            u = dxdt_state * xdt
            sds = s2 * ds2
            dxdt = dxdt_state
            for hh in range(2):
                h = hs[hh]
                mh = low if hh == 0 else jnp.logical_not(low)
                dym = jnp.where(mh, dy2, 0.0).astype(BF16)
                dxdt = dxdt + _dot(ms[hh].astype(BF16), dym, TN)
                dm = _dot(dym, xdtb, NT)
                w = dm * ms[hh]
                dg = dg + dm * decays[hh]
                u_col = jnp.sum(jnp.where(mh, u, 0.0), axis=-1, keepdims=True)
                dlend = jnp.sum(u_col, axis=0, keepdims=True) + cols[hh][3] * jnp.sum(jnp.where(low1 if hh == 0 else jnp.logical_not(low1), jnp.sum(sds, axis=0, keepdims=True), 0.0), axis=-1, keepdims=True)
                col = jnp.sum(w, axis=-1, keepdims=True) + jnp.sum(jnp.where(mh, inter, 0.0), axis=-1, keepdims=True) - u_col
                dla_cols = dla_cols + jnp.where(lane == h, col + jnp.where(last, dlend, 0.0), 0.0)
                dla_rows = dla_rows - jnp.where(sub == h, jnp.sum(w, axis=0, keepdims=True), 0.0)
            for hh in range(2):
                mh = low if hh == 0 else jnp.logical_not(low)
                dtsum = dtsum + jnp.where(lane == hs[hh], jnp.sum(jnp.where(mh, dxdt * x2, 0.0), axis=-1, keepdims=True), 0.0)
            dcm = dcm + _dot(dysb, s2.astype(BF16), NT)
            dbm = dbm + _dot((xdt * eend).astype(BF16), ds2.astype(BF16), NT)
            dstate[ps, :] = ds2 * eend_s + _dot(cm, dysb, TN)
            dx_ref[:, sl] = dxdt * dtc + dy2 * dsk_ref[:, sl]
            ddsk_ref[:, sl] += jnp.sum(dy2 * x2, axis=0, keepdims=True)
        dgb = dg.astype(BF16)
        dc_ref[...] = dcm + _dot(dgb, bm)
        db_ref[...] = dbm + _dot(dgb, cm, TN)
        dla = dla_cols + dla_rows.T
        triu = lax.broadcasted_iota(jnp.int32, (CHUNK, CHUNK), 0) <= lax.broadcasted_iota(jnp.int32, (CHUNK, CHUNK), 1)
        ddta = _dot(triu.astype(F32), dla, precision=HIGHEST)
        ddt = ddta * a_ref[...] + dtsum
        da_ref[0] += jnp.sum(ddta * dt, axis=0, keepdims=True)
        ddtr = ddt * _sigmoid(pre)
        ddtr_ref[0] = ddtr
        dbias_ref[0] += jnp.sum(ddtr, axis=0, keepdims=True)

    vec = BS((1, 1, LANES), lambda g, c: (g, 0, 0))
    outs = pl.pallas_call(
        body, name=name, grid=(SSM_GROUPS, nc // cb),
        in_specs=_ssd_specs(cb * CHUNK, di, gw, rev) + [BS((cb, 1, npair * D_STATE, LANES), lambda g, c: (rev(c), g, 0, 0)),
                                                        BS((cb * CHUNK, gw), lambda g, c: (rev(c), g))],
        out_specs=[BS((cb * CHUNK, gw), lambda g, c: (rev(c), g)), BS((cb * CHUNK, D_STATE), lambda g, c: (rev(c), g)),
                   BS((cb * CHUNK, D_STATE), lambda g, c: (rev(c), g)), BS((1, cb * CHUNK, LANES), lambda g, c: (g, rev(c), 0)),
                   vec, vec, BS((1, gw), lambda g, c: (0, g))],
        out_shape=[SDS((t, di), F32), SDS((t, SSM_GROUPS * D_STATE), F32), SDS((t, SSM_GROUPS * D_STATE), F32),
                   SDS((SSM_GROUPS, t, LANES), F32), SDS((SSM_GROUPS, 1, LANES), F32), SDS((SSM_GROUPS, 1, LANES), F32),
                   SDS((1, di), F32)],
        scratch_shapes=[pltpu.VMEM((npair * D_STATE, LANES), F32)], compiler_params=_cp("parallel", "arbitrary"),
    )(xc, xc, xc, dtr, dt_bias, a, dskip, st, dy)
    return outs


def _gate_norm_fwd(y, proj, zoff, w, *, name):
    t, di = y.shape
    gw = di // SSM_GROUPS
    tm = _pick(t, (512, 256, 128))
    zb = zoff // gw

    def body(y_ref, z_ref, w_ref, o_ref):
        yg = y_ref[...] * _silu(z_ref[...].astype(F32))
        r = lax.rsqrt(jnp.mean(yg * yg, axis=-1, keepdims=True) + EPS)
        o_ref[...] = (yg * r * w_ref[...]).astype(BF16)

    return pl.pallas_call(
        body, name=name, grid=(t // tm, SSM_GROUPS),
        in_specs=[BS((tm, gw), lambda i, g: (i, g)), BS((tm, gw), lambda i, g: (i, zb + g)), BS((1, gw), lambda i, g: (0, g))],
        out_specs=BS((tm, gw), lambda i, g: (i, g)), out_shape=SDS((t, di), BF16), compiler_params=_cp("parallel", "parallel"),
    )(y, proj, w.reshape(1, di))


def _gate_norm_bwd(dssm, y, proj, zoff, w, *, name):
    t, di = y.shape
    gw = di // SSM_GROUPS
    tm = _pick(t, (512, 256, 128))
    zb = zoff // gw

    def body(d_ref, y_ref, z_ref, w_ref, dy_ref, dz_ref, dw_ref):
        z = z_ref[...].astype(F32)
        yv = y_ref[...]
        sz = _silu(z)
        yg = yv * sz
        r = lax.rsqrt(jnp.mean(yg * yg, axis=-1, keepdims=True) + EPS)
        n = yg * r
        d = d_ref[...]
        dn = d * w_ref[...]
        dyg = r * (dn - n * jnp.mean(dn * n, axis=-1, keepdims=True))
        dy_ref[...] = dyg * sz
        dz_ref[...] = (dyg * yv * _dsilu(z)).astype(BF16)

        @pl.when(pl.program_id(1) == 0)
        def _():
            dw_ref[...] = jnp.zeros_like(dw_ref)

        dw_ref[...] += jnp.sum(d * n, axis=0, keepdims=True)

    blk = BS((tm, gw), lambda g, i: (i, g))
    dy, dz, dw = pl.pallas_call(
        body, name=name, grid=(SSM_GROUPS, t // tm),
        in_specs=[blk, blk, BS((tm, gw), lambda g, i: (i, zb + g)), BS((1, gw), lambda g, i: (0, g))],
        out_specs=[blk, blk, BS((1, gw), lambda g, i: (0, g))],
        out_shape=[SDS((t, di), F32), SDS((t, di), BF16), SDS((1, di), F32)], compiler_params=_cp("parallel", "arbitrary"),
    )(dssm, y, proj, w.reshape(1, di))
    return dy, dz, dw[0]


def _merge_fwd(proj, goff, ga, gs, *, name):
    t, d = ga.shape
    cw = _conv_block(d, goff)
    tm = _pick(t, (512, 256, 128))
    gb = goff // cw

    def body(g0, g1, a_ref, s_ref, o_ref):
        o_ref[...] = (_sigmoid(g0[...].astype(F32)) * a_ref[...] + _sigmoid(g1[...].astype(F32)) * s_ref[...]).astype(BF16)

    blk = BS((tm, cw), lambda i, j: (i, j))
    return pl.pallas_call(
        body, name=name, grid=(t // tm, d // cw),
        in_specs=[BS((tm, cw), lambda i, j: (i, gb + j)), BS((tm, cw), lambda i, j: (i, gb + d // cw + j)), blk, blk],
        out_specs=blk, out_shape=SDS((t, d), BF16), compiler_params=_cp("parallel", "parallel"),
    )(proj, proj, ga, gs)


def _merge_bwd(proj, goff, ga, gs, dm, *, name):
    t, d = ga.shape
    cw = _conv_block(d, goff)
    tm = _pick(t, (512, 256, 128))
    gb = goff // cw

    def body(g0, g1, a_ref, s_ref, dm_ref, da_ref, ds_ref, dg0_ref, dg1_ref):
        dmv = dm_ref[...]
        s0 = _sigmoid(g0[...].astype(F32))
        s1 = _sigmoid(g1[...].astype(F32))
        da_ref[...] = (s0 * dmv).astype(BF16)
        ds_ref[...] = (s1 * dmv).astype(BF16)
        dg0_ref[...] = (dmv * a_ref[...] * s0 * (1.0 - s0)).astype(BF16)
        dg1_ref[...] = (dmv * s_ref[...] * s1 * (1.0 - s1)).astype(BF16)

    blk = BS((tm, cw), lambda i, j: (i, j))
    return pl.pallas_call(
        body, name=name, grid=(t // tm, d // cw),
        in_specs=[BS((tm, cw), lambda i, j: (i, gb + j)), BS((tm, cw), lambda i, j: (i, gb + d // cw + j)), blk, blk, blk],
        out_specs=[blk] * 4, out_shape=[SDS((t, d), BF16)] * 4, compiler_params=_cp("parallel", "parallel"),
    )(proj, proj, ga, gs, dm)


def _swiglu_fwd(u, *, name):
    t, two_f = u.shape
    f = two_f // 2
    cw = _wide(f)
    tm = _pick(t, (512, 256, 128))

    def body(g_ref, u_ref, o_ref):
        o_ref[...] = (_silu(g_ref[...].astype(F32)) * u_ref[...].astype(F32)).astype(BF16)

    return pl.pallas_call(
        body, name=name, grid=(t // tm, f // cw),
        in_specs=[BS((tm, cw), lambda i, j: (i, j)), BS((tm, cw), lambda i, j: (i, f // cw + j))],
        out_specs=BS((tm, cw), lambda i, j: (i, j)), out_shape=SDS((t, f), BF16), compiler_params=_cp("parallel", "parallel"),
    )(u, u)


def _swiglu_bwd(u, df, *, name):
    t, two_f = u.shape
    f = two_f // 2
    cw = _wide(f)
    tm = _pick(t, (512, 256, 128))

    def body(g_ref, u_ref, d_ref, dg_ref, du_ref):
        gt = g_ref[...].astype(F32)
        d = d_ref[...].astype(F32)
        dg_ref[...] = (d * u_ref[...].astype(F32) * _dsilu(gt)).astype(BF16)
        du_ref[...] = (d * _silu(gt)).astype(BF16)

    blk = BS((tm, cw), lambda i, j: (i, j))
    return pl.pallas_call(
        body, name=name, grid=(t // tm, f // cw),
        in_specs=[blk, BS((tm, cw), lambda i, j: (i, f // cw + j)), blk],
        out_specs=[blk, blk], out_shape=[SDS((t, f), BF16)] * 2, compiler_params=_cp("parallel", "parallel"),
    )(u, u, df)


def _row_block(rows, cols, n_arrays):
    budget = VMEM_LIMIT_BYTES // 3
    for tr in (512, 256, 128, 64, 32, 16, 8):
        if rows % tr == 0 and tr * cols * 4 * n_arrays * 2 <= budget:
            return tr
    raise ValueError((rows, cols))


def _add_own_layer(g0, g1, got, core, *, name):
    rows, cols = got.shape
    tr = _row_block(rows, cols, 4)

    def body(core_ref, g0_ref, g1_ref, got_ref, o_ref):
        o_ref[...] = jnp.where(core_ref[0] == 0, g0_ref[...], g1_ref[...]) + got_ref[...]

    blk = BS((tr, cols), lambda i, cr: (i, 0))
    grid_spec = pltpu.PrefetchScalarGridSpec(
        num_scalar_prefetch=1, grid=(rows // tr,),
        in_specs=[BS((tr, cols), lambda i, cr: (i * (1 - cr[0]), 0)), BS((tr, cols), lambda i, cr: (i * cr[0], 0)), blk],
        out_specs=blk)
    return pl.pallas_call(body, name=name, grid_spec=grid_spec, out_shape=SDS((rows, cols), F32),
                          compiler_params=_cp("arbitrary"))(core, g0, g1, got)


def _sum_chips(a, *, name):
    _, rows, cols = a.shape
    tr = _row_block(rows, cols, 5)

    def body(a_ref, o_ref):
        o_ref[...] = ((a_ref[0] + a_ref[1]) + a_ref[2]) + a_ref[3]

    return pl.pallas_call(body, name=name, grid=(rows // tr,), in_specs=[BS((N_CHIPS, tr, cols), lambda i: (0, i, 0))],
                          out_specs=BS((tr, cols), lambda i: (i, 0)), out_shape=SDS((rows, cols), F32),
                          compiler_params=_cp("parallel"))(a)


def _adamw(w, g, m, v, *, name):
    rows, cols = w.shape
    tr = _row_block(rows, cols, 7) if rows % 8 == 0 else rows
    c1 = 1.0 - ADAM_B1 ** ADAM_STEP
    c2 = 1.0 - ADAM_B2 ** ADAM_STEP

    def body(w_ref, g_ref, m_ref, v_ref, d_ref, nm_ref, nv_ref):
        gv = g_ref[...]
        nm = ADAM_B1 * m_ref[...] + (1.0 - ADAM_B1) * gv
        nv = ADAM_B2 * v_ref[...] + (1.0 - ADAM_B2) * (gv * gv)
        d_ref[...] = -ADAM_LR * ((nm / c1) / (jnp.sqrt(nv / c2) + ADAM_EPS) + ADAM_WD * w_ref[...])
        nm_ref[...] = nm
        nv_ref[...] = nv

    blk = BS((tr, cols), lambda i: (i, 0))
    return pl.pallas_call(body, name=name, grid=(rows // tr,), in_specs=[blk] * 4, out_specs=[blk] * 3,
                          out_shape=[SDS((rows, cols), F32)] * 3, compiler_params=_cp("parallel"))(w, g, m, v)


def _adamw_layers(w, g_own, g_other, m, v, core, *, name):
    _, rows, cols = w.shape
    tr = _row_block(rows, cols, 9)
    c1 = 1.0 - ADAM_B1 ** ADAM_STEP
    c2 = 1.0 - ADAM_B2 ** ADAM_STEP

    def body(core_ref, w_ref, own_ref, oth_ref, m_ref, v_ref, g_ref, d_ref, nm_ref, nv_ref):
        gv = jnp.where(pl.program_id(0) == core_ref[0], own_ref[...], oth_ref[...])
        nm = ADAM_B1 * m_ref[0] + (1.0 - ADAM_B1) * gv
        nv = ADAM_B2 * v_ref[0] + (1.0 - ADAM_B2) * (gv * gv)
        g_ref[0] = gv
        d_ref[0] = -ADAM_LR * ((nm / c1) / (jnp.sqrt(nv / c2) + ADAM_EPS) + ADAM_WD * w_ref[0])
        nm_ref[0] = nm
        nv_ref[0] = nv

    own_here = lambda l, cr: 1 - (l - cr[0]) * (l - cr[0])
    slab = BS((1, tr, cols), lambda l, i, cr: (l, i, 0))
    grid_spec = pltpu.PrefetchScalarGridSpec(
        num_scalar_prefetch=1, grid=(2, rows // tr),
        in_specs=[slab, BS((tr, cols), lambda l, i, cr: (i * own_here(l, cr), 0)),
                  BS((tr, cols), lambda l, i, cr: (i * (1 - own_here(l, cr)), 0)), slab, slab],
        out_specs=[slab] * 4)
    return pl.pallas_call(body, name=name, grid_spec=grid_spec, out_shape=[SDS(w.shape, F32)] * 4,
                          compiler_params=_cp("arbitrary", "arbitrary"))(core, w, g_own, g_other, m, v)


ANY = BS(memory_space=pl.ANY)


def _place():
    x, y, c = lax.axis_index("x"), lax.axis_index("y"), lax.axis_index("c")
    return x, y, c, [(1 - x, y), (x, 1 - y), (1 - x, 1 - y)]


def _gather_shards(arrs, *, name):
    n = len(arrs)

    def body(*refs):
        ins, outs = refs[:n], refs[n:2 * n]
        send_sems, recv_sems, pass_send_sems, pass_recv_sems = refs[2 * n:]
        x, y, c, chips = _place()
        s = 2 * x + y

        def ici(i, j, src_chip, to):
            src = ins[i].at[c] if src_chip is None else outs[i].at[src_chip, c]
            return pltpu.make_async_remote_copy(
                src_ref=src, dst_ref=outs[i].at[s if src_chip is None else src_chip, c], send_sem=send_sems.at[i * 3 + j],
                recv_sem=recv_sems.at[i * 3 + j], device_id=to, device_id_type=MESH)

        def d2d(i, j, src_chip, layer):
            slab = outs[i].at[src_chip, layer]
            return pltpu.make_async_remote_copy(
                src_ref=slab, dst_ref=slab, send_sem=pass_send_sems.at[i * 3 + j], recv_sem=pass_recv_sems.at[i * 3 + j],
                device_id=(x, y, 1 - c), device_id_type=MESH)

        sent = []
        for i in range(n):
            for j, (px, py) in enumerate(chips):
                cp = ici(i, j, None, (px, py, c))
                cp.start()
                sent.append(cp)
        passed = []
        for i in range(n):
            for j, (px, py) in enumerate(chips):
                ici(i, j, 2 * px + py, (x, y, c)).wait_recv()
                cp = d2d(i, j, 2 * px + py, c)
                cp.start()
                passed.append(cp)
        for i in range(n):
            for j, (px, py) in enumerate(chips):
                d2d(i, j, 2 * px + py, 1 - c).wait_recv()
        for cp in sent + passed:
            cp.wait_send()

    return pl.pallas_call(
        body, name=name, in_specs=[ANY] * n, out_specs=[ANY] * n,
        out_shape=[SDS((N_CHIPS,) + a.shape, a.dtype) for a in arrs],
        scratch_shapes=[pltpu.SemaphoreType.DMA((3 * n,))] * 4,
    )(*arrs)


def _pair_swap_layers(layer0, layer1, *, name):
    n = len(layer0)

    def body(*refs):
        in0, in1, outs = refs[:n], refs[n:2 * n], refs[2 * n:3 * n]
        send_sems, recv_sems = refs[3 * n:]
        x, y, c, _ = _place()

        def copy(src, i):
            return pltpu.make_async_remote_copy(
                src_ref=src[i], dst_ref=outs[i], send_sem=send_sems.at[i], recv_sem=recv_sems.at[i],
                device_id=(x, y, 1 - c), device_id_type=MESH)

        @pl.when(c == 0)
        def _():
            for i in range(n):
                copy(in1, i).start()

        @pl.when(c == 1)
        def _():
            for i in range(n):
                copy(in0, i).start()

        for i in range(n):
            copy(in0, i).wait()

    return pl.pallas_call(
        body, name=name, in_specs=[ANY] * (2 * n), out_specs=[ANY] * n, out_shape=[SDS(a.shape, a.dtype) for a in layer0],
        scratch_shapes=[pltpu.SemaphoreType.DMA((n,)), pltpu.SemaphoreType.DMA((n,))],
    )(*layer0, *layer1)


def _scatter_to_chips(arrs, *, name):
    n = len(arrs)

    def body(*refs):
        ins, outs = refs[:n], refs[n:2 * n]
        send_sems, recv_sems = refs[2 * n:]
        x, y, c, chips = _place()
        s = 2 * x + y
        copies = []
        for i in range(n):
            for j, (px, py) in enumerate(chips):
                cp = pltpu.make_async_remote_copy(
                    src_ref=ins[i].at[2 * px + py], dst_ref=outs[i].at[s], send_sem=send_sems.at[i * 3 + j],
                    recv_sem=recv_sems.at[i * 3 + j], device_id=(px, py, c), device_id_type=MESH)
                cp.start()
                copies.append(cp)
        for cp in copies:
            cp.wait()

    return pl.pallas_call(
        body, name=name, in_specs=[ANY] * n, out_specs=[ANY] * n, out_shape=[SDS(a.shape, a.dtype) for a in arrs],
        scratch_shapes=[pltpu.SemaphoreType.DMA((3 * n,)), pltpu.SemaphoreType.DMA((3 * n,))],
    )(*arrs)


def _pair_swap(arrs, *, name):
    n = len(arrs)

    def body(*refs):
        ins, outs = refs[:n], refs[n:2 * n]
        send_sems, recv_sems = refs[2 * n:]
        x, y, c, _ = _place()
        copies = []
        for i in range(n):
            cp = pltpu.make_async_remote_copy(
                src_ref=ins[i], dst_ref=outs[i], send_sem=send_sems.at[i], recv_sem=recv_sems.at[i],
                device_id=(x, y, 1 - c), device_id_type=MESH)
            cp.start()
            copies.append(cp)
        for cp in copies:
            cp.wait()

    return pl.pallas_call(
        body, name=name, in_specs=[ANY] * n, out_specs=[ANY] * n, out_shape=[SDS(a.shape, a.dtype) for a in arrs],
        scratch_shapes=[pltpu.SemaphoreType.DMA((n,)), pltpu.SemaphoreType.DMA((n,))],
    )(*arrs)


def _allreduce_small(v, *, name):
    rows, cols = v.shape

    def body(v_ref, o_ref, gath, send_sems, recv_sems):
        x, y, c, _ = _place()
        me = 4 * x + 2 * y + c
        gath[me] = v_ref[...]
        copies = []
        for k in range(1, N_DEV):
            fx, fy, fc = (k >> 2) & 1, (k >> 1) & 1, k & 1
            peer = (1 - x if fx else x, 1 - y if fy else y, 1 - c if fc else c)
            cp = pltpu.make_async_remote_copy(
                src_ref=v_ref, dst_ref=gath.at[me], send_sem=send_sems.at[k - 1], recv_sem=recv_sems.at[k - 1],
                device_id=peer, device_id_type=MESH)
            cp.start()
            copies.append(cp)
        for cp in copies:
            cp.wait()
        acc = gath[0]
        for k in range(1, N_DEV):
            acc = acc + gath[k]
        o_ref[...] = acc

    vm = BS(memory_space=pltpu.VMEM)
    return pl.pallas_call(
        body, name=name, in_specs=[vm], out_specs=vm, out_shape=SDS((rows, cols), F32),
        scratch_shapes=[pltpu.VMEM((N_DEV, rows, cols), F32), pltpu.SemaphoreType.DMA((N_DEV - 1,)), pltpu.SemaphoreType.DMA((N_DEV - 1,))],
    )(v)


def _t5_bucket(dist):
    max_exact = N_REL_BUCKETS // 2
    d_f = jnp.maximum(dist, 1).astype(F32)
    large = max_exact + (jnp.log(d_f / max_exact) / math.log(REL_MAX_DISTANCE / max_exact) * (N_REL_BUCKETS - max_exact)).astype(jnp.int32)
    return jnp.where(dist < max_exact, dist, jnp.minimum(large, N_REL_BUCKETS - 1))


def _rel_buckets(dilation):
    qi = jnp.arange(ATTN_BLOCK)[:, None]
    kj = jnp.arange(2 * ATTN_BLOCK)[None, :]
    return _t5_bucket(jnp.clip(qi + ATTN_BLOCK - kj, 0, N_STEPS) * dilation)


def _layer_fwd(h, p, biases, lname):
    sv = {"h": h}
    xn1 = _rms_fwd(h, p["norm1_w"], name=lname + "norm1")
    proj = _matmul(xn1, p["w_main"], out_dtype=BF16, name=lname + "in_proj")
    dtr = _matmul(xn1, p["w_dt"], out_dtype=F32, name=lname + "in_proj_dt")
    xn1_rm, qkvs = [xn1], [proj]
    for g in range(1, N_GROUPS_ATTN):
        xn1_rm.append(_to_residue_major(xn1, DILATIONS[g]))
        qkvs.append(_matmul(xn1_rm[g], p["w_qkv"][g], out_dtype=BF16, name=f"{lname}in_proj_qkv{g}"))
    os_, lses, lses_rm = [], [], []
    for g, d in enumerate(DILATIONS):
        o, lse = _attn_fwd(qkvs[g], (0, 1, 2), biases[g], d, name=f"{lname}attn{g}")
        os_.append(_to_token_major(o, d))
        lses.append(_to_token_major(lse, d))
        lses_rm.append(lse)
    sv.update(xn1_rm=xn1_rm, qkvs=qkvs, lses_rm=lses_rm)
    attn = _combine_fwd(os_, lses, name=lname + "attn_combine")
    xc = _conv_fwd(proj, p["off_xbc"], p["conv_w"], p["conv_b"], name=lname + "conv")
    y, st = _ssd_fwd(xc, dtr, p["dt_bias"], p["a"], p["dskip"], name=lname + "ssd")
    ssm = _gate_norm_fwd(y, proj, p["off_z"], p["ssm_norm_w"], name=lname + "gate_norm")
    ga = _matmul(attn, p["w_attn_branch"], name=lname + "attn_branch")
    gs = _matmul(ssm, p["w_ssm_branch"], name=lname + "ssm_branch")
    merged = _merge_fwd(proj, p["off_gate"], ga, gs, name=lname + "merge")
    h1 = _matmul(merged, p["w_out"], res=h, name=lname + "out_proj")
    xn2 = _rms_fwd(h1, p["norm2_w"], name=lname + "norm2")
    u = _matmul(xn2, p["w_ffn_in"], out_dtype=BF16, name=lname + "ffn_in")
    f = _swiglu_fwd(u, name=lname + "swiglu")
    h2 = _matmul(f, p["w_ffn_out"], res=h1, name=lname + "ffn_out")
    sv.update(xn1=xn1, proj=proj, dtr=dtr, os=os_, lses=lses, attn=attn, xc=xc, y=y, st=st, ssm=ssm, ga=ga, gs=gs,
              merged=merged, h1=h1, xn2=xn2, u=u, f=f)
    return h2, sv


def _layer_bwd(dh2, p, sv, biases, lname):
    gr = {}
    lname = lname + "bwd_"
    df = _matmul(dh2, p["w_ffn_out"], tb=True, out_dtype=BF16, name=lname + "ffn_out_dx")
    gr["w_ffn_out"] = _matmul(sv["f"], dh2, ta=True, name=lname + "ffn_out_dw")
    dgate, dup = _swiglu_bwd(sv["u"], df, name=lname + "swiglu")
    du = jnp.concatenate([dgate, dup], axis=1)
    dxn2 = _matmul(du, p["w_ffn_in"], tb=True, name=lname + "ffn_in_dx")
    gr["w_ffn_in"] = _matmul(sv["xn2"], du, ta=True, name=lname + "ffn_in_dw")
    dh1, gr["norm2_w"] = _rms_bwd(sv["h1"], p["norm2_w"], [dxn2], dh2, name=lname + "norm2")
    dmerged = _matmul(dh1, p["w_out"], tb=True, name=lname + "out_proj_dx")
    gr["w_out"] = _matmul(sv["merged"], dh1, ta=True, name=lname + "out_proj_dw")
    dga, dgs, dg0, dg1 = _merge_bwd(sv["proj"], p["off_gate"], sv["ga"], sv["gs"], dmerged, name=lname + "merge")
    dattn = _matmul(dga, p["w_attn_branch"], tb=True, name=lname + "attn_branch_dx")
    gr["w_attn_branch"] = _matmul(sv["attn"], dga, ta=True, name=lname + "attn_branch_dw")
    dssm = _matmul(dgs, p["w_ssm_branch"], tb=True, name=lname + "ssm_branch_dx")
    gr["w_ssm_branch"] = _matmul(sv["ssm"], dgs, ta=True, name=lname + "ssm_branch_dw")
    dy, dz, gr["ssm_norm_w"] = _gate_norm_bwd(dssm, sv["y"], sv["proj"], p["off_z"], p["ssm_norm_w"], name=lname + "gate_norm")
    dxs, dbm, dcm, ddtr4, da4, dbias4, ddsk = _ssd_bwd(sv["xc"], sv["dtr"], p["dt_bias"], p["a"], p["dskip"], sv["st"], dy,
                                                       name=lname + "ssd")
    nsh = p["n_ssm_heads"]
    ddtr = jnp.sum(ddtr4, axis=0)
    gr["a_log"] = jnp.sum(da4, axis=(0, 1))[:nsh] * p["a"][0, :nsh]
    gr["dt_bias"] = jnp.sum(dbias4, axis=(0, 1))[:nsh]
    gr["d_skip"] = jnp.sum(ddsk.reshape(nsh, HEAD_DIM), axis=1)
    di = dxs.shape[1]
    dxbc, dcw, dcb = [], [], []
    for part, (lo, hi) in zip((dxs, dbm, dcm), ((0, di), (di, di + dbm.shape[1]), (di + dbm.shape[1], di + 2 * dbm.shape[1]))):
        dx_, dw_, db_ = _conv_bwd(sv["proj"], p["off_xbc"] + lo, p["conv_w"][:, lo:hi], p["conv_b"][lo:hi], part,
                                  name=f"{lname}conv{lo}")
        dxbc.append(dx_)
        dcw.append(dw_)
        dcb.append(db_)
    gr["conv_w"] = jnp.concatenate(dcw, axis=1)
    gr["conv_b"] = jnp.concatenate(dcb, axis=0)
    dos, corrs = _combine_bwd(dattn, sv["os"], sv["lses"], name=lname + "attn_combine")
    dqkvs, dbiases = [], []
    for g, d in enumerate(DILATIONS):
        dq, dk, dv, dbias = _attn_bwd(sv["qkvs"][g], (0, 1, 2), biases[g], sv["lses_rm"][g], _to_residue_major(dos[g], d),
                                      _to_residue_major(corrs[g], d), d, name=f"{lname}attn{g}")
        dqkvs.append([dq, dk, dv])
        dbiases.append(dbias)
    dmain = jnp.concatenate(dqkvs[0] + [dz] + dxbc + [dg0, dg1], axis=1)
    dxn1 = [_matmul(dmain, p["w_main"], tb=True, name=lname + "in_proj_dx"),
            _matmul(ddtr, p["w_dt"], tb=True, name=lname + "in_proj_dt_dx")]
    dw_main = _matmul(sv["xn1"], dmain, ta=True, name=lname + "in_proj_dw")
    dw_dt = _matmul(sv["xn1"], ddtr, ta=True, name=lname + "in_proj_dt_dw")
    dw_qkv = [dw_main]
    for g in range(1, N_GROUPS_ATTN):
        dqkv = jnp.concatenate(dqkvs[g], axis=1)
        dxn1.append(_to_token_major(_matmul(dqkv, p["w_qkv"][g], tb=True, name=f"{lname}in_proj_qkv{g}_dx"), DILATIONS[g]))
        dw_qkv.append(_matmul(sv["xn1_rm"][g], dqkv, ta=True, name=f"{lname}in_proj_qkv{g}_dw"))
    awg, og = dqkvs[0][0].shape[1], p["off_gate"]
    cols = [dw[:, i * awg:(i + 1) * awg] for i in range(3) for dw in dw_qkv]
    gr["w_in"] = jnp.concatenate(cols + [dw_main[:, 3 * awg:og], dw_dt[:, :nsh], dw_main[:, og:]], axis=1)
    dh, gr["norm1_w"] = _rms_bwd(sv["h"], p["norm1_w"], dxn1, dh1, name=lname + "norm1")
    return dh, gr, dbiases


def _layer_params(l, w, n_ssm_heads, hg):
    awg = hg * HEAD_DIM
    aw = N_GROUPS_ATTN * awg
    di = n_ssm_heads * HEAD_DIM
    xbc = di + 2 * SSM_GROUPS * D_STATE
    in_dt = 3 * aw + di + xbc
    w_in = w["w_in"][l]
    qkv_cols = lambda g: [w_in[:, (i * N_GROUPS_ATTN + g) * awg:(i * N_GROUPS_ATTN + g + 1) * awg] for i in range(3)]
    pad = lambda v: jnp.pad(v.astype(F32), (0, LANES - n_ssm_heads)).reshape(1, LANES)
    return dict(
        n_ssm_heads=n_ssm_heads, off_z=3 * awg, off_xbc=3 * awg + di, off_gate=3 * awg + di + xbc,
        w_main=jnp.concatenate(qkv_cols(0) + [w_in[:, 3 * aw:in_dt], w_in[:, in_dt + n_ssm_heads:]], axis=1),
        w_qkv=[None] + [jnp.concatenate(qkv_cols(g), axis=1) for g in range(1, N_GROUPS_ATTN)],
        w_dt=jnp.pad(w_in[:, in_dt:in_dt + n_ssm_heads], ((0, 0), (0, LANES - n_ssm_heads))),
        norm1_w=w["norm1_w"][l], norm2_w=w["norm2_w"][l], conv_w=w["conv_w"][l], conv_b=w["conv_b"][l],
        dt_bias=pad(w["dt_bias"][l]), a=pad(-jnp.exp(w["a_log"][l])),
        dskip=jnp.repeat(w["d_skip"][l], HEAD_DIM).reshape(1, di), ssm_norm_w=w["ssm_norm_w"][l],
        w_attn_branch=w["w_attn_branch"][l], w_ssm_branch=w["w_ssm_branch"][l], w_out=w["w_out"][l],
        w_ffn_in=w["w_ffn_in"][l], w_ffn_out=w["w_ffn_out"][l],
    )


def _local_step(x, tgt, w):
    depth = w["norm1_w"].shape[0]
    n_ssm_heads = w["dt_bias"].shape[1]
    hg = w["rel_bias"].shape[1] // N_GROUPS_ATTN
    onehots = [(_rel_buckets(dil)[:, :, None] == jnp.arange(N_REL_BUCKETS)[None, None, :]).astype(F32) for dil in DILATIONS]
    biases = [jnp.einsum("qkb,bh->hqk", oh, w["rel_bias"][:, g * hg:(g + 1) * hg].astype(F32), precision=HIGHEST)
              for g, oh in enumerate(onehots)]
    params = [_layer_params(l, w, n_ssm_heads, hg) for l in range(depth)]
    h = x
    saved = []
    for l in range(depth):
        h, sv = _layer_fwd(h, params[l], biases, f"l{l}_")
        saved.append(sv)
    loss, dh, g_final = _loss_head(h, w["final_norm_w"], tgt, name="loss_head")
    grads = [None] * depth
    dbias_tot = [jnp.zeros(b.shape, F32) for b in biases]
    for l in reversed(range(depth)):
        dh, grads[l], dbiases = _layer_bwd(dh, params[l], saved[l], biases, f"l{l}_")
        dbias_tot = [a + b for a, b in zip(dbias_tot, dbiases)]
    out = {k: [gl[k] for gl in grads] if k in MATRICES else jnp.stack([gl[k] for gl in grads]) for k in grads[0]}
    out["final_norm_w"] = g_final
    drel = []
    for g, (oh, db) in enumerate(zip(onehots, dbias_tot)):
        oh_t = jnp.pad(oh.reshape(-1, N_REL_BUCKETS).T, ((0, LANES - N_REL_BUCKETS), (0, 0)))
        db_rows = jnp.pad(db.reshape(hg, -1), ((0, LANES - hg), (0, 0)))
        drel.append(_matmul(oh_t, db_rows, tb=True, name=f"rel_bias_fold{g}")[:N_REL_BUCKETS, :hg])
    out["rel_bias"] = jnp.concatenate(drel, axis=1)
    return loss, dh, out


MATRICES = ("w_in", "w_attn_branch", "w_ssm_branch", "w_out", "w_ffn_in", "w_ffn_out")
COL_SHARDED = ("w_in", "w_attn_branch", "w_ffn_in")
SMALL = ("norm1_w", "conv_b", "dt_bias", "a_log", "d_skip", "ssm_norm_w", "norm2_w", "rel_bias", "final_norm_w")
WEIGHTS = ("norm1_w", "w_in", "conv_w", "conv_b", "dt_bias", "a_log", "d_skip", "ssm_norm_w", "w_attn_branch",
           "w_ssm_branch", "w_out", "norm2_w", "w_ffn_in", "w_ffn_out", "rel_bias", "final_norm_w")
SMALL_COLS = 1024


def _unshard(name, g):
    _, depth, r, c = g.shape
    if name in COL_SHARDED or name == "conv_w":
        return jnp.transpose(g, (1, 2, 0, 3)).reshape(depth, r, N_CHIPS * c)
    return jnp.transpose(g, (1, 0, 2, 3)).reshape(depth, N_CHIPS * r, c)


def _to_shards(name, g):
    r, c = g.shape
    if name in COL_SHARDED:
        return jnp.transpose(g.reshape(r, N_CHIPS, c // N_CHIPS), (1, 0, 2))
    return g.reshape(N_CHIPS, r // N_CHIPS, c)


def kernel(x, norm1_w, w_in, conv_w, conv_b, dt_bias, a_log, d_skip, ssm_norm_w, w_attn_branch, w_ssm_branch, w_out, norm2_w, w_ffn_in, w_ffn_out, rel_bias, final_norm_w, loss_target, m_norm1_w, m_w_in, m_conv_w, m_conv_b, m_dt_bias, m_a_log, m_d_skip, m_ssm_norm_w, m_w_attn_branch, m_w_ssm_branch, m_w_out, m_norm2_w, m_w_ffn_in, m_w_ffn_out, m_rel_bias, m_final_norm_w, v_norm1_w, v_w_in, v_conv_w, v_conv_b, v_dt_bias, v_a_log, v_d_skip, v_ssm_norm_w, v_w_attn_branch, v_w_ssm_branch, v_w_out, v_norm2_w, v_w_ffn_in, v_w_ffn_out, v_rel_bias, v_final_norm_w):
    env = dict(locals())
    wts = {k: env[k] for k in WEIGHTS}
    mom = {k: env["m_" + k] for k in WEIGHTS}
    var = {k: env["v_" + k] for k in WEIGHTS}
    chip = 2 * lax.axis_index("x") + lax.axis_index("y")
    core = lax.axis_index("c")

    shards = [wts[k].astype(BF16) for k in MATRICES] + [conv_w]
    gathered = _gather_shards(shards, name="gather_weights")
    full = {k: wts[k] for k in SMALL}
    for k, own, g in zip(MATRICES + ("conv_w",), shards, gathered):
        full[k] = _unshard(k, lax.dynamic_update_index_in_dim(g, own, chip, axis=0))

    loss, dx, grads = _local_step(x[0], loss_target[0], full)
    loss = lax.psum(loss, ("x", "y", "c"))

    core1 = core.reshape(1).astype(jnp.int32)
    from_pair = _pair_swap_layers([grads[k][0] for k in MATRICES], [grads[k][1] for k in MATRICES], name="reduce_pair_swap")
    scatter_in = [_to_shards(k, _add_own_layer(grads[k][0], grads[k][1], got, core1, name="reduce_pair_add_" + k))
                  for k, got in zip(MATRICES, from_pair)]
    scattered = _scatter_to_chips(scatter_in, name="reduce_scatter")
    own_layer = []
    for k, sent, got in zip(MATRICES, scatter_in, scattered):
        got = lax.dynamic_update_index_in_dim(got, lax.dynamic_index_in_dim(sent, chip, axis=0, keepdims=False), chip, axis=0)
        own_layer.append(_sum_chips(got, name="reduce_sum_" + k))
    other_layer = _pair_swap(own_layer, name="reduce_pair_exchange")
    reduced = {}

    small_names = SMALL + ("conv_w",)
    flat = jnp.concatenate([grads[k].reshape(-1) for k in small_names])
    n_small = flat.shape[0]
    rows = -(-n_small // SMALL_COLS)
    rows = -(-rows // 8) * 8
    flat = jnp.pad(flat, (0, rows * SMALL_COLS - n_small)).reshape(rows, SMALL_COLS)
    flat = _allreduce_small(flat, name="allreduce_small").reshape(-1)
    pos = 0
    for k in small_names:
        size = math.prod(grads[k].shape)
        reduced[k] = flat[pos:pos + size].reshape(grads[k].shape)
        pos += size
    cs = conv_w.shape[2]
    reduced["conv_w"] = lax.dynamic_slice_in_dim(reduced["conv_w"], chip * cs, cs, axis=2)

    delta, new_m, new_v = {}, {}, {}
    for k, own, other in zip(MATRICES, own_layer, other_layer):
        reduced[k], delta[k], new_m[k], new_v[k] = _adamw_layers(wts[k], own, other, mom[k], var[k], core1, name="adamw_" + k)
    pack = lambda src: jnp.pad(jnp.concatenate([src[k].reshape(-1) for k in small_names]),
                               (0, rows * SMALL_COLS - n_shard)).reshape(rows, SMALL_COLS)
    n_shard = sum(math.prod(wts[k].shape) for k in small_names)
    d_, m_, v_ = _adamw(pack(wts), pack(reduced), pack(mom), pack(var), name="adamw_small")
    pos = 0
    for k in small_names:
        size = math.prod(wts[k].shape)
        for dst, src in ((delta, d_), (new_m, m_), (new_v, v_)):
            dst[k] = src.reshape(-1)[pos:pos + size].reshape(wts[k].shape)
        pos += size

    return (loss, dx[None], *[reduced[k] for k in WEIGHTS], *[delta[k] for k in WEIGHTS],
            *[new_m[k] for k in WEIGHTS], *[new_v[k] for k in WEIGHTS])
```

```python
import functools
import math

import jax
import jax.numpy as jnp
from jax import lax
from jax.experimental import pallas as pl
from jax.experimental.pallas import tpu as pltpu

F32, BF16 = jnp.float32, jnp.bfloat16
SDS = jax.ShapeDtypeStruct
BS = pl.BlockSpec
MESH = pl.DeviceIdType.MESH
HIGHEST = lax.Precision.HIGHEST

EPS = 1e-6
HEAD_DIM = 64
ATTN_BLOCK = 128
DILATIONS = (1, 4, 16)
N_GROUPS_ATTN = len(DILATIONS)
N_STEPS = 128
N_REL_BUCKETS = 32
REL_MAX_DISTANCE = 2048
SSM_GROUPS = 4
D_STATE = 128
CHUNK = 128
CONV_WIDTH = 4
HALO = 16
LANES = 128
N_CHIPS = 4
N_DEV = 8
VMEM_LIMIT_BYTES = 48 * 1024 * 1024

ADAM_LR, ADAM_B1, ADAM_B2, ADAM_EPS, ADAM_WD, ADAM_STEP = 0.001, 0.9, 0.999, 1e-08, 0.01, 10

NT = (((1,), (1,)), ((), ()))
TN = (((0,), (0,)), ((), ()))
NN = (((1,), (0,)), ((), ()))


def _cp(*sem):
    return pltpu.CompilerParams(dimension_semantics=sem if sem else None, vmem_limit_bytes=VMEM_LIMIT_BYTES)


def _pick(n, cands):
    for c in cands:
        if n % c == 0:
            return c
    raise ValueError(f"no block size of {cands} divides {n}")


def _divisors(n, cap):
    out = [c for c in range(LANES, min(n, cap) + 1, LANES) if n % c == 0]
    return out or [n]


def _wide(n, cap=2048):
    return _divisors(n, cap)[-1]


MXU_FLOPS = 9.0e14
HBM_BYTES_PER_S = 3.0e12
ACC_BYTES_PER_S = 4.0e12
GRID_STEP_S = 0.4e-6
TILE_VMEM_BYTES = 36 * 1024 * 1024


def _matmul_tiles(m, n, k, a_bytes, b_bytes, o_bytes, has_res):
    best = None
    for tm in _divisors(m, 2048):
        for tn in _divisors(n, 2048):
            for tk in _divisors(k, 4096):
                ni, nj, nk = m // tm, n // tn, k // tk
                vmem = 2 * (tm * tk * a_bytes + tk * tn * b_bytes + tm * tn * (o_bytes + (4 if has_res else 0)))
                vmem += tm * tn * 4 * (2 if nk > 1 else 1) + (tm * tk + tk * tn) * 2
                if vmem > TILE_VMEM_BYTES:
                    continue
                hbm = m * k * a_bytes * (nj if nk > 1 else 1) + k * n * b_bytes * (ni if nj * nk > 1 else 1)
                hbm += m * n * (o_bytes + (4 if has_res else 0))
                t = max(2.0 * m * n * k / MXU_FLOPS, hbm / HBM_BYTES_PER_S) + ni * nj * nk * GRID_STEP_S
                if nk > 1:
                    t += m * n * 8.0 * nk / ACC_BYTES_PER_S
                if best is None or t < best[0]:
                    best = (t, tm, tn, tk)
    assert best is not None, (m, n, k)
    return best[1:]


def _dot(a, b, dims=NN, precision=None):
    return lax.dot_general(a, b, dims, precision=precision, preferred_element_type=F32)


def _silu(x):
    return x / (1.0 + jnp.exp(-x))


def _sigmoid(x):
    return 1.0 / (1.0 + jnp.exp(-x))


def _dsilu(x):
    s = _sigmoid(x)
    return s * (1.0 + x * (1.0 - s))


def _matmul(a, b, *, name, ta=False, tb=False, out_dtype=F32, res=None):
    (kdim, m) = a.shape if ta else a.shape[::-1]
    (n, k2) = b.shape if tb else b.shape[::-1]
    assert kdim == k2, (a.shape, b.shape, ta, tb)
    tm, tn, tk = _matmul_tiles(m, n, kdim, a.dtype.itemsize, b.dtype.itemsize, jnp.dtype(out_dtype).itemsize, res is not None)
    nk = kdim // tk
    a_spec = BS((tk, tm), lambda i, j, k: (k, i)) if ta else BS((tm, tk), lambda i, j, k: (i, k))
    b_spec = BS((tn, tk), lambda i, j, k: (j, k)) if tb else BS((tk, tn), lambda i, j, k: (k, j))
    dims = (((0 if ta else 1,), (1 if tb else 0,)), ((), ()))
    has_res = res is not None

    def body(*refs):
        a_ref, b_ref = refs[:2]
        r_ref = refs[2] if has_res else None
        o_ref = refs[3] if has_res else refs[2]
        prod = _dot(a_ref[...].astype(BF16), b_ref[...].astype(BF16), dims)
        if nk == 1:
            o_ref[...] = (prod + r_ref[...] if has_res else prod).astype(o_ref.dtype)
            return
        acc = refs[-1]
        k = pl.program_id(2)

        @pl.when(k == 0)
        def _():
            acc[...] = prod

        @pl.when(k > 0)
        def _():
            acc[...] += prod

        @pl.when(k == nk - 1)
        def _():
            r = acc[...]
            if has_res:
                r = r + r_ref[...]
            o_ref[...] = r.astype(o_ref.dtype)

    in_specs = [a_spec, b_spec]
    args = [a, b]
    if has_res:
        in_specs.append(BS((tm, tn), lambda i, j, k: (i, j)))
        args.append(res)
    return pl.pallas_call(
        body, name=name, grid=(m // tm, n // tn, nk), in_specs=in_specs,
        out_specs=BS((tm, tn), lambda i, j, k: (i, j)), out_shape=SDS((m, n), out_dtype),
        scratch_shapes=[pltpu.VMEM((tm, tn), F32)] if nk > 1 else [],
        compiler_params=_cp("parallel", "parallel", "arbitrary"),
    )(*args)


def _rms_fwd(h, w, *, name):
    t, d = h.shape
    tm = _pick(t, (512, 256, 128))

    def body(h_ref, w_ref, o_ref):
        x = h_ref[...]
        r = lax.rsqrt(jnp.mean(x * x, axis=-1, keepdims=True) + EPS)
        o_ref[...] = (x * r * w_ref[...]).astype(BF16)

    return pl.pallas_call(
        body, name=name, grid=(t // tm,), in_specs=[BS((tm, d), lambda i: (i, 0)), BS((1, d), lambda i: (0, 0))],
        out_specs=BS((tm, d), lambda i: (i, 0)), out_shape=SDS((t, d), BF16), compiler_params=_cp("parallel"),
    )(h, w.reshape(1, d))


def _rms_bwd(h, w, dys, dres, *, name):
    t, d = h.shape
    tm = _pick(t, (512, 256, 128))
    n_dy = len(dys)

    def body(*refs):
        h_ref, w_ref = refs[:2]
        dy_refs = refs[2:2 + n_dy]
        dres_ref, dh_ref, dw_ref = refs[2 + n_dy:]
        x = h_ref[...]
        dy = dy_refs[0][...]
        for r_ in dy_refs[1:]:
            dy = dy + r_[...]
        r = lax.rsqrt(jnp.mean(x * x, axis=-1, keepdims=True) + EPS)
        g = dy * w_ref[...]
        proj = jnp.sum(g * x, axis=-1, keepdims=True) * (1.0 / d)
        dh_ref[...] = dres_ref[...] + r * g - x * (r * r * r) * proj

        @pl.when(pl.program_id(0) == 0)
        def _():
            dw_ref[...] = jnp.zeros_like(dw_ref)

        dw_ref[...] += jnp.sum(dy * x * r, axis=0, keepdims=True)

    row = BS((tm, d), lambda i: (i, 0))
    vec = BS((1, d), lambda i: (0, 0))
    dh, dw = pl.pallas_call(
        body, name=name, grid=(t // tm,), in_specs=[row, vec] + [row] * n_dy + [row],
        out_specs=[row, vec], out_shape=[SDS((t, d), F32), SDS((1, d), F32)], compiler_params=_cp("arbitrary"),
    )(h, w.reshape(1, d), *dys, dres)
    return dh, dw[0]


def _loss_head(h, w, tgt, *, name):
    t, d = h.shape
    tm = _pick(t, (512, 256, 128))

    def body(h_ref, w_ref, t_ref, loss_ref, dh_ref, dw_ref):
        x = h_ref[...]
        r = lax.rsqrt(jnp.mean(x * x, axis=-1, keepdims=True) + EPS)
        err = x * r * w_ref[...] - t_ref[...]
        loss_ref[...] = jnp.zeros(loss_ref.shape, F32) + 0.5 * jnp.sum(err * err) * (1.0 / d)
        dy = err * (1.0 / d)
        g = dy * w_ref[...]
        proj = jnp.sum(g * x, axis=-1, keepdims=True) * (1.0 / d)
        dh_ref[...] = r * g - x * (r * r * r) * proj

        @pl.when(pl.program_id(0) == 0)
        def _():
            dw_ref[...] = jnp.zeros_like(dw_ref)

        dw_ref[...] += jnp.sum(dy * x * r, axis=0, keepdims=True)

    row = BS((tm, d), lambda i: (i, 0))
    vec = BS((1, d), lambda i: (0, 0))
    loss, dh, dw = pl.pallas_call(
        body, name=name, grid=(t // tm,), in_specs=[row, vec, row],
        out_specs=[BS((1, 8, LANES), lambda i: (i, 0, 0)), row, vec],
        out_shape=[SDS((t // tm, 8, LANES), F32), SDS((t, d), F32), SDS((1, d), F32)], compiler_params=_cp("arbitrary"),
    )(h, w.reshape(1, d), tgt)
    return jnp.sum(loss[:, 0, 0]), dh, dw[0]


def _attn_masks(mb):
    qi = lax.broadcasted_iota(jnp.int32, (ATTN_BLOCK, 2 * ATTN_BLOCK), 0)
    kj = lax.broadcasted_iota(jnp.int32, (ATTN_BLOCK, 2 * ATTN_BLOCK), 1)
    steps = qi + ATTN_BLOCK - kj
    valid = (steps >= 0) & (steps <= N_STEPS) & ((kj >= ATTN_BLOCK) | (mb > 0))
    low = lax.broadcasted_iota(jnp.int32, (ATTN_BLOCK, LANES), 1) < HEAD_DIM
    return valid, low


def _to_residue_major(a, d):
    t, c = a.shape
    return a if d == 1 else a.reshape(t // d, d, c).transpose(1, 0, 2).reshape(t, c)


def _to_token_major(a, d):
    t, c = a.shape
    return a if d == 1 else a.reshape(d, t // d, c).transpose(1, 0, 2).reshape(t, c)


def _attn_specs(cols, nb, sub, awg, clamp):
    nsb = nb // sub

    def cur(col):
        return BS((sub * ATTN_BLOCK, awg), lambda r, sb: (r * nsb + clamp(sb), col))

    def prev(col):
        return BS((ATTN_BLOCK, awg), lambda r, sb: (r * nb + jnp.maximum(clamp(sb) * sub - 1, 0), col))

    return [cur(cols[0]), cur(cols[1]), prev(cols[1]), cur(cols[2]), prev(cols[2])]


def _sub_rows(s, n=1):
    return pl.ds(pl.multiple_of(s * ATTN_BLOCK, ATTN_BLOCK), n * ATTN_BLOCK)


def _attn_fwd(qkv, cols, bias, d, *, name):
    t = qkv.shape[0]
    hg = bias.shape[0]
    awg = hg * HEAD_DIM
    nb = t // d // ATTN_BLOCK
    sub = _pick(nb, (4, 2, 1))
    rows = sub * ATTN_BLOCK
    scale = HEAD_DIM ** -0.5

    def body(q_ref, kc_ref, kp_ref, vc_ref, vp_ref, b_ref, o_ref, l_ref, kext, vext):
        sb = pl.program_id(1)
        kext[pl.ds(0, ATTN_BLOCK), :] = kp_ref[...]
        kext[pl.ds(ATTN_BLOCK, rows), :] = kc_ref[...]
        vext[pl.ds(0, ATTN_BLOCK), :] = vp_ref[...]
        vext[pl.ds(ATTN_BLOCK, rows), :] = vc_ref[...]

        def one_block(s, carry):
            valid, low = _attn_masks(sb * sub + s)
            for pi in range(awg // LANES):
                sl = slice(pi * LANES, (pi + 1) * LANES)
                q2 = q_ref[_sub_rows(s), sl]
                k2 = kext[_sub_rows(s, 2), sl]
                v2 = vext[_sub_rows(s, 2), sl]
                outs, lses = [], []
                for hh in range(2):
                    mh = low if hh == 0 else jnp.logical_not(low)
                    qm = jnp.where(mh, q2, jnp.zeros_like(q2))
                    sc = _dot(qm, k2, NT) * scale + b_ref[pi * 2 + hh]
                    sc = jnp.where(valid, sc, -jnp.inf)
                    m = jnp.max(sc, axis=-1, keepdims=True)
                    p = jnp.exp(sc - m)
                    den = jnp.sum(p, axis=-1, keepdims=True)
                    outs.append(_dot(p.astype(BF16), v2) / den)
                    lses.append(jnp.broadcast_to(m + jnp.log(den), (ATTN_BLOCK, LANES)))
                o_ref[_sub_rows(s), sl] = jnp.where(low, outs[0], outs[1])
                l_ref[_sub_rows(s), sl] = jnp.where(low, lses[0], lses[1])
            return carry

        lax.fori_loop(0, sub, one_block, 0)

    out_spec = BS((rows, awg), lambda r, sb: (r * (nb // sub) + sb, 0))
    return pl.pallas_call(
        body, name=name, grid=(d, nb // sub),
        in_specs=_attn_specs(cols, nb, sub, awg, lambda sb: sb) + [BS(bias.shape, lambda r, sb: (0, 0, 0))],
        out_specs=[out_spec, out_spec], out_shape=[SDS((t, awg), F32)] * 2,
        scratch_shapes=[pltpu.VMEM((rows + ATTN_BLOCK, awg), BF16)] * 2, compiler_params=_cp("parallel", "parallel"),
    )(*([qkv] * 5), bias)


def _attn_bwd(qkv, cols, bias, lse, do, corr, d, *, name):
    t = qkv.shape[0]
    hg = bias.shape[0]
    awg = hg * HEAD_DIM
    nb = t // d // ATTN_BLOCK
    sub = _pick(nb, (4, 2, 1))
    nsb = nb // sub
    rows = sub * ATTN_BLOCK
    scale = HEAD_DIM ** -0.5

    def body(q_ref, kc_ref, kp_ref, vc_ref, vp_ref, b_ref, l_ref, do_ref, c_ref, dq_ref, dk_ref, dv_ref, db_ref,
             kext, vext, dkext, dvext, ck, cv):
        r, sb = pl.program_id(0), pl.program_id(1)

        @pl.when((r == 0) & (sb == 0))
        def _():
            db_ref[...] = jnp.zeros_like(db_ref)

        @pl.when(sb == 0)
        def _():
            ck[...] = jnp.zeros_like(ck)
            cv[...] = jnp.zeros_like(cv)

        @pl.when(sb < nsb)
        def _():
            kext[pl.ds(0, ATTN_BLOCK), :] = kp_ref[...]
            kext[pl.ds(ATTN_BLOCK, rows), :] = kc_ref[...]
            vext[pl.ds(0, ATTN_BLOCK), :] = vp_ref[...]
            vext[pl.ds(ATTN_BLOCK, rows), :] = vc_ref[...]
            dkext[...] = jnp.zeros_like(dkext)
            dvext[...] = jnp.zeros_like(dvext)

            def one_block(s, carry):
                valid, low = _attn_masks(sb * sub + s)
                for pi in range(awg // LANES):
                    sl = slice(pi * LANES, (pi + 1) * LANES)
                    q2 = q_ref[_sub_rows(s), sl]
                    k2 = kext[_sub_rows(s, 2), sl]
                    v2 = vext[_sub_rows(s, 2), sl]
                    do2 = do_ref[_sub_rows(s), sl]
                    lse2 = l_ref[_sub_rows(s), sl]
                    corr2 = c_ref[_sub_rows(s), sl]
                    dk2 = jnp.zeros((2 * ATTN_BLOCK, LANES), F32)
                    dv2 = jnp.zeros((2 * ATTN_BLOCK, LANES), F32)
                    dqs = []
                    for hh in range(2):
                        mh = low if hh == 0 else jnp.logical_not(low)
                        qm = jnp.where(mh, q2, jnp.zeros_like(q2))
                        dom = jnp.where(mh, do2, jnp.zeros_like(do2))
                        lse_c = jnp.max(jnp.where(mh, lse2, -jnp.inf), axis=-1, keepdims=True)
                        corr_c = jnp.max(jnp.where(mh, corr2, -jnp.inf), axis=-1, keepdims=True)
                        sc = _dot(qm, k2, NT) * scale + b_ref[pi * 2 + hh]
                        p = jnp.exp(jnp.where(valid, sc, -jnp.inf) - lse_c)
                        ds = p * (_dot(dom, v2, NT) + corr_c)
                        db_ref[pi * 2 + hh] += ds
                        dsb = ds.astype(BF16)
                        dqs.append(_dot(dsb, k2) * scale)
                        dk2 = dk2 + _dot(dsb, qm, TN) * scale
                        dv2 = dv2 + _dot(p.astype(BF16), dom, TN)
                    dq_ref[_sub_rows(s), sl] = jnp.where(low, dqs[0], dqs[1]).astype(BF16)
                    dkext[_sub_rows(s, 2), sl] += dk2
                    dvext[_sub_rows(s, 2), sl] += dv2
                return carry

            lax.fori_loop(0, sub, one_block, 0)
            head, tail = pl.ds(0, rows - ATTN_BLOCK), pl.ds(rows - ATTN_BLOCK, ATTN_BLOCK)
            for out_ref, carry_ref, ext in ((dk_ref, ck, dkext), (dv_ref, cv, dvext)):
                if sub > 1:
                    out_ref[head, :] = carry_ref[head, :].astype(BF16)
                out_ref[tail, :] = (carry_ref[tail, :] + ext[pl.ds(0, ATTN_BLOCK), :]).astype(BF16)
                carry_ref[...] = ext[pl.ds(ATTN_BLOCK, rows), :]

        @pl.when(sb == nsb)
        def _():
            dk_ref[...] = ck[...].astype(BF16)
            dv_ref[...] = cv[...].astype(BF16)

    clamp = lambda sb: jnp.minimum(sb, nsb - 1)
    cur = BS((rows, awg), lambda r, sb: (r * nsb + clamp(sb), 0))
    prev = BS((rows, awg), lambda r, sb: (r * nsb + jnp.maximum(sb - 1, 0), 0))
    bias_spec = BS(bias.shape, lambda r, sb: (0, 0, 0))
    return pl.pallas_call(
        body, name=name, grid=(d, nsb + 1),
        in_specs=_attn_specs(cols, nb, sub, awg, clamp) + [bias_spec, cur, cur, cur],
        out_specs=[cur, prev, prev, bias_spec],
        out_shape=[SDS((t, awg), BF16)] * 3 + [SDS(bias.shape, F32)],
        scratch_shapes=[pltpu.VMEM((rows + ATTN_BLOCK, awg), BF16)] * 2 + [pltpu.VMEM((rows + ATTN_BLOCK, awg), F32)] * 2
        + [pltpu.VMEM((rows, awg), F32)] * 2,
        compiler_params=_cp("arbitrary", "arbitrary"),
    )(*([qkv] * 5), bias, lse, do, corr)


def _head_sum(x, low):
    a = jnp.sum(jnp.where(low, x, 0.0), axis=-1, keepdims=True)
    b = jnp.sum(jnp.where(low, 0.0, x), axis=-1, keepdims=True)
    return jnp.where(low, a, b)


def _combine_weights(lses):
    mx = jnp.maximum(jnp.maximum(lses[0], lses[1]), lses[2])
    es = [jnp.exp(l - mx) for l in lses]
    tot = es[0] + es[1] + es[2]
    return [e / tot for e in es]


def _combine_fwd(os_, lses, *, name):
    t, awg = os_[0].shape
    tm = _pick(t, (512, 256, 128))

    def body(o0, o1, o2, l0, l1, l2, out_ref):
        al = _combine_weights([l0[...], l1[...], l2[...]])
        out_ref[...] = (al[0] * o0[...] + al[1] * o1[...] + al[2] * o2[...]).astype(BF16)

    blk = BS((tm, awg), lambda i: (i, 0))
    return pl.pallas_call(
        body, name=name, grid=(t // tm,), in_specs=[blk] * 6, out_specs=blk,
        out_shape=SDS((t, awg), BF16), compiler_params=_cp("parallel"),
    )(*os_, *lses)


def _combine_bwd(dattn, os_, lses, *, name):
    t, awg = dattn.shape
    tm = _pick(t, (512, 256, 128))

    def body(da_ref, o0, o1, o2, l0, l1, l2, d0, d1, d2, c0, c1, c2):
        low = lax.broadcasted_iota(jnp.int32, (tm, LANES), 1) < HEAD_DIM
        for pi in range(awg // LANES):
            sl = slice(pi * LANES, (pi + 1) * LANES)
            da = da_ref[:, sl]
            al = _combine_weights([l0[:, sl], l1[:, sl], l2[:, sl]])
            tot = jnp.zeros((tm, LANES), F32)
            for a, o in zip(al, (o0, o1, o2)):
                tot = tot + a * _head_sum(da * o[:, sl], low)
            for a, d_ref, c_ref in zip(al, (d0, d1, d2), (c0, c1, c2)):
                d_ref[:, sl] = (a * da).astype(BF16)
                c_ref[:, sl] = -a * tot

    blk = BS((tm, awg), lambda i: (i, 0))
    outs = pl.pallas_call(
        body, name=name, grid=(t // tm,), in_specs=[blk] * 7, out_specs=[blk] * 6,
        out_shape=[SDS((t, awg), BF16)] * 3 + [SDS((t, awg), F32)] * 3, compiler_params=_cp("parallel"),
    )(dattn, *os_, *lses)
    return outs[:3], outs[3:]


def _conv_block(width, *offsets):
    for c in (512, 256, 128):
        if width % c == 0 and all(o % c == 0 for o in offsets):
            return c
    raise ValueError((width, offsets))


CONV_ROWS = 32


def _conv_pre(x_ref, halo_ref, w_ref, b_ref, ext, i, tm):
    ext[pl.ds(0, HALO), :] = jnp.where(i > 0, halo_ref[...].astype(F32), 0.0)
    ext[pl.ds(HALO, tm), :] = x_ref[...].astype(F32)
    taps = [w_ref[pl.ds(k, 1), :] for k in range(CONV_WIDTH)]
    bias = b_ref[...]
    for r0 in range(0, tm, CONV_ROWS):
        xs = [ext[pl.ds(HALO + r0 - (CONV_WIDTH - 1) + k, CONV_ROWS), :] for k in range(CONV_WIDTH)]
        pre = bias + taps[0] * xs[0]
        for k in range(1, CONV_WIDTH):
            pre = pre + taps[k] * xs[k]
        yield r0, pre, xs


def _fold8(v):
    return jnp.sum(v.reshape(v.shape[0] // 8, 8, v.shape[1]), axis=0)


def _conv_fwd(proj, off, w, b, *, name):
    t = proj.shape[0]
    c = w.shape[1]
    cw = _conv_block(c, off)
    tm = _pick(t, (512, 256, 128))
    ob = off // cw

    def body(x_ref, halo_ref, w_ref, b_ref, o_ref, ext):
        for r0, pre, _ in _conv_pre(x_ref, halo_ref, w_ref, b_ref, ext, pl.program_id(1), tm):
            o_ref[pl.ds(r0, CONV_ROWS), :] = _silu(pre).astype(BF16)

    return pl.pallas_call(
        body, name=name, grid=(c // cw, t // tm),
        in_specs=[BS((tm, cw), lambda j, i: (i, ob + j)),
                  BS((HALO, cw), lambda j, i: (jnp.maximum(i * (tm // HALO) - 1, 0), ob + j)),
                  BS((CONV_WIDTH, cw), lambda j, i: (0, j)), BS((1, cw), lambda j, i: (0, j))],
        out_specs=BS((tm, cw), lambda j, i: (i, j)), out_shape=SDS((t, c), BF16),
        scratch_shapes=[pltpu.VMEM((HALO + tm, cw), F32)], compiler_params=_cp("parallel", "arbitrary"),
    )(proj, proj, w, b.reshape(1, c))


def _conv_bwd(proj, off, w, b, dxc, *, name):
    t = proj.shape[0]
    c = w.shape[1]
    cw = _conv_block(c, off)
    tm = _pick(t, (512, 256, 128))
    ob = off // cw
    nt = t // tm

    def body_pre(x_ref, halo_ref, w_ref, b_ref, d_ref, dp_ref, dw_ref, db_ref, ext):
        i = pl.program_id(1)

        @pl.when(i == 0)
        def _():
            dw_ref[...] = jnp.zeros_like(dw_ref)
            db_ref[...] = jnp.zeros_like(db_ref)

        db_acc = jnp.zeros((8, cw), F32)
        dw_acc = [jnp.zeros((8, cw), F32) for _ in range(CONV_WIDTH)]
        for r0, pre, xs in _conv_pre(x_ref, halo_ref, w_ref, b_ref, ext, i, tm):
            dpre = d_ref[pl.ds(r0, CONV_ROWS), :] * _dsilu(pre)
            dp_ref[pl.ds(r0, CONV_ROWS), :] = dpre
            db_acc = db_acc + _fold8(dpre)
            dw_acc = [acc + _fold8(dpre * x) for acc, x in zip(dw_acc, xs)]
        db_ref[...] += jnp.sum(db_acc, axis=0, keepdims=True)
        for k in range(CONV_WIDTH):
            dw_ref[pl.ds(k, 1), :] += jnp.sum(dw_acc[k], axis=0, keepdims=True)

    dpre, dw, db = pl.pallas_call(
        body_pre, name=name + "_pre", grid=(c // cw, nt),
        in_specs=[BS((tm, cw), lambda j, i: (i, ob + j)),
                  BS((HALO, cw), lambda j, i: (jnp.maximum(i * (tm // HALO) - 1, 0), ob + j)),
                  BS((CONV_WIDTH, cw), lambda j, i: (0, j)), BS((1, cw), lambda j, i: (0, j)),
                  BS((tm, cw), lambda j, i: (i, j))],
        out_specs=[BS((tm, cw), lambda j, i: (i, j)), BS((CONV_WIDTH, cw), lambda j, i: (0, j)), BS((1, cw), lambda j, i: (0, j))],
        out_shape=[SDS((t, c), F32), SDS((CONV_WIDTH, c), F32), SDS((1, c), F32)],
        scratch_shapes=[pltpu.VMEM((HALO + tm, cw), F32)], compiler_params=_cp("parallel", "arbitrary"),
    )(proj, proj, w, b.reshape(1, c), dxc)

    def body_in(dp_ref, nxt_ref, w_ref, dx_ref, ext):
        i = pl.program_id(1)
        ext[pl.ds(0, tm), :] = dp_ref[...]
        ext[pl.ds(tm, 8), :] = jnp.where(i < nt - 1, nxt_ref[...], 0.0)
        taps = [w_ref[pl.ds(k, 1), :] for k in range(CONV_WIDTH)]
        for r0 in range(0, tm, CONV_ROWS):
            dx = taps[CONV_WIDTH - 1] * ext[pl.ds(r0, CONV_ROWS), :]
            for k in range(CONV_WIDTH - 1):
                dx = dx + taps[k] * ext[pl.ds(r0 + CONV_WIDTH - 1 - k, CONV_ROWS), :]
            dx_ref[pl.ds(r0, CONV_ROWS), :] = dx.astype(BF16)

    dx = pl.pallas_call(
        body_in, name=name + "_in", grid=(c // cw, nt),
        in_specs=[BS((tm, cw), lambda j, i: (i, j)),
                  BS((8, cw), lambda j, i: (jnp.minimum((i + 1) * (tm // 8), t // 8 - 1), j)),
                  BS((CONV_WIDTH, cw), lambda j, i: (0, j))],
        out_specs=BS((tm, cw), lambda j, i: (i, j)), out_shape=SDS((t, c), BF16),
        scratch_shapes=[pltpu.VMEM((tm + 8, cw), F32)], compiler_params=_cp("parallel", "arbitrary"),
    )(dpre, dpre, w)
    return dx, dw, db[0]


def _softplus(x):
    return jnp.maximum(x, 0.0) + jnp.log(1.0 + jnp.exp(-jnp.abs(x)))


def _ssd_common(dtr_ref, bias_ref, a_ref):
    pre = dtr_ref[...] + bias_ref[...]
    dt = _softplus(pre)
    ri = lax.broadcasted_iota(jnp.int32, (CHUNK, CHUNK), 0)
    ci = lax.broadcasted_iota(jnp.int32, (CHUNK, CHUNK), 1)
    tril = ri >= ci
    la = _dot(tril.astype(F32), dt * a_ref[...], precision=HIGHEST)
    return pre, dt, la, la.T, tril


def _lane_col(x, lane, h):
    return jnp.sum(jnp.where(lane == h, x, 0.0), axis=-1, keepdims=True)


def _ssd_specs(rows, di, gw, cidx):
    nbx = di // LANES
    return [BS((rows, gw), lambda g, c: (cidx(c), g)),
            BS((rows, D_STATE), lambda g, c: (cidx(c), nbx + g)),
            BS((rows, D_STATE), lambda g, c: (cidx(c), nbx + SSM_GROUPS + g)),
            BS((rows, LANES), lambda g, c: (cidx(c), 0)),
            BS((1, LANES), lambda g, c: (0, 0)), BS((1, LANES), lambda g, c: (0, 0)),
            BS((1, gw), lambda g, c: (0, g))]


def _chunk_rows(ci):
    return pl.ds(pl.multiple_of(ci * CHUNK, CHUNK), CHUNK)


def _ssd_fwd(xc, dtr, dt_bias, a, dskip, *, name):
    t = xc.shape[0]
    di = xc.shape[1] - 2 * SSM_GROUPS * D_STATE
    gw = di // SSM_GROUPS
    hpg = gw // HEAD_DIM
    npair = gw // LANES
    nc = t // CHUNK
    cb = _pick(nc, (4, 2, 1))

    def body(xb_ref, bb_ref, cb_ref, dtrb_ref, bias_ref, a_ref, dsk_ref, yb_ref, st_ref, state):
        @pl.when(pl.program_id(1) == 0)
        def _():
            state[...] = jnp.zeros_like(state)

        g = pl.program_id(0)

        def one_chunk(ci, carry):
            rows = _chunk_rows(ci)
            ssd_chunk(xb_ref.at[rows], bb_ref.at[rows], cb_ref.at[rows], dtrb_ref.at[rows], bias_ref, a_ref, dsk_ref,
                      yb_ref.at[rows], st_ref, state, ci, g)
            return carry

        lax.fori_loop(0, cb, one_chunk, 0)

    def ssd_chunk(x_ref, b_ref, c_ref, dtr_ref, bias_ref, a_ref, dsk_ref, y_ref, st_ref, state, ci, g):
        st_ref[ci, 0] = state[...]
        _, dt, la, la_t, tril = _ssd_common(dtr_ref, bias_ref, a_ref)
        lane = lax.broadcasted_iota(jnp.int32, (CHUNK, LANES), 1)
        sub = lax.broadcasted_iota(jnp.int32, (LANES, CHUNK), 0)
        lane1 = lax.broadcasted_iota(jnp.int32, (1, LANES), 1)
        low, low1 = lane < HEAD_DIM, lane1 < HEAD_DIM
        last = lax.broadcasted_iota(jnp.int32, (CHUNK, LANES), 0) == CHUNK - 1
        lend = jnp.sum(jnp.where(last, la, 0.0), axis=0, keepdims=True)
        bm, cm = b_ref[...], c_ref[...]
        gmat = _dot(cm, bm, NT)
        for p in range(npair):
            sl = slice(p * LANES, (p + 1) * LANES)
            ps = slice(p * D_STATE, (p + 1) * D_STATE)
            x2 = x_ref[:, sl].astype(F32)
            cols, ms = [], []
            for hh in range(2):
                h = g * hpg + p * 2 + hh
                col_la = _lane_col(la, lane, h)
                row_la = jnp.sum(jnp.where(sub == h, la_t, 0.0), axis=0, keepdims=True)
                lend_h = _lane_col(lend, lane1, h)
                decay = jnp.exp(jnp.where(tril, col_la - row_la, -jnp.inf))
                ms.append((gmat * decay).astype(BF16))
                cols.append((_lane_col(dt, lane, h), jnp.exp(col_la), jnp.exp(lend_h - col_la), jnp.exp(lend_h)))
            pair = lambda k: jnp.where(low, cols[0][k], cols[1][k])
            xdt = x2 * pair(0)
            xdtb = xdt.astype(BF16)
            s2 = state[ps, :]
            y = jnp.where(low, _dot(ms[0], xdtb), _dot(ms[1], xdtb))
            y = y + pair(1) * _dot(cm, s2.astype(BF16)) + x2 * dsk_ref[:, sl]
            y_ref[:, sl] = y
            state[ps, :] = s2 * jnp.where(low1, cols[0][3], cols[1][3]) + _dot(bm, (xdt * pair(2)).astype(BF16), TN)

    y, st = pl.pallas_call(
        body, name=name, grid=(SSM_GROUPS, nc // cb), in_specs=_ssd_specs(cb * CHUNK, di, gw, lambda c: c),
        out_specs=[BS((cb * CHUNK, gw), lambda g, c: (c, g)), BS((cb, 1, npair * D_STATE, LANES), lambda g, c: (c, g, 0, 0))],
        out_shape=[SDS((t, di), F32), SDS((nc, SSM_GROUPS, npair * D_STATE, LANES), F32)],
        scratch_shapes=[pltpu.VMEM((npair * D_STATE, LANES), F32)], compiler_params=_cp("parallel", "arbitrary"),
    )(xc, xc, xc, dtr, dt_bias, a, dskip)
    return y, st


def _ssd_bwd(xc, dtr, dt_bias, a, dskip, st, dy, *, name):
    t = xc.shape[0]
    di = xc.shape[1] - 2 * SSM_GROUPS * D_STATE
    gw = di // SSM_GROUPS
    hpg = gw // HEAD_DIM
    npair = gw // LANES
    nc = t // CHUNK
    cb = _pick(nc, (4, 2, 1))
    rev = lambda c: nc // cb - 1 - c

    def body(xb_ref, bb_ref, cb_ref, dtrb_ref, bias_ref, a_ref, dsk_ref, st_ref, dyb_ref,
             dxb_ref, dbb_ref, dcb_ref, ddtrb_ref, da_ref, dbias_ref, ddsk_ref, dstate):
        @pl.when(pl.program_id(1) == 0)
        def _():
            dstate[...] = jnp.zeros_like(dstate)
            da_ref[...] = jnp.zeros_like(da_ref)
            dbias_ref[...] = jnp.zeros_like(dbias_ref)
            ddsk_ref[...] = jnp.zeros_like(ddsk_ref)

        g = pl.program_id(0)

        def one_chunk(j, carry):
            ci = cb - 1 - j
            rows = _chunk_rows(ci)
            ssd_chunk(xb_ref.at[rows], bb_ref.at[rows], cb_ref.at[rows], dtrb_ref.at[rows], bias_ref, a_ref, dsk_ref,
                      st_ref.at[ci], dyb_ref.at[rows], dxb_ref.at[rows], dbb_ref.at[rows], dcb_ref.at[rows],
                      ddtrb_ref.at[:, rows], da_ref, dbias_ref, ddsk_ref, dstate, g)
            return carry

        lax.fori_loop(0, cb, one_chunk, 0)

    def ssd_chunk(x_ref, b_ref, c_ref, dtr_ref, bias_ref, a_ref, dsk_ref, st_ref, dy_ref,
                  dx_ref, db_ref, dc_ref, ddtr_ref, da_ref, dbias_ref, ddsk_ref, dstate, g):
        pre, dt, la, la_t, tril = _ssd_common(dtr_ref, bias_ref, a_ref)
        lane = lax.broadcasted_iota(jnp.int32, (CHUNK, LANES), 1)
        sub = lax.broadcasted_iota(jnp.int32, (LANES, CHUNK), 0)
        lane1 = lax.broadcasted_iota(jnp.int32, (1, LANES), 1)
        low, low1 = lane < HEAD_DIM, lane1 < HEAD_DIM
        last = lax.broadcasted_iota(jnp.int32, (CHUNK, LANES), 0) == CHUNK - 1
        lend = jnp.sum(jnp.where(last, la, 0.0), axis=0, keepdims=True)
        bm, cm = b_ref[...], c_ref[...]
        gmat = _dot(cm, bm, NT)
        dg = jnp.zeros((CHUNK, CHUNK), F32)
        dla_cols = jnp.zeros((CHUNK, LANES), F32)
        dla_rows = jnp.zeros((LANES, CHUNK), F32)
        dtsum = jnp.zeros((CHUNK, LANES), F32)
        dbm = jnp.zeros((CHUNK, D_STATE), F32)
        dcm = jnp.zeros((CHUNK, D_STATE), F32)
        for p in range(npair):
            sl = slice(p * LANES, (p + 1) * LANES)
            ps = slice(p * D_STATE, (p + 1) * D_STATE)
            x2 = x_ref[:, sl].astype(F32)
            dy2 = dy_ref[:, sl]
            s2 = st_ref[0, ps, :]
            ds2 = dstate[ps, :]
            hs, cols, ms, decays = [], [], [], []
            for hh in range(2):
                h = g * hpg + p * 2 + hh
                col_la = _lane_col(la, lane, h)
                row_la = jnp.sum(jnp.where(sub == h, la_t, 0.0), axis=0, keepdims=True)
                lend_h = _lane_col(lend, lane1, h)
                decay = jnp.exp(jnp.where(tril, col_la - row_la, -jnp.inf))
                hs.append(h)
                decays.append(decay)
                ms.append(gmat * decay)
                cols.append((_lane_col(dt, lane, h), jnp.exp(col_la), jnp.exp(lend_h - col_la), jnp.exp(lend_h)))
            pair = lambda k: jnp.where(low, cols[0][k], cols[1][k])
            dtc, ec, eend = pair(0), pair(1), pair(2)
            eend_s = jnp.where(low1, cols[0][3], cols[1][3])
            xdt = x2 * dtc
            xdtb = xdt.astype(BF16)
            dys = dy2 * ec
            dysb = dys.astype(BF16)
            dxdt_state = eend * _dot(bm, ds2.astype(BF16))
            inter = dys * _dot(cm, s2.astype(BF16))
            u = dxdt_state * xdt
            sds = s2 * ds2
            dxdt = dxdt_state
            for hh in range(2):
                h = hs[hh]
                mh = low if hh == 0 else jnp.logical_not(low)
                dym = jnp.where(mh, dy2, 0.0).astype(BF16)
                dxdt = dxdt + _dot(ms[hh].astype(BF16), dym, TN)
                dm = _dot(dym, xdtb, NT)
                w = dm * ms[hh]
                dg = dg + dm * decays[hh]
                u_col = jnp.sum(jnp.where(mh, u, 0.0), axis=-1, keepdims=True)
                dlend = jnp.sum(u_col, axis=0, keepdims=True) + cols[hh][3] * jnp.sum(jnp.where(low1 if hh == 0 else jnp.logical_not(low1), jnp.sum(sds, axis=0, keepdims=True), 0.0), axis=-1, keepdims=True)
                col = jnp.sum(w, axis=-1, keepdims=True) + jnp.sum(jnp.where(mh, inter, 0.0), axis=-1, keepdims=True) - u_col
                dla_cols = dla_cols + jnp.where(lane == h, col + jnp.where(last, dlend, 0.0), 0.0)
                dla_rows = dla_rows - jnp.where(sub == h, jnp.sum(w, axis=0, keepdims=True), 0.0)
            for hh in range(2):
                mh = low if hh == 0 else jnp.logical_not(low)
                dtsum = dtsum + jnp.where(lane == hs[hh], jnp.sum(jnp.where(mh, dxdt * x2, 0.0), axis=-1, keepdims=True), 0.0)
            dcm = dcm + _dot(dysb, s2.astype(BF16), NT)
            dbm = dbm + _dot((xdt * eend).astype(BF16), ds2.astype(BF16), NT)
            dstate[ps, :] = ds2 * eend_s + _dot(cm, dysb, TN)
            dx_ref[:, sl] = dxdt * dtc + dy2 * dsk_ref[:, sl]
            ddsk_ref[:, sl] += jnp.sum(dy2 * x2, axis=0, keepdims=True)
        dgb = dg.astype(BF16)
        dc_ref[...] = dcm + _dot(dgb, bm)
        db_ref[...] = dbm + _dot(dgb, cm, TN)
        dla = dla_cols + dla_rows.T
        triu = lax.broadcasted_iota(jnp.int32, (CHUNK, CHUNK), 0) <= lax.broadcasted_iota(jnp.int32, (CHUNK, CHUNK), 1)
        ddta = _dot(triu.astype(F32), dla, precision=HIGHEST)
        ddt = ddta * a_ref[...] + dtsum
        da_ref[0] += jnp.sum(ddta * dt, axis=0, keepdims=True)
        ddtr = ddt * _sigmoid(pre)
        ddtr_ref[0] = ddtr
        dbias_ref[0] += jnp.sum(ddtr, axis=0, keepdims=True)

    vec = BS((1, 1, LANES), lambda g, c: (g, 0, 0))
    outs = pl.pallas_call(
        body, name=name, grid=(SSM_GROUPS, nc // cb),
        in_specs=_ssd_specs(cb * CHUNK, di, gw, rev) + [BS((cb, 1, npair * D_STATE, LANES), lambda g, c: (rev(c), g, 0, 0)),
                                                        BS((cb * CHUNK, gw), lambda g, c: (rev(c), g))],
        out_specs=[BS((cb * CHUNK, gw), lambda g, c: (rev(c), g)), BS((cb * CHUNK, D_STATE), lambda g, c: (rev(c), g)),
                   BS((cb * CHUNK, D_STATE), lambda g, c: (rev(c), g)), BS((1, cb * CHUNK, LANES), lambda g, c: (g, rev(c), 0)),
                   vec, vec, BS((1, gw), lambda g, c: (0, g))],
        out_shape=[SDS((t, di), F32), SDS((t, SSM_GROUPS * D_STATE), F32), SDS((t, SSM_GROUPS * D_STATE), F32),
                   SDS((SSM_GROUPS, t, LANES), F32), SDS((SSM_GROUPS, 1, LANES), F32), SDS((SSM_GROUPS, 1, LANES), F32),
                   SDS((1, di), F32)],
        scratch_shapes=[pltpu.VMEM((npair * D_STATE, LANES), F32)], compiler_params=_cp("parallel", "arbitrary"),
    )(xc, xc, xc, dtr, dt_bias, a, dskip, st, dy)
    return outs


def _gate_norm_fwd(y, proj, zoff, w, *, name):
    t, di = y.shape
    gw = di // SSM_GROUPS
    tm = _pick(t, (512, 256, 128))
    zb = zoff // gw

    def body(y_ref, z_ref, w_ref, o_ref):
        yg = y_ref[...] * _silu(z_ref[...].astype(F32))
        r = lax.rsqrt(jnp.mean(yg * yg, axis=-1, keepdims=True) + EPS)
        o_ref[...] = (yg * r * w_ref[...]).astype(BF16)

    return pl.pallas_call(
        body, name=name, grid=(t // tm, SSM_GROUPS),
        in_specs=[BS((tm, gw), lambda i, g: (i, g)), BS((tm, gw), lambda i, g: (i, zb + g)), BS((1, gw), lambda i, g: (0, g))],
        out_specs=BS((tm, gw), lambda i, g: (i, g)), out_shape=SDS((t, di), BF16), compiler_params=_cp("parallel", "parallel"),
    )(y, proj, w.reshape(1, di))


def _gate_norm_bwd(dssm, y, proj, zoff, w, *, name):
    t, di = y.shape
    gw = di // SSM_GROUPS
    tm = _pick(t, (512, 256, 128))
    zb = zoff // gw

    def body(d_ref, y_ref, z_ref, w_ref, dy_ref, dz_ref, dw_ref):
        z = z_ref[...].astype(F32)
        yv = y_ref[...]
        sz = _silu(z)
        yg = yv * sz
        r = lax.rsqrt(jnp.mean(yg * yg, axis=-1, keepdims=True) + EPS)
        n = yg * r
        d = d_ref[...]
        dn = d * w_ref[...]
        dyg = r * (dn - n * jnp.mean(dn * n, axis=-1, keepdims=True))
        dy_ref[...] = dyg * sz
        dz_ref[...] = (dyg * yv * _dsilu(z)).astype(BF16)

        @pl.when(pl.program_id(1) == 0)
        def _():
            dw_ref[...] = jnp.zeros_like(dw_ref)

        dw_ref[...] += jnp.sum(d * n, axis=0, keepdims=True)

    blk = BS((tm, gw), lambda g, i: (i, g))
    dy, dz, dw = pl.pallas_call(
        body, name=name, grid=(SSM_GROUPS, t // tm),
        in_specs=[blk, blk, BS((tm, gw), lambda g, i: (i, zb + g)), BS((1, gw), lambda g, i: (0, g))],
        out_specs=[blk, blk, BS((1, gw), lambda g, i: (0, g))],
        out_shape=[SDS((t, di), F32), SDS((t, di), BF16), SDS((1, di), F32)], compiler_params=_cp("parallel", "arbitrary"),
    )(dssm, y, proj, w.reshape(1, di))
    return dy, dz, dw[0]


def _merge_fwd(proj, goff, ga, gs, *, name):
    t, d = ga.shape
    cw = _conv_block(d, goff)
    tm = _pick(t, (512, 256, 128))
    gb = goff // cw

    def body(g0, g1, a_ref, s_ref, o_ref):
        o_ref[...] = (_sigmoid(g0[...].astype(F32)) * a_ref[...] + _sigmoid(g1[...].astype(F32)) * s_ref[...]).astype(BF16)

    blk = BS((tm, cw), lambda i, j: (i, j))
    return pl.pallas_call(
        body, name=name, grid=(t // tm, d // cw),
        in_specs=[BS((tm, cw), lambda i, j: (i, gb + j)), BS((tm, cw), lambda i, j: (i, gb + d // cw + j)), blk, blk],
        out_specs=blk, out_shape=SDS((t, d), BF16), compiler_params=_cp("parallel", "parallel"),
    )(proj, proj, ga, gs)


def _merge_bwd(proj, goff, ga, gs, dm, *, name):
    t, d = ga.shape
    cw = _conv_block(d, goff)
    tm = _pick(t, (512, 256, 128))
    gb = goff // cw

    def body(g0, g1, a_ref, s_ref, dm_ref, da_ref, ds_ref, dg0_ref, dg1_ref):
        dmv = dm_ref[...]
        s0 = _sigmoid(g0[...].astype(F32))
        s1 = _sigmoid(g1[...].astype(F32))
        da_ref[...] = (s0 * dmv).astype(BF16)
        ds_ref[...] = (s1 * dmv).astype(BF16)
        dg0_ref[...] = (dmv * a_ref[...] * s0 * (1.0 - s0)).astype(BF16)
        dg1_ref[...] = (dmv * s_ref[...] * s1 * (1.0 - s1)).astype(BF16)

    blk = BS((tm, cw), lambda i, j: (i, j))
    return pl.pallas_call(
        body, name=name, grid=(t // tm, d // cw),
        in_specs=[BS((tm, cw), lambda i, j: (i, gb + j)), BS((tm, cw), lambda i, j: (i, gb + d // cw + j)), blk, blk, blk],
        out_specs=[blk] * 4, out_shape=[SDS((t, d), BF16)] * 4, compiler_params=_cp("parallel", "parallel"),
    )(proj, proj, ga, gs, dm)


def _swiglu_fwd(u, *, name):
    t, two_f = u.shape
    f = two_f // 2
    cw = _wide(f)
    tm = _pick(t, (512, 256, 128))

    def body(g_ref, u_ref, o_ref):
        o_ref[...] = (_silu(g_ref[...].astype(F32)) * u_ref[...].astype(F32)).astype(BF16)

    return pl.pallas_call(
        body, name=name, grid=(t // tm, f // cw),
        in_specs=[BS((tm, cw), lambda i, j: (i, j)), BS((tm, cw), lambda i, j: (i, f // cw + j))],
        out_specs=BS((tm, cw), lambda i, j: (i, j)), out_shape=SDS((t, f), BF16), compiler_params=_cp("parallel", "parallel"),
    )(u, u)


def _swiglu_bwd(u, df, *, name):
    t, two_f = u.shape
    f = two_f // 2
    cw = _wide(f)
    tm = _pick(t, (512, 256, 128))

    def body(g_ref, u_ref, d_ref, dg_ref, du_ref):
        gt = g_ref[...].astype(F32)
        d = d_ref[...].astype(F32)
        dg_ref[...] = (d * u_ref[...].astype(F32) * _dsilu(gt)).astype(BF16)
        du_ref[...] = (d * _silu(gt)).astype(BF16)

    blk = BS((tm, cw), lambda i, j: (i, j))
    return pl.pallas_call(
        body, name=name, grid=(t // tm, f // cw),
        in_specs=[blk, BS((tm, cw), lambda i, j: (i, f // cw + j)), blk],
        out_specs=[blk, blk], out_shape=[SDS((t, f), BF16)] * 2, compiler_params=_cp("parallel", "parallel"),
    )(u, u, df)


def _row_block(rows, cols, n_arrays):
    budget = VMEM_LIMIT_BYTES // 3
    for tr in (512, 256, 128, 64, 32, 16, 8):
        if rows % tr == 0 and tr * cols * 4 * n_arrays * 2 <= budget:
            return tr
    raise ValueError((rows, cols))


def _concat_cols(pieces, *, name):
    pieces = [p if isinstance(p, tuple) else (p, p.shape[1], 0) for p in pieces]
    rows, dtype = pieces[0][0].shape[0], pieces[0][0].dtype
    widths = [w for _, w, _ in pieces]
    total = sum(widths)
    assert all(w % LANES == 0 for w in widths) and all(a.dtype == dtype and a.shape[0] == rows for a, _, _ in pieces)
    tr = next(c for c in (512, 256, 128, 64, 32, 16) if rows % c == 0 and 4 * c * total * dtype.itemsize <= VMEM_LIMIT_BYTES // 2)

    def body(*refs):
        o_ref = refs[-1]
        off = 0
        for p_ref, w in zip(refs[:-1], widths):
            o_ref[:, off:off + w] = p_ref[...]
            off += w

    return pl.pallas_call(
        body, name=name, grid=(rows // tr,), in_specs=[BS((tr, w), lambda i, j=j: (i, j)) for _, w, j in pieces],
        out_specs=BS((tr, total), lambda i: (i, 0)), out_shape=SDS((rows, total), dtype), compiler_params=_cp("parallel"),
    )(*[a for a, _, _ in pieces])


def _add_own_layer(g0, g1, got, core, *, name):
    rows, cols = got.shape
    tr = _row_block(rows, cols, 4)

    def body(core_ref, g0_ref, g1_ref, got_ref, o_ref):
        o_ref[...] = (jnp.where(core_ref[0] == 0, g0_ref[...], g1_ref[...]) + got_ref[...]).astype(o_ref.dtype)

    blk = BS((tr, cols), lambda i, cr: (i, 0))
    grid_spec = pltpu.PrefetchScalarGridSpec(
        num_scalar_prefetch=1, grid=(rows // tr,),
        in_specs=[BS((tr, cols), lambda i, cr: (i * (1 - cr[0]), 0)), BS((tr, cols), lambda i, cr: (i * cr[0], 0)), blk],
        out_specs=blk)
    return pl.pallas_call(body, name=name, grid_spec=grid_spec, out_shape=SDS((rows, cols), BF16),
                          compiler_params=_cp("arbitrary"))(core, g0, g1, got)


def _sum_chips(a, *, name):
    _, rows, cols = a.shape
    tr = _row_block(rows, cols, 5)

    def body(a_ref, o_ref):
        o_ref[...] = ((a_ref[0].astype(F32) + a_ref[1].astype(F32)) + a_ref[2].astype(F32)) + a_ref[3].astype(F32)

    return pl.pallas_call(body, name=name, grid=(rows // tr,), in_specs=[BS((N_CHIPS, tr, cols), lambda i: (0, i, 0))],
                          out_specs=BS((tr, cols), lambda i: (i, 0)), out_shape=SDS((rows, cols), F32),
                          compiler_params=_cp("parallel"))(a)


def _adamw(w, g, m, v, *, name):
    rows, cols = w.shape
    tr = _row_block(rows, cols, 7) if rows % 8 == 0 else rows
    c1 = 1.0 - ADAM_B1 ** ADAM_STEP
    c2 = 1.0 - ADAM_B2 ** ADAM_STEP

    def body(w_ref, g_ref, m_ref, v_ref, d_ref, nm_ref, nv_ref):
        gv = g_ref[...]
        nm = ADAM_B1 * m_ref[...] + (1.0 - ADAM_B1) * gv
        nv = ADAM_B2 * v_ref[...] + (1.0 - ADAM_B2) * (gv * gv)
        d_ref[...] = -ADAM_LR * ((nm / c1) / (jnp.sqrt(nv / c2) + ADAM_EPS) + ADAM_WD * w_ref[...])
        nm_ref[...] = nm
        nv_ref[...] = nv

    blk = BS((tr, cols), lambda i: (i, 0))
    return pl.pallas_call(body, name=name, grid=(rows // tr,), in_specs=[blk] * 4, out_specs=[blk] * 3,
                          out_shape=[SDS((rows, cols), F32)] * 3, compiler_params=_cp("parallel"))(w, g, m, v)


def _adamw_layers(w, g_own, g_other, m, v, core, *, name):
    _, rows, cols = w.shape
    tr = _row_block(rows, cols, 9)
    c1 = 1.0 - ADAM_B1 ** ADAM_STEP
    c2 = 1.0 - ADAM_B2 ** ADAM_STEP

    def body(core_ref, w_ref, own_ref, oth_ref, m_ref, v_ref, g_ref, d_ref, nm_ref, nv_ref):
        gv = jnp.where(pl.program_id(0) == core_ref[0], own_ref[...], oth_ref[...])
        nm = ADAM_B1 * m_ref[0] + (1.0 - ADAM_B1) * gv
        nv = ADAM_B2 * v_ref[0] + (1.0 - ADAM_B2) * (gv * gv)
        g_ref[0] = gv
        d_ref[0] = -ADAM_LR * ((nm / c1) / (jnp.sqrt(nv / c2) + ADAM_EPS) + ADAM_WD * w_ref[0])
        nm_ref[0] = nm
        nv_ref[0] = nv

    own_here = lambda l, cr: 1 - (l - cr[0]) * (l - cr[0])
    slab = BS((1, tr, cols), lambda l, i, cr: (l, i, 0))
    grid_spec = pltpu.PrefetchScalarGridSpec(
        num_scalar_prefetch=1, grid=(2, rows // tr),
        in_specs=[slab, BS((tr, cols), lambda l, i, cr: (i * own_here(l, cr), 0)),
                  BS((tr, cols), lambda l, i, cr: (i * (1 - own_here(l, cr)), 0)), slab, slab],
        out_specs=[slab] * 4)
    return pl.pallas_call(body, name=name, grid_spec=grid_spec, out_shape=[SDS(w.shape, F32)] * 4,
                          compiler_params=_cp("arbitrary", "arbitrary"))(core, w, g_own, g_other, m, v)


ANY = BS(memory_space=pl.ANY)


def _place():
    x, y, c = lax.axis_index("x"), lax.axis_index("y"), lax.axis_index("c")
    return x, y, c, [(1 - x, y), (x, 1 - y), (1 - x, 1 - y)]


def _gather_shards(arrs, *, name):
    n = len(arrs)

    def body(*refs):
        ins, outs = refs[:n], refs[n:2 * n]
        send_sems, recv_sems, pass_send_sems, pass_recv_sems = refs[2 * n:]
        x, y, c, chips = _place()
        s = 2 * x + y

        def ici(i, j, src_chip, to):
            src = ins[i].at[c] if src_chip is None else outs[i].at[src_chip, c]
            return pltpu.make_async_remote_copy(
                src_ref=src, dst_ref=outs[i].at[s if src_chip is None else src_chip, c], send_sem=send_sems.at[i * 3 + j],
                recv_sem=recv_sems.at[i * 3 + j], device_id=to, device_id_type=MESH)

        def d2d(i, j, src_chip, layer):
            slab = outs[i].at[src_chip, layer]
            return pltpu.make_async_remote_copy(
                src_ref=slab, dst_ref=slab, send_sem=pass_send_sems.at[i * 3 + j], recv_sem=pass_recv_sems.at[i * 3 + j],
                device_id=(x, y, 1 - c), device_id_type=MESH)

        sent = []
        for i in range(n):
            for j, (px, py) in enumerate(chips):
                cp = ici(i, j, None, (px, py, c))
                cp.start()
                sent.append(cp)
        passed = []
        for i in range(n):
            for j, (px, py) in enumerate(chips):
                ici(i, j, 2 * px + py, (x, y, c)).wait_recv()
                cp = d2d(i, j, 2 * px + py, c)
                cp.start()
                passed.append(cp)
        for i in range(n):
            for j, (px, py) in enumerate(chips):
                d2d(i, j, 2 * px + py, 1 - c).wait_recv()
        for cp in sent + passed:
            cp.wait_send()

    return pl.pallas_call(
        body, name=name, in_specs=[ANY] * n, out_specs=[ANY] * n,
        out_shape=[SDS((N_CHIPS,) + a.shape, a.dtype) for a in arrs],
        scratch_shapes=[pltpu.SemaphoreType.DMA((3 * n,))] * 4,
    )(*arrs)


def _pair_swap_layers(layer0, layer1, *, name):
    n = len(layer0)

    def body(*refs):
        in0, in1, outs = refs[:n], refs[n:2 * n], refs[2 * n:3 * n]
        send_sems, recv_sems = refs[3 * n:]
        x, y, c, _ = _place()

        def copy(src, i):
            return pltpu.make_async_remote_copy(
                src_ref=src[i], dst_ref=outs[i], send_sem=send_sems.at[i], recv_sem=recv_sems.at[i],
                device_id=(x, y, 1 - c), device_id_type=MESH)

        @pl.when(c == 0)
        def _():
            for i in range(n):
                copy(in1, i).start()

        @pl.when(c == 1)
        def _():
            for i in range(n):
                copy(in0, i).start()

        for i in range(n):
            copy(in0, i).wait()

    return pl.pallas_call(
        body, name=name, in_specs=[ANY] * (2 * n), out_specs=[ANY] * n, out_shape=[SDS(a.shape, a.dtype) for a in layer0],
        scratch_shapes=[pltpu.SemaphoreType.DMA((n,)), pltpu.SemaphoreType.DMA((n,))],
    )(*layer0, *layer1)


def _scatter_to_chips(arrs, *, name):
    n = len(arrs)

    def body(*refs):
        ins, outs = refs[:n], refs[n:2 * n]
        send_sems, recv_sems = refs[2 * n:]
        x, y, c, chips = _place()
        s = 2 * x + y
        copies = []
        for i in range(n):
            for j, (px, py) in enumerate(chips):
                cp = pltpu.make_async_remote_copy(
                    src_ref=ins[i].at[2 * px + py], dst_ref=outs[i].at[s], send_sem=send_sems.at[i * 3 + j],
                    recv_sem=recv_sems.at[i * 3 + j], device_id=(px, py, c), device_id_type=MESH)
                cp.start()
                copies.append(cp)
        for cp in copies:
            cp.wait()

    return pl.pallas_call(
        body, name=name, in_specs=[ANY] * n, out_specs=[ANY] * n, out_shape=[SDS(a.shape, a.dtype) for a in arrs],
        scratch_shapes=[pltpu.SemaphoreType.DMA((3 * n,)), pltpu.SemaphoreType.DMA((3 * n,))],
    )(*arrs)


def _pair_swap(arrs, *, name):
    n = len(arrs)

    def body(*refs):
        ins, outs = refs[:n], refs[n:2 * n]
        send_sems, recv_sems = refs[2 * n:]
        x, y, c, _ = _place()
        copies = []
        for i in range(n):
            cp = pltpu.make_async_remote_copy(
                src_ref=ins[i], dst_ref=outs[i], send_sem=send_sems.at[i], recv_sem=recv_sems.at[i],
                device_id=(x, y, 1 - c), device_id_type=MESH)
            cp.start()
            copies.append(cp)
        for cp in copies:
            cp.wait()

    return pl.pallas_call(
        body, name=name, in_specs=[ANY] * n, out_specs=[ANY] * n, out_shape=[SDS(a.shape, a.dtype) for a in arrs],
        scratch_shapes=[pltpu.SemaphoreType.DMA((n,)), pltpu.SemaphoreType.DMA((n,))],
    )(*arrs)


def _allreduce_small(v, *, name):
    rows, cols = v.shape

    def body(v_ref, o_ref, gath, send_sems, recv_sems):
        x, y, c, _ = _place()
        me = 4 * x + 2 * y + c
        gath[me] = v_ref[...]
        copies = []
        for k in range(1, N_DEV):
            fx, fy, fc = (k >> 2) & 1, (k >> 1) & 1, k & 1
            peer = (1 - x if fx else x, 1 - y if fy else y, 1 - c if fc else c)
            cp = pltpu.make_async_remote_copy(
                src_ref=v_ref, dst_ref=gath.at[me], send_sem=send_sems.at[k - 1], recv_sem=recv_sems.at[k - 1],
                device_id=peer, device_id_type=MESH)
            cp.start()
            copies.append(cp)
        for cp in copies:
            cp.wait()
        acc = gath[0]
        for k in range(1, N_DEV):
            acc = acc + gath[k]
        o_ref[...] = acc

    vm = BS(memory_space=pltpu.VMEM)
    return pl.pallas_call(
        body, name=name, in_specs=[vm], out_specs=vm, out_shape=SDS((rows, cols), F32),
        scratch_shapes=[pltpu.VMEM((N_DEV, rows, cols), F32), pltpu.SemaphoreType.DMA((N_DEV - 1,)), pltpu.SemaphoreType.DMA((N_DEV - 1,))],
    )(v)


def _t5_bucket(dist):
    max_exact = N_REL_BUCKETS // 2
    d_f = jnp.maximum(dist, 1).astype(F32)
    large = max_exact + (jnp.log(d_f / max_exact) / math.log(REL_MAX_DISTANCE / max_exact) * (N_REL_BUCKETS - max_exact)).astype(jnp.int32)
    return jnp.where(dist < max_exact, dist, jnp.minimum(large, N_REL_BUCKETS - 1))


def _rel_buckets(dilation):
    qi = jnp.arange(ATTN_BLOCK)[:, None]
    kj = jnp.arange(2 * ATTN_BLOCK)[None, :]
    return _t5_bucket(jnp.clip(qi + ATTN_BLOCK - kj, 0, N_STEPS) * dilation)


def _layer_fwd(h, p, biases, lname):
    sv = {"h": h}
    xn1 = _rms_fwd(h, p["norm1_w"], name=lname + "norm1")
    proj = _matmul(xn1, p["w_main"], out_dtype=BF16, name=lname + "in_proj")
    dtr = _matmul(xn1, p["w_dt"], out_dtype=F32, name=lname + "in_proj_dt")
    xn1_rm, qkvs = [xn1], [proj]
    for g in range(1, N_GROUPS_ATTN):
        xn1_rm.append(_to_residue_major(xn1, DILATIONS[g]))
        qkvs.append(_matmul(xn1_rm[g], p["w_qkv"][g], out_dtype=BF16, name=f"{lname}in_proj_qkv{g}"))
    os_, lses, lses_rm = [], [], []
    for g, d in enumerate(DILATIONS):
        o, lse = _attn_fwd(qkvs[g], (0, 1, 2), biases[g], d, name=f"{lname}attn{g}")
        os_.append(_to_token_major(o, d))
        lses.append(_to_token_major(lse, d))
        lses_rm.append(lse)
    sv.update(xn1_rm=xn1_rm, qkvs=qkvs, lses_rm=lses_rm)
    attn = _combine_fwd(os_, lses, name=lname + "attn_combine")
    xc = _conv_fwd(proj, p["off_xbc"], p["conv_w"], p["conv_b"], name=lname + "conv")
    y, st = _ssd_fwd(xc, dtr, p["dt_bias"], p["a"], p["dskip"], name=lname + "ssd")
    ssm = _gate_norm_fwd(y, proj, p["off_z"], p["ssm_norm_w"], name=lname + "gate_norm")
    ga = _matmul(attn, p["w_attn_branch"], name=lname + "attn_branch")
    gs = _matmul(ssm, p["w_ssm_branch"], name=lname + "ssm_branch")
    merged = _merge_fwd(proj, p["off_gate"], ga, gs, name=lname + "merge")
    h1 = _matmul(merged, p["w_out"], res=h, name=lname + "out_proj")
    xn2 = _rms_fwd(h1, p["norm2_w"], name=lname + "norm2")
    u = _matmul(xn2, p["w_ffn_in"], out_dtype=BF16, name=lname + "ffn_in")
    f = _swiglu_fwd(u, name=lname + "swiglu")
    h2 = _matmul(f, p["w_ffn_out"], res=h1, name=lname + "ffn_out")
    sv.update(xn1=xn1, proj=proj, dtr=dtr, os=os_, lses=lses, attn=attn, xc=xc, y=y, st=st, ssm=ssm, ga=ga, gs=gs,
              merged=merged, h1=h1, xn2=xn2, u=u, f=f)
    return h2, sv


def _layer_bwd(dh2, p, sv, biases, lname):
    gr = {}
    lname = lname + "bwd_"
    df = _matmul(dh2, p["w_ffn_out"], tb=True, out_dtype=BF16, name=lname + "ffn_out_dx")
    gr["w_ffn_out"] = _matmul(sv["f"], dh2, ta=True, name=lname + "ffn_out_dw")
    dgate, dup = _swiglu_bwd(sv["u"], df, name=lname + "swiglu")
    du = _concat_cols([dgate, dup], name=lname + "swiglu_join")
    dxn2 = _matmul(du, p["w_ffn_in"], tb=True, name=lname + "ffn_in_dx")
    gr["w_ffn_in"] = _matmul(sv["xn2"], du, ta=True, name=lname + "ffn_in_dw")
    dh1, gr["norm2_w"] = _rms_bwd(sv["h1"], p["norm2_w"], [dxn2], dh2, name=lname + "norm2")
    dmerged = _matmul(dh1, p["w_out"], tb=True, name=lname + "out_proj_dx")
    gr["w_out"] = _matmul(sv["merged"], dh1, ta=True, name=lname + "out_proj_dw")
    dga, dgs, dg0, dg1 = _merge_bwd(sv["proj"], p["off_gate"], sv["ga"], sv["gs"], dmerged, name=lname + "merge")
    dattn = _matmul(dga, p["w_attn_branch"], tb=True, name=lname + "attn_branch_dx")
    gr["w_attn_branch"] = _matmul(sv["attn"], dga, ta=True, name=lname + "attn_branch_dw")
    dssm = _matmul(dgs, p["w_ssm_branch"], tb=True, name=lname + "ssm_branch_dx")
    gr["w_ssm_branch"] = _matmul(sv["ssm"], dgs, ta=True, name=lname + "ssm_branch_dw")
    dy, dz, gr["ssm_norm_w"] = _gate_norm_bwd(dssm, sv["y"], sv["proj"], p["off_z"], p["ssm_norm_w"], name=lname + "gate_norm")
    dxs, dbm, dcm, ddtr4, da4, dbias4, ddsk = _ssd_bwd(sv["xc"], sv["dtr"], p["dt_bias"], p["a"], p["dskip"], sv["st"], dy,
                                                       name=lname + "ssd")
    nsh = p["n_ssm_heads"]
    ddtr = jnp.sum(ddtr4, axis=0)
    gr["a_log"] = jnp.sum(da4, axis=(0, 1))[:nsh] * p["a"][0, :nsh]
    gr["dt_bias"] = jnp.sum(dbias4, axis=(0, 1))[:nsh]
    gr["d_skip"] = jnp.sum(ddsk.reshape(nsh, HEAD_DIM), axis=1)
    di = dxs.shape[1]
    dxbc, dcw, dcb = [], [], []
    for part, (lo, hi) in zip((dxs, dbm, dcm), ((0, di), (di, di + dbm.shape[1]), (di + dbm.shape[1], di + 2 * dbm.shape[1]))):
        dx_, dw_, db_ = _conv_bwd(sv["proj"], p["off_xbc"] + lo, p["conv_w"][:, lo:hi], p["conv_b"][lo:hi], part,
                                  name=f"{lname}conv{lo}")
        dxbc.append(dx_)
        dcw.append(dw_)
        dcb.append(db_)
    gr["conv_w"] = jnp.concatenate(dcw, axis=1)
    gr["conv_b"] = jnp.concatenate(dcb, axis=0)
    dos, corrs = _combine_bwd(dattn, sv["os"], sv["lses"], name=lname + "attn_combine")
    dqkvs, dbiases = [], []
    for g, d in enumerate(DILATIONS):
        dq, dk, dv, dbias = _attn_bwd(sv["qkvs"][g], (0, 1, 2), biases[g], sv["lses_rm"][g], _to_residue_major(dos[g], d),
                                      _to_residue_major(corrs[g], d), d, name=f"{lname}attn{g}")
        dqkvs.append([dq, dk, dv])
        dbiases.append(dbias)
    dmain = _concat_cols(dqkvs[0] + [dz] + dxbc + [dg0, dg1], name=lname + "in_proj_join")
    dxn1 = [_matmul(dmain, p["w_main"], tb=True, name=lname + "in_proj_dx"),
            _matmul(ddtr, p["w_dt"], tb=True, name=lname + "in_proj_dt_dx")]
    dw_main = _matmul(sv["xn1"], dmain, ta=True, name=lname + "in_proj_dw")
    dw_dt = _matmul(sv["xn1"], ddtr, ta=True, name=lname + "in_proj_dt_dw")
    dw_qkv = [dw_main]
    for g in range(1, N_GROUPS_ATTN):
        dqkv = _concat_cols(dqkvs[g], name=f"{lname}in_proj_qkv{g}_join")
        dxn1.append(_to_token_major(_matmul(dqkv, p["w_qkv"][g], tb=True, name=f"{lname}in_proj_qkv{g}_dx"), DILATIONS[g]))
        dw_qkv.append(_matmul(sv["xn1_rm"][g], dqkv, ta=True, name=f"{lname}in_proj_qkv{g}_dw"))
    awg, og = dqkvs[0][0].shape[1], p["off_gate"]
    cols = [dw[:, i * awg:(i + 1) * awg] for i in range(3) for dw in dw_qkv]
    gr["w_in"] = jnp.concatenate(cols + [dw_main[:, 3 * awg:og], dw_dt[:, :nsh], dw_main[:, og:]], axis=1)
    dh, gr["norm1_w"] = _rms_bwd(sv["h"], p["norm1_w"], dxn1, dh1, name=lname + "norm1")
    return dh, gr, dbiases


def _layer_params(l, w, n_ssm_heads, hg):
    awg = hg * HEAD_DIM
    aw = N_GROUPS_ATTN * awg
    di = n_ssm_heads * HEAD_DIM
    xbc = di + 2 * SSM_GROUPS * D_STATE
    in_dt = 3 * aw + di + xbc
    w_in = w["w_in"][l]
    qkv_cols = lambda g: [(w_in, awg, i * N_GROUPS_ATTN + g) for i in range(3)]
    z_xbc_cols = [(w_in, awg, j) for j in range(3 * N_GROUPS_ATTN, in_dt // awg)]
    assert in_dt % awg == 0
    pad = lambda v: jnp.pad(v.astype(F32), (0, LANES - n_ssm_heads)).reshape(1, LANES)
    return dict(
        n_ssm_heads=n_ssm_heads, off_z=3 * awg, off_xbc=3 * awg + di, off_gate=3 * awg + di + xbc,
        w_main=_concat_cols(qkv_cols(0) + z_xbc_cols + [w_in[:, in_dt + n_ssm_heads:]], name=f"l{l}_w_main"),
        w_qkv=[None] + [_concat_cols(qkv_cols(g), name=f"l{l}_w_qkv{g}") for g in range(1, N_GROUPS_ATTN)],
        w_dt=jnp.pad(w_in[:, in_dt:in_dt + n_ssm_heads], ((0, 0), (0, LANES - n_ssm_heads))),
        norm1_w=w["norm1_w"][l], norm2_w=w["norm2_w"][l], conv_w=w["conv_w"][l], conv_b=w["conv_b"][l],
        dt_bias=pad(w["dt_bias"][l]), a=pad(-jnp.exp(w["a_log"][l])),
        dskip=jnp.repeat(w["d_skip"][l], HEAD_DIM).reshape(1, di), ssm_norm_w=w["ssm_norm_w"][l],
        w_attn_branch=w["w_attn_branch"][l], w_ssm_branch=w["w_ssm_branch"][l], w_out=w["w_out"][l],
        w_ffn_in=w["w_ffn_in"][l], w_ffn_out=w["w_ffn_out"][l],
    )


def _local_step(x, tgt, w):
    depth = w["norm1_w"].shape[0]
    n_ssm_heads = w["dt_bias"].shape[1]
    hg = w["rel_bias"].shape[1] // N_GROUPS_ATTN
    onehots = [(_rel_buckets(dil)[:, :, None] == jnp.arange(N_REL_BUCKETS)[None, None, :]).astype(F32) for dil in DILATIONS]
    biases = [jnp.einsum("qkb,bh->hqk", oh, w["rel_bias"][:, g * hg:(g + 1) * hg].astype(F32), precision=HIGHEST)
              for g, oh in enumerate(onehots)]
    params = [_layer_params(l, w, n_ssm_heads, hg) for l in range(depth)]
    h = x
    saved = []
    for l in range(depth):
        h, sv = _layer_fwd(h, params[l], biases, f"l{l}_")
        saved.append(sv)
    loss, dh, g_final = _loss_head(h, w["final_norm_w"], tgt, name="loss_head")
    grads = [None] * depth
    dbias_tot = [jnp.zeros(b.shape, F32) for b in biases]
    for l in reversed(range(depth)):
        dh, grads[l], dbiases = _layer_bwd(dh, params[l], saved[l], biases, f"l{l}_")
        dbias_tot = [a + b for a, b in zip(dbias_tot, dbiases)]
    out = {k: [gl[k] for gl in grads] if k in MATRICES else jnp.stack([gl[k] for gl in grads]) for k in grads[0]}
    out["final_norm_w"] = g_final
    drel = []
    for g, (oh, db) in enumerate(zip(onehots, dbias_tot)):
        oh_t = jnp.pad(oh.reshape(-1, N_REL_BUCKETS).T, ((0, LANES - N_REL_BUCKETS), (0, 0)))
        db_rows = jnp.pad(db.reshape(hg, -1), ((0, LANES - hg), (0, 0)))
        drel.append(_matmul(oh_t, db_rows, tb=True, name=f"rel_bias_fold{g}")[:N_REL_BUCKETS, :hg])
    out["rel_bias"] = jnp.concatenate(drel, axis=1)
    return loss, dh, out


MATRICES = ("w_in", "w_attn_branch", "w_ssm_branch", "w_out", "w_ffn_in", "w_ffn_out")
COL_SHARDED = ("w_in", "w_attn_branch", "w_ffn_in")
SMALL = ("norm1_w", "conv_b", "dt_bias", "a_log", "d_skip", "ssm_norm_w", "norm2_w", "rel_bias", "final_norm_w")
WEIGHTS = ("norm1_w", "w_in", "conv_w", "conv_b", "dt_bias", "a_log", "d_skip", "ssm_norm_w", "w_attn_branch",
           "w_ssm_branch", "w_out", "norm2_w", "w_ffn_in", "w_ffn_out", "rel_bias", "final_norm_w")
SMALL_COLS = 1024


def _unshard(name, g):
    _, depth, r, c = g.shape
    if name in COL_SHARDED or name == "conv_w":
        return jnp.transpose(g, (1, 2, 0, 3)).reshape(depth, r, N_CHIPS * c)
    return jnp.transpose(g, (1, 0, 2, 3)).reshape(depth, N_CHIPS * r, c)


def _to_shards(name, g):
    r, c = g.shape
    if name in COL_SHARDED:
        return jnp.transpose(g.reshape(r, N_CHIPS, c // N_CHIPS), (1, 0, 2))
    return g.reshape(N_CHIPS, r // N_CHIPS, c)


def kernel(x, norm1_w, w_in, conv_w, conv_b, dt_bias, a_log, d_skip, ssm_norm_w, w_attn_branch, w_ssm_branch, w_out, norm2_w, w_ffn_in, w_ffn_out, rel_bias, final_norm_w, loss_target, m_norm1_w, m_w_in, m_conv_w, m_conv_b, m_dt_bias, m_a_log, m_d_skip, m_ssm_norm_w, m_w_attn_branch, m_w_ssm_branch, m_w_out, m_norm2_w, m_w_ffn_in, m_w_ffn_out, m_rel_bias, m_final_norm_w, v_norm1_w, v_w_in, v_conv_w, v_conv_b, v_dt_bias, v_a_log, v_d_skip, v_ssm_norm_w, v_w_attn_branch, v_w_ssm_branch, v_w_out, v_norm2_w, v_w_ffn_in, v_w_ffn_out, v_rel_bias, v_final_norm_w):
    env = dict(locals())
    wts = {k: env[k] for k in WEIGHTS}
    mom = {k: env["m_" + k] for k in WEIGHTS}
    var = {k: env["v_" + k] for k in WEIGHTS}
    chip = 2 * lax.axis_index("x") + lax.axis_index("y")
    core = lax.axis_index("c")

    shards = [wts[k].astype(BF16) for k in MATRICES] + [conv_w]
    gathered = _gather_shards(shards, name="gather_weights")
    full = {k: wts[k] for k in SMALL}
    for k, own, g in zip(MATRICES + ("conv_w",), shards, gathered):
        full[k] = _unshard(k, lax.dynamic_update_index_in_dim(g, own, chip, axis=0))

    loss, dx, grads = _local_step(x[0], loss_target[0], full)
    loss = lax.psum(loss, ("x", "y", "c"))

    core1 = core.reshape(1).astype(jnp.int32)
    from_pair = _pair_swap_layers([grads[k][0] for k in MATRICES], [grads[k][1] for k in MATRICES], name="reduce_pair_swap")
    scatter_in = [_to_shards(k, _add_own_layer(grads[k][0], grads[k][1], got, core1, name="reduce_pair_add_" + k))
                  for k, got in zip(MATRICES, from_pair)]
    scattered = _scatter_to_chips(scatter_in, name="reduce_scatter")
    own_layer = []
    for k, sent, got in zip(MATRICES, scatter_in, scattered):
        got = lax.dynamic_update_index_in_dim(got, lax.dynamic_index_in_dim(sent, chip, axis=0, keepdims=False), chip, axis=0)
        own_layer.append(_sum_chips(got, name="reduce_sum_" + k))
    other_layer = _pair_swap(own_layer, name="reduce_pair_exchange")
    reduced = {}

    small_names = SMALL + ("conv_w",)
    flat = jnp.concatenate([grads[k].reshape(-1) for k in small_names])
    n_small = flat.shape[0]
    rows = -(-n_small // SMALL_COLS)
    rows = -(-rows // 8) * 8
    flat = jnp.pad(flat, (0, rows * SMALL_COLS - n_small)).reshape(rows, SMALL_COLS)
    flat = _allreduce_small(flat, name="allreduce_small").reshape(-1)
    pos = 0
    for k in small_names:
        size = math.prod(grads[k].shape)
        reduced[k] = flat[pos:pos + size].reshape(grads[k].shape)
        pos += size
    cs = conv_w.shape[2]
    reduced["conv_w"] = lax.dynamic_slice_in_dim(reduced["conv_w"], chip * cs, cs, axis=2)

    delta, new_m, new_v = {}, {}, {}
    for k, own, other in zip(MATRICES, own_layer, other_layer):
        reduced[k], delta[k], new_m[k], new_v[k] = _adamw_layers(wts[k], own, other, mom[k], var[k], core1, name="adamw_" + k)
    pack = lambda src: jnp.pad(jnp.concatenate([src[k].reshape(-1) for k in small_names]),
                               (0, rows * SMALL_COLS - n_shard)).reshape(rows, SMALL_COLS)
    n_shard = sum(math.prod(wts[k].shape) for k in small_names)
    d_, m_, v_ = _adamw(pack(wts), pack(reduced), pack(mom), pack(var), name="adamw_small")
    pos = 0
    for k in small_names:
        size = math.prod(wts[k].shape)
        for dst, src in ((delta, d_), (new_m, m_), (new_v, v_)):
            dst[k] = src.reshape(-1)[pos:pos + size].reshape(wts[k].shape)
        pos += size

    return (loss, dx[None], *[reduced[k] for k in WEIGHTS], *[delta[k] for k in WEIGHTS],
            *[new_m[k] for k in WEIGHTS], *[new_v[k] for k in WEIGHTS])
```

```python
import functools
import math

import jax
import jax.numpy as jnp
from jax import lax
from jax.experimental import pallas as pl
from jax.experimental.pallas import tpu as pltpu

F32, BF16 = jnp.float32, jnp.bfloat16
SDS = jax.ShapeDtypeStruct
BS = pl.BlockSpec
MESH = pl.DeviceIdType.MESH
HIGHEST = lax.Precision.HIGHEST

EPS = 1e-6
HEAD_DIM = 64
ATTN_BLOCK = 128
DILATIONS = (1, 4, 16)
N_GROUPS_ATTN = len(DILATIONS)
N_STEPS = 128
N_REL_BUCKETS = 32
REL_MAX_DISTANCE = 2048
SSM_GROUPS = 4
D_STATE = 128
CHUNK = 128
CONV_WIDTH = 4
HALO = 16
LANES = 128
N_CHIPS = 4
N_DEV = 8
VMEM_LIMIT_BYTES = 48 * 1024 * 1024

ADAM_LR, ADAM_B1, ADAM_B2, ADAM_EPS, ADAM_WD, ADAM_STEP = 0.001, 0.9, 0.999, 1e-08, 0.01, 10

NT = (((1,), (1,)), ((), ()))
TN = (((0,), (0,)), ((), ()))
NN = (((1,), (0,)), ((), ()))


def _cp(*sem):
    return pltpu.CompilerParams(dimension_semantics=sem if sem else None, vmem_limit_bytes=VMEM_LIMIT_BYTES)


def _pick(n, cands):
    for c in cands:
        if n % c == 0:
            return c
    raise ValueError(f"no block size of {cands} divides {n}")


def _divisors(n, cap):
    out = [c for c in range(LANES, min(n, cap) + 1, LANES) if n % c == 0]
    return out or [n]


def _wide(n, cap=2048):
    return _divisors(n, cap)[-1]


MXU_FLOPS = 9.0e14
HBM_BYTES_PER_S = 3.0e12
ACC_BYTES_PER_S = 4.0e12
GRID_STEP_S = 0.4e-6
TILE_VMEM_BYTES = 36 * 1024 * 1024


def _matmul_tiles(m, n, k, a_bytes, b_bytes, o_bytes, has_res):
    best = None
    for tm in _divisors(m, 2048):
        for tn in _divisors(n, 2048):
            for tk in _divisors(k, 4096):
                ni, nj, nk = m // tm, n // tn, k // tk
                vmem = 2 * (tm * tk * a_bytes + tk * tn * b_bytes + tm * tn * (o_bytes + (4 if has_res else 0)))
                vmem += tm * tn * 4 * (2 if nk > 1 else 1) + (tm * tk + tk * tn) * 2
                if vmem > TILE_VMEM_BYTES:
                    continue
                hbm = m * k * a_bytes * (nj if nk > 1 else 1) + k * n * b_bytes * (ni if nj * nk > 1 else 1)
                hbm += m * n * (o_bytes + (4 if has_res else 0))
                t = max(2.0 * m * n * k / MXU_FLOPS, hbm / HBM_BYTES_PER_S) + ni * nj * nk * GRID_STEP_S
                if nk > 1:
                    t += m * n * 8.0 * nk / ACC_BYTES_PER_S
                if best is None or t < best[0]:
                    best = (t, tm, tn, tk)
    assert best is not None, (m, n, k)
    return best[1:]


def _dot(a, b, dims=NN, precision=None):
    return lax.dot_general(a, b, dims, precision=precision, preferred_element_type=F32)


def _silu(x):
    return x / (1.0 + jnp.exp(-x))


def _sigmoid(x):
    return 1.0 / (1.0 + jnp.exp(-x))


def _dsilu(x):
    s = _sigmoid(x)
    return s * (1.0 + x * (1.0 - s))


def _matmul(a, b, *, name, ta=False, tb=False, out_dtype=F32, res=None):
    (kdim, m) = a.shape if ta else a.shape[::-1]
    (n, k2) = b.shape if tb else b.shape[::-1]
    assert kdim == k2, (a.shape, b.shape, ta, tb)
    tm, tn, tk = _matmul_tiles(m, n, kdim, a.dtype.itemsize, b.dtype.itemsize, jnp.dtype(out_dtype).itemsize, res is not None)
    nk = kdim // tk
    a_spec = BS((tk, tm), lambda i, j, k: (k, i)) if ta else BS((tm, tk), lambda i, j, k: (i, k))
    b_spec = BS((tn, tk), lambda i, j, k: (j, k)) if tb else BS((tk, tn), lambda i, j, k: (k, j))
    dims = (((0 if ta else 1,), (1 if tb else 0,)), ((), ()))
    has_res = res is not None

    def body(*refs):
        a_ref, b_ref = refs[:2]
        r_ref = refs[2] if has_res else None
        o_ref = refs[3] if has_res else refs[2]
        prod = _dot(a_ref[...].astype(BF16), b_ref[...].astype(BF16), dims)
        if nk == 1:
            o_ref[...] = (prod + r_ref[...] if has_res else prod).astype(o_ref.dtype)
            return
        acc = refs[-1]
        k = pl.program_id(2)

        @pl.when(k == 0)
        def _():
            acc[...] = prod

        @pl.when(k > 0)
        def _():
            acc[...] += prod

        @pl.when(k == nk - 1)
        def _():
            r = acc[...]
            if has_res:
                r = r + r_ref[...]
            o_ref[...] = r.astype(o_ref.dtype)

    in_specs = [a_spec, b_spec]
    args = [a, b]
    if has_res:
        in_specs.append(BS((tm, tn), lambda i, j, k: (i, j)))
        args.append(res)
    return pl.pallas_call(
        body, name=name, grid=(m // tm, n // tn, nk), in_specs=in_specs,
        out_specs=BS((tm, tn), lambda i, j, k: (i, j)), out_shape=SDS((m, n), out_dtype),
        scratch_shapes=[pltpu.VMEM((tm, tn), F32)] if nk > 1 else [],
        compiler_params=_cp("parallel", "parallel", "arbitrary"),
    )(*args)


def _rms_fwd(h, w, *, name):
    t, d = h.shape
    tm = _pick(t, (512, 256, 128))

    def body(h_ref, w_ref, o_ref):
        x = h_ref[...]
        r = lax.rsqrt(jnp.mean(x * x, axis=-1, keepdims=True) + EPS)
        o_ref[...] = (x * r * w_ref[...]).astype(BF16)

    return pl.pallas_call(
        body, name=name, grid=(t // tm,), in_specs=[BS((tm, d), lambda i: (i, 0)), BS((1, d), lambda i: (0, 0))],
        out_specs=BS((tm, d), lambda i: (i, 0)), out_shape=SDS((t, d), BF16), compiler_params=_cp("parallel"),
    )(h, w.reshape(1, d))


def _rms_bwd(h, w, dys, dres, *, name):
    t, d = h.shape
    tm = _pick(t, (512, 256, 128))
    n_dy = len(dys)

    def body(*refs):
        h_ref, w_ref = refs[:2]
        dy_refs = refs[2:2 + n_dy]
        dres_ref, dh_ref, dw_ref = refs[2 + n_dy:]
        x = h_ref[...]
        dy = dy_refs[0][...]
        for r_ in dy_refs[1:]:
            dy = dy + r_[...]
        r = lax.rsqrt(jnp.mean(x * x, axis=-1, keepdims=True) + EPS)
        g = dy * w_ref[...]
        proj = jnp.sum(g * x, axis=-1, keepdims=True) * (1.0 / d)
        dh_ref[...] = dres_ref[...] + r * g - x * (r * r * r) * proj

        @pl.when(pl.program_id(0) == 0)
        def _():
            dw_ref[...] = jnp.zeros_like(dw_ref)

        dw_ref[...] += jnp.sum(dy * x * r, axis=0, keepdims=True)

    row = BS((tm, d), lambda i: (i, 0))
    vec = BS((1, d), lambda i: (0, 0))
    dh, dw = pl.pallas_call(
        body, name=name, grid=(t // tm,), in_specs=[row, vec] + [row] * n_dy + [row],
        out_specs=[row, vec], out_shape=[SDS((t, d), F32), SDS((1, d), F32)], compiler_params=_cp("arbitrary"),
    )(h, w.reshape(1, d), *dys, dres)
    return dh, dw[0]


def _loss_head(h, w, tgt, *, name):
    t, d = h.shape
    tm = _pick(t, (512, 256, 128))

    def body(h_ref, w_ref, t_ref, loss_ref, dh_ref, dw_ref):
        x = h_ref[...]
        r = lax.rsqrt(jnp.mean(x * x, axis=-1, keepdims=True) + EPS)
        err = x * r * w_ref[...] - t_ref[...]
        loss_ref[...] = jnp.zeros(loss_ref.shape, F32) + 0.5 * jnp.sum(err * err) * (1.0 / d)
        dy = err * (1.0 / d)
        g = dy * w_ref[...]
        proj = jnp.sum(g * x, axis=-1, keepdims=True) * (1.0 / d)
        dh_ref[...] = r * g - x * (r * r * r) * proj

        @pl.when(pl.program_id(0) == 0)
        def _():
            dw_ref[...] = jnp.zeros_like(dw_ref)

        dw_ref[...] += jnp.sum(dy * x * r, axis=0, keepdims=True)

    row = BS((tm, d), lambda i: (i, 0))
    vec = BS((1, d), lambda i: (0, 0))
    loss, dh, dw = pl.pallas_call(
        body, name=name, grid=(t // tm,), in_specs=[row, vec, row],
        out_specs=[BS((1, 8, LANES), lambda i: (i, 0, 0)), row, vec],
        out_shape=[SDS((t // tm, 8, LANES), F32), SDS((t, d), F32), SDS((1, d), F32)], compiler_params=_cp("arbitrary"),
    )(h, w.reshape(1, d), tgt)
    return jnp.sum(loss[:, 0, 0]), dh, dw[0]


def _attn_masks(mb):
    qi = lax.broadcasted_iota(jnp.int32, (ATTN_BLOCK, 2 * ATTN_BLOCK), 0)
    kj = lax.broadcasted_iota(jnp.int32, (ATTN_BLOCK, 2 * ATTN_BLOCK), 1)
    steps = qi + ATTN_BLOCK - kj
    valid = (steps >= 0) & (steps <= N_STEPS) & ((kj >= ATTN_BLOCK) | (mb > 0))
    low = lax.broadcasted_iota(jnp.int32, (ATTN_BLOCK, LANES), 1) < HEAD_DIM
    return valid, low


def _to_residue_major(a, d):
    t, c = a.shape
    return a if d == 1 else a.reshape(t // d, d, c).transpose(1, 0, 2).reshape(t, c)


def _to_token_major(a, d):
    t, c = a.shape
    return a if d == 1 else a.reshape(d, t // d, c).transpose(1, 0, 2).reshape(t, c)


def _attn_specs(cols, nb, sub, awg, clamp):
    nsb = nb // sub

    def cur(col):
        return BS((sub * ATTN_BLOCK, awg), lambda r, sb: (r * nsb + clamp(sb), col))

    def prev(col):
        return BS((ATTN_BLOCK, awg), lambda r, sb: (r * nb + jnp.maximum(clamp(sb) * sub - 1, 0), col))

    return [cur(cols[0]), cur(cols[1]), prev(cols[1]), cur(cols[2]), prev(cols[2])]


def _sub_rows(s, n=1):
    return pl.ds(pl.multiple_of(s * ATTN_BLOCK, ATTN_BLOCK), n * ATTN_BLOCK)


def _attn_fwd(qkv, cols, bias, d, *, name):
    t = qkv.shape[0]
    hg = bias.shape[0]
    awg = hg * HEAD_DIM
    nb = t // d // ATTN_BLOCK
    sub = _pick(nb, (4, 2, 1))
    rows = sub * ATTN_BLOCK
    scale = HEAD_DIM ** -0.5

    def body(q_ref, kc_ref, kp_ref, vc_ref, vp_ref, b_ref, o_ref, l_ref, kext, vext):
        sb = pl.program_id(1)
        kext[pl.ds(0, ATTN_BLOCK), :] = kp_ref[...]
        kext[pl.ds(ATTN_BLOCK, rows), :] = kc_ref[...]
        vext[pl.ds(0, ATTN_BLOCK), :] = vp_ref[...]
        vext[pl.ds(ATTN_BLOCK, rows), :] = vc_ref[...]

        def one_block(s, carry):
            valid, low = _attn_masks(sb * sub + s)
            for pi in range(awg // LANES):
                sl = slice(pi * LANES, (pi + 1) * LANES)
                q2 = q_ref[_sub_rows(s), sl]
                k2 = kext[_sub_rows(s, 2), sl]
                v2 = vext[_sub_rows(s, 2), sl]
                outs, lses = [], []
                for hh in range(2):
                    mh = low if hh == 0 else jnp.logical_not(low)
                    qm = jnp.where(mh, q2, jnp.zeros_like(q2))
                    sc = _dot(qm, k2, NT) * scale + b_ref[pi * 2 + hh]
                    sc = jnp.where(valid, sc, -jnp.inf)
                    m = jnp.max(sc, axis=-1, keepdims=True)
                    p = jnp.exp(sc - m)
                    den = jnp.sum(p, axis=-1, keepdims=True)
                    outs.append(_dot(p.astype(BF16), v2) / den)
                    lses.append(jnp.broadcast_to(m + jnp.log(den), (ATTN_BLOCK, LANES)))
                o_ref[_sub_rows(s), sl] = jnp.where(low, outs[0], outs[1])
                l_ref[_sub_rows(s), sl] = jnp.where(low, lses[0], lses[1])
            return carry

        lax.fori_loop(0, sub, one_block, 0)

    out_spec = BS((rows, awg), lambda r, sb: (r * (nb // sub) + sb, 0))
    return pl.pallas_call(
        body, name=name, grid=(d, nb // sub),
        in_specs=_attn_specs(cols, nb, sub, awg, lambda sb: sb) + [BS(bias.shape, lambda r, sb: (0, 0, 0))],
        out_specs=[out_spec, out_spec], out_shape=[SDS((t, awg), F32)] * 2,
        scratch_shapes=[pltpu.VMEM((rows + ATTN_BLOCK, awg), BF16)] * 2, compiler_params=_cp("parallel", "parallel"),
    )(*([qkv] * 5), bias)


def _attn_bwd(qkv, cols, bias, lse, do, corr, d, *, name):
    t = qkv.shape[0]
    hg = bias.shape[0]
    awg = hg * HEAD_DIM
    nb = t // d // ATTN_BLOCK
    sub = _pick(nb, (4, 2, 1))
    nsb = nb // sub
    rows = sub * ATTN_BLOCK
    scale = HEAD_DIM ** -0.5

    def body(q_ref, kc_ref, kp_ref, vc_ref, vp_ref, b_ref, l_ref, do_ref, c_ref, dq_ref, dk_ref, dv_ref, db_ref,
             kext, vext, dkext, dvext, ck, cv):
        r, sb = pl.program_id(0), pl.program_id(1)

        @pl.when((r == 0) & (sb == 0))
        def _():
            db_ref[...] = jnp.zeros_like(db_ref)

        @pl.when(sb == 0)
        def _():
            ck[...] = jnp.zeros_like(ck)
            cv[...] = jnp.zeros_like(cv)

        @pl.when(sb < nsb)
        def _():
            kext[pl.ds(0, ATTN_BLOCK), :] = kp_ref[...]
            kext[pl.ds(ATTN_BLOCK, rows), :] = kc_ref[...]
            vext[pl.ds(0, ATTN_BLOCK), :] = vp_ref[...]
            vext[pl.ds(ATTN_BLOCK, rows), :] = vc_ref[...]
            dkext[...] = jnp.zeros_like(dkext)
            dvext[...] = jnp.zeros_like(dvext)

            def one_block(s, carry):
                valid, low = _attn_masks(sb * sub + s)
                for pi in range(awg // LANES):
                    sl = slice(pi * LANES, (pi + 1) * LANES)
                    q2 = q_ref[_sub_rows(s), sl]
                    k2 = kext[_sub_rows(s, 2), sl]
                    v2 = vext[_sub_rows(s, 2), sl]
                    do2 = do_ref[_sub_rows(s), sl]
                    lse2 = l_ref[_sub_rows(s), sl]
                    corr2 = c_ref[_sub_rows(s), sl]
                    dk2 = jnp.zeros((2 * ATTN_BLOCK, LANES), F32)
                    dv2 = jnp.zeros((2 * ATTN_BLOCK, LANES), F32)
                    dqs = []
                    for hh in range(2):
                        mh = low if hh == 0 else jnp.logical_not(low)
                        qm = jnp.where(mh, q2, jnp.zeros_like(q2))
                        dom = jnp.where(mh, do2, jnp.zeros_like(do2))
                        lse_c = jnp.max(jnp.where(mh, lse2, -jnp.inf), axis=-1, keepdims=True)
                        corr_c = jnp.max(jnp.where(mh, corr2, -jnp.inf), axis=-1, keepdims=True)
                        sc = _dot(qm, k2, NT) * scale + b_ref[pi * 2 + hh]
                        p = jnp.exp(jnp.where(valid, sc, -jnp.inf) - lse_c)
                        ds = p * (_dot(dom, v2, NT) + corr_c)
                        db_ref[pi * 2 + hh] += ds
                        dsb = ds.astype(BF16)
                        dqs.append(_dot(dsb, k2) * scale)
                        dk2 = dk2 + _dot(dsb, qm, TN) * scale
                        dv2 = dv2 + _dot(p.astype(BF16), dom, TN)
                    dq_ref[_sub_rows(s), sl] = jnp.where(low, dqs[0], dqs[1]).astype(BF16)
                    dkext[_sub_rows(s, 2), sl] += dk2
                    dvext[_sub_rows(s, 2), sl] += dv2
                return carry

            lax.fori_loop(0, sub, one_block, 0)
            head, tail = pl.ds(0, rows - ATTN_BLOCK), pl.ds(rows - ATTN_BLOCK, ATTN_BLOCK)
            for out_ref, carry_ref, ext in ((dk_ref, ck, dkext), (dv_ref, cv, dvext)):
                if sub > 1:
                    out_ref[head, :] = carry_ref[head, :].astype(BF16)
                out_ref[tail, :] = (carry_ref[tail, :] + ext[pl.ds(0, ATTN_BLOCK), :]).astype(BF16)
                carry_ref[...] = ext[pl.ds(ATTN_BLOCK, rows), :]

        @pl.when(sb == nsb)
        def _():
            dk_ref[...] = ck[...].astype(BF16)
            dv_ref[...] = cv[...].astype(BF16)

    clamp = lambda sb: jnp.minimum(sb, nsb - 1)
    cur = BS((rows, awg), lambda r, sb: (r * nsb + clamp(sb), 0))
    prev = BS((rows, awg), lambda r, sb: (r * nsb + jnp.maximum(sb - 1, 0), 0))
    bias_spec = BS(bias.shape, lambda r, sb: (0, 0, 0))
    return pl.pallas_call(
        body, name=name, grid=(d, nsb + 1),
        in_specs=_attn_specs(cols, nb, sub, awg, clamp) + [bias_spec, cur, cur, cur],
        out_specs=[cur, prev, prev, bias_spec],
        out_shape=[SDS((t, awg), BF16)] * 3 + [SDS(bias.shape, F32)],
        scratch_shapes=[pltpu.VMEM((rows + ATTN_BLOCK, awg), BF16)] * 2 + [pltpu.VMEM((rows + ATTN_BLOCK, awg), F32)] * 2
        + [pltpu.VMEM((rows, awg), F32)] * 2,
        compiler_params=_cp("arbitrary", "arbitrary"),
    )(*([qkv] * 5), bias, lse, do, corr)


def _head_sum(x, low):
    a = jnp.sum(jnp.where(low, x, 0.0), axis=-1, keepdims=True)
    b = jnp.sum(jnp.where(low, 0.0, x), axis=-1, keepdims=True)
    return jnp.where(low, a, b)


def _combine_weights(lses):
    mx = jnp.maximum(jnp.maximum(lses[0], lses[1]), lses[2])
    es = [jnp.exp(l - mx) for l in lses]
    tot = es[0] + es[1] + es[2]
    return [e / tot for e in es]


def _combine_fwd(os_, lses, *, name):
    t, awg = os_[0].shape
    tm = _pick(t, (512, 256, 128))

    def body(o0, o1, o2, l0, l1, l2, out_ref):
        al = _combine_weights([l0[...], l1[...], l2[...]])
        out_ref[...] = (al[0] * o0[...] + al[1] * o1[...] + al[2] * o2[...]).astype(BF16)

    blk = BS((tm, awg), lambda i: (i, 0))
    return pl.pallas_call(
        body, name=name, grid=(t // tm,), in_specs=[blk] * 6, out_specs=blk,
        out_shape=SDS((t, awg), BF16), compiler_params=_cp("parallel"),
    )(*os_, *lses)


def _combine_bwd(dattn, os_, lses, *, name):
    t, awg = dattn.shape
    tm = _pick(t, (512, 256, 128))

    def body(da_ref, o0, o1, o2, l0, l1, l2, d0, d1, d2, c0, c1, c2):
        low = lax.broadcasted_iota(jnp.int32, (tm, LANES), 1) < HEAD_DIM
        for pi in range(awg // LANES):
            sl = slice(pi * LANES, (pi + 1) * LANES)
            da = da_ref[:, sl]
            al = _combine_weights([l0[:, sl], l1[:, sl], l2[:, sl]])
            tot = jnp.zeros((tm, LANES), F32)
            for a, o in zip(al, (o0, o1, o2)):
                tot = tot + a * _head_sum(da * o[:, sl], low)
            for a, d_ref, c_ref in zip(al, (d0, d1, d2), (c0, c1, c2)):
                d_ref[:, sl] = (a * da).astype(BF16)
                c_ref[:, sl] = -a * tot

    blk = BS((tm, awg), lambda i: (i, 0))
    outs = pl.pallas_call(
        body, name=name, grid=(t // tm,), in_specs=[blk] * 7, out_specs=[blk] * 6,
        out_shape=[SDS((t, awg), BF16)] * 3 + [SDS((t, awg), F32)] * 3, compiler_params=_cp("parallel"),
    )(dattn, *os_, *lses)
    return outs[:3], outs[3:]


def _conv_block(width, *offsets):
    for c in (512, 256, 128):
        if width % c == 0 and all(o % c == 0 for o in offsets):
            return c
    raise ValueError((width, offsets))


CONV_ROWS = 32


def _conv_pre(x_ref, halo_ref, w_ref, b_ref, ext, i, tm):
    ext[pl.ds(0, HALO), :] = jnp.where(i > 0, halo_ref[...].astype(F32), 0.0)
    ext[pl.ds(HALO, tm), :] = x_ref[...].astype(F32)
    taps = [w_ref[pl.ds(k, 1), :] for k in range(CONV_WIDTH)]
    bias = b_ref[...]
    for r0 in range(0, tm, CONV_ROWS):
        xs = [ext[pl.ds(HALO + r0 - (CONV_WIDTH - 1) + k, CONV_ROWS), :] for k in range(CONV_WIDTH)]
        pre = bias + taps[0] * xs[0]
        for k in range(1, CONV_WIDTH):
            pre = pre + taps[k] * xs[k]
        yield r0, pre, xs


def _fold8(v):
    return jnp.sum(v.reshape(v.shape[0] // 8, 8, v.shape[1]), axis=0)


def _conv_fwd(proj, off, w, b, *, name):
    t = proj.shape[0]
    c = w.shape[1]
    cw = _conv_block(c, off)
    tm = _pick(t, (512, 256, 128))
    ob = off // cw

    def body(x_ref, halo_ref, w_ref, b_ref, o_ref, ext):
        for r0, pre, _ in _conv_pre(x_ref, halo_ref, w_ref, b_ref, ext, pl.program_id(1), tm):
            o_ref[pl.ds(r0, CONV_ROWS), :] = _silu(pre).astype(BF16)

    return pl.pallas_call(
        body, name=name, grid=(c // cw, t // tm),
        in_specs=[BS((tm, cw), lambda j, i: (i, ob + j)),
                  BS((HALO, cw), lambda j, i: (jnp.maximum(i * (tm // HALO) - 1, 0), ob + j)),
                  BS((CONV_WIDTH, cw), lambda j, i: (0, j)), BS((1, cw), lambda j, i: (0, j))],
        out_specs=BS((tm, cw), lambda j, i: (i, j)), out_shape=SDS((t, c), BF16),
        scratch_shapes=[pltpu.VMEM((HALO + tm, cw), F32)], compiler_params=_cp("parallel", "arbitrary"),
    )(proj, proj, w, b.reshape(1, c))


def _conv_bwd(proj, off, w, b, dxc, *, name):
    t = proj.shape[0]
    c = w.shape[1]
    cw = _conv_block(c, off)
    tm = _pick(t, (512, 256, 128))
    ob = off // cw
    nt = t // tm

    def body_pre(x_ref, halo_ref, w_ref, b_ref, d_ref, dp_ref, dw_ref, db_ref, ext):
        i = pl.program_id(1)

        @pl.when(i == 0)
        def _():
            dw_ref[...] = jnp.zeros_like(dw_ref)
            db_ref[...] = jnp.zeros_like(db_ref)

        db_acc = jnp.zeros((8, cw), F32)
        dw_acc = [jnp.zeros((8, cw), F32) for _ in range(CONV_WIDTH)]
        for r0, pre, xs in _conv_pre(x_ref, halo_ref, w_ref, b_ref, ext, i, tm):
            dpre = d_ref[pl.ds(r0, CONV_ROWS), :] * _dsilu(pre)
            dp_ref[pl.ds(r0, CONV_ROWS), :] = dpre
            db_acc = db_acc + _fold8(dpre)
            dw_acc = [acc + _fold8(dpre * x) for acc, x in zip(dw_acc, xs)]
        db_ref[...] += jnp.sum(db_acc, axis=0, keepdims=True)
        for k in range(CONV_WIDTH):
            dw_ref[pl.ds(k, 1), :] += jnp.sum(dw_acc[k], axis=0, keepdims=True)

    dpre, dw, db = pl.pallas_call(
        body_pre, name=name + "_pre", grid=(c // cw, nt),
        in_specs=[BS((tm, cw), lambda j, i: (i, ob + j)),
                  BS((HALO, cw), lambda j, i: (jnp.maximum(i * (tm // HALO) - 1, 0), ob + j)),
                  BS((CONV_WIDTH, cw), lambda j, i: (0, j)), BS((1, cw), lambda j, i: (0, j)),
                  BS((tm, cw), lambda j, i: (i, j))],
        out_specs=[BS((tm, cw), lambda j, i: (i, j)), BS((CONV_WIDTH, cw), lambda j, i: (0, j)), BS((1, cw), lambda j, i: (0, j))],
        out_shape=[SDS((t, c), F32), SDS((CONV_WIDTH, c), F32), SDS((1, c), F32)],
        scratch_shapes=[pltpu.VMEM((HALO + tm, cw), F32)], compiler_params=_cp("parallel", "arbitrary"),
    )(proj, proj, w, b.reshape(1, c), dxc)

    def body_in(dp_ref, nxt_ref, w_ref, dx_ref, ext):
        i = pl.program_id(1)
        ext[pl.ds(0, tm), :] = dp_ref[...]
        ext[pl.ds(tm, 8), :] = jnp.where(i < nt - 1, nxt_ref[...], 0.0)
        taps = [w_ref[pl.ds(k, 1), :] for k in range(CONV_WIDTH)]
        for r0 in range(0, tm, CONV_ROWS):
            dx = taps[CONV_WIDTH - 1] * ext[pl.ds(r0, CONV_ROWS), :]
            for k in range(CONV_WIDTH - 1):
                dx = dx + taps[k] * ext[pl.ds(r0 + CONV_WIDTH - 1 - k, CONV_ROWS), :]
            dx_ref[pl.ds(r0, CONV_ROWS), :] = dx.astype(BF16)

    dx = pl.pallas_call(
        body_in, name=name + "_in", grid=(c // cw, nt),
        in_specs=[BS((tm, cw), lambda j, i: (i, j)),
                  BS((8, cw), lambda j, i: (jnp.minimum((i + 1) * (tm // 8), t // 8 - 1), j)),
                  BS((CONV_WIDTH, cw), lambda j, i: (0, j))],
        out_specs=BS((tm, cw), lambda j, i: (i, j)), out_shape=SDS((t, c), BF16),
        scratch_shapes=[pltpu.VMEM((tm + 8, cw), F32)], compiler_params=_cp("parallel", "arbitrary"),
    )(dpre, dpre, w)
    return dx, dw, db[0]


def _softplus(x):
    return jnp.maximum(x, 0.0) + jnp.log(1.0 + jnp.exp(-jnp.abs(x)))


def _tril():
    return lax.broadcasted_iota(jnp.int32, (CHUNK, CHUNK), 0) >= lax.broadcasted_iota(jnp.int32, (CHUNK, CHUNK), 1)


def _ssd_prep(dtr, dt_bias, a, *, name):
    t = dtr.shape[0]
    nc = t // CHUNK
    cb = _pick(nc, (4, 2, 1))

    def body(dtr_ref, bias_ref, a_ref, o_ref):
        tril = _tril().astype(F32)
        for ci in range(cb):
            rows = pl.ds(ci * CHUNK, CHUNK)
            pre = dtr_ref[rows, :] + bias_ref[...]
            dt = _softplus(pre)
            la = _dot(tril, dt * a_ref[...], precision=HIGHEST)
            for k, v in enumerate((dt, la, la.T, _sigmoid(pre))):
                o_ref[rows, k * LANES:(k + 1) * LANES] = v

    vec = BS((1, LANES), lambda i: (0, 0))
    return pl.pallas_call(
        body, name=name, grid=(nc // cb,), in_specs=[BS((cb * CHUNK, LANES), lambda i: (i, 0)), vec, vec],
        out_specs=BS((cb * CHUNK, 4 * LANES), lambda i: (i, 0)), out_shape=SDS((t, 4 * LANES), F32),
        compiler_params=_cp("parallel"),
    )(dtr, dt_bias, a)


def _ssd_common(time_ref):
    part = lambda k: time_ref[:, k * LANES:(k + 1) * LANES]
    return part(3), part(0), part(1), part(2), _tril()


def _lane_col(x, lane, h):
    return jnp.sum(jnp.where(lane == h, x, 0.0), axis=-1, keepdims=True)


def _ssd_specs(rows, di, gw, cidx):
    nbx = di // LANES
    return [BS((rows, gw), lambda g, c: (cidx(c), g)),
            BS((rows, D_STATE), lambda g, c: (cidx(c), nbx + g)),
            BS((rows, D_STATE), lambda g, c: (cidx(c), nbx + SSM_GROUPS + g)),
            BS((rows, 4 * LANES), lambda g, c: (cidx(c), 0)),
            BS((1, LANES), lambda g, c: (0, 0)),
            BS((1, gw), lambda g, c: (0, g))]


def _chunk_rows(ci):
    return pl.ds(pl.multiple_of(ci * CHUNK, CHUNK), CHUNK)


def _ssd_fwd(xc, prep, a, dskip, *, name):
    t = xc.shape[0]
    di = xc.shape[1] - 2 * SSM_GROUPS * D_STATE
    gw = di // SSM_GROUPS
    hpg = gw // HEAD_DIM
    npair = gw // LANES
    nc = t // CHUNK
    cb = _pick(nc, (4, 2, 1))

    def body(xb_ref, bb_ref, cb_ref, timeb_ref, a_ref, dsk_ref, yb_ref, st_ref, state):
        @pl.when(pl.program_id(1) == 0)
        def _():
            state[...] = jnp.zeros_like(state)

        g = pl.program_id(0)

        def one_chunk(ci, carry):
            rows = _chunk_rows(ci)
            ssd_chunk(xb_ref.at[rows], bb_ref.at[rows], cb_ref.at[rows], timeb_ref.at[rows], dsk_ref,
                      yb_ref.at[rows], st_ref, state, ci, g)
            return carry

        lax.fori_loop(0, cb, one_chunk, 0)

    def ssd_chunk(x_ref, b_ref, c_ref, time_ref, dsk_ref, y_ref, st_ref, state, ci, g):
        st_ref[ci, 0] = state[...]
        _, dt, la, la_t, tril = _ssd_common(time_ref)
        lane = lax.broadcasted_iota(jnp.int32, (CHUNK, LANES), 1)
        sub = lax.broadcasted_iota(jnp.int32, (LANES, CHUNK), 0)
        lane1 = lax.broadcasted_iota(jnp.int32, (1, LANES), 1)
        low, low1 = lane < HEAD_DIM, lane1 < HEAD_DIM
        last = lax.broadcasted_iota(jnp.int32, (CHUNK, LANES), 0) == CHUNK - 1
        lend = jnp.sum(jnp.where(last, la, 0.0), axis=0, keepdims=True)
        bm, cm = b_ref[...], c_ref[...]
        gmat = _dot(cm, bm, NT)
        for p in range(npair):
            sl = slice(p * LANES, (p + 1) * LANES)
            ps = slice(p * D_STATE, (p + 1) * D_STATE)
            x2 = x_ref[:, sl].astype(F32)
            cols, ms = [], []
            for hh in range(2):
                h = g * hpg + p * 2 + hh
                col_la = _lane_col(la, lane, h)
                row_la = jnp.sum(jnp.where(sub == h, la_t, 0.0), axis=0, keepdims=True)
                lend_h = _lane_col(lend, lane1, h)
                decay = jnp.exp(jnp.where(tril, col_la - row_la, -jnp.inf))
                ms.append((gmat * decay).astype(BF16))
                cols.append((_lane_col(dt, lane, h), jnp.exp(col_la), jnp.exp(lend_h - col_la), jnp.exp(lend_h)))
            pair = lambda k: jnp.where(low, cols[0][k], cols[1][k])
            xdt = x2 * pair(0)
            xdtb = xdt.astype(BF16)
            s2 = state[ps, :]
            y = jnp.where(low, _dot(ms[0], xdtb), _dot(ms[1], xdtb))
            y = y + pair(1) * _dot(cm, s2.astype(BF16)) + x2 * dsk_ref[:, sl]
            y_ref[:, sl] = y
            state[ps, :] = s2 * jnp.where(low1, cols[0][3], cols[1][3]) + _dot(bm, (xdt * pair(2)).astype(BF16), TN)

    y, st = pl.pallas_call(
        body, name=name, grid=(SSM_GROUPS, nc // cb), in_specs=_ssd_specs(cb * CHUNK, di, gw, lambda c: c),
        out_specs=[BS((cb * CHUNK, gw), lambda g, c: (c, g)), BS((cb, 1, npair * D_STATE, LANES), lambda g, c: (c, g, 0, 0))],
        out_shape=[SDS((t, di), F32), SDS((nc, SSM_GROUPS, npair * D_STATE, LANES), F32)],
        scratch_shapes=[pltpu.VMEM((npair * D_STATE, LANES), F32)], compiler_params=_cp("parallel", "arbitrary"),
    )(xc, xc, xc, prep, a, dskip)
    return y, st


def _ssd_bwd(xc, prep, a, dskip, st, dy, *, name):
    t = xc.shape[0]
    di = xc.shape[1] - 2 * SSM_GROUPS * D_STATE
    gw = di // SSM_GROUPS
    hpg = gw // HEAD_DIM
    npair = gw // LANES
    nc = t // CHUNK
    cb = _pick(nc, (4, 2, 1))
    rev = lambda c: nc // cb - 1 - c

    def body(xb_ref, bb_ref, cb_ref, timeb_ref, a_ref, dsk_ref, st_ref, dyb_ref,
             dxb_ref, dbb_ref, dcb_ref, ddtrb_ref, da_ref, dbias_ref, ddsk_ref, dstate):
        @pl.when(pl.program_id(1) == 0)
        def _():
            dstate[...] = jnp.zeros_like(dstate)
            da_ref[...] = jnp.zeros_like(da_ref)
            dbias_ref[...] = jnp.zeros_like(dbias_ref)
            ddsk_ref[...] = jnp.zeros_like(ddsk_ref)

        g = pl.program_id(0)

        def one_chunk(j, carry):
            ci = cb - 1 - j
            rows = _chunk_rows(ci)
            ssd_chunk(xb_ref.at[rows], bb_ref.at[rows], cb_ref.at[rows], timeb_ref.at[rows], a_ref, dsk_ref,
                      st_ref.at[ci], dyb_ref.at[rows], dxb_ref.at[rows], dbb_ref.at[rows], dcb_ref.at[rows],
                      ddtrb_ref.at[:, rows], da_ref, dbias_ref, ddsk_ref, dstate, g)
            return carry

        lax.fori_loop(0, cb, one_chunk, 0)

    def ssd_chunk(x_ref, b_ref, c_ref, time_ref, a_ref, dsk_ref, st_ref, dy_ref,
                  dx_ref, db_ref, dc_ref, ddtr_ref, da_ref, dbias_ref, ddsk_ref, dstate, g):
        sig, dt, la, la_t, tril = _ssd_common(time_ref)
        lane = lax.broadcasted_iota(jnp.int32, (CHUNK, LANES), 1)
        sub = lax.broadcasted_iota(jnp.int32, (LANES, CHUNK), 0)
        lane1 = lax.broadcasted_iota(jnp.int32, (1, LANES), 1)
        low, low1 = lane < HEAD_DIM, lane1 < HEAD_DIM
        last = lax.broadcasted_iota(jnp.int32, (CHUNK, LANES), 0) == CHUNK - 1
        lend = jnp.sum(jnp.where(last, la, 0.0), axis=0, keepdims=True)
        bm, cm = b_ref[...], c_ref[...]
        gmat = _dot(cm, bm, NT)
        dg = jnp.zeros((CHUNK, CHUNK), F32)
        dla_cols = jnp.zeros((CHUNK, LANES), F32)
        dla_rows = jnp.zeros((LANES, CHUNK), F32)
        dtsum = jnp.zeros((CHUNK, LANES), F32)
        dbm = jnp.zeros((CHUNK, D_STATE), F32)
        dcm = jnp.zeros((CHUNK, D_STATE), F32)
        for p in range(npair):
            sl = slice(p * LANES, (p + 1) * LANES)
            ps = slice(p * D_STATE, (p + 1) * D_STATE)
            x2 = x_ref[:, sl].astype(F32)
            dy2 = dy_ref[:, sl]
            s2 = st_ref[0, ps, :]
            ds2 = dstate[ps, :]
            hs, cols, ms, decays = [], [], [], []
            for hh in range(2):
                h = g * hpg + p * 2 + hh
                col_la = _lane_col(la, lane, h)
                row_la = jnp.sum(jnp.where(sub == h, la_t, 0.0), axis=0, keepdims=True)
                lend_h = _lane_col(lend, lane1, h)
                decay = jnp.exp(jnp.where(tril, col_la - row_la, -jnp.inf))
                hs.append(h)
                decays.append(decay)
                ms.append(gmat * decay)
                cols.append((_lane_col(dt, lane, h), jnp.exp(col_la), jnp.exp(lend_h - col_la), jnp.exp(lend_h)))
            pair = lambda k: jnp.where(low, cols[0][k], cols[1][k])
            dtc, ec, eend = pair(0), pair(1), pair(2)
            eend_s = jnp.where(low1, cols[0][3], cols[1][3])
            xdt = x2 * dtc
            xdtb = xdt.astype(BF16)
            dys = dy2 * ec
            dysb = dys.astype(BF16)
            dxdt_state = eend * _dot(bm, ds2.astype(BF16))
            inter = dys * _dot(cm, s2.astype(BF16))
            u = dxdt_state * xdt
            sds = s2 * ds2
            dxdt = dxdt_state
            for hh in range(2):
                h = hs[hh]
                mh = low if hh == 0 else jnp.logical_not(low)
                dym = jnp.where(mh, dy2, 0.0).astype(BF16)
                dxdt = dxdt + _dot(ms[hh].astype(BF16), dym, TN)
                dm = _dot(dym, xdtb, NT)
                w = dm * ms[hh]
                dg = dg + dm * decays[hh]
                u_col = jnp.sum(jnp.where(mh, u, 0.0), axis=-1, keepdims=True)
                dlend = jnp.sum(u_col, axis=0, keepdims=True) + cols[hh][3] * jnp.sum(jnp.where(low1 if hh == 0 else jnp.logical_not(low1), jnp.sum(sds, axis=0, keepdims=True), 0.0), axis=-1, keepdims=True)
                col = jnp.sum(w, axis=-1, keepdims=True) + jnp.sum(jnp.where(mh, inter, 0.0), axis=-1, keepdims=True) - u_col
                dla_cols = dla_cols + jnp.where(lane == h, col + jnp.where(last, dlend, 0.0), 0.0)
                dla_rows = dla_rows - jnp.where(sub == h, jnp.sum(w, axis=0, keepdims=True), 0.0)
            for hh in range(2):
                mh = low if hh == 0 else jnp.logical_not(low)
                dtsum = dtsum + jnp.where(lane == hs[hh], jnp.sum(jnp.where(mh, dxdt * x2, 0.0), axis=-1, keepdims=True), 0.0)
            dcm = dcm + _dot(dysb, s2.astype(BF16), NT)
            dbm = dbm + _dot((xdt * eend).astype(BF16), ds2.astype(BF16), NT)
            dstate[ps, :] = ds2 * eend_s + _dot(cm, dysb, TN)
            dx_ref[:, sl] = dxdt * dtc + dy2 * dsk_ref[:, sl]
            ddsk_ref[:, sl] += jnp.sum(dy2 * x2, axis=0, keepdims=True)
        dgb = dg.astype(BF16)
        dc_ref[...] = dcm + _dot(dgb, bm)
        db_ref[...] = dbm + _dot(dgb, cm, TN)
        dla = dla_cols + dla_rows.T
        triu = lax.broadcasted_iota(jnp.int32, (CHUNK, CHUNK), 0) <= lax.broadcasted_iota(jnp.int32, (CHUNK, CHUNK), 1)
        ddta = _dot(triu.astype(F32), dla, precision=HIGHEST)
        ddt = ddta * a_ref[...] + dtsum
        da_ref[0] += jnp.sum(ddta * dt, axis=0, keepdims=True)
        ddtr = ddt * sig
        ddtr_ref[0] = ddtr
        dbias_ref[0] += jnp.sum(ddtr, axis=0, keepdims=True)

    vec = BS((1, 1, LANES), lambda g, c: (g, 0, 0))
    outs = pl.pallas_call(
        body, name=name, grid=(SSM_GROUPS, nc // cb),
        in_specs=_ssd_specs(cb * CHUNK, di, gw, rev) + [BS((cb, 1, npair * D_STATE, LANES), lambda g, c: (rev(c), g, 0, 0)),
                                                        BS((cb * CHUNK, gw), lambda g, c: (rev(c), g))],
        out_specs=[BS((cb * CHUNK, gw), lambda g, c: (rev(c), g)), BS((cb * CHUNK, D_STATE), lambda g, c: (rev(c), g)),
                   BS((cb * CHUNK, D_STATE), lambda g, c: (rev(c), g)), BS((1, cb * CHUNK, LANES), lambda g, c: (g, rev(c), 0)),
                   vec, vec, BS((1, gw), lambda g, c: (0, g))],
        out_shape=[SDS((t, di), F32), SDS((t, SSM_GROUPS * D_STATE), F32), SDS((t, SSM_GROUPS * D_STATE), F32),
                   SDS((SSM_GROUPS, t, LANES), F32), SDS((SSM_GROUPS, 1, LANES), F32), SDS((SSM_GROUPS, 1, LANES), F32),
                   SDS((1, di), F32)],
        scratch_shapes=[pltpu.VMEM((npair * D_STATE, LANES), F32)], compiler_params=_cp("parallel", "arbitrary"),
    )(xc, xc, xc, prep, a, dskip, st, dy)
    return outs


def _gate_norm_fwd(y, proj, zoff, w, *, name):
    t, di = y.shape
    gw = di // SSM_GROUPS
    tm = _pick(t, (512, 256, 128))
    zb = zoff // gw

    def body(y_ref, z_ref, w_ref, o_ref):
        yg = y_ref[...] * _silu(z_ref[...].astype(F32))
        r = lax.rsqrt(jnp.mean(yg * yg, axis=-1, keepdims=True) + EPS)
        o_ref[...] = (yg * r * w_ref[...]).astype(BF16)

    return pl.pallas_call(
        body, name=name, grid=(t // tm, SSM_GROUPS),
        in_specs=[BS((tm, gw), lambda i, g: (i, g)), BS((tm, gw), lambda i, g: (i, zb + g)), BS((1, gw), lambda i, g: (0, g))],
        out_specs=BS((tm, gw), lambda i, g: (i, g)), out_shape=SDS((t, di), BF16), compiler_params=_cp("parallel", "parallel"),
    )(y, proj, w.reshape(1, di))


def _gate_norm_bwd(dssm, y, proj, zoff, w, *, name):
    t, di = y.shape
    gw = di // SSM_GROUPS
    tm = _pick(t, (512, 256, 128))
    zb = zoff // gw

    def body(d_ref, y_ref, z_ref, w_ref, dy_ref, dz_ref, dw_ref):
        z = z_ref[...].astype(F32)
        yv = y_ref[...]
        sz = _silu(z)
        yg = yv * sz
        r = lax.rsqrt(jnp.mean(yg * yg, axis=-1, keepdims=True) + EPS)
        n = yg * r
        d = d_ref[...]
        dn = d * w_ref[...]
        dyg = r * (dn - n * jnp.mean(dn * n, axis=-1, keepdims=True))
        dy_ref[...] = dyg * sz
        dz_ref[...] = (dyg * yv * _dsilu(z)).astype(BF16)

        @pl.when(pl.program_id(1) == 0)
        def _():
            dw_ref[...] = jnp.zeros_like(dw_ref)

        dw_ref[...] += jnp.sum(d * n, axis=0, keepdims=True)

    blk = BS((tm, gw), lambda g, i: (i, g))
    dy, dz, dw = pl.pallas_call(
        body, name=name, grid=(SSM_GROUPS, t // tm),
        in_specs=[blk, blk, BS((tm, gw), lambda g, i: (i, zb + g)), BS((1, gw), lambda g, i: (0, g))],
        out_specs=[blk, blk, BS((1, gw), lambda g, i: (0, g))],
        out_shape=[SDS((t, di), F32), SDS((t, di), BF16), SDS((1, di), F32)], compiler_params=_cp("parallel", "arbitrary"),
    )(dssm, y, proj, w.reshape(1, di))
    return dy, dz, dw[0]


def _merge_fwd(proj, goff, ga, gs, *, name):
    t, d = ga.shape
    cw = _conv_block(d, goff)
    tm = _pick(t, (512, 256, 128))
    gb = goff // cw

    def body(g0, g1, a_ref, s_ref, o_ref):
        o_ref[...] = (_sigmoid(g0[...].astype(F32)) * a_ref[...] + _sigmoid(g1[...].astype(F32)) * s_ref[...]).astype(BF16)

    blk = BS((tm, cw), lambda i, j: (i, j))
    return pl.pallas_call(
        body, name=name, grid=(t // tm, d // cw),
        in_specs=[BS((tm, cw), lambda i, j: (i, gb + j)), BS((tm, cw), lambda i, j: (i, gb + d // cw + j)), blk, blk],
        out_specs=blk, out_shape=SDS((t, d), BF16), compiler_params=_cp("parallel", "parallel"),
    )(proj, proj, ga, gs)


def _merge_bwd(proj, goff, ga, gs, dm, *, name):
    t, d = ga.shape
    cw = _conv_block(d, goff)
    tm = _pick(t, (512, 256, 128))
    gb = goff // cw

    def body(g0, g1, a_ref, s_ref, dm_ref, da_ref, ds_ref, dg0_ref, dg1_ref):
        dmv = dm_ref[...]
        s0 = _sigmoid(g0[...].astype(F32))
        s1 = _sigmoid(g1[...].astype(F32))
        da_ref[...] = (s0 * dmv).astype(BF16)
        ds_ref[...] = (s1 * dmv).astype(BF16)
        dg0_ref[...] = (dmv * a_ref[...] * s0 * (1.0 - s0)).astype(BF16)
        dg1_ref[...] = (dmv * s_ref[...] * s1 * (1.0 - s1)).astype(BF16)

    blk = BS((tm, cw), lambda i, j: (i, j))
    return pl.pallas_call(
        body, name=name, grid=(t // tm, d // cw),
        in_specs=[BS((tm, cw), lambda i, j: (i, gb + j)), BS((tm, cw), lambda i, j: (i, gb + d // cw + j)), blk, blk, blk],
        out_specs=[blk] * 4, out_shape=[SDS((t, d), BF16)] * 4, compiler_params=_cp("parallel", "parallel"),
    )(proj, proj, ga, gs, dm)


def _swiglu_fwd(u, *, name):
    t, two_f = u.shape
    f = two_f // 2
    cw = _wide(f)
    tm = _pick(t, (512, 256, 128))

    def body(g_ref, u_ref, o_ref):
        o_ref[...] = (_silu(g_ref[...].astype(F32)) * u_ref[...].astype(F32)).astype(BF16)

    return pl.pallas_call(
        body, name=name, grid=(t // tm, f // cw),
        in_specs=[BS((tm, cw), lambda i, j: (i, j)), BS((tm, cw), lambda i, j: (i, f // cw + j))],
        out_specs=BS((tm, cw), lambda i, j: (i, j)), out_shape=SDS((t, f), BF16), compiler_params=_cp("parallel", "parallel"),
    )(u, u)


def _swiglu_bwd(u, df, *, name):
    t, two_f = u.shape
    f = two_f // 2
    cw = _wide(f)
    tm = _pick(t, (256, 128))

    def body(u_ref, d_ref, o_ref):
        for j in range(f // cw):
            gate, up = slice(j * cw, (j + 1) * cw), slice(f + j * cw, f + (j + 1) * cw)
            gt = u_ref[:, gate].astype(F32)
            d = d_ref[:, gate].astype(F32)
            o_ref[:, gate] = (d * u_ref[:, up].astype(F32) * _dsilu(gt)).astype(BF16)
            o_ref[:, up] = (d * _silu(gt)).astype(BF16)

    return pl.pallas_call(
        body, name=name, grid=(t // tm,), in_specs=[BS((tm, two_f), lambda i: (i, 0)), BS((tm, f), lambda i: (i, 0))],
        out_specs=BS((tm, two_f), lambda i: (i, 0)), out_shape=SDS((t, two_f), BF16), compiler_params=_cp("parallel"),
    )(u, df)


def _row_block(rows, cols, n_arrays):
    budget = VMEM_LIMIT_BYTES // 3
    for tr in (512, 256, 128, 64, 32, 16, 8):
        if rows % tr == 0 and tr * cols * 4 * n_arrays * 2 <= budget:
            return tr
    raise ValueError((rows, cols))


def _concat_cols(pieces, *, name):
    pieces = [p if isinstance(p, tuple) else (p, p.shape[1], 0) for p in pieces]
    rows, dtype = pieces[0][0].shape[0], pieces[0][0].dtype
    widths = [w for _, w, _ in pieces]
    total = sum(widths)
    assert all(w % LANES == 0 for w in widths) and all(a.dtype == dtype and a.shape[0] == rows for a, _, _ in pieces)
    tr = next(c for c in (512, 256, 128, 64, 32, 16) if rows % c == 0 and 4 * c * total * dtype.itemsize <= VMEM_LIMIT_BYTES // 2)

    def body(*refs):
        o_ref = refs[-1]
        off = 0
        for p_ref, w in zip(refs[:-1], widths):
            o_ref[:, off:off + w] = p_ref[...]
            off += w

    return pl.pallas_call(
        body, name=name, grid=(rows // tr,), in_specs=[BS((tr, w), lambda i, j=j: (i, j)) for _, w, j in pieces],
        out_specs=BS((tr, total), lambda i: (i, 0)), out_shape=SDS((rows, total), dtype), compiler_params=_cp("parallel"),
    )(*[a for a, _, _ in pieces])


def _add_own_layer(g0, g1, got, core, *, name):
    rows, cols = got.shape
    tr = _row_block(rows, cols, 4)

    def body(core_ref, g0_ref, g1_ref, got_ref, o_ref):
        o_ref[...] = (jnp.where(core_ref[0] == 0, g0_ref[...], g1_ref[...]) + got_ref[...]).astype(o_ref.dtype)

    blk = BS((tr, cols), lambda i, cr: (i, 0))
    grid_spec = pltpu.PrefetchScalarGridSpec(
        num_scalar_prefetch=1, grid=(rows // tr,),
        in_specs=[BS((tr, cols), lambda i, cr: (i * (1 - cr[0]), 0)), BS((tr, cols), lambda i, cr: (i * cr[0], 0)), blk],
        out_specs=blk)
    return pl.pallas_call(body, name=name, grid_spec=grid_spec, out_shape=SDS((rows, cols), BF16),
                          compiler_params=_cp("arbitrary"))(core, g0, g1, got)


def _sum_chips(a, *, name):
    _, rows, cols = a.shape
    tr = _row_block(rows, cols, 5)

    def body(a_ref, o_ref):
        o_ref[...] = ((a_ref[0].astype(F32) + a_ref[1].astype(F32)) + a_ref[2].astype(F32)) + a_ref[3].astype(F32)

    return pl.pallas_call(body, name=name, grid=(rows // tr,), in_specs=[BS((N_CHIPS, tr, cols), lambda i: (0, i, 0))],
                          out_specs=BS((tr, cols), lambda i: (i, 0)), out_shape=SDS((rows, cols), F32),
                          compiler_params=_cp("parallel"))(a)


def _adamw(w, g, m, v, *, name):
    rows, cols = w.shape
    tr = _row_block(rows, cols, 7) if rows % 8 == 0 else rows
    c1 = 1.0 - ADAM_B1 ** ADAM_STEP
    c2 = 1.0 - ADAM_B2 ** ADAM_STEP

    def body(w_ref, g_ref, m_ref, v_ref, d_ref, nm_ref, nv_ref):
        gv = g_ref[...]
        nm = ADAM_B1 * m_ref[...] + (1.0 - ADAM_B1) * gv
        nv = ADAM_B2 * v_ref[...] + (1.0 - ADAM_B2) * (gv * gv)
        d_ref[...] = -ADAM_LR * ((nm / c1) / (jnp.sqrt(nv / c2) + ADAM_EPS) + ADAM_WD * w_ref[...])
        nm_ref[...] = nm
        nv_ref[...] = nv

    blk = BS((tr, cols), lambda i: (i, 0))
    return pl.pallas_call(body, name=name, grid=(rows // tr,), in_specs=[blk] * 4, out_specs=[blk] * 3,
                          out_shape=[SDS((rows, cols), F32)] * 3, compiler_params=_cp("parallel"))(w, g, m, v)


def _adamw_layers(w, g_own, g_other, m, v, core, *, name):
    _, rows, cols = w.shape
    tr = _row_block(rows, cols, 9)
    c1 = 1.0 - ADAM_B1 ** ADAM_STEP
    c2 = 1.0 - ADAM_B2 ** ADAM_STEP

    def body(core_ref, w_ref, own_ref, oth_ref, m_ref, v_ref, g_ref, d_ref, nm_ref, nv_ref):
        gv = jnp.where(pl.program_id(0) == core_ref[0], own_ref[...], oth_ref[...])
        nm = ADAM_B1 * m_ref[0] + (1.0 - ADAM_B1) * gv
        nv = ADAM_B2 * v_ref[0] + (1.0 - ADAM_B2) * (gv * gv)
        g_ref[0] = gv
        d_ref[0] = -ADAM_LR * ((nm / c1) / (jnp.sqrt(nv / c2) + ADAM_EPS) + ADAM_WD * w_ref[0])
        nm_ref[0] = nm
        nv_ref[0] = nv

    own_here = lambda l, cr: 1 - (l - cr[0]) * (l - cr[0])
    slab = BS((1, tr, cols), lambda l, i, cr: (l, i, 0))
    grid_spec = pltpu.PrefetchScalarGridSpec(
        num_scalar_prefetch=1, grid=(2, rows // tr),
        in_specs=[slab, BS((tr, cols), lambda l, i, cr: (i * own_here(l, cr), 0)),
                  BS((tr, cols), lambda l, i, cr: (i * (1 - own_here(l, cr)), 0)), slab, slab],
        out_specs=[slab] * 4)
    return pl.pallas_call(body, name=name, grid_spec=grid_spec, out_shape=[SDS(w.shape, F32)] * 4,
                          compiler_params=_cp("arbitrary", "arbitrary"))(core, w, g_own, g_other, m, v)


ANY = BS(memory_space=pl.ANY)


def _place():
    x, y, c = lax.axis_index("x"), lax.axis_index("y"), lax.axis_index("c")
    return x, y, c, [(1 - x, y), (x, 1 - y), (1 - x, 1 - y)]


def _gather_shards(arrs, *, name):
    n = len(arrs)

    def body(*refs):
        ins, outs = refs[:n], refs[n:2 * n]
        send_sems, recv_sems, pass_send_sems, pass_recv_sems = refs[2 * n:]
        x, y, c, chips = _place()
        s = 2 * x + y

        def ici(i, j, src_chip, to):
            src = ins[i].at[c] if src_chip is None else outs[i].at[src_chip, c]
            return pltpu.make_async_remote_copy(
                src_ref=src, dst_ref=outs[i].at[s if src_chip is None else src_chip, c], send_sem=send_sems.at[i * 3 + j],
                recv_sem=recv_sems.at[i * 3 + j], device_id=to, device_id_type=MESH)

        def d2d(i, j, src_chip, layer):
            slab = outs[i].at[src_chip, layer]
            return pltpu.make_async_remote_copy(
                src_ref=slab, dst_ref=slab, send_sem=pass_send_sems.at[i * 3 + j], recv_sem=pass_recv_sems.at[i * 3 + j],
                device_id=(x, y, 1 - c), device_id_type=MESH)

        sent = []
        for i in range(n):
            for j, (px, py) in enumerate(chips):
                cp = ici(i, j, None, (px, py, c))
                cp.start()
                sent.append(cp)
        passed = []
        for i in range(n):
            for j, (px, py) in enumerate(chips):
                ici(i, j, 2 * px + py, (x, y, c)).wait_recv()
                cp = d2d(i, j, 2 * px + py, c)
                cp.start()
                passed.append(cp)
        for i in range(n):
            for j, (px, py) in enumerate(chips):
                d2d(i, j, 2 * px + py, 1 - c).wait_recv()
        for cp in sent + passed:
            cp.wait_send()

    return pl.pallas_call(
        body, name=name, in_specs=[ANY] * n, out_specs=[ANY] * n,
        out_shape=[SDS((N_CHIPS,) + a.shape, a.dtype) for a in arrs],
        scratch_shapes=[pltpu.SemaphoreType.DMA((3 * n,))] * 4,
    )(*arrs)


def _pair_swap_layers(layer0, layer1, *, name):
    n = len(layer0)

    def body(*refs):
        in0, in1, outs = refs[:n], refs[n:2 * n], refs[2 * n:3 * n]
        send_sems, recv_sems = refs[3 * n:]
        x, y, c, _ = _place()

        def copy(src, i):
            return pltpu.make_async_remote_copy(
                src_ref=src[i], dst_ref=outs[i], send_sem=send_sems.at[i], recv_sem=recv_sems.at[i],
                device_id=(x, y, 1 - c), device_id_type=MESH)

        @pl.when(c == 0)
        def _():
            for i in range(n):
                copy(in1, i).start()

        @pl.when(c == 1)
        def _():
            for i in range(n):
                copy(in0, i).start()

        for i in range(n):
            copy(in0, i).wait()

    return pl.pallas_call(
        body, name=name, in_specs=[ANY] * (2 * n), out_specs=[ANY] * n, out_shape=[SDS(a.shape, a.dtype) for a in layer0],
        scratch_shapes=[pltpu.SemaphoreType.DMA((n,)), pltpu.SemaphoreType.DMA((n,))],
    )(*layer0, *layer1)


def _scatter_to_chips(arrs, *, name):
    n = len(arrs)

    def body(*refs):
        ins, outs = refs[:n], refs[n:2 * n]
        send_sems, recv_sems = refs[2 * n:]
        x, y, c, chips = _place()
        s = 2 * x + y
        copies = []
        for i in range(n):
            for j, (px, py) in enumerate(chips):
                cp = pltpu.make_async_remote_copy(
                    src_ref=ins[i].at[2 * px + py], dst_ref=outs[i].at[s], send_sem=send_sems.at[i * 3 + j],
                    recv_sem=recv_sems.at[i * 3 + j], device_id=(px, py, c), device_id_type=MESH)
                cp.start()
                copies.append(cp)
        for cp in copies:
            cp.wait()

    return pl.pallas_call(
        body, name=name, in_specs=[ANY] * n, out_specs=[ANY] * n, out_shape=[SDS(a.shape, a.dtype) for a in arrs],
        scratch_shapes=[pltpu.SemaphoreType.DMA((3 * n,)), pltpu.SemaphoreType.DMA((3 * n,))],
    )(*arrs)


def _pair_swap(arrs, *, name):
    n = len(arrs)

    def body(*refs):
        ins, outs = refs[:n], refs[n:2 * n]
        send_sems, recv_sems = refs[2 * n:]
        x, y, c, _ = _place()
        copies = []
        for i in range(n):
            cp = pltpu.make_async_remote_copy(
                src_ref=ins[i], dst_ref=outs[i], send_sem=send_sems.at[i], recv_sem=recv_sems.at[i],
                device_id=(x, y, 1 - c), device_id_type=MESH)
            cp.start()
            copies.append(cp)
        for cp in copies:
            cp.wait()

    return pl.pallas_call(
        body, name=name, in_specs=[ANY] * n, out_specs=[ANY] * n, out_shape=[SDS(a.shape, a.dtype) for a in arrs],
        scratch_shapes=[pltpu.SemaphoreType.DMA((n,)), pltpu.SemaphoreType.DMA((n,))],
    )(*arrs)


def _allreduce_small(v, *, name):
    rows, cols = v.shape

    def body(v_ref, o_ref, gath, send_sems, recv_sems):
        x, y, c, _ = _place()
        me = 4 * x + 2 * y + c
        gath[me] = v_ref[...]
        copies = []
        for k in range(1, N_DEV):
            fx, fy, fc = (k >> 2) & 1, (k >> 1) & 1, k & 1
            peer = (1 - x if fx else x, 1 - y if fy else y, 1 - c if fc else c)
            cp = pltpu.make_async_remote_copy(
                src_ref=v_ref, dst_ref=gath.at[me], send_sem=send_sems.at[k - 1], recv_sem=recv_sems.at[k - 1],
                device_id=peer, device_id_type=MESH)
            cp.start()
            copies.append(cp)
        for cp in copies:
            cp.wait()
        acc = gath[0]
        for k in range(1, N_DEV):
            acc = acc + gath[k]
        o_ref[...] = acc

    vm = BS(memory_space=pltpu.VMEM)
    return pl.pallas_call(
        body, name=name, in_specs=[vm], out_specs=vm, out_shape=SDS((rows, cols), F32),
        scratch_shapes=[pltpu.VMEM((N_DEV, rows, cols), F32), pltpu.SemaphoreType.DMA((N_DEV - 1,)), pltpu.SemaphoreType.DMA((N_DEV - 1,))],
    )(v)


def _t5_bucket(dist):
    max_exact = N_REL_BUCKETS // 2
    d_f = jnp.maximum(dist, 1).astype(F32)
    large = max_exact + (jnp.log(d_f / max_exact) / math.log(REL_MAX_DISTANCE / max_exact) * (N_REL_BUCKETS - max_exact)).astype(jnp.int32)
    return jnp.where(dist < max_exact, dist, jnp.minimum(large, N_REL_BUCKETS - 1))


def _rel_buckets(dilation):
    qi = jnp.arange(ATTN_BLOCK)[:, None]
    kj = jnp.arange(2 * ATTN_BLOCK)[None, :]
    return _t5_bucket(jnp.clip(qi + ATTN_BLOCK - kj, 0, N_STEPS) * dilation)


def _layer_fwd(h, p, biases, lname):
    sv = {"h": h}
    xn1 = _rms_fwd(h, p["norm1_w"], name=lname + "norm1")
    proj = _matmul(xn1, p["w_main"], out_dtype=BF16, name=lname + "in_proj")
    dtr = _matmul(xn1, p["w_dt"], out_dtype=F32, name=lname + "in_proj_dt")
    xn1_rm, qkvs = [xn1], [proj]
    for g in range(1, N_GROUPS_ATTN):
        xn1_rm.append(_to_residue_major(xn1, DILATIONS[g]))
        qkvs.append(_matmul(xn1_rm[g], p["w_qkv"][g], out_dtype=BF16, name=f"{lname}in_proj_qkv{g}"))
    os_, lses, lses_rm = [], [], []
    for g, d in enumerate(DILATIONS):
        o, lse = _attn_fwd(qkvs[g], (0, 1, 2), biases[g], d, name=f"{lname}attn{g}")
        os_.append(_to_token_major(o, d))
        lses.append(_to_token_major(lse, d))
        lses_rm.append(lse)
    sv.update(xn1_rm=xn1_rm, qkvs=qkvs, lses_rm=lses_rm)
    attn = _combine_fwd(os_, lses, name=lname + "attn_combine")
    xc = _conv_fwd(proj, p["off_xbc"], p["conv_w"], p["conv_b"], name=lname + "conv")
    prep = _ssd_prep(dtr, p["dt_bias"], p["a"], name=lname + "ssd_prep")
    sv["prep"] = prep
    y, st = _ssd_fwd(xc, prep, p["a"], p["dskip"], name=lname + "ssd")
    ssm = _gate_norm_fwd(y, proj, p["off_z"], p["ssm_norm_w"], name=lname + "gate_norm")
    ga = _matmul(attn, p["w_attn_branch"], name=lname + "attn_branch")
    gs = _matmul(ssm, p["w_ssm_branch"], name=lname + "ssm_branch")
    merged = _merge_fwd(proj, p["off_gate"], ga, gs, name=lname + "merge")
    h1 = _matmul(merged, p["w_out"], res=h, name=lname + "out_proj")
    xn2 = _rms_fwd(h1, p["norm2_w"], name=lname + "norm2")
    u = _matmul(xn2, p["w_ffn_in"], out_dtype=BF16, name=lname + "ffn_in")
    f = _swiglu_fwd(u, name=lname + "swiglu")
    h2 = _matmul(f, p["w_ffn_out"], res=h1, name=lname + "ffn_out")
    sv.update(xn1=xn1, proj=proj, dtr=dtr, os=os_, lses=lses, attn=attn, xc=xc, y=y, st=st, ssm=ssm, ga=ga, gs=gs,
              merged=merged, h1=h1, xn2=xn2, u=u, f=f)
    return h2, sv


def _layer_bwd(dh2, p, sv, biases, lname):
    gr = {}
    lname = lname + "bwd_"
    df = _matmul(dh2, p["w_ffn_out"], tb=True, out_dtype=BF16, name=lname + "ffn_out_dx")
    gr["w_ffn_out"] = _matmul(sv["f"], dh2, ta=True, name=lname + "ffn_out_dw")
    du = _swiglu_bwd(sv["u"], df, name=lname + "swiglu")
    dxn2 = _matmul(du, p["w_ffn_in"], tb=True, name=lname + "ffn_in_dx")
    gr["w_ffn_in"] = _matmul(sv["xn2"], du, ta=True, name=lname + "ffn_in_dw")
    dh1, gr["norm2_w"] = _rms_bwd(sv["h1"], p["norm2_w"], [dxn2], dh2, name=lname + "norm2")
    dmerged = _matmul(dh1, p["w_out"], tb=True, name=lname + "out_proj_dx")
    gr["w_out"] = _matmul(sv["merged"], dh1, ta=True, name=lname + "out_proj_dw")
    dga, dgs, dg0, dg1 = _merge_bwd(sv["proj"], p["off_gate"], sv["ga"], sv["gs"], dmerged, name=lname + "merge")
    dattn = _matmul(dga, p["w_attn_branch"], tb=True, name=lname + "attn_branch_dx")
    gr["w_attn_branch"] = _matmul(sv["attn"], dga, ta=True, name=lname + "attn_branch_dw")
    dssm = _matmul(dgs, p["w_ssm_branch"], tb=True, name=lname + "ssm_branch_dx")
    gr["w_ssm_branch"] = _matmul(sv["ssm"], dgs, ta=True, name=lname + "ssm_branch_dw")
    dy, dz, gr["ssm_norm_w"] = _gate_norm_bwd(dssm, sv["y"], sv["proj"], p["off_z"], p["ssm_norm_w"], name=lname + "gate_norm")
    dxs, dbm, dcm, ddtr4, da4, dbias4, ddsk = _ssd_bwd(sv["xc"], sv["prep"], p["a"], p["dskip"], sv["st"], dy,
                                                       name=lname + "ssd")
    nsh = p["n_ssm_heads"]
    ddtr = jnp.sum(ddtr4, axis=0)
    gr["a_log"] = jnp.sum(da4, axis=(0, 1))[:nsh] * p["a"][0, :nsh]
    gr["dt_bias"] = jnp.sum(dbias4, axis=(0, 1))[:nsh]
    gr["d_skip"] = jnp.sum(ddsk.reshape(nsh, HEAD_DIM), axis=1)
    di = dxs.shape[1]
    dxbc, dcw, dcb = [], [], []
    for part, (lo, hi) in zip((dxs, dbm, dcm), ((0, di), (di, di + dbm.shape[1]), (di + dbm.shape[1], di + 2 * dbm.shape[1]))):
        dx_, dw_, db_ = _conv_bwd(sv["proj"], p["off_xbc"] + lo, p["conv_w"][:, lo:hi], p["conv_b"][lo:hi], part,
                                  name=f"{lname}conv{lo}")
        dxbc.append(dx_)
        dcw.append(dw_)
        dcb.append(db_)
    gr["conv_w"] = jnp.concatenate(dcw, axis=1)
    gr["conv_b"] = jnp.concatenate(dcb, axis=0)
    dos, corrs = _combine_bwd(dattn, sv["os"], sv["lses"], name=lname + "attn_combine")
    dqkvs, dbiases = [], []
    for g, d in enumerate(DILATIONS):
        dq, dk, dv, dbias = _attn_bwd(sv["qkvs"][g], (0, 1, 2), biases[g], sv["lses_rm"][g], _to_residue_major(dos[g], d),
                                      _to_residue_major(corrs[g], d), d, name=f"{lname}attn{g}")
        dqkvs.append([dq, dk, dv])
        dbiases.append(dbias)
    dmain = _concat_cols(dqkvs[0] + [dz] + dxbc + [dg0, dg1], name=lname + "in_proj_join")
    dxn1 = [_matmul(dmain, p["w_main"], tb=True, name=lname + "in_proj_dx"),
            _matmul(ddtr, p["w_dt"], tb=True, name=lname + "in_proj_dt_dx")]
    dw_main = _matmul(sv["xn1"], dmain, ta=True, name=lname + "in_proj_dw")
    dw_dt = _matmul(sv["xn1"], ddtr, ta=True, name=lname + "in_proj_dt_dw")
    dw_qkv = [dw_main]
    for g in range(1, N_GROUPS_ATTN):
        dqkv = _concat_cols(dqkvs[g], name=f"{lname}in_proj_qkv{g}_join")
        dxn1.append(_to_token_major(_matmul(dqkv, p["w_qkv"][g], tb=True, name=f"{lname}in_proj_qkv{g}_dx"), DILATIONS[g]))
        dw_qkv.append(_matmul(sv["xn1_rm"][g], dqkv, ta=True, name=f"{lname}in_proj_qkv{g}_dw"))
    awg, og = dqkvs[0][0].shape[1], p["off_gate"]
    cols = [dw[:, i * awg:(i + 1) * awg] for i in range(3) for dw in dw_qkv]
    gr["w_in"] = jnp.concatenate(cols + [dw_main[:, 3 * awg:og], dw_dt[:, :nsh], dw_main[:, og:]], axis=1)
    dh, gr["norm1_w"] = _rms_bwd(sv["h"], p["norm1_w"], dxn1, dh1, name=lname + "norm1")
    return dh, gr, dbiases


def _layer_params(l, w, n_ssm_heads, hg):
    awg = hg * HEAD_DIM
    aw = N_GROUPS_ATTN * awg
    di = n_ssm_heads * HEAD_DIM
    xbc = di + 2 * SSM_GROUPS * D_STATE
    in_dt = 3 * aw + di + xbc
    w_in = w["w_in"][l]
    qkv_cols = lambda g: [(w_in, awg, i * N_GROUPS_ATTN + g) for i in range(3)]
    z_xbc_cols = [(w_in, awg, j) for j in range(3 * N_GROUPS_ATTN, in_dt // awg)]
    assert in_dt % awg == 0
    pad = lambda v: jnp.pad(v.astype(F32), (0, LANES - n_ssm_heads)).reshape(1, LANES)
    return dict(
        n_ssm_heads=n_ssm_heads, off_z=3 * awg, off_xbc=3 * awg + di, off_gate=3 * awg + di + xbc,
        w_main=_concat_cols(qkv_cols(0) + z_xbc_cols + [w_in[:, in_dt + n_ssm_heads:]], name=f"l{l}_w_main"),
        w_qkv=[None] + [_concat_cols(qkv_cols(g), name=f"l{l}_w_qkv{g}") for g in range(1, N_GROUPS_ATTN)],
        w_dt=jnp.pad(w_in[:, in_dt:in_dt + n_ssm_heads], ((0, 0), (0, LANES - n_ssm_heads))),
        norm1_w=w["norm1_w"][l], norm2_w=w["norm2_w"][l], conv_w=w["conv_w"][l], conv_b=w["conv_b"][l],
        dt_bias=pad(w["dt_bias"][l]), a=pad(-jnp.exp(w["a_log"][l])),
        dskip=jnp.repeat(w["d_skip"][l], HEAD_DIM).reshape(1, di), ssm_norm_w=w["ssm_norm_w"][l],
        w_attn_branch=w["w_attn_branch"][l], w_ssm_branch=w["w_ssm_branch"][l], w_out=w["w_out"][l],
        w_ffn_in=w["w_ffn_in"][l], w_ffn_out=w["w_ffn_out"][l],
    )


def _local_step(x, tgt, w):
    depth = w["norm1_w"].shape[0]
    n_ssm_heads = w["dt_bias"].shape[1]
    hg = w["rel_bias"].shape[1] // N_GROUPS_ATTN
    onehots = [(_rel_buckets(dil)[:, :, None] == jnp.arange(N_REL_BUCKETS)[None, None, :]).astype(F32) for dil in DILATIONS]
    biases = [jnp.einsum("qkb,bh->hqk", oh, w["rel_bias"][:, g * hg:(g + 1) * hg].astype(F32), precision=HIGHEST)
              for g, oh in enumerate(onehots)]
    params = [_layer_params(l, w, n_ssm_heads, hg) for l in range(depth)]
    h = x
    saved = []
    for l in range(depth):
        h, sv = _layer_fwd(h, params[l], biases, f"l{l}_")
        saved.append(sv)
    loss, dh, g_final = _loss_head(h, w["final_norm_w"], tgt, name="loss_head")
    grads = [None] * depth
    dbias_tot = [jnp.zeros(b.shape, F32) for b in biases]
    for l in reversed(range(depth)):
        dh, grads[l], dbiases = _layer_bwd(dh, params[l], saved[l], biases, f"l{l}_")
        dbias_tot = [a + b for a, b in zip(dbias_tot, dbiases)]
    out = {k: [gl[k] for gl in grads] if k in MATRICES else jnp.stack([gl[k] for gl in grads]) for k in grads[0]}
    out["final_norm_w"] = g_final
    drel = []
    for g, (oh, db) in enumerate(zip(onehots, dbias_tot)):
        oh_t = jnp.pad(oh.reshape(-1, N_REL_BUCKETS).T, ((0, LANES - N_REL_BUCKETS), (0, 0)))
        db_rows = jnp.pad(db.reshape(hg, -1), ((0, LANES - hg), (0, 0)))
        drel.append(_matmul(oh_t, db_rows, tb=True, name=f"rel_bias_fold{g}")[:N_REL_BUCKETS, :hg])
    out["rel_bias"] = jnp.concatenate(drel, axis=1)
    return loss, dh, out


MATRICES = ("w_in", "w_attn_branch", "w_ssm_branch", "w_out", "w_ffn_in", "w_ffn_out")
COL_SHARDED = ("w_in", "w_attn_branch", "w_ffn_in")
SMALL = ("norm1_w", "conv_b", "dt_bias", "a_log", "d_skip", "ssm_norm_w", "norm2_w", "rel_bias", "final_norm_w")
WEIGHTS = ("norm1_w", "w_in", "conv_w", "conv_b", "dt_bias", "a_log", "d_skip", "ssm_norm_w", "w_attn_branch",
           "w_ssm_branch", "w_out", "norm2_w", "w_ffn_in", "w_ffn_out", "rel_bias", "final_norm_w")
SMALL_COLS = 1024


def _unshard(name, g):
    _, depth, r, c = g.shape
    if name in COL_SHARDED or name == "conv_w":
        return jnp.transpose(g, (1, 2, 0, 3)).reshape(depth, r, N_CHIPS * c)
    return jnp.transpose(g, (1, 0, 2, 3)).reshape(depth, N_CHIPS * r, c)


def _to_shards(name, g):
    r, c = g.shape
    if name in COL_SHARDED:
        return jnp.transpose(g.reshape(r, N_CHIPS, c // N_CHIPS), (1, 0, 2))
    return g.reshape(N_CHIPS, r // N_CHIPS, c)


def kernel(x, norm1_w, w_in, conv_w, conv_b, dt_bias, a_log, d_skip, ssm_norm_w, w_attn_branch, w_ssm_branch, w_out, norm2_w, w_ffn_in, w_ffn_out, rel_bias, final_norm_w, loss_target, m_norm1_w, m_w_in, m_conv_w, m_conv_b, m_dt_bias, m_a_log, m_d_skip, m_ssm_norm_w, m_w_attn_branch, m_w_ssm_branch, m_w_out, m_norm2_w, m_w_ffn_in, m_w_ffn_out, m_rel_bias, m_final_norm_w, v_norm1_w, v_w_in, v_conv_w, v_conv_b, v_dt_bias, v_a_log, v_d_skip, v_ssm_norm_w, v_w_attn_branch, v_w_ssm_branch, v_w_out, v_norm2_w, v_w_ffn_in, v_w_ffn_out, v_rel_bias, v_final_norm_w):
    env = dict(locals())
    wts = {k: env[k] for k in WEIGHTS}
    mom = {k: env["m_" + k] for k in WEIGHTS}
    var = {k: env["v_" + k] for k in WEIGHTS}
    chip = 2 * lax.axis_index("x") + lax.axis_index("y")
    core = lax.axis_index("c")

    shards = [wts[k].astype(BF16) for k in MATRICES] + [conv_w]
    gathered = _gather_shards(shards, name="gather_weights")
    full = {k: wts[k] for k in SMALL}
    for k, own, g in zip(MATRICES + ("conv_w",), shards, gathered):
        full[k] = _unshard(k, lax.dynamic_update_index_in_dim(g, own, chip, axis=0))

    loss, dx, grads = _local_step(x[0], loss_target[0], full)
    loss = lax.psum(loss, ("x", "y", "c"))

    core1 = core.reshape(1).astype(jnp.int32)
    from_pair = _pair_swap_layers([grads[k][0] for k in MATRICES], [grads[k][1] for k in MATRICES], name="reduce_pair_swap")
    scatter_in = [_to_shards(k, _add_own_layer(grads[k][0], grads[k][1], got, core1, name="reduce_pair_add_" + k))
                  for k, got in zip(MATRICES, from_pair)]
    scattered = _scatter_to_chips(scatter_in, name="reduce_scatter")
    own_layer = []
    for k, sent, got in zip(MATRICES, scatter_in, scattered):
        got = lax.dynamic_update_index_in_dim(got, lax.dynamic_index_in_dim(sent, chip, axis=0, keepdims=False), chip, axis=0)
        own_layer.append(_sum_chips(got, name="reduce_sum_" + k))
    other_layer = _pair_swap(own_layer, name="reduce_pair_exchange")
    reduced = {}

    small_names = SMALL + ("conv_w",)
    flat = jnp.concatenate([grads[k].reshape(-1) for k in small_names])
    n_small = flat.shape[0]
    rows = -(-n_small // SMALL_COLS)
    rows = -(-rows // 8) * 8
    flat = jnp.pad(flat, (0, rows * SMALL_COLS - n_small)).reshape(rows, SMALL_COLS)
    flat = _allreduce_small(flat, name="allreduce_small").reshape(-1)
    pos = 0
    for k in small_names:
        size = math.prod(grads[k].shape)
        reduced[k] = flat[pos:pos + size].reshape(grads[k].shape)
        pos += size
    cs = conv_w.shape[2]
    reduced["conv_w"] = lax.dynamic_slice_in_dim(reduced["conv_w"], chip * cs, cs, axis=2)

    delta, new_m, new_v = {}, {}, {}
    for k, own, other in zip(MATRICES, own_layer, other_layer):
        reduced[k], delta[k], new_m[k], new_v[k] = _adamw_layers(wts[k], own, other, mom[k], var[k], core1, name="adamw_" + k)
    pack = lambda src: jnp.pad(jnp.concatenate([src[k].reshape(-1) for k in small_names]),
                               (0, rows * SMALL_COLS - n_shard)).reshape(rows, SMALL_COLS)
    n_shard = sum(math.prod(wts[k].shape) for k in small_names)
    d_, m_, v_ = _adamw(pack(wts), pack(reduced), pack(mom), pack(var), name="adamw_small")
    pos = 0
    for k in small_names:
        size = math.prod(wts[k].shape)
        for dst, src in ((delta, d_), (new_m, m_), (new_v, v_)):
            dst[k] = src.reshape(-1)[pos:pos + size].reshape(wts[k].shape)
        pos += size

    return (loss, dx[None], *[reduced[k] for k in WEIGHTS], *[delta[k] for k in WEIGHTS],
            *[new_m[k] for k in WEIGHTS], *[new_v[k] for k in WEIGHTS])
```

```python
import functools
import math

import jax
import jax.numpy as jnp
from jax import lax
from jax.experimental import pallas as pl
from jax.experimental.pallas import tpu as pltpu

F32, BF16 = jnp.float32, jnp.bfloat16
SDS = jax.ShapeDtypeStruct
BS = pl.BlockSpec
MESH = pl.DeviceIdType.MESH
HIGHEST = lax.Precision.HIGHEST

EPS = 1e-6
HEAD_DIM = 64
ATTN_BLOCK = 128
DILATIONS = (1, 4, 16)
N_GROUPS_ATTN = len(DILATIONS)
N_STEPS = 128
N_REL_BUCKETS = 32
REL_MAX_DISTANCE = 2048
SSM_GROUPS = 4
D_STATE = 128
CHUNK = 128
CONV_WIDTH = 4
HALO = 16
LANES = 128
N_CHIPS = 4
N_DEV = 8
VMEM_LIMIT_BYTES = 48 * 1024 * 1024

ADAM_LR, ADAM_B1, ADAM_B2, ADAM_EPS, ADAM_WD, ADAM_STEP = 0.001, 0.9, 0.999, 1e-08, 0.01, 10

NT = (((1,), (1,)), ((), ()))
TN = (((0,), (0,)), ((), ()))
NN = (((1,), (0,)), ((), ()))


def _cp(*sem):
    return pltpu.CompilerParams(dimension_semantics=sem if sem else None, vmem_limit_bytes=VMEM_LIMIT_BYTES)


def _pick(n, cands):
    for c in cands:
        if n % c == 0:
            return c
    raise ValueError(f"no block size of {cands} divides {n}")


def _divisors(n, cap):
    out = [c for c in range(LANES, min(n, cap) + 1, LANES) if n % c == 0]
    return out or [n]


def _wide(n, cap=2048):
    return _divisors(n, cap)[-1]


MXU_FLOPS = 9.0e14
HBM_BYTES_PER_S = 3.0e12
ACC_BYTES_PER_S = 4.0e12
GRID_STEP_S = 0.4e-6
TILE_VMEM_BYTES = 36 * 1024 * 1024


def _matmul_tiles(m, n, k, a_bytes, b_bytes, o_bytes, has_res):
    best = None
    for tm in _divisors(m, 2048):
        for tn in _divisors(n, 2048):
            for tk in _divisors(k, 4096):
                ni, nj, nk = m // tm, n // tn, k // tk
                vmem = 2 * (tm * tk * a_bytes + tk * tn * b_bytes + tm * tn * (o_bytes + (4 if has_res else 0)))
                vmem += tm * tn * 4 * (2 if nk > 1 else 1) + (tm * tk + tk * tn) * 2
                if vmem > TILE_VMEM_BYTES:
                    continue
                hbm = m * k * a_bytes * (nj if nk > 1 else 1) + k * n * b_bytes * (ni if nj * nk > 1 else 1)
                hbm += m * n * (o_bytes + (4 if has_res else 0))
                t = max(2.0 * m * n * k / MXU_FLOPS, hbm / HBM_BYTES_PER_S) + ni * nj * nk * GRID_STEP_S
                if nk > 1:
                    t += m * n * 8.0 * nk / ACC_BYTES_PER_S
                if best is None or t < best[0]:
                    best = (t, tm, tn, tk)
    assert best is not None, (m, n, k)
    return best[1:]


def _dot(a, b, dims=NN, precision=None):
    return lax.dot_general(a, b, dims, precision=precision, preferred_element_type=F32)


def _silu(x):
    return x / (1.0 + jnp.exp(-x))


def _sigmoid(x):
    return 1.0 / (1.0 + jnp.exp(-x))


def _dsilu(x):
    s = _sigmoid(x)
    return s * (1.0 + x * (1.0 - s))


def _matmul(a, b, *, name, ta=False, tb=False, out_dtype=F32, res=None):
    (kdim, m) = a.shape if ta else a.shape[::-1]
    (n, k2) = b.shape if tb else b.shape[::-1]
    assert kdim == k2, (a.shape, b.shape, ta, tb)
    tm, tn, tk = _matmul_tiles(m, n, kdim, a.dtype.itemsize, b.dtype.itemsize, jnp.dtype(out_dtype).itemsize, res is not None)
    nk = kdim // tk
    a_spec = BS((tk, tm), lambda i, j, k: (k, i)) if ta else BS((tm, tk), lambda i, j, k: (i, k))
    b_spec = BS((tn, tk), lambda i, j, k: (j, k)) if tb else BS((tk, tn), lambda i, j, k: (k, j))
    dims = (((0 if ta else 1,), (1 if tb else 0,)), ((), ()))
    has_res = res is not None

    def body(*refs):
        a_ref, b_ref = refs[:2]
        r_ref = refs[2] if has_res else None
        o_ref = refs[3] if has_res else refs[2]
        prod = _dot(a_ref[...].astype(BF16), b_ref[...].astype(BF16), dims)
        if nk == 1:
            o_ref[...] = (prod + r_ref[...] if has_res else prod).astype(o_ref.dtype)
            return
        acc = refs[-1]
        k = pl.program_id(2)

        @pl.when(k == 0)
        def _():
            acc[...] = prod

        @pl.when(k > 0)
        def _():
            acc[...] += prod

        @pl.when(k == nk - 1)
        def _():
            r = acc[...]
            if has_res:
                r = r + r_ref[...]
            o_ref[...] = r.astype(o_ref.dtype)

    in_specs = [a_spec, b_spec]
    args = [a, b]
    if has_res:
        in_specs.append(BS((tm, tn), lambda i, j, k: (i, j)))
        args.append(res)
    return pl.pallas_call(
        body, name=name, grid=(m // tm, n // tn, nk), in_specs=in_specs,
        out_specs=BS((tm, tn), lambda i, j, k: (i, j)), out_shape=SDS((m, n), out_dtype),
        scratch_shapes=[pltpu.VMEM((tm, tn), F32)] if nk > 1 else [],
        compiler_params=_cp("parallel", "parallel", "arbitrary"),
    )(*args)


def _rms_fwd(h, w, *, name):
    t, d = h.shape
    tm = _pick(t, (512, 256, 128))

    def body(h_ref, w_ref, o_ref):
        x = h_ref[...]
        r = lax.rsqrt(jnp.mean(x * x, axis=-1, keepdims=True) + EPS)
        o_ref[...] = (x * r * w_ref[...]).astype(BF16)

    return pl.pallas_call(
        body, name=name, grid=(t // tm,), in_specs=[BS((tm, d), lambda i: (i, 0)), BS((1, d), lambda i: (0, 0))],
        out_specs=BS((tm, d), lambda i: (i, 0)), out_shape=SDS((t, d), BF16), compiler_params=_cp("parallel"),
    )(h, w.reshape(1, d))


def _rms_bwd(h, w, dys, dres, *, name):
    t, d = h.shape
    tm = _pick(t, (512, 256, 128))
    n_dy = len(dys)

    def body(*refs):
        h_ref, w_ref = refs[:2]
        dy_refs = refs[2:2 + n_dy]
        dres_ref, dh_ref, dw_ref = refs[2 + n_dy:]
        x = h_ref[...]
        dy = dy_refs[0][...]
        for r_ in dy_refs[1:]:
            dy = dy + r_[...]
        r = lax.rsqrt(jnp.mean(x * x, axis=-1, keepdims=True) + EPS)
        g = dy * w_ref[...]
        proj = jnp.sum(g * x, axis=-1, keepdims=True) * (1.0 / d)
        dh_ref[...] = dres_ref[...] + r * g - x * (r * r * r) * proj

        @pl.when(pl.program_id(0) == 0)
        def _():
            dw_ref[...] = jnp.zeros_like(dw_ref)

        dw_ref[...] += jnp.sum(dy * x * r, axis=0, keepdims=True)

    row = BS((tm, d), lambda i: (i, 0))
    vec = BS((1, d), lambda i: (0, 0))
    dh, dw = pl.pallas_call(
        body, name=name, grid=(t // tm,), in_specs=[row, vec] + [row] * n_dy + [row],
        out_specs=[row, vec], out_shape=[SDS((t, d), F32), SDS((1, d), F32)], compiler_params=_cp("arbitrary"),
    )(h, w.reshape(1, d), *dys, dres)
    return dh, dw[0]


def _loss_head(h, w, tgt, *, name):
    t, d = h.shape
    tm = _pick(t, (512, 256, 128))

    def body(h_ref, w_ref, t_ref, loss_ref, dh_ref, dw_ref):
        x = h_ref[...]
        r = lax.rsqrt(jnp.mean(x * x, axis=-1, keepdims=True) + EPS)
        err = x * r * w_ref[...] - t_ref[...]
        loss_ref[...] = jnp.zeros(loss_ref.shape, F32) + 0.5 * jnp.sum(err * err) * (1.0 / d)
        dy = err * (1.0 / d)
        g = dy * w_ref[...]
        proj = jnp.sum(g * x, axis=-1, keepdims=True) * (1.0 / d)
        dh_ref[...] = r * g - x * (r * r * r) * proj

        @pl.when(pl.program_id(0) == 0)
        def _():
            dw_ref[...] = jnp.zeros_like(dw_ref)

        dw_ref[...] += jnp.sum(dy * x * r, axis=0, keepdims=True)

    row = BS((tm, d), lambda i: (i, 0))
    vec = BS((1, d), lambda i: (0, 0))
    loss, dh, dw = pl.pallas_call(
        body, name=name, grid=(t // tm,), in_specs=[row, vec, row],
        out_specs=[BS((1, 8, LANES), lambda i: (i, 0, 0)), row, vec],
        out_shape=[SDS((t // tm, 8, LANES), F32), SDS((t, d), F32), SDS((1, d), F32)], compiler_params=_cp("arbitrary"),
    )(h, w.reshape(1, d), tgt)
    return jnp.sum(loss[:, 0, 0]), dh, dw[0]


def _attn_masks(mb):
    qi = lax.broadcasted_iota(jnp.int32, (2 * ATTN_BLOCK, 2 * ATTN_BLOCK), 0) & (ATTN_BLOCK - 1)
    kj = lax.broadcasted_iota(jnp.int32, (2 * ATTN_BLOCK, 2 * ATTN_BLOCK), 1)
    steps = qi + ATTN_BLOCK - kj
    valid = (steps >= 0) & (steps <= N_STEPS) & ((kj >= ATTN_BLOCK) | (mb > 0))
    low = lax.broadcasted_iota(jnp.int32, (ATTN_BLOCK, LANES), 1) < HEAD_DIM
    return valid, low


def _stack_heads(x2, low):
    zero = jnp.zeros_like(x2)
    return jnp.concatenate([jnp.where(low, x2, zero), jnp.where(low, zero, x2)], axis=0)


def _head_cols(x2, low):
    return jnp.concatenate([jnp.max(jnp.where(low, x2, -jnp.inf), axis=-1, keepdims=True),
                            jnp.max(jnp.where(low, -jnp.inf, x2), axis=-1, keepdims=True)], axis=0)


def _to_residue_major(a, d):
    t, c = a.shape
    return a if d == 1 else a.reshape(t // d, d, c).transpose(1, 0, 2).reshape(t, c)


def _to_token_major(a, d):
    t, c = a.shape
    return a if d == 1 else a.reshape(d, t // d, c).transpose(1, 0, 2).reshape(t, c)


def _attn_specs(cols, nb, sub, awg, clamp):
    nsb = nb // sub

    def cur(col):
        return BS((sub * ATTN_BLOCK, awg), lambda r, sb: (r * nsb + clamp(sb), col))

    def prev(col):
        return BS((ATTN_BLOCK, awg), lambda r, sb: (r * nb + jnp.maximum(clamp(sb) * sub - 1, 0), col))

    return [cur(cols[0]), cur(cols[1]), prev(cols[1]), cur(cols[2]), prev(cols[2])]


def _sub_rows(s, n=1):
    return pl.ds(pl.multiple_of(s * ATTN_BLOCK, ATTN_BLOCK), n * ATTN_BLOCK)


def _attn_fwd(qkv, cols, bias, d, *, name):
    t = qkv.shape[0]
    hg = bias.shape[0]
    awg = hg * HEAD_DIM
    nb = t // d // ATTN_BLOCK
    sub = _pick(nb, (4, 2, 1))
    rows = sub * ATTN_BLOCK
    scale = HEAD_DIM ** -0.5

    def body(q_ref, kc_ref, kp_ref, vc_ref, vp_ref, b_ref, o_ref, l_ref, kext, vext):
        sb = pl.program_id(1)
        kext[pl.ds(0, ATTN_BLOCK), :] = kp_ref[...]
        kext[pl.ds(ATTN_BLOCK, rows), :] = kc_ref[...]
        vext[pl.ds(0, ATTN_BLOCK), :] = vp_ref[...]
        vext[pl.ds(ATTN_BLOCK, rows), :] = vc_ref[...]

        def one_block(s, carry):
            valid, low = _attn_masks(sb * sub + s)
            for pi in range(awg // LANES):
                sl = slice(pi * LANES, (pi + 1) * LANES)
                k2 = kext[_sub_rows(s, 2), sl]
                v2 = vext[_sub_rows(s, 2), sl]
                qs = _stack_heads(q_ref[_sub_rows(s), sl], low)
                bias2 = b_ref[pi * 2:pi * 2 + 2].reshape(2 * ATTN_BLOCK, 2 * ATTN_BLOCK)
                sc = jnp.where(valid, _dot(qs, k2, NT) * scale + bias2, -jnp.inf)
                m = jnp.max(sc, axis=-1, keepdims=True)
                p = jnp.exp(sc - m)
                den = jnp.sum(p, axis=-1, keepdims=True)
                o = _dot(p.astype(BF16), v2) / den
                lse = jnp.broadcast_to(m + jnp.log(den), (2 * ATTN_BLOCK, LANES))
                o_ref[_sub_rows(s), sl] = jnp.where(low, o[:ATTN_BLOCK], o[ATTN_BLOCK:])
                l_ref[_sub_rows(s), sl] = jnp.where(low, lse[:ATTN_BLOCK], lse[ATTN_BLOCK:])
            return carry

        lax.fori_loop(0, sub, one_block, 0)

    out_spec = BS((rows, awg), lambda r, sb: (r * (nb // sub) + sb, 0))
    return pl.pallas_call(
        body, name=name, grid=(d, nb // sub),
        in_specs=_attn_specs(cols, nb, sub, awg, lambda sb: sb) + [BS(bias.shape, lambda r, sb: (0, 0, 0))],
        out_specs=[out_spec, out_spec], out_shape=[SDS((t, awg), F32)] * 2,
        scratch_shapes=[pltpu.VMEM((rows + ATTN_BLOCK, awg), BF16)] * 2, compiler_params=_cp("parallel", "parallel"),
    )(*([qkv] * 5), bias)


def _attn_bwd(qkv, cols, bias, lse, do, corr, d, *, name):
    t = qkv.shape[0]
    hg = bias.shape[0]
    awg = hg * HEAD_DIM
    nb = t // d // ATTN_BLOCK
    sub = _pick(nb, (4, 2, 1))
    nsb = nb // sub
    rows = sub * ATTN_BLOCK
    scale = HEAD_DIM ** -0.5

    def body(q_ref, kc_ref, kp_ref, vc_ref, vp_ref, b_ref, l_ref, do_ref, c_ref, dq_ref, dk_ref, dv_ref, db_ref,
             kext, vext, dkext, dvext, ck, cv):
        r, sb = pl.program_id(0), pl.program_id(1)

        @pl.when((r == 0) & (sb == 0))
        def _():
            db_ref[...] = jnp.zeros_like(db_ref)

        @pl.when(sb == 0)
        def _():
            ck[...] = jnp.zeros_like(ck)
            cv[...] = jnp.zeros_like(cv)

        @pl.when(sb < nsb)
        def _():
            kext[pl.ds(0, ATTN_BLOCK), :] = kp_ref[...]
            kext[pl.ds(ATTN_BLOCK, rows), :] = kc_ref[...]
            vext[pl.ds(0, ATTN_BLOCK), :] = vp_ref[...]
            vext[pl.ds(ATTN_BLOCK, rows), :] = vc_ref[...]
            dkext[...] = jnp.zeros_like(dkext)
            dvext[...] = jnp.zeros_like(dvext)

            def one_block(s, carry):
                valid, low = _attn_masks(sb * sub + s)
                for pi in range(awg // LANES):
                    sl = slice(pi * LANES, (pi + 1) * LANES)
                    k2 = kext[_sub_rows(s, 2), sl]
                    v2 = vext[_sub_rows(s, 2), sl]
                    qs = _stack_heads(q_ref[_sub_rows(s), sl], low)
                    dos = _stack_heads(do_ref[_sub_rows(s), sl], low)
                    lse_c = _head_cols(l_ref[_sub_rows(s), sl], low)
                    corr_c = _head_cols(c_ref[_sub_rows(s), sl], low)
                    bias2 = b_ref[pi * 2:pi * 2 + 2].reshape(2 * ATTN_BLOCK, 2 * ATTN_BLOCK)
                    sc = _dot(qs, k2, NT) * scale + bias2
                    p = jnp.exp(jnp.where(valid, sc, -jnp.inf) - lse_c)
                    ds = p * (_dot(dos, v2, NT) + corr_c)
                    db_ref[pi * 2:pi * 2 + 2] += ds.reshape(2, ATTN_BLOCK, 2 * ATTN_BLOCK)
                    dsb = ds.astype(BF16)
                    dq = _dot(dsb, k2) * scale
                    dq_ref[_sub_rows(s), sl] = jnp.where(low, dq[:ATTN_BLOCK], dq[ATTN_BLOCK:]).astype(BF16)
                    dkext[_sub_rows(s, 2), sl] += _dot(dsb, qs, TN) * scale
                    dvext[_sub_rows(s, 2), sl] += _dot(p.astype(BF16), dos, TN)
                return carry

            lax.fori_loop(0, sub, one_block, 0)
            head, tail = pl.ds(0, rows - ATTN_BLOCK), pl.ds(rows - ATTN_BLOCK, ATTN_BLOCK)
            for out_ref, carry_ref, ext in ((dk_ref, ck, dkext), (dv_ref, cv, dvext)):
                if sub > 1:
                    out_ref[head, :] = carry_ref[head, :].astype(BF16)
                out_ref[tail, :] = (carry_ref[tail, :] + ext[pl.ds(0, ATTN_BLOCK), :]).astype(BF16)
                carry_ref[...] = ext[pl.ds(ATTN_BLOCK, rows), :]

        @pl.when(sb == nsb)
        def _():
            dk_ref[...] = ck[...].astype(BF16)
            dv_ref[...] = cv[...].astype(BF16)

    clamp = lambda sb: jnp.minimum(sb, nsb - 1)
    cur = BS((rows, awg), lambda r, sb: (r * nsb + clamp(sb), 0))
    prev = BS((rows, awg), lambda r, sb: (r * nsb + jnp.maximum(sb - 1, 0), 0))
    bias_spec = BS(bias.shape, lambda r, sb: (0, 0, 0))
    return pl.pallas_call(
        body, name=name, grid=(d, nsb + 1),
        in_specs=_attn_specs(cols, nb, sub, awg, clamp) + [bias_spec, cur, cur, cur],
        out_specs=[cur, prev, prev, bias_spec],
        out_shape=[SDS((t, awg), BF16)] * 3 + [SDS(bias.shape, F32)],
        scratch_shapes=[pltpu.VMEM((rows + ATTN_BLOCK, awg), BF16)] * 2 + [pltpu.VMEM((rows + ATTN_BLOCK, awg), F32)] * 2
        + [pltpu.VMEM((rows, awg), F32)] * 2,
        compiler_params=_cp("arbitrary", "arbitrary"),
    )(*([qkv] * 5), bias, lse, do, corr)


def _head_sum(x, low):
    a = jnp.sum(jnp.where(low, x, 0.0), axis=-1, keepdims=True)
    b = jnp.sum(jnp.where(low, 0.0, x), axis=-1, keepdims=True)
    return jnp.where(low, a, b)


def _combine_weights(lses):
    mx = jnp.maximum(jnp.maximum(lses[0], lses[1]), lses[2])
    es = [jnp.exp(l - mx) for l in lses]
    tot = es[0] + es[1] + es[2]
    return [e / tot for e in es]


def _combine_fwd(os_, lses, *, name):
    t, awg = os_[0].shape
    tm = _pick(t, (512, 256, 128))

    def body(o0, o1, o2, l0, l1, l2, out_ref):
        al = _combine_weights([l0[...], l1[...], l2[...]])
        out_ref[...] = (al[0] * o0[...] + al[1] * o1[...] + al[2] * o2[...]).astype(BF16)

    blk = BS((tm, awg), lambda i: (i, 0))
    return pl.pallas_call(
        body, name=name, grid=(t // tm,), in_specs=[blk] * 6, out_specs=blk,
        out_shape=SDS((t, awg), BF16), compiler_params=_cp("parallel"),
    )(*os_, *lses)


def _combine_bwd(dattn, os_, lses, *, name):
    t, awg = dattn.shape
    tm = _pick(t, (512, 256, 128))

    def body(da_ref, o0, o1, o2, l0, l1, l2, d0, d1, d2, c0, c1, c2):
        low = lax.broadcasted_iota(jnp.int32, (tm, LANES), 1) < HEAD_DIM
        for pi in range(awg // LANES):
            sl = slice(pi * LANES, (pi + 1) * LANES)
            da = da_ref[:, sl]
            al = _combine_weights([l0[:, sl], l1[:, sl], l2[:, sl]])
            tot = jnp.zeros((tm, LANES), F32)
            for a, o in zip(al, (o0, o1, o2)):
                tot = tot + a * _head_sum(da * o[:, sl], low)
            for a, d_ref, c_ref in zip(al, (d0, d1, d2), (c0, c1, c2)):
                d_ref[:, sl] = (a * da).astype(BF16)
                c_ref[:, sl] = -a * tot

    blk = BS((tm, awg), lambda i: (i, 0))
    outs = pl.pallas_call(
        body, name=name, grid=(t // tm,), in_specs=[blk] * 7, out_specs=[blk] * 6,
        out_shape=[SDS((t, awg), BF16)] * 3 + [SDS((t, awg), F32)] * 3, compiler_params=_cp("parallel"),
    )(dattn, *os_, *lses)
    return outs[:3], outs[3:]


def _conv_block(width, *offsets):
    for c in (512, 256, 128):
        if width % c == 0 and all(o % c == 0 for o in offsets):
            return c
    raise ValueError((width, offsets))


CONV_ROWS = 32


def _conv_pre(x_ref, halo_ref, w_ref, b_ref, ext, i, tm):
    ext[pl.ds(0, HALO), :] = jnp.where(i > 0, halo_ref[...].astype(F32), 0.0)
    ext[pl.ds(HALO, tm), :] = x_ref[...].astype(F32)
    taps = [w_ref[pl.ds(k, 1), :] for k in range(CONV_WIDTH)]
    bias = b_ref[...]
    for r0 in range(0, tm, CONV_ROWS):
        xs = [ext[pl.ds(HALO + r0 - (CONV_WIDTH - 1) + k, CONV_ROWS), :] for k in range(CONV_WIDTH)]
        pre = bias + taps[0] * xs[0]
        for k in range(1, CONV_WIDTH):
            pre = pre + taps[k] * xs[k]
        yield r0, pre, xs


def _fold8(v):
    return jnp.sum(v.reshape(v.shape[0] // 8, 8, v.shape[1]), axis=0)


def _conv_fwd(proj, off, w, b, *, name):
    t = proj.shape[0]
    c = w.shape[1]
    cw = _conv_block(c, off)
    tm = _pick(t, (512, 256, 128))
    ob = off // cw

    def body(x_ref, halo_ref, w_ref, b_ref, o_ref, ext):
        for r0, pre, _ in _conv_pre(x_ref, halo_ref, w_ref, b_ref, ext, pl.program_id(1), tm):
            o_ref[pl.ds(r0, CONV_ROWS), :] = _silu(pre).astype(BF16)

    return pl.pallas_call(
        body, name=name, grid=(c // cw, t // tm),
        in_specs=[BS((tm, cw), lambda j, i: (i, ob + j)),
                  BS((HALO, cw), lambda j, i: (jnp.maximum(i * (tm // HALO) - 1, 0), ob + j)),
                  BS((CONV_WIDTH, cw), lambda j, i: (0, j)), BS((1, cw), lambda j, i: (0, j))],
        out_specs=BS((tm, cw), lambda j, i: (i, j)), out_shape=SDS((t, c), BF16),
        scratch_shapes=[pltpu.VMEM((HALO + tm, cw), F32)], compiler_params=_cp("parallel", "arbitrary"),
    )(proj, proj, w, b.reshape(1, c))


def _conv_bwd(proj, off, w, b, dxc, *, name):
    t = proj.shape[0]
    c = w.shape[1]
    cw = _conv_block(c, off)
    tm = _pick(t, (512, 256, 128))
    ob = off // cw
    nt = t // tm

    def body_pre(x_ref, halo_ref, w_ref, b_ref, d_ref, dp_ref, dw_ref, db_ref, ext):
        i = pl.program_id(1)

        @pl.when(i == 0)
        def _():
            dw_ref[...] = jnp.zeros_like(dw_ref)
            db_ref[...] = jnp.zeros_like(db_ref)

        db_acc = jnp.zeros((8, cw), F32)
        dw_acc = [jnp.zeros((8, cw), F32) for _ in range(CONV_WIDTH)]
        for r0, pre, xs in _conv_pre(x_ref, halo_ref, w_ref, b_ref, ext, i, tm):
            dpre = d_ref[pl.ds(r0, CONV_ROWS), :] * _dsilu(pre)
            dp_ref[pl.ds(r0, CONV_ROWS), :] = dpre
            db_acc = db_acc + _fold8(dpre)
            dw_acc = [acc + _fold8(dpre * x) for acc, x in zip(dw_acc, xs)]
        db_ref[...] += jnp.sum(db_acc, axis=0, keepdims=True)
        for k in range(CONV_WIDTH):
            dw_ref[pl.ds(k, 1), :] += jnp.sum(dw_acc[k], axis=0, keepdims=True)

    dpre, dw, db = pl.pallas_call(
        body_pre, name=name + "_pre", grid=(c // cw, nt),
        in_specs=[BS((tm, cw), lambda j, i: (i, ob + j)),
                  BS((HALO, cw), lambda j, i: (jnp.maximum(i * (tm // HALO) - 1, 0), ob + j)),
                  BS((CONV_WIDTH, cw), lambda j, i: (0, j)), BS((1, cw), lambda j, i: (0, j)),
                  BS((tm, cw), lambda j, i: (i, j))],
        out_specs=[BS((tm, cw), lambda j, i: (i, j)), BS((CONV_WIDTH, cw), lambda j, i: (0, j)), BS((1, cw), lambda j, i: (0, j))],
        out_shape=[SDS((t, c), F32), SDS((CONV_WIDTH, c), F32), SDS((1, c), F32)],
        scratch_shapes=[pltpu.VMEM((HALO + tm, cw), F32)], compiler_params=_cp("parallel", "arbitrary"),
    )(proj, proj, w, b.reshape(1, c), dxc)

    def body_in(dp_ref, nxt_ref, w_ref, dx_ref, ext):
        i = pl.program_id(1)
        ext[pl.ds(0, tm), :] = dp_ref[...]
        ext[pl.ds(tm, 8), :] = jnp.where(i < nt - 1, nxt_ref[...], 0.0)
        taps = [w_ref[pl.ds(k, 1), :] for k in range(CONV_WIDTH)]
        for r0 in range(0, tm, CONV_ROWS):
            dx = taps[CONV_WIDTH - 1] * ext[pl.ds(r0, CONV_ROWS), :]
            for k in range(CONV_WIDTH - 1):
                dx = dx + taps[k] * ext[pl.ds(r0 + CONV_WIDTH - 1 - k, CONV_ROWS), :]
            dx_ref[pl.ds(r0, CONV_ROWS), :] = dx.astype(BF16)

    dx = pl.pallas_call(
        body_in, name=name + "_in", grid=(c // cw, nt),
        in_specs=[BS((tm, cw), lambda j, i: (i, j)),
                  BS((8, cw), lambda j, i: (jnp.minimum((i + 1) * (tm // 8), t // 8 - 1), j)),
                  BS((CONV_WIDTH, cw), lambda j, i: (0, j))],
        out_specs=BS((tm, cw), lambda j, i: (i, j)), out_shape=SDS((t, c), BF16),
        scratch_shapes=[pltpu.VMEM((tm + 8, cw), F32)], compiler_params=_cp("parallel", "arbitrary"),
    )(dpre, dpre, w)
    return dx, dw, db[0]


def _softplus(x):
    return jnp.maximum(x, 0.0) + jnp.log(1.0 + jnp.exp(-jnp.abs(x)))


def _tril():
    return lax.broadcasted_iota(jnp.int32, (CHUNK, CHUNK), 0) >= lax.broadcasted_iota(jnp.int32, (CHUNK, CHUNK), 1)


def _ssd_prep(dtr, dt_bias, a, *, name):
    t = dtr.shape[0]
    nc = t // CHUNK
    cb = _pick(nc, (4, 2, 1))

    def body(dtr_ref, bias_ref, a_ref, o_ref):
        tril = _tril().astype(F32)
        for ci in range(cb):
            rows = pl.ds(ci * CHUNK, CHUNK)
            pre = dtr_ref[rows, :] + bias_ref[...]
            dt = _softplus(pre)
            la = _dot(tril, dt * a_ref[...], precision=HIGHEST)
            for k, v in enumerate((dt, la, la.T, _sigmoid(pre))):
                o_ref[rows, k * LANES:(k + 1) * LANES] = v

    vec = BS((1, LANES), lambda i: (0, 0))
    return pl.pallas_call(
        body, name=name, grid=(nc // cb,), in_specs=[BS((cb * CHUNK, LANES), lambda i: (i, 0)), vec, vec],
        out_specs=BS((cb * CHUNK, 4 * LANES), lambda i: (i, 0)), out_shape=SDS((t, 4 * LANES), F32),
        compiler_params=_cp("parallel"),
    )(dtr, dt_bias, a)


def _ssd_common(time_ref):
    part = lambda k: time_ref[:, k * LANES:(k + 1) * LANES]
    return part(3), part(0), part(1), part(2), _tril()


def _lane_col(x, lane, h):
    return jnp.sum(jnp.where(lane == h, x, 0.0), axis=-1, keepdims=True)


def _ssd_specs(rows, di, gw, cidx):
    nbx = di // LANES
    return [BS((rows, gw), lambda g, c: (cidx(c), g)),
            BS((rows, D_STATE), lambda g, c: (cidx(c), nbx + g)),
            BS((rows, D_STATE), lambda g, c: (cidx(c), nbx + SSM_GROUPS + g)),
            BS((rows, 4 * LANES), lambda g, c: (cidx(c), 0)),
            BS((1, LANES), lambda g, c: (0, 0)),
            BS((1, gw), lambda g, c: (0, g))]


def _chunk_rows(ci):
    return pl.ds(pl.multiple_of(ci * CHUNK, CHUNK), CHUNK)


def _ssd_fwd(xc, prep, a, dskip, *, name):
    t = xc.shape[0]
    di = xc.shape[1] - 2 * SSM_GROUPS * D_STATE
    gw = di // SSM_GROUPS
    hpg = gw // HEAD_DIM
    npair = gw // LANES
    nc = t // CHUNK
    cb = _pick(nc, (4, 2, 1))

    def body(xb_ref, bb_ref, cb_ref, timeb_ref, a_ref, dsk_ref, yb_ref, st_ref, state):
        @pl.when(pl.program_id(1) == 0)
        def _():
            state[...] = jnp.zeros_like(state)

        g = pl.program_id(0)

        def one_chunk(ci, carry):
            rows = _chunk_rows(ci)
            ssd_chunk(xb_ref.at[rows], bb_ref.at[rows], cb_ref.at[rows], timeb_ref.at[rows], dsk_ref,
                      yb_ref.at[rows], st_ref, state, ci, g)
            return carry

        lax.fori_loop(0, cb, one_chunk, 0)

    def ssd_chunk(x_ref, b_ref, c_ref, time_ref, dsk_ref, y_ref, st_ref, state, ci, g):
        st_ref[ci, 0] = state[...]
        _, dt, la, la_t, tril = _ssd_common(time_ref)
        lane = lax.broadcasted_iota(jnp.int32, (CHUNK, LANES), 1)
        sub = lax.broadcasted_iota(jnp.int32, (LANES, CHUNK), 0)
        lane1 = lax.broadcasted_iota(jnp.int32, (1, LANES), 1)
        low, low1 = lane < HEAD_DIM, lane1 < HEAD_DIM
        last = lax.broadcasted_iota(jnp.int32, (CHUNK, LANES), 0) == CHUNK - 1
        lend = jnp.sum(jnp.where(last, la, 0.0), axis=0, keepdims=True)
        bm, cm = b_ref[...], c_ref[...]
        gmat = _dot(cm, bm, NT)
        for p in range(npair):
            sl = slice(p * LANES, (p + 1) * LANES)
            ps = slice(p * D_STATE, (p + 1) * D_STATE)
            x2 = x_ref[:, sl].astype(F32)
            cols, ms = [], []
            for hh in range(2):
                h = g * hpg + p * 2 + hh
                col_la = _lane_col(la, lane, h)
                row_la = jnp.sum(jnp.where(sub == h, la_t, 0.0), axis=0, keepdims=True)
                lend_h = _lane_col(lend, lane1, h)
                decay = jnp.exp(jnp.where(tril, col_la - row_la, -jnp.inf))
                ms.append((gmat * decay).astype(BF16))
                cols.append((_lane_col(dt, lane, h), jnp.exp(col_la), jnp.exp(lend_h - col_la), jnp.exp(lend_h)))
            pair = lambda k: jnp.where(low, cols[0][k], cols[1][k])
            xdt = x2 * pair(0)
            xdtb = xdt.astype(BF16)
            s2 = state[ps, :]
            y = jnp.where(low, _dot(ms[0], xdtb), _dot(ms[1], xdtb))
            y = y + pair(1) * _dot(cm, s2.astype(BF16)) + x2 * dsk_ref[:, sl]
            y_ref[:, sl] = y
            state[ps, :] = s2 * jnp.where(low1, cols[0][3], cols[1][3]) + _dot(bm, (xdt * pair(2)).astype(BF16), TN)

    y, st = pl.pallas_call(
        body, name=name, grid=(SSM_GROUPS, nc // cb), in_specs=_ssd_specs(cb * CHUNK, di, gw, lambda c: c),
        out_specs=[BS((cb * CHUNK, gw), lambda g, c: (c, g)), BS((cb, 1, npair * D_STATE, LANES), lambda g, c: (c, g, 0, 0))],
        out_shape=[SDS((t, di), F32), SDS((nc, SSM_GROUPS, npair * D_STATE, LANES), F32)],
        scratch_shapes=[pltpu.VMEM((npair * D_STATE, LANES), F32)], compiler_params=_cp("parallel", "arbitrary"),
    )(xc, xc, xc, prep, a, dskip)
    return y, st


def _ssd_bwd(xc, prep, a, dskip, st, dy, *, name):
    t = xc.shape[0]
    di = xc.shape[1] - 2 * SSM_GROUPS * D_STATE
    gw = di // SSM_GROUPS
    hpg = gw // HEAD_DIM
    npair = gw // LANES
    nc = t // CHUNK
    cb = _pick(nc, (4, 2, 1))
    rev = lambda c: nc // cb - 1 - c

    def body(xb_ref, bb_ref, cb_ref, timeb_ref, a_ref, dsk_ref, st_ref, dyb_ref,
             dxb_ref, dbb_ref, dcb_ref, ddtrb_ref, da_ref, dbias_ref, ddsk_ref, dstate):
        @pl.when(pl.program_id(1) == 0)
        def _():
            dstate[...] = jnp.zeros_like(dstate)
            da_ref[...] = jnp.zeros_like(da_ref)
            dbias_ref[...] = jnp.zeros_like(dbias_ref)
            ddsk_ref[...] = jnp.zeros_like(ddsk_ref)

        g = pl.program_id(0)

        def one_chunk(j, carry):
            ci = cb - 1 - j
            rows = _chunk_rows(ci)
            ssd_chunk(xb_ref.at[rows], bb_ref.at[rows], cb_ref.at[rows], timeb_ref.at[rows], a_ref, dsk_ref,
                      st_ref.at[ci], dyb_ref.at[rows], dxb_ref.at[rows], dbb_ref.at[rows], dcb_ref.at[rows],
                      ddtrb_ref.at[:, rows], da_ref, dbias_ref, ddsk_ref, dstate, g)
            return carry

        lax.fori_loop(0, cb, one_chunk, 0)

    def ssd_chunk(x_ref, b_ref, c_ref, time_ref, a_ref, dsk_ref, st_ref, dy_ref,
                  dx_ref, db_ref, dc_ref, ddtr_ref, da_ref, dbias_ref, ddsk_ref, dstate, g):
        sig, dt, la, la_t, tril = _ssd_common(time_ref)
        lane = lax.broadcasted_iota(jnp.int32, (CHUNK, LANES), 1)
        sub = lax.broadcasted_iota(jnp.int32, (LANES, CHUNK), 0)
        lane1 = lax.broadcasted_iota(jnp.int32, (1, LANES), 1)
        low, low1 = lane < HEAD_DIM, lane1 < HEAD_DIM
        last = lax.broadcasted_iota(jnp.int32, (CHUNK, LANES), 0) == CHUNK - 1
        lend = jnp.sum(jnp.where(last, la, 0.0), axis=0, keepdims=True)
        bm, cm = b_ref[...], c_ref[...]
        gmat = _dot(cm, bm, NT)
        dg = jnp.zeros((CHUNK, CHUNK), F32)
        dla_cols = jnp.zeros((CHUNK, LANES), F32)
        dla_rows = jnp.zeros((LANES, CHUNK), F32)
        dtsum = jnp.zeros((CHUNK, LANES), F32)
        dbm = jnp.zeros((CHUNK, D_STATE), F32)
        dcm = jnp.zeros((CHUNK, D_STATE), F32)
        for p in range(npair):
            sl = slice(p * LANES, (p + 1) * LANES)
            ps = slice(p * D_STATE, (p + 1) * D_STATE)
            x2 = x_ref[:, sl].astype(F32)
            dy2 = dy_ref[:, sl]
            s2 = st_ref[0, ps, :]
            ds2 = dstate[ps, :]
            hs, cols, ms, decays = [], [], [], []
            for hh in range(2):
                h = g * hpg + p * 2 + hh
                col_la = _lane_col(la, lane, h)
                row_la = jnp.sum(jnp.where(sub == h, la_t, 0.0), axis=0, keepdims=True)
                lend_h = _lane_col(lend, lane1, h)
                decay = jnp.exp(jnp.where(tril, col_la - row_la, -jnp.inf))
                hs.append(h)
                decays.append(decay)
                ms.append(gmat * decay)
                cols.append((_lane_col(dt, lane, h), jnp.exp(col_la), jnp.exp(lend_h - col_la), jnp.exp(lend_h)))
            pair = lambda k: jnp.where(low, cols[0][k], cols[1][k])
            dtc, ec, eend = pair(0), pair(1), pair(2)
            eend_s = jnp.where(low1, cols[0][3], cols[1][3])
            xdt = x2 * dtc
            xdtb = xdt.astype(BF16)
            dys = dy2 * ec
            dysb = dys.astype(BF16)
            dxdt_state = eend * _dot(bm, ds2.astype(BF16))
            inter = dys * _dot(cm, s2.astype(BF16))
            u = dxdt_state * xdt
            sds = s2 * ds2
            dxdt = dxdt_state
            for hh in range(2):
                h = hs[hh]
                mh = low if hh == 0 else jnp.logical_not(low)
                dym = jnp.where(mh, dy2, 0.0).astype(BF16)
                dxdt = dxdt + _dot(ms[hh].astype(BF16), dym, TN)
                dm = _dot(dym, xdtb, NT)
                w = dm * ms[hh]
                dg = dg + dm * decays[hh]
                u_col = jnp.sum(jnp.where(mh, u, 0.0), axis=-1, keepdims=True)
                dlend = jnp.sum(u_col, axis=0, keepdims=True) + cols[hh][3] * jnp.sum(jnp.where(low1 if hh == 0 else jnp.logical_not(low1), jnp.sum(sds, axis=0, keepdims=True), 0.0), axis=-1, keepdims=True)
                col = jnp.sum(w, axis=-1, keepdims=True) + jnp.sum(jnp.where(mh, inter, 0.0), axis=-1, keepdims=True) - u_col
                dla_cols = dla_cols + jnp.where(lane == h, col + jnp.where(last, dlend, 0.0), 0.0)
                dla_rows = dla_rows - jnp.where(sub == h, jnp.sum(w, axis=0, keepdims=True), 0.0)
            for hh in range(2):
                mh = low if hh == 0 else jnp.logical_not(low)
                dtsum = dtsum + jnp.where(lane == hs[hh], jnp.sum(jnp.where(mh, dxdt * x2, 0.0), axis=-1, keepdims=True), 0.0)
            dcm = dcm + _dot(dysb, s2.astype(BF16), NT)
            dbm = dbm + _dot((xdt * eend).astype(BF16), ds2.astype(BF16), NT)
            dstate[ps, :] = ds2 * eend_s + _dot(cm, dysb, TN)
            dx_ref[:, sl] = dxdt * dtc + dy2 * dsk_ref[:, sl]
            ddsk_ref[:, sl] += jnp.sum(dy2 * x2, axis=0, keepdims=True)
        dgb = dg.astype(BF16)
        dc_ref[...] = dcm + _dot(dgb, bm)
        db_ref[...] = dbm + _dot(dgb, cm, TN)
        dla = dla_cols + dla_rows.T
        triu = lax.broadcasted_iota(jnp.int32, (CHUNK, CHUNK), 0) <= lax.broadcasted_iota(jnp.int32, (CHUNK, CHUNK), 1)
        ddta = _dot(triu.astype(F32), dla, precision=HIGHEST)
        ddt = ddta * a_ref[...] + dtsum
        da_ref[0] += jnp.sum(ddta * dt, axis=0, keepdims=True)
        ddtr = ddt * sig
        ddtr_ref[0] = ddtr
        dbias_ref[0] += jnp.sum(ddtr, axis=0, keepdims=True)

    vec = BS((1, 1, LANES), lambda g, c: (g, 0, 0))
    outs = pl.pallas_call(
        body, name=name, grid=(SSM_GROUPS, nc // cb),
        in_specs=_ssd_specs(cb * CHUNK, di, gw, rev) + [BS((cb, 1, npair * D_STATE, LANES), lambda g, c: (rev(c), g, 0, 0)),
                                                        BS((cb * CHUNK, gw), lambda g, c: (rev(c), g))],
        out_specs=[BS((cb * CHUNK, gw), lambda g, c: (rev(c), g)), BS((cb * CHUNK, D_STATE), lambda g, c: (rev(c), g)),
                   BS((cb * CHUNK, D_STATE), lambda g, c: (rev(c), g)), BS((1, cb * CHUNK, LANES), lambda g, c: (g, rev(c), 0)),
                   vec, vec, BS((1, gw), lambda g, c: (0, g))],
        out_shape=[SDS((t, di), F32), SDS((t, SSM_GROUPS * D_STATE), F32), SDS((t, SSM_GROUPS * D_STATE), F32),
                   SDS((SSM_GROUPS, t, LANES), F32), SDS((SSM_GROUPS, 1, LANES), F32), SDS((SSM_GROUPS, 1, LANES), F32),
                   SDS((1, di), F32)],
        scratch_shapes=[pltpu.VMEM((npair * D_STATE, LANES), F32)], compiler_params=_cp("parallel", "arbitrary"),
    )(xc, xc, xc, prep, a, dskip, st, dy)
    return outs


def _gate_norm_fwd(y, proj, zoff, w, *, name):
    t, di = y.shape
    gw = di // SSM_GROUPS
    tm = _pick(t, (512, 256, 128))
    zb = zoff // gw

    def body(y_ref, z_ref, w_ref, o_ref):
        yg = y_ref[...] * _silu(z_ref[...].astype(F32))
        r = lax.rsqrt(jnp.mean(yg * yg, axis=-1, keepdims=True) + EPS)
        o_ref[...] = (yg * r * w_ref[...]).astype(BF16)

    return pl.pallas_call(
        body, name=name, grid=(t // tm, SSM_GROUPS),
        in_specs=[BS((tm, gw), lambda i, g: (i, g)), BS((tm, gw), lambda i, g: (i, zb + g)), BS((1, gw), lambda i, g: (0, g))],
        out_specs=BS((tm, gw), lambda i, g: (i, g)), out_shape=SDS((t, di), BF16), compiler_params=_cp("parallel", "parallel"),
    )(y, proj, w.reshape(1, di))


def _gate_norm_bwd(dssm, y, proj, zoff, w, *, name):
    t, di = y.shape
    gw = di // SSM_GROUPS
    tm = _pick(t, (512, 256, 128))
    zb = zoff // gw

    def body(d_ref, y_ref, z_ref, w_ref, dy_ref, dz_ref, dw_ref):
        z = z_ref[...].astype(F32)
        yv = y_ref[...]
        sz = _silu(z)
        yg = yv * sz
        r = lax.rsqrt(jnp.mean(yg * yg, axis=-1, keepdims=True) + EPS)
        n = yg * r
        d = d_ref[...]
        dn = d * w_ref[...]
        dyg = r * (dn - n * jnp.mean(dn * n, axis=-1, keepdims=True))
        dy_ref[...] = dyg * sz
        dz_ref[...] = (dyg * yv * _dsilu(z)).astype(BF16)

        @pl.when(pl.program_id(1) == 0)
        def _():
            dw_ref[...] = jnp.zeros_like(dw_ref)

        dw_ref[...] += jnp.sum(d * n, axis=0, keepdims=True)

    blk = BS((tm, gw), lambda g, i: (i, g))
    dy, dz, dw = pl.pallas_call(
        body, name=name, grid=(SSM_GROUPS, t // tm),
        in_specs=[blk, blk, BS((tm, gw), lambda g, i: (i, zb + g)), BS((1, gw), lambda g, i: (0, g))],
        out_specs=[blk, blk, BS((1, gw), lambda g, i: (0, g))],
        out_shape=[SDS((t, di), F32), SDS((t, di), BF16), SDS((1, di), F32)], compiler_params=_cp("parallel", "arbitrary"),
    )(dssm, y, proj, w.reshape(1, di))
    return dy, dz, dw[0]


def _merge_fwd(proj, goff, ga, gs, *, name):
    t, d = ga.shape
    cw = _conv_block(d, goff)
    tm = _pick(t, (512, 256, 128))
    gb = goff // cw

    def body(g0, g1, a_ref, s_ref, o_ref):
        o_ref[...] = (_sigmoid(g0[...].astype(F32)) * a_ref[...] + _sigmoid(g1[...].astype(F32)) * s_ref[...]).astype(BF16)

    blk = BS((tm, cw), lambda i, j: (i, j))
    return pl.pallas_call(
        body, name=name, grid=(t // tm, d // cw),
        in_specs=[BS((tm, cw), lambda i, j: (i, gb + j)), BS((tm, cw), lambda i, j: (i, gb + d // cw + j)), blk, blk],
        out_specs=blk, out_shape=SDS((t, d), BF16), compiler_params=_cp("parallel", "parallel"),
    )(proj, proj, ga, gs)


def _merge_bwd(proj, goff, ga, gs, dm, *, name):
    t, d = ga.shape
    cw = _conv_block(d, goff)
    tm = _pick(t, (512, 256, 128))
    gb = goff // cw

    def body(g0, g1, a_ref, s_ref, dm_ref, da_ref, ds_ref, dg0_ref, dg1_ref):
        dmv = dm_ref[...]
        s0 = _sigmoid(g0[...].astype(F32))
        s1 = _sigmoid(g1[...].astype(F32))
        da_ref[...] = (s0 * dmv).astype(BF16)
        ds_ref[...] = (s1 * dmv).astype(BF16)
        dg0_ref[...] = (dmv * a_ref[...] * s0 * (1.0 - s0)).astype(BF16)
        dg1_ref[...] = (dmv * s_ref[...] * s1 * (1.0 - s1)).astype(BF16)

    blk = BS((tm, cw), lambda i, j: (i, j))
    return pl.pallas_call(
        body, name=name, grid=(t // tm, d // cw),
        in_specs=[BS((tm, cw), lambda i, j: (i, gb + j)), BS((tm, cw), lambda i, j: (i, gb + d // cw + j)), blk, blk, blk],
        out_specs=[blk] * 4, out_shape=[SDS((t, d), BF16)] * 4, compiler_params=_cp("parallel", "parallel"),
    )(proj, proj, ga, gs, dm)


def _swiglu_fwd(u, *, name):
    t, two_f = u.shape
    f = two_f // 2
    cw = _wide(f)
    tm = _pick(t, (512, 256, 128))

    def body(g_ref, u_ref, o_ref):
        o_ref[...] = (_silu(g_ref[...].astype(F32)) * u_ref[...].astype(F32)).astype(BF16)

    return pl.pallas_call(
        body, name=name, grid=(t // tm, f // cw),
        in_specs=[BS((tm, cw), lambda i, j: (i, j)), BS((tm, cw), lambda i, j: (i, f // cw + j))],
        out_specs=BS((tm, cw), lambda i, j: (i, j)), out_shape=SDS((t, f), BF16), compiler_params=_cp("parallel", "parallel"),
    )(u, u)


def _swiglu_bwd(u, df, *, name):
    t, two_f = u.shape
    f = two_f // 2
    cw = _wide(f)
    tm = _pick(t, (256, 128))

    def body(u_ref, d_ref, o_ref):
        for j in range(f // cw):
            gate, up = slice(j * cw, (j + 1) * cw), slice(f + j * cw, f + (j + 1) * cw)
            gt = u_ref[:, gate].astype(F32)
            d = d_ref[:, gate].astype(F32)
            o_ref[:, gate] = (d * u_ref[:, up].astype(F32) * _dsilu(gt)).astype(BF16)
            o_ref[:, up] = (d * _silu(gt)).astype(BF16)

    return pl.pallas_call(
        body, name=name, grid=(t // tm,), in_specs=[BS((tm, two_f), lambda i: (i, 0)), BS((tm, f), lambda i: (i, 0))],
        out_specs=BS((tm, two_f), lambda i: (i, 0)), out_shape=SDS((t, two_f), BF16), compiler_params=_cp("parallel"),
    )(u, df)


def _row_block(rows, cols, n_arrays):
    budget = VMEM_LIMIT_BYTES // 3
    for tr in (512, 256, 128, 64, 32, 16, 8):
        if rows % tr == 0 and tr * cols * 4 * n_arrays * 2 <= budget:
            return tr
    raise ValueError((rows, cols))


def _concat_cols(pieces, *, name):
    pieces = [p if isinstance(p, tuple) else (p, p.shape[1], 0) for p in pieces]
    rows, dtype = pieces[0][0].shape[0], pieces[0][0].dtype
    widths = [w for _, w, _ in pieces]
    total = sum(widths)
    assert all(w % LANES == 0 for w in widths) and all(a.dtype == dtype and a.shape[0] == rows for a, _, _ in pieces)
    tr = next(c for c in (512, 256, 128, 64, 32, 16) if rows % c == 0 and 4 * c * total * dtype.itemsize <= VMEM_LIMIT_BYTES // 2)

    def body(*refs):
        o_ref = refs[-1]
        off = 0
        for p_ref, w in zip(refs[:-1], widths):
            o_ref[:, off:off + w] = p_ref[...]
            off += w

    return pl.pallas_call(
        body, name=name, grid=(rows // tr,), in_specs=[BS((tr, w), lambda i, j=j: (i, j)) for _, w, j in pieces],
        out_specs=BS((tr, total), lambda i: (i, 0)), out_shape=SDS((rows, total), dtype), compiler_params=_cp("parallel"),
    )(*[a for a, _, _ in pieces])


def _add_own_layer(g0, g1, got, core, *, name):
    rows, cols = got.shape
    tr = _row_block(rows, cols, 4)

    def body(core_ref, g0_ref, g1_ref, got_ref, o_ref):
        o_ref[...] = (jnp.where(core_ref[0] == 0, g0_ref[...], g1_ref[...]) + got_ref[...]).astype(o_ref.dtype)

    blk = BS((tr, cols), lambda i, cr: (i, 0))
    grid_spec = pltpu.PrefetchScalarGridSpec(
        num_scalar_prefetch=1, grid=(rows // tr,),
        in_specs=[BS((tr, cols), lambda i, cr: (i * (1 - cr[0]), 0)), BS((tr, cols), lambda i, cr: (i * cr[0], 0)), blk],
        out_specs=blk)
    return pl.pallas_call(body, name=name, grid_spec=grid_spec, out_shape=SDS((rows, cols), BF16),
                          compiler_params=_cp("arbitrary"))(core, g0, g1, got)


def _sum_chips(a, *, name):
    _, rows, cols = a.shape
    tr = _row_block(rows, cols, 5)

    def body(a_ref, o_ref):
        o_ref[...] = ((a_ref[0].astype(F32) + a_ref[1].astype(F32)) + a_ref[2].astype(F32)) + a_ref[3].astype(F32)

    return pl.pallas_call(body, name=name, grid=(rows // tr,), in_specs=[BS((N_CHIPS, tr, cols), lambda i: (0, i, 0))],
                          out_specs=BS((tr, cols), lambda i: (i, 0)), out_shape=SDS((rows, cols), F32),
                          compiler_params=_cp("parallel"))(a)


def _adamw(w, g, m, v, *, name):
    rows, cols = w.shape
    tr = _row_block(rows, cols, 7) if rows % 8 == 0 else rows
    c1 = 1.0 - ADAM_B1 ** ADAM_STEP
    c2 = 1.0 - ADAM_B2 ** ADAM_STEP

    def body(w_ref, g_ref, m_ref, v_ref, d_ref, nm_ref, nv_ref):
        gv = g_ref[...]
        nm = ADAM_B1 * m_ref[...] + (1.0 - ADAM_B1) * gv
        nv = ADAM_B2 * v_ref[...] + (1.0 - ADAM_B2) * (gv * gv)
        d_ref[...] = -ADAM_LR * ((nm / c1) / (jnp.sqrt(nv / c2) + ADAM_EPS) + ADAM_WD * w_ref[...])
        nm_ref[...] = nm
        nv_ref[...] = nv

    blk = BS((tr, cols), lambda i: (i, 0))
    return pl.pallas_call(body, name=name, grid=(rows // tr,), in_specs=[blk] * 4, out_specs=[blk] * 3,
                          out_shape=[SDS((rows, cols), F32)] * 3, compiler_params=_cp("parallel"))(w, g, m, v)


def _adamw_layers(w, g_own, g_other, m, v, core, *, name):
    _, rows, cols = w.shape
    tr = _row_block(rows, cols, 9)
    c1 = 1.0 - ADAM_B1 ** ADAM_STEP
    c2 = 1.0 - ADAM_B2 ** ADAM_STEP

    def body(core_ref, w_ref, own_ref, oth_ref, m_ref, v_ref, g_ref, d_ref, nm_ref, nv_ref):
        gv = jnp.where(pl.program_id(0) == core_ref[0], own_ref[...], oth_ref[...])
        nm = ADAM_B1 * m_ref[0] + (1.0 - ADAM_B1) * gv
        nv = ADAM_B2 * v_ref[0] + (1.0 - ADAM_B2) * (gv * gv)
        g_ref[0] = gv
        d_ref[0] = -ADAM_LR * ((nm / c1) / (jnp.sqrt(nv / c2) + ADAM_EPS) + ADAM_WD * w_ref[0])
        nm_ref[0] = nm
        nv_ref[0] = nv

    own_here = lambda l, cr: 1 - (l - cr[0]) * (l - cr[0])
    slab = BS((1, tr, cols), lambda l, i, cr: (l, i, 0))
    grid_spec = pltpu.PrefetchScalarGridSpec(
        num_scalar_prefetch=1, grid=(2, rows // tr),
        in_specs=[slab, BS((tr, cols), lambda l, i, cr: (i * own_here(l, cr), 0)),
                  BS((tr, cols), lambda l, i, cr: (i * (1 - own_here(l, cr)), 0)), slab, slab],
        out_specs=[slab] * 4)
    return pl.pallas_call(body, name=name, grid_spec=grid_spec, out_shape=[SDS(w.shape, F32)] * 4,
                          compiler_params=_cp("arbitrary", "arbitrary"))(core, w, g_own, g_other, m, v)


ANY = BS(memory_space=pl.ANY)


def _place():
    x, y, c = lax.axis_index("x"), lax.axis_index("y"), lax.axis_index("c")
    return x, y, c, [(1 - x, y), (x, 1 - y), (1 - x, 1 - y)]


def _gather_shards(arrs, *, name):
    n = len(arrs)

    def body(*refs):
        ins, outs = refs[:n], refs[n:2 * n]
        send_sems, recv_sems, pass_send_sems, pass_recv_sems = refs[2 * n:]
        x, y, c, chips = _place()
        s = 2 * x + y

        def ici(i, j, src_chip, to):
            src = ins[i].at[c] if src_chip is None else outs[i].at[src_chip, c]
            return pltpu.make_async_remote_copy(
                src_ref=src, dst_ref=outs[i].at[s if src_chip is None else src_chip, c], send_sem=send_sems.at[i * 3 + j],
                recv_sem=recv_sems.at[i * 3 + j], device_id=to, device_id_type=MESH)

        def d2d(i, j, src_chip, layer):
            slab = outs[i].at[src_chip, layer]
            return pltpu.make_async_remote_copy(
                src_ref=slab, dst_ref=slab, send_sem=pass_send_sems.at[i * 3 + j], recv_sem=pass_recv_sems.at[i * 3 + j],
                device_id=(x, y, 1 - c), device_id_type=MESH)

        sent = []
        for i in range(n):
            for j, (px, py) in enumerate(chips):
                cp = ici(i, j, None, (px, py, c))
                cp.start()
                sent.append(cp)
        passed = []
        for i in range(n):
            for j, (px, py) in enumerate(chips):
                ici(i, j, 2 * px + py, (x, y, c)).wait_recv()
                cp = d2d(i, j, 2 * px + py, c)
                cp.start()
                passed.append(cp)
        for i in range(n):
            for j, (px, py) in enumerate(chips):
                d2d(i, j, 2 * px + py, 1 - c).wait_recv()
        for cp in sent + passed:
            cp.wait_send()

    return pl.pallas_call(
        body, name=name, in_specs=[ANY] * n, out_specs=[ANY] * n,
        out_shape=[SDS((N_CHIPS,) + a.shape, a.dtype) for a in arrs],
        scratch_shapes=[pltpu.SemaphoreType.DMA((3 * n,))] * 4,
    )(*arrs)


def _pair_swap_layers(layer0, layer1, *, name):
    n = len(layer0)

    def body(*refs):
        in0, in1, outs = refs[:n], refs[n:2 * n], refs[2 * n:3 * n]
        send_sems, recv_sems = refs[3 * n:]
        x, y, c, _ = _place()

        def copy(src, i):
            return pltpu.make_async_remote_copy(
                src_ref=src[i], dst_ref=outs[i], send_sem=send_sems.at[i], recv_sem=recv_sems.at[i],
                device_id=(x, y, 1 - c), device_id_type=MESH)

        @pl.when(c == 0)
        def _():
            for i in range(n):
                copy(in1, i).start()

        @pl.when(c == 1)
        def _():
            for i in range(n):
                copy(in0, i).start()

        for i in range(n):
            copy(in0, i).wait()

    return pl.pallas_call(
        body, name=name, in_specs=[ANY] * (2 * n), out_specs=[ANY] * n, out_shape=[SDS(a.shape, a.dtype) for a in layer0],
        scratch_shapes=[pltpu.SemaphoreType.DMA((n,)), pltpu.SemaphoreType.DMA((n,))],
    )(*layer0, *layer1)


def _scatter_to_chips(arrs, *, name):
    n = len(arrs)

    def body(*refs):
        ins, outs = refs[:n], refs[n:2 * n]
        send_sems, recv_sems = refs[2 * n:]
        x, y, c, chips = _place()
        s = 2 * x + y
        copies = []
        for i in range(n):
            for j, (px, py) in enumerate(chips):
                cp = pltpu.make_async_remote_copy(
                    src_ref=ins[i].at[2 * px + py], dst_ref=outs[i].at[s], send_sem=send_sems.at[i * 3 + j],
                    recv_sem=recv_sems.at[i * 3 + j], device_id=(px, py, c), device_id_type=MESH)
                cp.start()
                copies.append(cp)
        for cp in copies:
            cp.wait()

    return pl.pallas_call(
        body, name=name, in_specs=[ANY] * n, out_specs=[ANY] * n, out_shape=[SDS(a.shape, a.dtype) for a in arrs],
        scratch_shapes=[pltpu.SemaphoreType.DMA((3 * n,)), pltpu.SemaphoreType.DMA((3 * n,))],
    )(*arrs)


def _pair_swap(arrs, *, name):
    n = len(arrs)

    def body(*refs):
        ins, outs = refs[:n], refs[n:2 * n]
        send_sems, recv_sems = refs[2 * n:]
        x, y, c, _ = _place()
        copies = []
        for i in range(n):
            cp = pltpu.make_async_remote_copy(
                src_ref=ins[i], dst_ref=outs[i], send_sem=send_sems.at[i], recv_sem=recv_sems.at[i],
                device_id=(x, y, 1 - c), device_id_type=MESH)
            cp.start()
            copies.append(cp)
        for cp in copies:
            cp.wait()

    return pl.pallas_call(
        body, name=name, in_specs=[ANY] * n, out_specs=[ANY] * n, out_shape=[SDS(a.shape, a.dtype) for a in arrs],
        scratch_shapes=[pltpu.SemaphoreType.DMA((n,)), pltpu.SemaphoreType.DMA((n,))],
    )(*arrs)


def _allreduce_small(v, *, name):
    rows, cols = v.shape

    def body(v_ref, o_ref, gath, send_sems, recv_sems):
        x, y, c, _ = _place()
        me = 4 * x + 2 * y + c
        gath[me] = v_ref[...]
        copies = []
        for k in range(1, N_DEV):
            fx, fy, fc = (k >> 2) & 1, (k >> 1) & 1, k & 1
            peer = (1 - x if fx else x, 1 - y if fy else y, 1 - c if fc else c)
            cp = pltpu.make_async_remote_copy(
                src_ref=v_ref, dst_ref=gath.at[me], send_sem=send_sems.at[k - 1], recv_sem=recv_sems.at[k - 1],
                device_id=peer, device_id_type=MESH)
            cp.start()
            copies.append(cp)
        for cp in copies:
            cp.wait()
        acc = gath[0]
        for k in range(1, N_DEV):
            acc = acc + gath[k]
        o_ref[...] = acc

    vm = BS(memory_space=pltpu.VMEM)
    return pl.pallas_call(
        body, name=name, in_specs=[vm], out_specs=vm, out_shape=SDS((rows, cols), F32),
        scratch_shapes=[pltpu.VMEM((N_DEV, rows, cols), F32), pltpu.SemaphoreType.DMA((N_DEV - 1,)), pltpu.SemaphoreType.DMA((N_DEV - 1,))],
    )(v)


def _t5_bucket(dist):
    max_exact = N_REL_BUCKETS // 2
    d_f = jnp.maximum(dist, 1).astype(F32)
    large = max_exact + (jnp.log(d_f / max_exact) / math.log(REL_MAX_DISTANCE / max_exact) * (N_REL_BUCKETS - max_exact)).astype(jnp.int32)
    return jnp.where(dist < max_exact, dist, jnp.minimum(large, N_REL_BUCKETS - 1))


def _rel_buckets(dilation):
    qi = jnp.arange(ATTN_BLOCK)[:, None]
    kj = jnp.arange(2 * ATTN_BLOCK)[None, :]
    return _t5_bucket(jnp.clip(qi + ATTN_BLOCK - kj, 0, N_STEPS) * dilation)


def _layer_fwd(h, p, biases, lname):
    sv = {"h": h}
    xn1 = _rms_fwd(h, p["norm1_w"], name=lname + "norm1")
    proj = _matmul(xn1, p["w_main"], out_dtype=BF16, name=lname + "in_proj")
    dtr = _matmul(xn1, p["w_dt"], out_dtype=F32, name=lname + "in_proj_dt")
    xn1_rm, qkvs = [xn1], [proj]
    for g in range(1, N_GROUPS_ATTN):
        xn1_rm.append(_to_residue_major(xn1, DILATIONS[g]))
        qkvs.append(_matmul(xn1_rm[g], p["w_qkv"][g], out_dtype=BF16, name=f"{lname}in_proj_qkv{g}"))
    os_, lses, lses_rm = [], [], []
    for g, d in enumerate(DILATIONS):
        o, lse = _attn_fwd(qkvs[g], (0, 1, 2), biases[g], d, name=f"{lname}attn{g}")
        os_.append(_to_token_major(o, d))
        lses.append(_to_token_major(lse, d))
        lses_rm.append(lse)
    sv.update(xn1_rm=xn1_rm, qkvs=qkvs, lses_rm=lses_rm)
    attn = _combine_fwd(os_, lses, name=lname + "attn_combine")
    xc = _conv_fwd(proj, p["off_xbc"], p["conv_w"], p["conv_b"], name=lname + "conv")
    prep = _ssd_prep(dtr, p["dt_bias"], p["a"], name=lname + "ssd_prep")
    sv["prep"] = prep
    y, st = _ssd_fwd(xc, prep, p["a"], p["dskip"], name=lname + "ssd")
    ssm = _gate_norm_fwd(y, proj, p["off_z"], p["ssm_norm_w"], name=lname + "gate_norm")
    ga = _matmul(attn, p["w_attn_branch"], name=lname + "attn_branch")
    gs = _matmul(ssm, p["w_ssm_branch"], name=lname + "ssm_branch")
    merged = _merge_fwd(proj, p["off_gate"], ga, gs, name=lname + "merge")
    h1 = _matmul(merged, p["w_out"], res=h, name=lname + "out_proj")
    xn2 = _rms_fwd(h1, p["norm2_w"], name=lname + "norm2")
    u = _matmul(xn2, p["w_ffn_in"], out_dtype=BF16, name=lname + "ffn_in")
    f = _swiglu_fwd(u, name=lname + "swiglu")
    h2 = _matmul(f, p["w_ffn_out"], res=h1, name=lname + "ffn_out")
    sv.update(xn1=xn1, proj=proj, dtr=dtr, os=os_, lses=lses, attn=attn, xc=xc, y=y, st=st, ssm=ssm, ga=ga, gs=gs,
              merged=merged, h1=h1, xn2=xn2, u=u, f=f)
    return h2, sv


def _layer_bwd(dh2, p, sv, biases, lname):
    gr = {}
    lname = lname + "bwd_"
    df = _matmul(dh2, p["w_ffn_out"], tb=True, out_dtype=BF16, name=lname + "ffn_out_dx")
    gr["w_ffn_out"] = _matmul(sv["f"], dh2, ta=True, name=lname + "ffn_out_dw")
    du = _swiglu_bwd(sv["u"], df, name=lname + "swiglu")
    dxn2 = _matmul(du, p["w_ffn_in"], tb=True, name=lname + "ffn_in_dx")
    gr["w_ffn_in"] = _matmul(sv["xn2"], du, ta=True, name=lname + "ffn_in_dw")
    dh1, gr["norm2_w"] = _rms_bwd(sv["h1"], p["norm2_w"], [dxn2], dh2, name=lname + "norm2")
    dmerged = _matmul(dh1, p["w_out"], tb=True, name=lname + "out_proj_dx")
    gr["w_out"] = _matmul(sv["merged"], dh1, ta=True, name=lname + "out_proj_dw")
    dga, dgs, dg0, dg1 = _merge_bwd(sv["proj"], p["off_gate"], sv["ga"], sv["gs"], dmerged, name=lname + "merge")
    dattn = _matmul(dga, p["w_attn_branch"], tb=True, name=lname + "attn_branch_dx")
    gr["w_attn_branch"] = _matmul(sv["attn"], dga, ta=True, name=lname + "attn_branch_dw")
    dssm = _matmul(dgs, p["w_ssm_branch"], tb=True, name=lname + "ssm_branch_dx")
    gr["w_ssm_branch"] = _matmul(sv["ssm"], dgs, ta=True, name=lname + "ssm_branch_dw")
    dy, dz, gr["ssm_norm_w"] = _gate_norm_bwd(dssm, sv["y"], sv["proj"], p["off_z"], p["ssm_norm_w"], name=lname + "gate_norm")
    dxs, dbm, dcm, ddtr4, da4, dbias4, ddsk = _ssd_bwd(sv["xc"], sv["prep"], p["a"], p["dskip"], sv["st"], dy,
                                                       name=lname + "ssd")
    nsh = p["n_ssm_heads"]
    ddtr = jnp.sum(ddtr4, axis=0)
    gr["a_log"] = jnp.sum(da4, axis=(0, 1))[:nsh] * p["a"][0, :nsh]
    gr["dt_bias"] = jnp.sum(dbias4, axis=(0, 1))[:nsh]
    gr["d_skip"] = jnp.sum(ddsk.reshape(nsh, HEAD_DIM), axis=1)
    di = dxs.shape[1]
    dxbc, dcw, dcb = [], [], []
    for part, (lo, hi) in zip((dxs, dbm, dcm), ((0, di), (di, di + dbm.shape[1]), (di + dbm.shape[1], di + 2 * dbm.shape[1]))):
        dx_, dw_, db_ = _conv_bwd(sv["proj"], p["off_xbc"] + lo, p["conv_w"][:, lo:hi], p["conv_b"][lo:hi], part,
                                  name=f"{lname}conv{lo}")
        dxbc.append(dx_)
        dcw.append(dw_)
        dcb.append(db_)
    gr["conv_w"] = jnp.concatenate(dcw, axis=1)
    gr["conv_b"] = jnp.concatenate(dcb, axis=0)
    dos, corrs = _combine_bwd(dattn, sv["os"], sv["lses"], name=lname + "attn_combine")
    dqkvs, dbiases = [], []
    for g, d in enumerate(DILATIONS):
        dq, dk, dv, dbias = _attn_bwd(sv["qkvs"][g], (0, 1, 2), biases[g], sv["lses_rm"][g], _to_residue_major(dos[g], d),
                                      _to_residue_major(corrs[g], d), d, name=f"{lname}attn{g}")
        dqkvs.append([dq, dk, dv])
        dbiases.append(dbias)
    dmain = _concat_cols(dqkvs[0] + [dz] + dxbc + [dg0, dg1], name=lname + "in_proj_join")
    dxn1 = [_matmul(dmain, p["w_main"], tb=True, name=lname + "in_proj_dx"),
            _matmul(ddtr, p["w_dt"], tb=True, name=lname + "in_proj_dt_dx")]
    dw_main = _matmul(sv["xn1"], dmain, ta=True, name=lname + "in_proj_dw")
    dw_dt = _matmul(sv["xn1"], ddtr, ta=True, name=lname + "in_proj_dt_dw")
    dw_qkv = [dw_main]
    for g in range(1, N_GROUPS_ATTN):
        dqkv = _concat_cols(dqkvs[g], name=f"{lname}in_proj_qkv{g}_join")
        dxn1.append(_to_token_major(_matmul(dqkv, p["w_qkv"][g], tb=True, name=f"{lname}in_proj_qkv{g}_dx"), DILATIONS[g]))
        dw_qkv.append(_matmul(sv["xn1_rm"][g], dqkv, ta=True, name=f"{lname}in_proj_qkv{g}_dw"))
    awg, og = dqkvs[0][0].shape[1], p["off_gate"]
    cols = [dw[:, i * awg:(i + 1) * awg] for i in range(3) for dw in dw_qkv]
    gr["w_in"] = jnp.concatenate(cols + [dw_main[:, 3 * awg:og], dw_dt[:, :nsh], dw_main[:, og:]], axis=1)
    dh, gr["norm1_w"] = _rms_bwd(sv["h"], p["norm1_w"], dxn1, dh1, name=lname + "norm1")
    return dh, gr, dbiases


def _layer_params(l, w, n_ssm_heads, hg):
    awg = hg * HEAD_DIM
    aw = N_GROUPS_ATTN * awg
    di = n_ssm_heads * HEAD_DIM
    xbc = di + 2 * SSM_GROUPS * D_STATE
    in_dt = 3 * aw + di + xbc
    w_in = w["w_in"][l]
    qkv_cols = lambda g: [(w_in, awg, i * N_GROUPS_ATTN + g) for i in range(3)]
    z_xbc_cols = [(w_in, awg, j) for j in range(3 * N_GROUPS_ATTN, in_dt // awg)]
    assert in_dt % awg == 0
    pad = lambda v: jnp.pad(v.astype(F32), (0, LANES - n_ssm_heads)).reshape(1, LANES)
    return dict(
        n_ssm_heads=n_ssm_heads, off_z=3 * awg, off_xbc=3 * awg + di, off_gate=3 * awg + di + xbc,
        w_main=_concat_cols(qkv_cols(0) + z_xbc_cols + [w_in[:, in_dt + n_ssm_heads:]], name=f"l{l}_w_main"),
        w_qkv=[None] + [_concat_cols(qkv_cols(g), name=f"l{l}_w_qkv{g}") for g in range(1, N_GROUPS_ATTN)],
        w_dt=jnp.pad(w_in[:, in_dt:in_dt + n_ssm_heads], ((0, 0), (0, LANES - n_ssm_heads))),
        norm1_w=w["norm1_w"][l], norm2_w=w["norm2_w"][l], conv_w=w["conv_w"][l], conv_b=w["conv_b"][l],
        dt_bias=pad(w["dt_bias"][l]), a=pad(-jnp.exp(w["a_log"][l])),
        dskip=jnp.repeat(w["d_skip"][l], HEAD_DIM).reshape(1, di), ssm_norm_w=w["ssm_norm_w"][l],
        w_attn_branch=w["w_attn_branch"][l], w_ssm_branch=w["w_ssm_branch"][l], w_out=w["w_out"][l],
        w_ffn_in=w["w_ffn_in"][l], w_ffn_out=w["w_ffn_out"][l],
    )


def _local_step(x, tgt, w):
    depth = w["norm1_w"].shape[0]
    n_ssm_heads = w["dt_bias"].shape[1]
    hg = w["rel_bias"].shape[1] // N_GROUPS_ATTN
    onehots = [(_rel_buckets(dil)[:, :, None] == jnp.arange(N_REL_BUCKETS)[None, None, :]).astype(F32) for dil in DILATIONS]
    biases = [jnp.einsum("qkb,bh->hqk", oh, w["rel_bias"][:, g * hg:(g + 1) * hg].astype(F32), precision=HIGHEST)
              for g, oh in enumerate(onehots)]
    params = [_layer_params(l, w, n_ssm_heads, hg) for l in range(depth)]
    h = x
    saved = []
    for l in range(depth):
        h, sv = _layer_fwd(h, params[l], biases, f"l{l}_")
        saved.append(sv)
    loss, dh, g_final = _loss_head(h, w["final_norm_w"], tgt, name="loss_head")
    grads = [None] * depth
    dbias_tot = [jnp.zeros(b.shape, F32) for b in biases]
    for l in reversed(range(depth)):
        dh, grads[l], dbiases = _layer_bwd(dh, params[l], saved[l], biases, f"l{l}_")
        dbias_tot = [a + b for a, b in zip(dbias_tot, dbiases)]
    out = {k: [gl[k] for gl in grads] if k in MATRICES else jnp.stack([gl[k] for gl in grads]) for k in grads[0]}
    out["final_norm_w"] = g_final
    drel = []
    for g, (oh, db) in enumerate(zip(onehots, dbias_tot)):
        oh_t = jnp.pad(oh.reshape(-1, N_REL_BUCKETS).T, ((0, LANES - N_REL_BUCKETS), (0, 0)))
        db_rows = jnp.pad(db.reshape(hg, -1), ((0, LANES - hg), (0, 0)))
        drel.append(_matmul(oh_t, db_rows, tb=True, name=f"rel_bias_fold{g}")[:N_REL_BUCKETS, :hg])
    out["rel_bias"] = jnp.concatenate(drel, axis=1)
    return loss, dh, out


MATRICES = ("w_in", "w_attn_branch", "w_ssm_branch", "w_out", "w_ffn_in", "w_ffn_out")
COL_SHARDED = ("w_in", "w_attn_branch", "w_ffn_in")
SMALL = ("norm1_w", "conv_b", "dt_bias", "a_log", "d_skip", "ssm_norm_w", "norm2_w", "rel_bias", "final_norm_w")
WEIGHTS = ("norm1_w", "w_in", "conv_w", "conv_b", "dt_bias", "a_log", "d_skip", "ssm_norm_w", "w_attn_branch",
           "w_ssm_branch", "w_out", "norm2_w", "w_ffn_in", "w_ffn_out", "rel_bias", "final_norm_w")
SMALL_COLS = 1024


def _unshard(name, g):
    _, depth, r, c = g.shape
    if name in COL_SHARDED or name == "conv_w":
        return jnp.transpose(g, (1, 2, 0, 3)).reshape(depth, r, N_CHIPS * c)
    return jnp.transpose(g, (1, 0, 2, 3)).reshape(depth, N_CHIPS * r, c)


def _to_shards(name, g):
    r, c = g.shape
    if name in COL_SHARDED:
        return jnp.transpose(g.reshape(r, N_CHIPS, c // N_CHIPS), (1, 0, 2))
    return g.reshape(N_CHIPS, r // N_CHIPS, c)


def kernel(x, norm1_w, w_in, conv_w, conv_b, dt_bias, a_log, d_skip, ssm_norm_w, w_attn_branch, w_ssm_branch, w_out, norm2_w, w_ffn_in, w_ffn_out, rel_bias, final_norm_w, loss_target, m_norm1_w, m_w_in, m_conv_w, m_conv_b, m_dt_bias, m_a_log, m_d_skip, m_ssm_norm_w, m_w_attn_branch, m_w_ssm_branch, m_w_out, m_norm2_w, m_w_ffn_in, m_w_ffn_out, m_rel_bias, m_final_norm_w, v_norm1_w, v_w_in, v_conv_w, v_conv_b, v_dt_bias, v_a_log, v_d_skip, v_ssm_norm_w, v_w_attn_branch, v_w_ssm_branch, v_w_out, v_norm2_w, v_w_ffn_in, v_w_ffn_out, v_rel_bias, v_final_norm_w):
    env = dict(locals())
    wts = {k: env[k] for k in WEIGHTS}
    mom = {k: env["m_" + k] for k in WEIGHTS}
    var = {k: env["v_" + k] for k in WEIGHTS}
    chip = 2 * lax.axis_index("x") + lax.axis_index("y")
    core = lax.axis_index("c")

    shards = [wts[k].astype(BF16) for k in MATRICES] + [conv_w]
    gathered = _gather_shards(shards, name="gather_weights")
    full = {k: wts[k] for k in SMALL}
    for k, own, g in zip(MATRICES + ("conv_w",), shards, gathered):
        full[k] = _unshard(k, lax.dynamic_update_index_in_dim(g, own, chip, axis=0))

    loss, dx, grads = _local_step(x[0], loss_target[0], full)
    loss = lax.psum(loss, ("x", "y", "c"))

    core1 = core.reshape(1).astype(jnp.int32)
    from_pair = _pair_swap_layers([grads[k][0] for k in MATRICES], [grads[k][1] for k in MATRICES], name="reduce_pair_swap")
    scatter_in = [_to_shards(k, _add_own_layer(grads[k][0], grads[k][1], got, core1, name="reduce_pair_add_" + k))
                  for k, got in zip(MATRICES, from_pair)]
    scattered = _scatter_to_chips(scatter_in, name="reduce_scatter")
    own_layer = []
    for k, sent, got in zip(MATRICES, scatter_in, scattered):
        got = lax.dynamic_update_index_in_dim(got, lax.dynamic_index_in_dim(sent, chip, axis=0, keepdims=False), chip, axis=0)
        own_layer.append(_sum_chips(got, name="reduce_sum_" + k))
    other_layer = _pair_swap(own_layer, name="reduce_pair_exchange")
    reduced = {}

    small_names = SMALL + ("conv_w",)
    flat = jnp.concatenate([grads[k].reshape(-1) for k in small_names])
    n_small = flat.shape[0]
    rows = -(-n_small // SMALL_COLS)
    rows = -(-rows // 8) * 8
    flat = jnp.pad(flat, (0, rows * SMALL_COLS - n_small)).reshape(rows, SMALL_COLS)
    flat = _allreduce_small(flat, name="allreduce_small").reshape(-1)
    pos = 0
    for k in small_names:
        size = math.prod(grads[k].shape)
        reduced[k] = flat[pos:pos + size].reshape(grads[k].shape)
        pos += size
    cs = conv_w.shape[2]
    reduced["conv_w"] = lax.dynamic_slice_in_dim(reduced["conv_w"], chip * cs, cs, axis=2)

    delta, new_m, new_v = {}, {}, {}
    for k, own, other in zip(MATRICES, own_layer, other_layer):
        reduced[k], delta[k], new_m[k], new_v[k] = _adamw_layers(wts[k], own, other, mom[k], var[k], core1, name="adamw_" + k)
    pack = lambda src: jnp.pad(jnp.concatenate([src[k].reshape(-1) for k in small_names]),
                               (0, rows * SMALL_COLS - n_shard)).reshape(rows, SMALL_COLS)
    n_shard = sum(math.prod(wts[k].shape) for k in small_names)
    d_, m_, v_ = _adamw(pack(wts), pack(reduced), pack(mom), pack(var), name="adamw_small")
    pos = 0
    for k in small_names:
        size = math.prod(wts[k].shape)
        for dst, src in ((delta, d_), (new_m, m_), (new_v, v_)):
            dst[k] = src.reshape(-1)[pos:pos + size].reshape(wts[k].shape)
        pos += size

    return (loss, dx[None], *[reduced[k] for k in WEIGHTS], *[delta[k] for k in WEIGHTS],
            *[new_m[k] for k in WEIGHTS], *[new_v[k] for k in WEIGHTS])
```

```python
import functools
import math

import jax
import jax.numpy as jnp
from jax import lax
from jax.experimental import pallas as pl
from jax.experimental.pallas import tpu as pltpu

F32, BF16 = jnp.float32, jnp.bfloat16
SDS = jax.ShapeDtypeStruct
BS = pl.BlockSpec
MESH = pl.DeviceIdType.MESH
HIGHEST = lax.Precision.HIGHEST

EPS = 1e-6
HEAD_DIM = 64
ATTN_BLOCK = 128
DILATIONS = (1, 4, 16)
N_GROUPS_ATTN = len(DILATIONS)
N_STEPS = 128
N_REL_BUCKETS = 32
REL_MAX_DISTANCE = 2048
SSM_GROUPS = 4
D_STATE = 128
CHUNK = 128
CONV_WIDTH = 4
HALO = 16
LANES = 128
N_CHIPS = 4
N_DEV = 8
VMEM_LIMIT_BYTES = 48 * 1024 * 1024

ADAM_LR, ADAM_B1, ADAM_B2, ADAM_EPS, ADAM_WD, ADAM_STEP = 0.001, 0.9, 0.999, 1e-08, 0.01, 10

NT = (((1,), (1,)), ((), ()))
TN = (((0,), (0,)), ((), ()))
NN = (((1,), (0,)), ((), ()))


def _cp(*sem):
    return pltpu.CompilerParams(dimension_semantics=sem if sem else None, vmem_limit_bytes=VMEM_LIMIT_BYTES)


def _pick(n, cands):
    for c in cands:
        if n % c == 0:
            return c
    raise ValueError(f"no block size of {cands} divides {n}")


def _divisors(n, cap):
    out = [c for c in range(LANES, min(n, cap) + 1, LANES) if n % c == 0]
    return out or [n]


def _wide(n, cap=2048):
    return _divisors(n, cap)[-1]


MXU_FLOPS = 9.0e14
HBM_BYTES_PER_S = 3.0e12
ACC_BYTES_PER_S = 4.0e12
GRID_STEP_S = 0.4e-6
TILE_VMEM_BYTES = 36 * 1024 * 1024


def _matmul_tiles(m, n, k, a_bytes, b_bytes, o_bytes, has_res):
    best = None
    for tm in _divisors(m, 2048):
        for tn in _divisors(n, 2048):
            for tk in _divisors(k, 4096):
                ni, nj, nk = m // tm, n // tn, k // tk
                vmem = 2 * (tm * tk * a_bytes + tk * tn * b_bytes + tm * tn * (o_bytes + (4 if has_res else 0)))
                vmem += tm * tn * 4 * (2 if nk > 1 else 1) + (tm * tk + tk * tn) * 2
                if vmem > TILE_VMEM_BYTES:
                    continue
                hbm = m * k * a_bytes * (nj if nk > 1 else 1) + k * n * b_bytes * (ni if nj * nk > 1 else 1)
                hbm += m * n * (o_bytes + (4 if has_res else 0))
                t = max(2.0 * m * n * k / MXU_FLOPS, hbm / HBM_BYTES_PER_S) + ni * nj * nk * GRID_STEP_S
                if nk > 1:
                    t += m * n * 8.0 * nk / ACC_BYTES_PER_S
                if best is None or t < best[0]:
                    best = (t, tm, tn, tk)
    assert best is not None, (m, n, k)
    return best[1:]


def _dot(a, b, dims=NN, precision=None):
    return lax.dot_general(a, b, dims, precision=precision, preferred_element_type=F32)


def _silu(x):
    return x / (1.0 + jnp.exp(-x))


def _sigmoid(x):
    return 1.0 / (1.0 + jnp.exp(-x))


def _dsilu(x):
    s = _sigmoid(x)
    return s * (1.0 + x * (1.0 - s))


def _matmul(a, b, *, name, ta=False, tb=False, out_dtype=F32, res=None):
    (kdim, m) = a.shape if ta else a.shape[::-1]
    (n, k2) = b.shape if tb else b.shape[::-1]
    assert kdim == k2, (a.shape, b.shape, ta, tb)
    tm, tn, tk = _matmul_tiles(m, n, kdim, a.dtype.itemsize, b.dtype.itemsize, jnp.dtype(out_dtype).itemsize, res is not None)
    nk = kdim // tk
    a_spec = BS((tk, tm), lambda i, j, k: (k, i)) if ta else BS((tm, tk), lambda i, j, k: (i, k))
    b_spec = BS((tn, tk), lambda i, j, k: (j, k)) if tb else BS((tk, tn), lambda i, j, k: (k, j))
    dims = (((0 if ta else 1,), (1 if tb else 0,)), ((), ()))
    has_res = res is not None

    def body(*refs):
        a_ref, b_ref = refs[:2]
        r_ref = refs[2] if has_res else None
        o_ref = refs[3] if has_res else refs[2]
        prod = _dot(a_ref[...].astype(BF16), b_ref[...].astype(BF16), dims)
        if nk == 1:
            o_ref[...] = (prod + r_ref[...] if has_res else prod).astype(o_ref.dtype)
            return
        acc = refs[-1]
        k = pl.program_id(2)

        @pl.when(k == 0)
        def _():
            acc[...] = prod

        @pl.when(k > 0)
        def _():
            acc[...] += prod

        @pl.when(k == nk - 1)
        def _():
            r = acc[...]
            if has_res:
                r = r + r_ref[...]
            o_ref[...] = r.astype(o_ref.dtype)

    in_specs = [a_spec, b_spec]
    args = [a, b]
    if has_res:
        in_specs.append(BS((tm, tn), lambda i, j, k: (i, j)))
        args.append(res)
    return pl.pallas_call(
        body, name=name, grid=(m // tm, n // tn, nk), in_specs=in_specs,
        out_specs=BS((tm, tn), lambda i, j, k: (i, j)), out_shape=SDS((m, n), out_dtype),
        scratch_shapes=[pltpu.VMEM((tm, tn), F32)] if nk > 1 else [],
        compiler_params=_cp("parallel", "parallel", "arbitrary"),
    )(*args)


def _rms_fwd(h, w, *, name):
    t, d = h.shape
    tm = _pick(t, (512, 256, 128))

    def body(h_ref, w_ref, o_ref):
        x = h_ref[...]
        r = lax.rsqrt(jnp.mean(x * x, axis=-1, keepdims=True) + EPS)
        o_ref[...] = (x * r * w_ref[...]).astype(BF16)

    return pl.pallas_call(
        body, name=name, grid=(t // tm,), in_specs=[BS((tm, d), lambda i: (i, 0)), BS((1, d), lambda i: (0, 0))],
        out_specs=BS((tm, d), lambda i: (i, 0)), out_shape=SDS((t, d), BF16), compiler_params=_cp("parallel"),
    )(h, w.reshape(1, d))


def _rms_bwd(h, w, dys, dres, *, name):
    t, d = h.shape
    tm = _pick(t, (512, 256, 128))
    n_dy = len(dys)

    def body(*refs):
        h_ref, w_ref = refs[:2]
        dy_refs = refs[2:2 + n_dy]
        dres_ref, dh_ref, dw_ref = refs[2 + n_dy:]
        x = h_ref[...]
        dy = dy_refs[0][...]
        for r_ in dy_refs[1:]:
            dy = dy + r_[...]
        r = lax.rsqrt(jnp.mean(x * x, axis=-1, keepdims=True) + EPS)
        g = dy * w_ref[...]
        proj = jnp.sum(g * x, axis=-1, keepdims=True) * (1.0 / d)
        dh_ref[...] = dres_ref[...] + r * g - x * (r * r * r) * proj

        @pl.when(pl.program_id(0) == 0)
        def _():
            dw_ref[...] = jnp.zeros_like(dw_ref)

        dw_ref[...] += jnp.sum(dy * x * r, axis=0, keepdims=True)

    row = BS((tm, d), lambda i: (i, 0))
    vec = BS((1, d), lambda i: (0, 0))
    dh, dw = pl.pallas_call(
        body, name=name, grid=(t // tm,), in_specs=[row, vec] + [row] * n_dy + [row],
        out_specs=[row, vec], out_shape=[SDS((t, d), F32), SDS((1, d), F32)], compiler_params=_cp("arbitrary"),
    )(h, w.reshape(1, d), *dys, dres)
    return dh, dw[0]


def _loss_head(h, w, tgt, *, name):
    t, d = h.shape
    tm = _pick(t, (512, 256, 128))

    def body(h_ref, w_ref, t_ref, loss_ref, dh_ref, dw_ref):
        x = h_ref[...]
        r = lax.rsqrt(jnp.mean(x * x, axis=-1, keepdims=True) + EPS)
        err = x * r * w_ref[...] - t_ref[...]
        loss_ref[...] = jnp.zeros(loss_ref.shape, F32) + 0.5 * jnp.sum(err * err) * (1.0 / d)
        dy = err * (1.0 / d)
        g = dy * w_ref[...]
        proj = jnp.sum(g * x, axis=-1, keepdims=True) * (1.0 / d)
        dh_ref[...] = r * g - x * (r * r * r) * proj

        @pl.when(pl.program_id(0) == 0)
        def _():
            dw_ref[...] = jnp.zeros_like(dw_ref)

        dw_ref[...] += jnp.sum(dy * x * r, axis=0, keepdims=True)

    row = BS((tm, d), lambda i: (i, 0))
    vec = BS((1, d), lambda i: (0, 0))
    loss, dh, dw = pl.pallas_call(
        body, name=name, grid=(t // tm,), in_specs=[row, vec, row],
        out_specs=[BS((1, 8, LANES), lambda i: (i, 0, 0)), row, vec],
        out_shape=[SDS((t // tm, 8, LANES), F32), SDS((t, d), F32), SDS((1, d), F32)], compiler_params=_cp("arbitrary"),
    )(h, w.reshape(1, d), tgt)
    return jnp.sum(loss[:, 0, 0]), dh, dw[0]


def _attn_masks(mb):
    qi = lax.broadcasted_iota(jnp.int32, (2 * ATTN_BLOCK, 2 * ATTN_BLOCK), 0) & (ATTN_BLOCK - 1)
    kj = lax.broadcasted_iota(jnp.int32, (2 * ATTN_BLOCK, 2 * ATTN_BLOCK), 1)
    steps = qi + ATTN_BLOCK - kj
    valid = (steps >= 0) & (steps <= N_STEPS) & ((kj >= ATTN_BLOCK) | (mb > 0))
    low = lax.broadcasted_iota(jnp.int32, (ATTN_BLOCK, LANES), 1) < HEAD_DIM
    return valid, low


def _stack_heads(x2, low):
    zero = jnp.zeros_like(x2)
    return jnp.concatenate([jnp.where(low, x2, zero), jnp.where(low, zero, x2)], axis=0)


def _head_cols(x2, low):
    return jnp.concatenate([jnp.max(jnp.where(low, x2, -jnp.inf), axis=-1, keepdims=True),
                            jnp.max(jnp.where(low, -jnp.inf, x2), axis=-1, keepdims=True)], axis=0)


def _to_residue_major(a, d):
    t, c = a.shape
    return a if d == 1 else a.reshape(t // d, d, c).transpose(1, 0, 2).reshape(t, c)


def _to_token_major(a, d):
    t, c = a.shape
    return a if d == 1 else a.reshape(d, t // d, c).transpose(1, 0, 2).reshape(t, c)


def _attn_specs(cols, nb, sub, awg, clamp):
    nsb = nb // sub

    def cur(col):
        return BS((sub * ATTN_BLOCK, awg), lambda r, sb: (r * nsb + clamp(sb), col))

    def prev(col):
        return BS((ATTN_BLOCK, awg), lambda r, sb: (r * nb + jnp.maximum(clamp(sb) * sub - 1, 0), col))

    return [cur(cols[0]), cur(cols[1]), prev(cols[1]), cur(cols[2]), prev(cols[2])]


def _sub_rows(s, n=1):
    return pl.ds(pl.multiple_of(s * ATTN_BLOCK, ATTN_BLOCK), n * ATTN_BLOCK)


def _attn_fwd(qkv, cols, bias, d, *, name):
    t = qkv.shape[0]
    hg = bias.shape[0]
    awg = hg * HEAD_DIM
    nb = t // d // ATTN_BLOCK
    sub = _pick(nb, (4, 2, 1))
    rows = sub * ATTN_BLOCK
    scale = HEAD_DIM ** -0.5

    def body(q_ref, kc_ref, kp_ref, vc_ref, vp_ref, b_ref, o_ref, l_ref, kext, vext):
        sb = pl.program_id(1)
        kext[pl.ds(0, ATTN_BLOCK), :] = kp_ref[...]
        kext[pl.ds(ATTN_BLOCK, rows), :] = kc_ref[...]
        vext[pl.ds(0, ATTN_BLOCK), :] = vp_ref[...]
        vext[pl.ds(ATTN_BLOCK, rows), :] = vc_ref[...]

        def one_block(s, carry):
            valid, low = _attn_masks(sb * sub + s)
            for pi in range(awg // LANES):
                sl = slice(pi * LANES, (pi + 1) * LANES)
                k2 = kext[_sub_rows(s, 2), sl]
                v2 = vext[_sub_rows(s, 2), sl]
                qs = _stack_heads(q_ref[_sub_rows(s), sl], low)
                bias2 = b_ref[pi * 2:pi * 2 + 2].reshape(2 * ATTN_BLOCK, 2 * ATTN_BLOCK)
                sc = jnp.where(valid, _dot(qs, k2, NT) * scale + bias2, -jnp.inf)
                m = jnp.max(sc, axis=-1, keepdims=True)
                p = jnp.exp(sc - m)
                den = jnp.sum(p, axis=-1, keepdims=True)
                o = _dot(p.astype(BF16), v2) / den
                lse = jnp.broadcast_to(m + jnp.log(den), (2 * ATTN_BLOCK, LANES))
                o_ref[_sub_rows(s), sl] = jnp.where(low, o[:ATTN_BLOCK], o[ATTN_BLOCK:])
                l_ref[_sub_rows(s), sl] = jnp.where(low, lse[:ATTN_BLOCK], lse[ATTN_BLOCK:])
            return carry

        lax.fori_loop(0, sub, one_block, 0)

    out_spec = BS((rows, awg), lambda r, sb: (r * (nb // sub) + sb, 0))
    return pl.pallas_call(
        body, name=name, grid=(d, nb // sub),
        in_specs=_attn_specs(cols, nb, sub, awg, lambda sb: sb) + [BS(bias.shape, lambda r, sb: (0, 0, 0))],
        out_specs=[out_spec, out_spec], out_shape=[SDS((t, awg), F32)] * 2,
        scratch_shapes=[pltpu.VMEM((rows + ATTN_BLOCK, awg), BF16)] * 2, compiler_params=_cp("parallel", "parallel"),
    )(*([qkv] * 5), bias)


def _attn_bwd(qkv, cols, bias, lse, do, corr, d, *, name):
    t = qkv.shape[0]
    hg = bias.shape[0]
    awg = hg * HEAD_DIM
    nb = t // d // ATTN_BLOCK
    sub = _pick(nb, (4, 2, 1))
    nsb = nb // sub
    rows = sub * ATTN_BLOCK
    scale = HEAD_DIM ** -0.5

    def body(q_ref, kc_ref, kp_ref, vc_ref, vp_ref, b_ref, l_ref, do_ref, c_ref, dq_ref, dk_ref, dv_ref, db_ref,
             kext, vext, dkext, dvext, ck, cv):
        r, sb = pl.program_id(0), pl.program_id(1)

        @pl.when((r == 0) & (sb == 0))
        def _():
            db_ref[...] = jnp.zeros_like(db_ref)

        @pl.when(sb == 0)
        def _():
            ck[...] = jnp.zeros_like(ck)
            cv[...] = jnp.zeros_like(cv)

        @pl.when(sb < nsb)
        def _():
            kext[pl.ds(0, ATTN_BLOCK), :] = kp_ref[...]
            kext[pl.ds(ATTN_BLOCK, rows), :] = kc_ref[...]
            vext[pl.ds(0, ATTN_BLOCK), :] = vp_ref[...]
            vext[pl.ds(ATTN_BLOCK, rows), :] = vc_ref[...]
            dkext[...] = jnp.zeros_like(dkext)
            dvext[...] = jnp.zeros_like(dvext)

            def one_block(s, carry):
                valid, low = _attn_masks(sb * sub + s)
                for pi in range(awg // LANES):
                    sl = slice(pi * LANES, (pi + 1) * LANES)
                    k2 = kext[_sub_rows(s, 2), sl]
                    v2 = vext[_sub_rows(s, 2), sl]
                    qs = _stack_heads(q_ref[_sub_rows(s), sl], low)
                    dos = _stack_heads(do_ref[_sub_rows(s), sl], low)
                    lse_c = _head_cols(l_ref[_sub_rows(s), sl], low)
                    corr_c = _head_cols(c_ref[_sub_rows(s), sl], low)
                    bias2 = b_ref[pi * 2:pi * 2 + 2].reshape(2 * ATTN_BLOCK, 2 * ATTN_BLOCK)
                    sc = _dot(qs, k2, NT) * scale + bias2
                    p = jnp.exp(jnp.where(valid, sc, -jnp.inf) - lse_c)
                    ds = p * (_dot(dos, v2, NT) + corr_c)
                    db_ref[pi * 2:pi * 2 + 2] += ds.reshape(2, ATTN_BLOCK, 2 * ATTN_BLOCK)
                    dsb = ds.astype(BF16)
                    dq = _dot(dsb, k2) * scale
                    dq_ref[_sub_rows(s), sl] = jnp.where(low, dq[:ATTN_BLOCK], dq[ATTN_BLOCK:]).astype(BF16)
                    dkext[_sub_rows(s, 2), sl] += _dot(dsb, qs, TN) * scale
                    dvext[_sub_rows(s, 2), sl] += _dot(p.astype(BF16), dos, TN)
                return carry

            lax.fori_loop(0, sub, one_block, 0)
            head, tail = pl.ds(0, rows - ATTN_BLOCK), pl.ds(rows - ATTN_BLOCK, ATTN_BLOCK)
            for out_ref, carry_ref, ext in ((dk_ref, ck, dkext), (dv_ref, cv, dvext)):
                if sub > 1:
                    out_ref[head, :] = carry_ref[head, :].astype(BF16)
                out_ref[tail, :] = (carry_ref[tail, :] + ext[pl.ds(0, ATTN_BLOCK), :]).astype(BF16)
                carry_ref[...] = ext[pl.ds(ATTN_BLOCK, rows), :]

        @pl.when(sb == nsb)
        def _():
            dk_ref[...] = ck[...].astype(BF16)
            dv_ref[...] = cv[...].astype(BF16)

    clamp = lambda sb: jnp.minimum(sb, nsb - 1)
    cur = BS((rows, awg), lambda r, sb: (r * nsb + clamp(sb), 0))
    prev = BS((rows, awg), lambda r, sb: (r * nsb + jnp.maximum(sb - 1, 0), 0))
    bias_spec = BS(bias.shape, lambda r, sb: (0, 0, 0))
    return pl.pallas_call(
        body, name=name, grid=(d, nsb + 1),
        in_specs=_attn_specs(cols, nb, sub, awg, clamp) + [bias_spec, cur, cur, cur],
        out_specs=[cur, prev, prev, bias_spec],
        out_shape=[SDS((t, awg), BF16)] * 3 + [SDS(bias.shape, F32)],
        scratch_shapes=[pltpu.VMEM((rows + ATTN_BLOCK, awg), BF16)] * 2 + [pltpu.VMEM((rows + ATTN_BLOCK, awg), F32)] * 2
        + [pltpu.VMEM((rows, awg), F32)] * 2,
        compiler_params=_cp("arbitrary", "arbitrary"),
    )(*([qkv] * 5), bias, lse, do, corr)


def _head_sum(x, low):
    a = jnp.sum(jnp.where(low, x, 0.0), axis=-1, keepdims=True)
    b = jnp.sum(jnp.where(low, 0.0, x), axis=-1, keepdims=True)
    return jnp.where(low, a, b)


def _combine_weights(lses):
    mx = jnp.maximum(jnp.maximum(lses[0], lses[1]), lses[2])
    es = [jnp.exp(l - mx) for l in lses]
    tot = es[0] + es[1] + es[2]
    return [e / tot for e in es]


def _combine_fwd(os_, lses, *, name):
    t, awg = os_[0].shape
    tm = _pick(t, (512, 256, 128))

    def body(o0, o1, o2, l0, l1, l2, out_ref):
        al = _combine_weights([l0[...], l1[...], l2[...]])
        out_ref[...] = (al[0] * o0[...] + al[1] * o1[...] + al[2] * o2[...]).astype(BF16)

    blk = BS((tm, awg), lambda i: (i, 0))
    return pl.pallas_call(
        body, name=name, grid=(t // tm,), in_specs=[blk] * 6, out_specs=blk,
        out_shape=SDS((t, awg), BF16), compiler_params=_cp("parallel"),
    )(*os_, *lses)


def _combine_bwd(dattn, os_, lses, *, name):
    t, awg = dattn.shape
    tm = _pick(t, (512, 256, 128))

    def body(da_ref, o0, o1, o2, l0, l1, l2, d0, d1, d2, c0, c1, c2):
        low = lax.broadcasted_iota(jnp.int32, (tm, LANES), 1) < HEAD_DIM
        for pi in range(awg // LANES):
            sl = slice(pi * LANES, (pi + 1) * LANES)
            da = da_ref[:, sl]
            al = _combine_weights([l0[:, sl], l1[:, sl], l2[:, sl]])
            tot = jnp.zeros((tm, LANES), F32)
            for a, o in zip(al, (o0, o1, o2)):
                tot = tot + a * _head_sum(da * o[:, sl], low)
            for a, d_ref, c_ref in zip(al, (d0, d1, d2), (c0, c1, c2)):
                d_ref[:, sl] = (a * da).astype(BF16)
                c_ref[:, sl] = -a * tot

    blk = BS((tm, awg), lambda i: (i, 0))
    outs = pl.pallas_call(
        body, name=name, grid=(t // tm,), in_specs=[blk] * 7, out_specs=[blk] * 6,
        out_shape=[SDS((t, awg), BF16)] * 3 + [SDS((t, awg), F32)] * 3, compiler_params=_cp("parallel"),
    )(dattn, *os_, *lses)
    return outs[:3], outs[3:]


def _conv_block(width, *offsets):
    for c in (512, 256, 128):
        if width % c == 0 and all(o % c == 0 for o in offsets):
            return c
    raise ValueError((width, offsets))


CONV_ROWS = 32


def _conv_pre(x_ref, halo_ref, w_ref, b_ref, ext, i, tm):
    ext[pl.ds(0, HALO), :] = jnp.where(i > 0, halo_ref[...].astype(F32), 0.0)
    ext[pl.ds(HALO, tm), :] = x_ref[...].astype(F32)
    taps = [w_ref[pl.ds(k, 1), :] for k in range(CONV_WIDTH)]
    bias = b_ref[...]
    for r0 in range(0, tm, CONV_ROWS):
        xs = [ext[pl.ds(HALO + r0 - (CONV_WIDTH - 1) + k, CONV_ROWS), :] for k in range(CONV_WIDTH)]
        pre = bias + taps[0] * xs[0]
        for k in range(1, CONV_WIDTH):
            pre = pre + taps[k] * xs[k]
        yield r0, pre, xs


def _fold8(v):
    return jnp.sum(v.reshape(v.shape[0] // 8, 8, v.shape[1]), axis=0)


def _conv_fwd(proj, off, w, b, *, name):
    t = proj.shape[0]
    c = w.shape[1]
    cw = _conv_block(c, off)
    tm = _pick(t, (512, 256, 128))
    ob = off // cw

    def body(x_ref, halo_ref, w_ref, b_ref, o_ref, ext):
        for r0, pre, _ in _conv_pre(x_ref, halo_ref, w_ref, b_ref, ext, pl.program_id(1), tm):
            o_ref[pl.ds(r0, CONV_ROWS), :] = _silu(pre).astype(BF16)

    return pl.pallas_call(
        body, name=name, grid=(c // cw, t // tm),
        in_specs=[BS((tm, cw), lambda j, i: (i, ob + j)),
                  BS((HALO, cw), lambda j, i: (jnp.maximum(i * (tm // HALO) - 1, 0), ob + j)),
                  BS((CONV_WIDTH, cw), lambda j, i: (0, j)), BS((1, cw), lambda j, i: (0, j))],
        out_specs=BS((tm, cw), lambda j, i: (i, j)), out_shape=SDS((t, c), BF16),
        scratch_shapes=[pltpu.VMEM((HALO + tm, cw), F32)], compiler_params=_cp("parallel", "arbitrary"),
    )(proj, proj, w, b.reshape(1, c))


def _conv_bwd(proj, off, w, b, dxc, *, name):
    t = proj.shape[0]
    c = w.shape[1]
    cw = _conv_block(c, off)
    tm = _pick(t, (512, 256, 128))
    ob = off // cw
    nt = t // tm

    def body_pre(x_ref, halo_ref, w_ref, b_ref, d_ref, dp_ref, dw_ref, db_ref, ext):
        i = pl.program_id(1)

        @pl.when(i == 0)
        def _():
            dw_ref[...] = jnp.zeros_like(dw_ref)
            db_ref[...] = jnp.zeros_like(db_ref)

        db_acc = jnp.zeros((8, cw), F32)
        dw_acc = [jnp.zeros((8, cw), F32) for _ in range(CONV_WIDTH)]
        for r0, pre, xs in _conv_pre(x_ref, halo_ref, w_ref, b_ref, ext, i, tm):
            dpre = d_ref[pl.ds(r0, CONV_ROWS), :] * _dsilu(pre)
            dp_ref[pl.ds(r0, CONV_ROWS), :] = dpre
            db_acc = db_acc + _fold8(dpre)
            dw_acc = [acc + _fold8(dpre * x) for acc, x in zip(dw_acc, xs)]
        db_ref[...] += jnp.sum(db_acc, axis=0, keepdims=True)
        for k in range(CONV_WIDTH):
            dw_ref[pl.ds(k, 1), :] += jnp.sum(dw_acc[k], axis=0, keepdims=True)

    dpre, dw, db = pl.pallas_call(
        body_pre, name=name + "_pre", grid=(c // cw, nt),
        in_specs=[BS((tm, cw), lambda j, i: (i, ob + j)),
                  BS((HALO, cw), lambda j, i: (jnp.maximum(i * (tm // HALO) - 1, 0), ob + j)),
                  BS((CONV_WIDTH, cw), lambda j, i: (0, j)), BS((1, cw), lambda j, i: (0, j)),
                  BS((tm, cw), lambda j, i: (i, j))],
        out_specs=[BS((tm, cw), lambda j, i: (i, j)), BS((CONV_WIDTH, cw), lambda j, i: (0, j)), BS((1, cw), lambda j, i: (0, j))],
        out_shape=[SDS((t, c), F32), SDS((CONV_WIDTH, c), F32), SDS((1, c), F32)],
        scratch_shapes=[pltpu.VMEM((HALO + tm, cw), F32)], compiler_params=_cp("parallel", "arbitrary"),
    )(proj, proj, w, b.reshape(1, c), dxc)

    def body_in(dp_ref, nxt_ref, w_ref, dx_ref, ext):
        i = pl.program_id(1)
        ext[pl.ds(0, tm), :] = dp_ref[...]
        ext[pl.ds(tm, 8), :] = jnp.where(i < nt - 1, nxt_ref[...], 0.0)
        taps = [w_ref[pl.ds(k, 1), :] for k in range(CONV_WIDTH)]
        for r0 in range(0, tm, CONV_ROWS):
            dx = taps[CONV_WIDTH - 1] * ext[pl.ds(r0, CONV_ROWS), :]
            for k in range(CONV_WIDTH - 1):
                dx = dx + taps[k] * ext[pl.ds(r0 + CONV_WIDTH - 1 - k, CONV_ROWS), :]
            dx_ref[pl.ds(r0, CONV_ROWS), :] = dx.astype(BF16)

    dx = pl.pallas_call(
        body_in, name=name + "_in", grid=(c // cw, nt),
        in_specs=[BS((tm, cw), lambda j, i: (i, j)),
                  BS((8, cw), lambda j, i: (jnp.minimum((i + 1) * (tm // 8), t // 8 - 1), j)),
                  BS((CONV_WIDTH, cw), lambda j, i: (0, j))],
        out_specs=BS((tm, cw), lambda j, i: (i, j)), out_shape=SDS((t, c), BF16),
        scratch_shapes=[pltpu.VMEM((tm + 8, cw), F32)], compiler_params=_cp("parallel", "arbitrary"),
    )(dpre, dpre, w)
    return dx, dw, db[0]


def _softplus(x):
    return jnp.maximum(x, 0.0) + jnp.log(1.0 + jnp.exp(-jnp.abs(x)))


def _tril():
    return lax.broadcasted_iota(jnp.int32, (CHUNK, CHUNK), 0) >= lax.broadcasted_iota(jnp.int32, (CHUNK, CHUNK), 1)


def _ssd_prep(dtr, dt_bias, a, *, name):
    t = dtr.shape[0]
    nc = t // CHUNK
    cb = _pick(nc, (4, 2, 1))

    def body(dtr_ref, bias_ref, a_ref, o_ref):
        tril = _tril().astype(F32)
        for ci in range(cb):
            rows = pl.ds(ci * CHUNK, CHUNK)
            pre = dtr_ref[rows, :] + bias_ref[...]
            dt = _softplus(pre)
            la = _dot(tril, dt * a_ref[...], precision=HIGHEST)
            for k, v in enumerate((dt, la, la.T, _sigmoid(pre))):
                o_ref[rows, k * LANES:(k + 1) * LANES] = v

    vec = BS((1, LANES), lambda i: (0, 0))
    return pl.pallas_call(
        body, name=name, grid=(nc // cb,), in_specs=[BS((cb * CHUNK, LANES), lambda i: (i, 0)), vec, vec],
        out_specs=BS((cb * CHUNK, 4 * LANES), lambda i: (i, 0)), out_shape=SDS((t, 4 * LANES), F32),
        compiler_params=_cp("parallel"),
    )(dtr, dt_bias, a)


def _ssd_common(time_ref):
    part = lambda k: time_ref[:, k * LANES:(k + 1) * LANES]
    return part(3), part(0), part(1), part(2), _tril()


def _lane_col(x, lane, h):
    return jnp.sum(jnp.where(lane == h, x, 0.0), axis=-1, keepdims=True)


def _ssd_specs(rows, di, gw, cidx):
    nbx = di // LANES
    return [BS((rows, gw), lambda g, c: (cidx(c), g)),
            BS((rows, D_STATE), lambda g, c: (cidx(c), nbx + g)),
            BS((rows, D_STATE), lambda g, c: (cidx(c), nbx + SSM_GROUPS + g)),
            BS((rows, 4 * LANES), lambda g, c: (cidx(c), 0)),
            BS((1, LANES), lambda g, c: (0, 0)),
            BS((1, gw), lambda g, c: (0, g))]


def _chunk_rows(ci):
    return pl.ds(pl.multiple_of(ci * CHUNK, CHUNK), CHUNK)


def _ssd_fwd(xc, prep, a, dskip, *, name):
    t = xc.shape[0]
    di = xc.shape[1] - 2 * SSM_GROUPS * D_STATE
    gw = di // SSM_GROUPS
    hpg = gw // HEAD_DIM
    npair = gw // LANES
    nc = t // CHUNK
    cb = _pick(nc, (4, 2, 1))

    def body(xb_ref, bb_ref, cb_ref, timeb_ref, a_ref, dsk_ref, yb_ref, st_ref, state):
        @pl.when(pl.program_id(1) == 0)
        def _():
            state[...] = jnp.zeros_like(state)

        g = pl.program_id(0)

        def one_chunk(ci, carry):
            rows = _chunk_rows(ci)
            ssd_chunk(xb_ref.at[rows], bb_ref.at[rows], cb_ref.at[rows], timeb_ref.at[rows], dsk_ref,
                      yb_ref.at[rows], st_ref, state, ci, g)
            return carry

        lax.fori_loop(0, cb, one_chunk, 0)

    def ssd_chunk(x_ref, b_ref, c_ref, time_ref, dsk_ref, y_ref, st_ref, state, ci, g):
        st_ref[ci, 0] = state[...]
        _, dt, la, la_t, tril = _ssd_common(time_ref)
        lane = lax.broadcasted_iota(jnp.int32, (CHUNK, LANES), 1)
        sub = lax.broadcasted_iota(jnp.int32, (LANES, CHUNK), 0)
        lane1 = lax.broadcasted_iota(jnp.int32, (1, LANES), 1)
        low, low1 = lane < HEAD_DIM, lane1 < HEAD_DIM
        last = lax.broadcasted_iota(jnp.int32, (CHUNK, LANES), 0) == CHUNK - 1
        lend = jnp.sum(jnp.where(last, la, 0.0), axis=0, keepdims=True)
        bm, cm = b_ref[...], c_ref[...]
        gmat = _dot(cm, bm, NT)
        for p in range(npair):
            sl = slice(p * LANES, (p + 1) * LANES)
            ps = slice(p * D_STATE, (p + 1) * D_STATE)
            x2 = x_ref[:, sl].astype(F32)
            cols, ms = [], []
            for hh in range(2):
                h = g * hpg + p * 2 + hh
                col_la = _lane_col(la, lane, h)
                row_la = jnp.sum(jnp.where(sub == h, la_t, 0.0), axis=0, keepdims=True)
                lend_h = _lane_col(lend, lane1, h)
                decay = jnp.exp(jnp.where(tril, col_la - row_la, -jnp.inf))
                ms.append((gmat * decay).astype(BF16))
                cols.append((_lane_col(dt, lane, h), jnp.exp(col_la), jnp.exp(lend_h - col_la), jnp.exp(lend_h)))
            pair = lambda k: jnp.where(low, cols[0][k], cols[1][k])
            xdt = x2 * pair(0)
            xdtb = xdt.astype(BF16)
            s2 = state[ps, :]
            yy = _dot(jnp.concatenate(ms, axis=0), xdtb)
            y = jnp.where(low, yy[:CHUNK], yy[CHUNK:])
            y = y + pair(1) * _dot(cm, s2.astype(BF16)) + x2 * dsk_ref[:, sl]
            y_ref[:, sl] = y
            state[ps, :] = s2 * jnp.where(low1, cols[0][3], cols[1][3]) + _dot(bm, (xdt * pair(2)).astype(BF16), TN)

    y, st = pl.pallas_call(
        body, name=name, grid=(SSM_GROUPS, nc // cb), in_specs=_ssd_specs(cb * CHUNK, di, gw, lambda c: c),
        out_specs=[BS((cb * CHUNK, gw), lambda g, c: (c, g)), BS((cb, 1, npair * D_STATE, LANES), lambda g, c: (c, g, 0, 0))],
        out_shape=[SDS((t, di), F32), SDS((nc, SSM_GROUPS, npair * D_STATE, LANES), F32)],
        scratch_shapes=[pltpu.VMEM((npair * D_STATE, LANES), F32)], compiler_params=_cp("parallel", "arbitrary"),
    )(xc, xc, xc, prep, a, dskip)
    return y, st


def _ssd_bwd(xc, prep, a, dskip, st, dy, *, name):
    t = xc.shape[0]
    di = xc.shape[1] - 2 * SSM_GROUPS * D_STATE
    gw = di // SSM_GROUPS
    hpg = gw // HEAD_DIM
    npair = gw // LANES
    nc = t // CHUNK
    cb = _pick(nc, (4, 2, 1))
    rev = lambda c: nc // cb - 1 - c

    def body(xb_ref, bb_ref, cb_ref, timeb_ref, a_ref, dsk_ref, st_ref, dyb_ref,
             dxb_ref, dbb_ref, dcb_ref, ddtrb_ref, da_ref, dbias_ref, ddsk_ref, dstate):
        @pl.when(pl.program_id(1) == 0)
        def _():
            dstate[...] = jnp.zeros_like(dstate)
            da_ref[...] = jnp.zeros_like(da_ref)
            dbias_ref[...] = jnp.zeros_like(dbias_ref)
            ddsk_ref[...] = jnp.zeros_like(ddsk_ref)

        g = pl.program_id(0)

        def one_chunk(j, carry):
            ci = cb - 1 - j
            rows = _chunk_rows(ci)
            ssd_chunk(xb_ref.at[rows], bb_ref.at[rows], cb_ref.at[rows], timeb_ref.at[rows], a_ref, dsk_ref,
                      st_ref.at[ci], dyb_ref.at[rows], dxb_ref.at[rows], dbb_ref.at[rows], dcb_ref.at[rows],
                      ddtrb_ref.at[:, rows], da_ref, dbias_ref, ddsk_ref, dstate, g)
            return carry

        lax.fori_loop(0, cb, one_chunk, 0)

    def ssd_chunk(x_ref, b_ref, c_ref, time_ref, a_ref, dsk_ref, st_ref, dy_ref,
                  dx_ref, db_ref, dc_ref, ddtr_ref, da_ref, dbias_ref, ddsk_ref, dstate, g):
        sig, dt, la, la_t, tril = _ssd_common(time_ref)
        lane = lax.broadcasted_iota(jnp.int32, (CHUNK, LANES), 1)
        sub = lax.broadcasted_iota(jnp.int32, (LANES, CHUNK), 0)
        lane1 = lax.broadcasted_iota(jnp.int32, (1, LANES), 1)
        low, low1 = lane < HEAD_DIM, lane1 < HEAD_DIM
        last = lax.broadcasted_iota(jnp.int32, (CHUNK, LANES), 0) == CHUNK - 1
        lend = jnp.sum(jnp.where(last, la, 0.0), axis=0, keepdims=True)
        bm, cm = b_ref[...], c_ref[...]
        gmat = _dot(cm, bm, NT)
        dg = jnp.zeros((CHUNK, CHUNK), F32)
        dla_cols = jnp.zeros((CHUNK, LANES), F32)
        dla_rows = jnp.zeros((LANES, CHUNK), F32)
        dtsum = jnp.zeros((CHUNK, LANES), F32)
        dbm = jnp.zeros((CHUNK, D_STATE), F32)
        dcm = jnp.zeros((CHUNK, D_STATE), F32)
        for p in range(npair):
            sl = slice(p * LANES, (p + 1) * LANES)
            ps = slice(p * D_STATE, (p + 1) * D_STATE)
            x2 = x_ref[:, sl].astype(F32)
            dy2 = dy_ref[:, sl]
            s2 = st_ref[0, ps, :]
            ds2 = dstate[ps, :]
            hs, cols, ms, decays = [], [], [], []
            for hh in range(2):
                h = g * hpg + p * 2 + hh
                col_la = _lane_col(la, lane, h)
                row_la = jnp.sum(jnp.where(sub == h, la_t, 0.0), axis=0, keepdims=True)
                lend_h = _lane_col(lend, lane1, h)
                decay = jnp.exp(jnp.where(tril, col_la - row_la, -jnp.inf))
                hs.append(h)
                decays.append(decay)
                ms.append(gmat * decay)
                cols.append((_lane_col(dt, lane, h), jnp.exp(col_la), jnp.exp(lend_h - col_la), jnp.exp(lend_h)))
            pair = lambda k: jnp.where(low, cols[0][k], cols[1][k])
            dtc, ec, eend = pair(0), pair(1), pair(2)
            eend_s = jnp.where(low1, cols[0][3], cols[1][3])
            xdt = x2 * dtc
            xdtb = xdt.astype(BF16)
            dys = dy2 * ec
            dysb = dys.astype(BF16)
            dxdt_state = eend * _dot(bm, ds2.astype(BF16))
            inter = dys * _dot(cm, s2.astype(BF16))
            u = dxdt_state * xdt
            sds = s2 * ds2
            dys2 = _stack_heads(dy2.astype(BF16), low)
            dxdt = dxdt_state + _dot(jnp.concatenate([ms[0].astype(BF16), ms[1].astype(BF16)], axis=0), dys2, TN)
            dms = _dot(dys2, xdtb, NT)
            for hh in range(2):
                h = hs[hh]
                mh = low if hh == 0 else jnp.logical_not(low)
                dm = dms[hh * CHUNK:(hh + 1) * CHUNK]
                w = dm * ms[hh]
                dg = dg + dm * decays[hh]
                u_col = jnp.sum(jnp.where(mh, u, 0.0), axis=-1, keepdims=True)
                dlend = jnp.sum(u_col, axis=0, keepdims=True) + cols[hh][3] * jnp.sum(jnp.where(low1 if hh == 0 else jnp.logical_not(low1), jnp.sum(sds, axis=0, keepdims=True), 0.0), axis=-1, keepdims=True)
                col = jnp.sum(w, axis=-1, keepdims=True) + jnp.sum(jnp.where(mh, inter, 0.0), axis=-1, keepdims=True) - u_col
                dla_cols = dla_cols + jnp.where(lane == h, col + jnp.where(last, dlend, 0.0), 0.0)
                dla_rows = dla_rows - jnp.where(sub == h, jnp.sum(w, axis=0, keepdims=True), 0.0)
            for hh in range(2):
                mh = low if hh == 0 else jnp.logical_not(low)
                dtsum = dtsum + jnp.where(lane == hs[hh], jnp.sum(jnp.where(mh, dxdt * x2, 0.0), axis=-1, keepdims=True), 0.0)
            dcm = dcm + _dot(dysb, s2.astype(BF16), NT)
            dbm = dbm + _dot((xdt * eend).astype(BF16), ds2.astype(BF16), NT)
            dstate[ps, :] = ds2 * eend_s + _dot(cm, dysb, TN)
            dx_ref[:, sl] = dxdt * dtc + dy2 * dsk_ref[:, sl]
            ddsk_ref[:, sl] += jnp.sum(dy2 * x2, axis=0, keepdims=True)
        dgb = dg.astype(BF16)
        dc_ref[...] = dcm + _dot(dgb, bm)
        db_ref[...] = dbm + _dot(dgb, cm, TN)
        dla = dla_cols + dla_rows.T
        triu = lax.broadcasted_iota(jnp.int32, (CHUNK, CHUNK), 0) <= lax.broadcasted_iota(jnp.int32, (CHUNK, CHUNK), 1)
        ddta = _dot(triu.astype(F32), dla, precision=HIGHEST)
        ddt = ddta * a_ref[...] + dtsum
        da_ref[0] += jnp.sum(ddta * dt, axis=0, keepdims=True)
        ddtr = ddt * sig
        ddtr_ref[0] = ddtr
        dbias_ref[0] += jnp.sum(ddtr, axis=0, keepdims=True)

    vec = BS((1, 1, LANES), lambda g, c: (g, 0, 0))
    outs = pl.pallas_call(
        body, name=name, grid=(SSM_GROUPS, nc // cb),
        in_specs=_ssd_specs(cb * CHUNK, di, gw, rev) + [BS((cb, 1, npair * D_STATE, LANES), lambda g, c: (rev(c), g, 0, 0)),
                                                        BS((cb * CHUNK, gw), lambda g, c: (rev(c), g))],
        out_specs=[BS((cb * CHUNK, gw), lambda g, c: (rev(c), g)), BS((cb * CHUNK, D_STATE), lambda g, c: (rev(c), g)),
                   BS((cb * CHUNK, D_STATE), lambda g, c: (rev(c), g)), BS((1, cb * CHUNK, LANES), lambda g, c: (g, rev(c), 0)),
                   vec, vec, BS((1, gw), lambda g, c: (0, g))],
        out_shape=[SDS((t, di), F32), SDS((t, SSM_GROUPS * D_STATE), F32), SDS((t, SSM_GROUPS * D_STATE), F32),
                   SDS((SSM_GROUPS, t, LANES), F32), SDS((SSM_GROUPS, 1, LANES), F32), SDS((SSM_GROUPS, 1, LANES), F32),
                   SDS((1, di), F32)],
        scratch_shapes=[pltpu.VMEM((npair * D_STATE, LANES), F32)], compiler_params=_cp("parallel", "arbitrary"),
    )(xc, xc, xc, prep, a, dskip, st, dy)
    return outs


def _gate_norm_fwd(y, proj, zoff, w, *, name):
    t, di = y.shape
    gw = di // SSM_GROUPS
    tm = _pick(t, (512, 256, 128))
    zb = zoff // gw

    def body(y_ref, z_ref, w_ref, o_ref):
        yg = y_ref[...] * _silu(z_ref[...].astype(F32))
        r = lax.rsqrt(jnp.mean(yg * yg, axis=-1, keepdims=True) + EPS)
        o_ref[...] = (yg * r * w_ref[...]).astype(BF16)

    return pl.pallas_call(
        body, name=name, grid=(t // tm, SSM_GROUPS),
        in_specs=[BS((tm, gw), lambda i, g: (i, g)), BS((tm, gw), lambda i, g: (i, zb + g)), BS((1, gw), lambda i, g: (0, g))],
        out_specs=BS((tm, gw), lambda i, g: (i, g)), out_shape=SDS((t, di), BF16), compiler_params=_cp("parallel", "parallel"),
    )(y, proj, w.reshape(1, di))


def _gate_norm_bwd(dssm, y, proj, zoff, w, *, name):
    t, di = y.shape
    gw = di // SSM_GROUPS
    tm = _pick(t, (512, 256, 128))
    zb = zoff // gw

    def body(d_ref, y_ref, z_ref, w_ref, dy_ref, dz_ref, dw_ref):
        z = z_ref[...].astype(F32)
        yv = y_ref[...]
        sz = _silu(z)
        yg = yv * sz
        r = lax.rsqrt(jnp.mean(yg * yg, axis=-1, keepdims=True) + EPS)
        n = yg * r
        d = d_ref[...]
        dn = d * w_ref[...]
        dyg = r * (dn - n * jnp.mean(dn * n, axis=-1, keepdims=True))
        dy_ref[...] = dyg * sz
        dz_ref[...] = (dyg * yv * _dsilu(z)).astype(BF16)

        @pl.when(pl.program_id(1) == 0)
        def _():
            dw_ref[...] = jnp.zeros_like(dw_ref)

        dw_ref[...] += jnp.sum(d * n, axis=0, keepdims=True)

    blk = BS((tm, gw), lambda g, i: (i, g))
    dy, dz, dw = pl.pallas_call(
        body, name=name, grid=(SSM_GROUPS, t // tm),
        in_specs=[blk, blk, BS((tm, gw), lambda g, i: (i, zb + g)), BS((1, gw), lambda g, i: (0, g))],
        out_specs=[blk, blk, BS((1, gw), lambda g, i: (0, g))],
        out_shape=[SDS((t, di), F32), SDS((t, di), BF16), SDS((1, di), F32)], compiler_params=_cp("parallel", "arbitrary"),
    )(dssm, y, proj, w.reshape(1, di))
    return dy, dz, dw[0]


def _merge_fwd(proj, goff, ga, gs, *, name):
    t, d = ga.shape
    cw = _conv_block(d, goff)
    tm = _pick(t, (512, 256, 128))
    gb = goff // cw

    def body(g0, g1, a_ref, s_ref, o_ref):
        o_ref[...] = (_sigmoid(g0[...].astype(F32)) * a_ref[...] + _sigmoid(g1[...].astype(F32)) * s_ref[...]).astype(BF16)

    blk = BS((tm, cw), lambda i, j: (i, j))
    return pl.pallas_call(
        body, name=name, grid=(t // tm, d // cw),
        in_specs=[BS((tm, cw), lambda i, j: (i, gb + j)), BS((tm, cw), lambda i, j: (i, gb + d // cw + j)), blk, blk],
        out_specs=blk, out_shape=SDS((t, d), BF16), compiler_params=_cp("parallel", "parallel"),
    )(proj, proj, ga, gs)


def _merge_bwd(proj, goff, ga, gs, dm, *, name):
    t, d = ga.shape
    cw = _conv_block(d, goff)
    tm = _pick(t, (512, 256, 128))
    gb = goff // cw

    def body(g0, g1, a_ref, s_ref, dm_ref, da_ref, ds_ref, dg0_ref, dg1_ref):
        dmv = dm_ref[...]
        s0 = _sigmoid(g0[...].astype(F32))
        s1 = _sigmoid(g1[...].astype(F32))
        da_ref[...] = (s0 * dmv).astype(BF16)
        ds_ref[...] = (s1 * dmv).astype(BF16)
        dg0_ref[...] = (dmv * a_ref[...] * s0 * (1.0 - s0)).astype(BF16)
        dg1_ref[...] = (dmv * s_ref[...] * s1 * (1.0 - s1)).astype(BF16)

    blk = BS((tm, cw), lambda i, j: (i, j))
    return pl.pallas_call(
        body, name=name, grid=(t // tm, d // cw),
        in_specs=[BS((tm, cw), lambda i, j: (i, gb + j)), BS((tm, cw), lambda i, j: (i, gb + d // cw + j)), blk, blk, blk],
        out_specs=[blk] * 4, out_shape=[SDS((t, d), BF16)] * 4, compiler_params=_cp("parallel", "parallel"),
    )(proj, proj, ga, gs, dm)


def _swiglu_fwd(u, *, name):
    t, two_f = u.shape
    f = two_f // 2
    cw = _wide(f)
    tm = _pick(t, (512, 256, 128))

    def body(g_ref, u_ref, o_ref):
        o_ref[...] = (_silu(g_ref[...].astype(F32)) * u_ref[...].astype(F32)).astype(BF16)

    return pl.pallas_call(
        body, name=name, grid=(t // tm, f // cw),
        in_specs=[BS((tm, cw), lambda i, j: (i, j)), BS((tm, cw), lambda i, j: (i, f // cw + j))],
        out_specs=BS((tm, cw), lambda i, j: (i, j)), out_shape=SDS((t, f), BF16), compiler_params=_cp("parallel", "parallel"),
    )(u, u)


def _swiglu_bwd(u, df, *, name):
    t, two_f = u.shape
    f = two_f // 2
    cw = _wide(f)
    tm = _pick(t, (256, 128))

    def body(u_ref, d_ref, o_ref):
        for j in range(f // cw):
            gate, up = slice(j * cw, (j + 1) * cw), slice(f + j * cw, f + (j + 1) * cw)
            gt = u_ref[:, gate].astype(F32)
            d = d_ref[:, gate].astype(F32)
            o_ref[:, gate] = (d * u_ref[:, up].astype(F32) * _dsilu(gt)).astype(BF16)
            o_ref[:, up] = (d * _silu(gt)).astype(BF16)

    return pl.pallas_call(
        body, name=name, grid=(t // tm,), in_specs=[BS((tm, two_f), lambda i: (i, 0)), BS((tm, f), lambda i: (i, 0))],
        out_specs=BS((tm, two_f), lambda i: (i, 0)), out_shape=SDS((t, two_f), BF16), compiler_params=_cp("parallel"),
    )(u, df)


def _row_block(rows, cols, n_arrays):
    budget = VMEM_LIMIT_BYTES // 3
    for tr in (512, 256, 128, 64, 32, 16, 8):
        if rows % tr == 0 and tr * cols * 4 * n_arrays * 2 <= budget:
            return tr
    raise ValueError((rows, cols))


def _concat_cols(pieces, *, name):
    pieces = [p if isinstance(p, tuple) else (p, p.shape[1], 0) for p in pieces]
    rows, dtype = pieces[0][0].shape[0], pieces[0][0].dtype
    widths = [w for _, w, _ in pieces]
    total = sum(widths)
    assert all(w % LANES == 0 for w in widths) and all(a.dtype == dtype and a.shape[0] == rows for a, _, _ in pieces)
    tr = next(c for c in (512, 256, 128, 64, 32, 16) if rows % c == 0 and 4 * c * total * dtype.itemsize <= VMEM_LIMIT_BYTES // 2)

    def body(*refs):
        o_ref = refs[-1]
        off = 0
        for p_ref, w in zip(refs[:-1], widths):
            o_ref[:, off:off + w] = p_ref[...]
            off += w

    return pl.pallas_call(
        body, name=name, grid=(rows // tr,), in_specs=[BS((tr, w), lambda i, j=j: (i, j)) for _, w, j in pieces],
        out_specs=BS((tr, total), lambda i: (i, 0)), out_shape=SDS((rows, total), dtype), compiler_params=_cp("parallel"),
    )(*[a for a, _, _ in pieces])


def _add_own_layer(g0, g1, got, core, *, name):
    rows, cols = got.shape
    tr = _row_block(rows, cols, 4)

    def body(core_ref, g0_ref, g1_ref, got_ref, o_ref):
        o_ref[...] = (jnp.where(core_ref[0] == 0, g0_ref[...], g1_ref[...]) + got_ref[...]).astype(o_ref.dtype)

    blk = BS((tr, cols), lambda i, cr: (i, 0))
    grid_spec = pltpu.PrefetchScalarGridSpec(
        num_scalar_prefetch=1, grid=(rows // tr,),
        in_specs=[BS((tr, cols), lambda i, cr: (i * (1 - cr[0]), 0)), BS((tr, cols), lambda i, cr: (i * cr[0], 0)), blk],
        out_specs=blk)
    return pl.pallas_call(body, name=name, grid_spec=grid_spec, out_shape=SDS((rows, cols), BF16),
                          compiler_params=_cp("arbitrary"))(core, g0, g1, got)


def _sum_chips(a, *, name):
    _, rows, cols = a.shape
    tr = _row_block(rows, cols, 5)

    def body(a_ref, o_ref):
        o_ref[...] = ((a_ref[0].astype(F32) + a_ref[1].astype(F32)) + a_ref[2].astype(F32)) + a_ref[3].astype(F32)

    return pl.pallas_call(body, name=name, grid=(rows // tr,), in_specs=[BS((N_CHIPS, tr, cols), lambda i: (0, i, 0))],
                          out_specs=BS((tr, cols), lambda i: (i, 0)), out_shape=SDS((rows, cols), F32),
                          compiler_params=_cp("parallel"))(a)


def _adamw(w, g, m, v, *, name):
    rows, cols = w.shape
    tr = _row_block(rows, cols, 7) if rows % 8 == 0 else rows
    c1 = 1.0 - ADAM_B1 ** ADAM_STEP
    c2 = 1.0 - ADAM_B2 ** ADAM_STEP

    def body(w_ref, g_ref, m_ref, v_ref, d_ref, nm_ref, nv_ref):
        gv = g_ref[...]
        nm = ADAM_B1 * m_ref[...] + (1.0 - ADAM_B1) * gv
        nv = ADAM_B2 * v_ref[...] + (1.0 - ADAM_B2) * (gv * gv)
        d_ref[...] = -ADAM_LR * ((nm / c1) / (jnp.sqrt(nv / c2) + ADAM_EPS) + ADAM_WD * w_ref[...])
        nm_ref[...] = nm
        nv_ref[...] = nv

    blk = BS((tr, cols), lambda i: (i, 0))
    return pl.pallas_call(body, name=name, grid=(rows // tr,), in_specs=[blk] * 4, out_specs=[blk] * 3,
                          out_shape=[SDS((rows, cols), F32)] * 3, compiler_params=_cp("parallel"))(w, g, m, v)


def _adamw_layers(w, g_own, g_other, m, v, core, *, name):
    _, rows, cols = w.shape
    tr = _row_block(rows, cols, 9)
    c1 = 1.0 - ADAM_B1 ** ADAM_STEP
    c2 = 1.0 - ADAM_B2 ** ADAM_STEP

    def body(core_ref, w_ref, own_ref, oth_ref, m_ref, v_ref, g_ref, d_ref, nm_ref, nv_ref):
        gv = jnp.where(pl.program_id(0) == core_ref[0], own_ref[...], oth_ref[...])
        nm = ADAM_B1 * m_ref[0] + (1.0 - ADAM_B1) * gv
        nv = ADAM_B2 * v_ref[0] + (1.0 - ADAM_B2) * (gv * gv)
        g_ref[0] = gv
        d_ref[0] = -ADAM_LR * ((nm / c1) / (jnp.sqrt(nv / c2) + ADAM_EPS) + ADAM_WD * w_ref[0])
        nm_ref[0] = nm
        nv_ref[0] = nv

    own_here = lambda l, cr: 1 - (l - cr[0]) * (l - cr[0])
    slab = BS((1, tr, cols), lambda l, i, cr: (l, i, 0))
    grid_spec = pltpu.PrefetchScalarGridSpec(
        num_scalar_prefetch=1, grid=(2, rows // tr),
        in_specs=[slab, BS((tr, cols), lambda l, i, cr: (i * own_here(l, cr), 0)),
                  BS((tr, cols), lambda l, i, cr: (i * (1 - own_here(l, cr)), 0)), slab, slab],
        out_specs=[slab] * 4)
    return pl.pallas_call(body, name=name, grid_spec=grid_spec, out_shape=[SDS(w.shape, F32)] * 4,
                          compiler_params=_cp("arbitrary", "arbitrary"))(core, w, g_own, g_other, m, v)


ANY = BS(memory_space=pl.ANY)


def _place():
    x, y, c = lax.axis_index("x"), lax.axis_index("y"), lax.axis_index("c")
    return x, y, c, [(1 - x, y), (x, 1 - y), (1 - x, 1 - y)]


def _gather_shards(arrs, *, name):
    n = len(arrs)

    def body(*refs):
        ins, outs = refs[:n], refs[n:2 * n]
        send_sems, recv_sems, pass_send_sems, pass_recv_sems = refs[2 * n:]
        x, y, c, chips = _place()
        s = 2 * x + y

        def ici(i, j, src_chip, to):
            src = ins[i].at[c] if src_chip is None else outs[i].at[src_chip, c]
            return pltpu.make_async_remote_copy(
                src_ref=src, dst_ref=outs[i].at[s if src_chip is None else src_chip, c], send_sem=send_sems.at[i * 3 + j],
                recv_sem=recv_sems.at[i * 3 + j], device_id=to, device_id_type=MESH)

        def d2d(i, j, src_chip, layer):
            slab = outs[i].at[src_chip, layer]
            return pltpu.make_async_remote_copy(
                src_ref=slab, dst_ref=slab, send_sem=pass_send_sems.at[i * 3 + j], recv_sem=pass_recv_sems.at[i * 3 + j],
                device_id=(x, y, 1 - c), device_id_type=MESH)

        sent = []
        for i in range(n):
            for j, (px, py) in enumerate(chips):
                cp = ici(i, j, None, (px, py, c))
                cp.start()
                sent.append(cp)
        passed = []
        for i in range(n):
            for j, (px, py) in enumerate(chips):
                ici(i, j, 2 * px + py, (x, y, c)).wait_recv()
                cp = d2d(i, j, 2 * px + py, c)
                cp.start()
                passed.append(cp)
        for i in range(n):
            for j, (px, py) in enumerate(chips):
                d2d(i, j, 2 * px + py, 1 - c).wait_recv()
        for cp in sent + passed:
            cp.wait_send()

    return pl.pallas_call(
        body, name=name, in_specs=[ANY] * n, out_specs=[ANY] * n,
        out_shape=[SDS((N_CHIPS,) + a.shape, a.dtype) for a in arrs],
        scratch_shapes=[pltpu.SemaphoreType.DMA((3 * n,))] * 4,
    )(*arrs)


def _pair_swap_layers(layer0, layer1, *, name):
    n = len(layer0)

    def body(*refs):
        in0, in1, outs = refs[:n], refs[n:2 * n], refs[2 * n:3 * n]
        send_sems, recv_sems = refs[3 * n:]
        x, y, c, _ = _place()

        def copy(src, i):
            return pltpu.make_async_remote_copy(
                src_ref=src[i], dst_ref=outs[i], send_sem=send_sems.at[i], recv_sem=recv_sems.at[i],
                device_id=(x, y, 1 - c), device_id_type=MESH)

        @pl.when(c == 0)
        def _():
            for i in range(n):
                copy(in1, i).start()

        @pl.when(c == 1)
        def _():
            for i in range(n):
                copy(in0, i).start()

        for i in range(n):
            copy(in0, i).wait()

    return pl.pallas_call(
        body, name=name, in_specs=[ANY] * (2 * n), out_specs=[ANY] * n, out_shape=[SDS(a.shape, a.dtype) for a in layer0],
        scratch_shapes=[pltpu.SemaphoreType.DMA((n,)), pltpu.SemaphoreType.DMA((n,))],
    )(*layer0, *layer1)


def _scatter_to_chips(arrs, *, name):
    n = len(arrs)

    def body(*refs):
        ins, outs = refs[:n], refs[n:2 * n]
        send_sems, recv_sems = refs[2 * n:]
        x, y, c, chips = _place()
        s = 2 * x + y
        copies = []
        for i in range(n):
            for j, (px, py) in enumerate(chips):
                cp = pltpu.make_async_remote_copy(
                    src_ref=ins[i].at[2 * px + py], dst_ref=outs[i].at[s], send_sem=send_sems.at[i * 3 + j],
                    recv_sem=recv_sems.at[i * 3 + j], device_id=(px, py, c), device_id_type=MESH)
                cp.start()
                copies.append(cp)
        for cp in copies:
            cp.wait()

    return pl.pallas_call(
        body, name=name, in_specs=[ANY] * n, out_specs=[ANY] * n, out_shape=[SDS(a.shape, a.dtype) for a in arrs],
        scratch_shapes=[pltpu.SemaphoreType.DMA((3 * n,)), pltpu.SemaphoreType.DMA((3 * n,))],
    )(*arrs)


def _pair_swap(arrs, *, name):
    n = len(arrs)

    def body(*refs):
        ins, outs = refs[:n], refs[n:2 * n]
        send_sems, recv_sems = refs[2 * n:]
        x, y, c, _ = _place()
        copies = []
        for i in range(n):
            cp = pltpu.make_async_remote_copy(
                src_ref=ins[i], dst_ref=outs[i], send_sem=send_sems.at[i], recv_sem=recv_sems.at[i],
                device_id=(x, y, 1 - c), device_id_type=MESH)
            cp.start()
            copies.append(cp)
        for cp in copies:
            cp.wait()

    return pl.pallas_call(
        body, name=name, in_specs=[ANY] * n, out_specs=[ANY] * n, out_shape=[SDS(a.shape, a.dtype) for a in arrs],
        scratch_shapes=[pltpu.SemaphoreType.DMA((n,)), pltpu.SemaphoreType.DMA((n,))],
    )(*arrs)


def _allreduce_small(v, *, name):
    rows, cols = v.shape

    def body(v_ref, o_ref, gath, send_sems, recv_sems):
        x, y, c, _ = _place()
        me = 4 * x + 2 * y + c
        gath[me] = v_ref[...]
        copies = []
        for k in range(1, N_DEV):
            fx, fy, fc = (k >> 2) & 1, (k >> 1) & 1, k & 1
            peer = (1 - x if fx else x, 1 - y if fy else y, 1 - c if fc else c)
            cp = pltpu.make_async_remote_copy(
                src_ref=v_ref, dst_ref=gath.at[me], send_sem=send_sems.at[k - 1], recv_sem=recv_sems.at[k - 1],
                device_id=peer, device_id_type=MESH)
            cp.start()
            copies.append(cp)
        for cp in copies:
            cp.wait()
        acc = gath[0]
        for k in range(1, N_DEV):
            acc = acc + gath[k]
        o_ref[...] = acc

    vm = BS(memory_space=pltpu.VMEM)
    return pl.pallas_call(
        body, name=name, in_specs=[vm], out_specs=vm, out_shape=SDS((rows, cols), F32),
        scratch_shapes=[pltpu.VMEM((N_DEV, rows, cols), F32), pltpu.SemaphoreType.DMA((N_DEV - 1,)), pltpu.SemaphoreType.DMA((N_DEV - 1,))],
    )(v)


def _t5_bucket(dist):
    max_exact = N_REL_BUCKETS // 2
    d_f = jnp.maximum(dist, 1).astype(F32)
    large = max_exact + (jnp.log(d_f / max_exact) / math.log(REL_MAX_DISTANCE / max_exact) * (N_REL_BUCKETS - max_exact)).astype(jnp.int32)
    return jnp.where(dist < max_exact, dist, jnp.minimum(large, N_REL_BUCKETS - 1))


def _rel_buckets(dilation):
    qi = jnp.arange(ATTN_BLOCK)[:, None]
    kj = jnp.arange(2 * ATTN_BLOCK)[None, :]
    return _t5_bucket(jnp.clip(qi + ATTN_BLOCK - kj, 0, N_STEPS) * dilation)


def _layer_fwd(h, p, biases, lname):
    sv = {"h": h}
    xn1 = _rms_fwd(h, p["norm1_w"], name=lname + "norm1")
    proj = _matmul(xn1, p["w_main"], out_dtype=BF16, name=lname + "in_proj")
    dtr = _matmul(xn1, p["w_dt"], out_dtype=F32, name=lname + "in_proj_dt")
    xn1_rm, qkvs = [xn1], [proj]
    for g in range(1, N_GROUPS_ATTN):
        xn1_rm.append(_to_residue_major(xn1, DILATIONS[g]))
        qkvs.append(_matmul(xn1_rm[g], p["w_qkv"][g], out_dtype=BF16, name=f"{lname}in_proj_qkv{g}"))
    os_, lses, lses_rm = [], [], []
    for g, d in enumerate(DILATIONS):
        o, lse = _attn_fwd(qkvs[g], (0, 1, 2), biases[g], d, name=f"{lname}attn{g}")
        os_.append(_to_token_major(o, d))
        lses.append(_to_token_major(lse, d))
        lses_rm.append(lse)
    sv.update(xn1_rm=xn1_rm, qkvs=qkvs, lses_rm=lses_rm)
    attn = _combine_fwd(os_, lses, name=lname + "attn_combine")
    xc = _conv_fwd(proj, p["off_xbc"], p["conv_w"], p["conv_b"], name=lname + "conv")
    prep = _ssd_prep(dtr, p["dt_bias"], p["a"], name=lname + "ssd_prep")
    sv["prep"] = prep
    y, st = _ssd_fwd(xc, prep, p["a"], p["dskip"], name=lname + "ssd")
    ssm = _gate_norm_fwd(y, proj, p["off_z"], p["ssm_norm_w"], name=lname + "gate_norm")
    ga = _matmul(attn, p["w_attn_branch"], name=lname + "attn_branch")
    gs = _matmul(ssm, p["w_ssm_branch"], name=lname + "ssm_branch")
    merged = _merge_fwd(proj, p["off_gate"], ga, gs, name=lname + "merge")
    h1 = _matmul(merged, p["w_out"], res=h, name=lname + "out_proj")
    xn2 = _rms_fwd(h1, p["norm2_w"], name=lname + "norm2")
    u = _matmul(xn2, p["w_ffn_in"], out_dtype=BF16, name=lname + "ffn_in")
    f = _swiglu_fwd(u, name=lname + "swiglu")
    h2 = _matmul(f, p["w_ffn_out"], res=h1, name=lname + "ffn_out")
    sv.update(xn1=xn1, proj=proj, dtr=dtr, os=os_, lses=lses, attn=attn, xc=xc, y=y, st=st, ssm=ssm, ga=ga, gs=gs,
              merged=merged, h1=h1, xn2=xn2, u=u, f=f)
    return h2, sv


def _layer_bwd(dh2, p, sv, biases, lname):
    gr = {}
    lname = lname + "bwd_"
    df = _matmul(dh2, p["w_ffn_out"], tb=True, out_dtype=BF16, name=lname + "ffn_out_dx")
    gr["w_ffn_out"] = _matmul(sv["f"], dh2, ta=True, name=lname + "ffn_out_dw")
    du = _swiglu_bwd(sv["u"], df, name=lname + "swiglu")
    dxn2 = _matmul(du, p["w_ffn_in"], tb=True, name=lname + "ffn_in_dx")
    gr["w_ffn_in"] = _matmul(sv["xn2"], du, ta=True, name=lname + "ffn_in_dw")
    dh1, gr["norm2_w"] = _rms_bwd(sv["h1"], p["norm2_w"], [dxn2], dh2, name=lname + "norm2")
    dmerged = _matmul(dh1, p["w_out"], tb=True, name=lname + "out_proj_dx")
    gr["w_out"] = _matmul(sv["merged"], dh1, ta=True, name=lname + "out_proj_dw")
    dga, dgs, dg0, dg1 = _merge_bwd(sv["proj"], p["off_gate"], sv["ga"], sv["gs"], dmerged, name=lname + "merge")
    dattn = _matmul(dga, p["w_attn_branch"], tb=True, name=lname + "attn_branch_dx")
    gr["w_attn_branch"] = _matmul(sv["attn"], dga, ta=True, name=lname + "attn_branch_dw")
    dssm = _matmul(dgs, p["w_ssm_branch"], tb=True, name=lname + "ssm_branch_dx")
    gr["w_ssm_branch"] = _matmul(sv["ssm"], dgs, ta=True, name=lname + "ssm_branch_dw")
    dy, dz, gr["ssm_norm_w"] = _gate_norm_bwd(dssm, sv["y"], sv["proj"], p["off_z"], p["ssm_norm_w"], name=lname + "gate_norm")
    dxs, dbm, dcm, ddtr4, da4, dbias4, ddsk = _ssd_bwd(sv["xc"], sv["prep"], p["a"], p["dskip"], sv["st"], dy,
                                                       name=lname + "ssd")
    nsh = p["n_ssm_heads"]
    ddtr = jnp.sum(ddtr4, axis=0)
    gr["a_log"] = jnp.sum(da4, axis=(0, 1))[:nsh] * p["a"][0, :nsh]
    gr["dt_bias"] = jnp.sum(dbias4, axis=(0, 1))[:nsh]
    gr["d_skip"] = jnp.sum(ddsk.reshape(nsh, HEAD_DIM), axis=1)
    di = dxs.shape[1]
    dxbc, dcw, dcb = [], [], []
    for part, (lo, hi) in zip((dxs, dbm, dcm), ((0, di), (di, di + dbm.shape[1]), (di + dbm.shape[1], di + 2 * dbm.shape[1]))):
        dx_, dw_, db_ = _conv_bwd(sv["proj"], p["off_xbc"] + lo, p["conv_w"][:, lo:hi], p["conv_b"][lo:hi], part,
                                  name=f"{lname}conv{lo}")
        dxbc.append(dx_)
        dcw.append(dw_)
        dcb.append(db_)
    gr["conv_w"] = jnp.concatenate(dcw, axis=1)
    gr["conv_b"] = jnp.concatenate(dcb, axis=0)
    dos, corrs = _combine_bwd(dattn, sv["os"], sv["lses"], name=lname + "attn_combine")
    dqkvs, dbiases = [], []
    for g, d in enumerate(DILATIONS):
        dq, dk, dv, dbias = _attn_bwd(sv["qkvs"][g], (0, 1, 2), biases[g], sv["lses_rm"][g], _to_residue_major(dos[g], d),
                                      _to_residue_major(corrs[g], d), d, name=f"{lname}attn{g}")
        dqkvs.append([dq, dk, dv])
        dbiases.append(dbias)
    dmain = _concat_cols(dqkvs[0] + [dz] + dxbc + [dg0, dg1], name=lname + "in_proj_join")
    dxn1 = [_matmul(dmain, p["w_main"], tb=True, name=lname + "in_proj_dx"),
            _matmul(ddtr, p["w_dt"], tb=True, name=lname + "in_proj_dt_dx")]
    dw_main = _matmul(sv["xn1"], dmain, ta=True, name=lname + "in_proj_dw")
    dw_dt = _matmul(sv["xn1"], ddtr, ta=True, name=lname + "in_proj_dt_dw")
    dw_qkv = [dw_main]
    for g in range(1, N_GROUPS_ATTN):
        dqkv = _concat_cols(dqkvs[g], name=f"{lname}in_proj_qkv{g}_join")
        dxn1.append(_to_token_major(_matmul(dqkv, p["w_qkv"][g], tb=True, name=f"{lname}in_proj_qkv{g}_dx"), DILATIONS[g]))
        dw_qkv.append(_matmul(sv["xn1_rm"][g], dqkv, ta=True, name=f"{lname}in_proj_qkv{g}_dw"))
    awg, og = dqkvs[0][0].shape[1], p["off_gate"]
    cols = [dw[:, i * awg:(i + 1) * awg] for i in range(3) for dw in dw_qkv]
    gr["w_in"] = jnp.concatenate(cols + [dw_main[:, 3 * awg:og], dw_dt[:, :nsh], dw_main[:, og:]], axis=1)
    dh, gr["norm1_w"] = _rms_bwd(sv["h"], p["norm1_w"], dxn1, dh1, name=lname + "norm1")
    return dh, gr, dbiases


def _layer_params(l, w, n_ssm_heads, hg):
    awg = hg * HEAD_DIM
    aw = N_GROUPS_ATTN * awg
    di = n_ssm_heads * HEAD_DIM
    xbc = di + 2 * SSM_GROUPS * D_STATE
    in_dt = 3 * aw + di + xbc
    w_in = w["w_in"][l]
    qkv_cols = lambda g: [(w_in, awg, i * N_GROUPS_ATTN + g) for i in range(3)]
    z_xbc_cols = [(w_in, awg, j) for j in range(3 * N_GROUPS_ATTN, in_dt // awg)]
    assert in_dt % awg == 0
    pad = lambda v: jnp.pad(v.astype(F32), (0, LANES - n_ssm_heads)).reshape(1, LANES)
    return dict(
        n_ssm_heads=n_ssm_heads, off_z=3 * awg, off_xbc=3 * awg + di, off_gate=3 * awg + di + xbc,
        w_main=_concat_cols(qkv_cols(0) + z_xbc_cols + [w_in[:, in_dt + n_ssm_heads:]], name=f"l{l}_w_main"),
        w_qkv=[None] + [_concat_cols(qkv_cols(g), name=f"l{l}_w_qkv{g}") for g in range(1, N_GROUPS_ATTN)],
        w_dt=jnp.pad(w_in[:, in_dt:in_dt + n_ssm_heads], ((0, 0), (0, LANES - n_ssm_heads))),
        norm1_w=w["norm1_w"][l], norm2_w=w["norm2_w"][l], conv_w=w["conv_w"][l], conv_b=w["conv_b"][l],
        dt_bias=pad(w["dt_bias"][l]), a=pad(-jnp.exp(w["a_log"][l])),
        dskip=jnp.repeat(w["d_skip"][l], HEAD_DIM).reshape(1, di), ssm_norm_w=w["ssm_norm_w"][l],
        w_attn_branch=w["w_attn_branch"][l], w_ssm_branch=w["w_ssm_branch"][l], w_out=w["w_out"][l],
        w_ffn_in=w["w_ffn_in"][l], w_ffn_out=w["w_ffn_out"][l],
    )


def _local_step(x, tgt, w):
    depth = w["norm1_w"].shape[0]
    n_ssm_heads = w["dt_bias"].shape[1]
    hg = w["rel_bias"].shape[1] // N_GROUPS_ATTN
    onehots = [(_rel_buckets(dil)[:, :, None] == jnp.arange(N_REL_BUCKETS)[None, None, :]).astype(F32) for dil in DILATIONS]
    biases = [jnp.einsum("qkb,bh->hqk", oh, w["rel_bias"][:, g * hg:(g + 1) * hg].astype(F32), precision=HIGHEST)
              for g, oh in enumerate(onehots)]
    params = [_layer_params(l, w, n_ssm_heads, hg) for l in range(depth)]
    h = x
    saved = []
    for l in range(depth):
        h, sv = _layer_fwd(h, params[l], biases, f"l{l}_")
        saved.append(sv)
    loss, dh, g_final = _loss_head(h, w["final_norm_w"], tgt, name="loss_head")
    grads = [None] * depth
    dbias_tot = [jnp.zeros(b.shape, F32) for b in biases]
    for l in reversed(range(depth)):
        dh, grads[l], dbiases = _layer_bwd(dh, params[l], saved[l], biases, f"l{l}_")
        dbias_tot = [a + b for a, b in zip(dbias_tot, dbiases)]
    out = {k: [gl[k] for gl in grads] if k in MATRICES else jnp.stack([gl[k] for gl in grads]) for k in grads[0]}
    out["final_norm_w"] = g_final
    drel = []
    for g, (oh, db) in enumerate(zip(onehots, dbias_tot)):
        oh_t = jnp.pad(oh.reshape(-1, N_REL_BUCKETS).T, ((0, LANES - N_REL_BUCKETS), (0, 0)))
        db_rows = jnp.pad(db.reshape(hg, -1), ((0, LANES - hg), (0, 0)))
        drel.append(_matmul(oh_t, db_rows, tb=True, name=f"rel_bias_fold{g}")[:N_REL_BUCKETS, :hg])
    out["rel_bias"] = jnp.concatenate(drel, axis=1)
    return loss, dh, out


MATRICES = ("w_in", "w_attn_branch", "w_ssm_branch", "w_out", "w_ffn_in", "w_ffn_out")
COL_SHARDED = ("w_in", "w_attn_branch", "w_ffn_in")
SMALL = ("norm1_w", "conv_b", "dt_bias", "a_log", "d_skip", "ssm_norm_w", "norm2_w", "rel_bias", "final_norm_w")
WEIGHTS = ("norm1_w", "w_in", "conv_w", "conv_b", "dt_bias", "a_log", "d_skip", "ssm_norm_w", "w_attn_branch",
           "w_ssm_branch", "w_out", "norm2_w", "w_ffn_in", "w_ffn_out", "rel_bias", "final_norm_w")
SMALL_COLS = 1024


def _unshard(name, g):
    _, depth, r, c = g.shape
    if name in COL_SHARDED or name == "conv_w":
        return jnp.transpose(g, (1, 2, 0, 3)).reshape(depth, r, N_CHIPS * c)
    return jnp.transpose(g, (1, 0, 2, 3)).reshape(depth, N_CHIPS * r, c)


def _to_shards(name, g):
    r, c = g.shape
    if name in COL_SHARDED:
        return jnp.transpose(g.reshape(r, N_CHIPS, c // N_CHIPS), (1, 0, 2))
    return g.reshape(N_CHIPS, r // N_CHIPS, c)


def kernel(x, norm1_w, w_in, conv_w, conv_b, dt_bias, a_log, d_skip, ssm_norm_w, w_attn_branch, w_ssm_branch, w_out, norm2_w, w_ffn_in, w_ffn_out, rel_bias, final_norm_w, loss_target, m_norm1_w, m_w_in, m_conv_w, m_conv_b, m_dt_bias, m_a_log, m_d_skip, m_ssm_norm_w, m_w_attn_branch, m_w_ssm_branch, m_w_out, m_norm2_w, m_w_ffn_in, m_w_ffn_out, m_rel_bias, m_final_norm_w, v_norm1_w, v_w_in, v_conv_w, v_conv_b, v_dt_bias, v_a_log, v_d_skip, v_ssm_norm_w, v_w_attn_branch, v_w_ssm_branch, v_w_out, v_norm2_w, v_w_ffn_in, v_w_ffn_out, v_rel_bias, v_final_norm_w):
    env = dict(locals())
    wts = {k: env[k] for k in WEIGHTS}
    mom = {k: env["m_" + k] for k in WEIGHTS}
    var = {k: env["v_" + k] for k in WEIGHTS}
    chip = 2 * lax.axis_index("x") + lax.axis_index("y")
    core = lax.axis_index("c")

    shards = [wts[k].astype(BF16) for k in MATRICES] + [conv_w]
    gathered = _gather_shards(shards, name="gather_weights")
    full = {k: wts[k] for k in SMALL}
    for k, own, g in zip(MATRICES + ("conv_w",), shards, gathered):
        full[k] = _unshard(k, lax.dynamic_update_index_in_dim(g, own, chip, axis=0))

    loss, dx, grads = _local_step(x[0], loss_target[0], full)
    loss = lax.psum(loss, ("x", "y", "c"))

    core1 = core.reshape(1).astype(jnp.int32)
    from_pair = _pair_swap_layers([grads[k][0] for k in MATRICES], [grads[k][1] for k in MATRICES], name="reduce_pair_swap")
    scatter_in = [_to_shards(k, _add_own_layer(grads[k][0], grads[k][1], got, core1, name="reduce_pair_add_" + k))
                  for k, got in zip(MATRICES, from_pair)]
    scattered = _scatter_to_chips(scatter_in, name="reduce_scatter")
    own_layer = []
    for k, sent, got in zip(MATRICES, scatter_in, scattered):
        got = lax.dynamic_update_index_in_dim(got, lax.dynamic_index_in_dim(sent, chip, axis=0, keepdims=False), chip, axis=0)
        own_layer.append(_sum_chips(got, name="reduce_sum_" + k))
    other_layer = _pair_swap(own_layer, name="reduce_pair_exchange")
    reduced = {}

    small_names = SMALL + ("conv_w",)
    flat = jnp.concatenate([grads[k].reshape(-1) for k in small_names])
    n_small = flat.shape[0]
    rows = -(-n_small // SMALL_COLS)
    rows = -(-rows // 8) * 8
    flat = jnp.pad(flat, (0, rows * SMALL_COLS - n_small)).reshape(rows, SMALL_COLS)
    flat = _allreduce_small(flat, name="allreduce_small").reshape(-1)
    pos = 0
    for k in small_names:
        size = math.prod(grads[k].shape)
        reduced[k] = flat[pos:pos + size].reshape(grads[k].shape)
        pos += size
    cs = conv_w.shape[2]
    reduced["conv_w"] = lax.dynamic_slice_in_dim(reduced["conv_w"], chip * cs, cs, axis=2)

    delta, new_m, new_v = {}, {}, {}
    for k, own, other in zip(MATRICES, own_layer, other_layer):
        reduced[k], delta[k], new_m[k], new_v[k] = _adamw_layers(wts[k], own, other, mom[k], var[k], core1, name="adamw_" + k)
    pack = lambda src: jnp.pad(jnp.concatenate([src[k].reshape(-1) for k in small_names]),
                               (0, rows * SMALL_COLS - n_shard)).reshape(rows, SMALL_COLS)
    n_shard = sum(math.prod(wts[k].shape) for k in small_names)
    d_, m_, v_ = _adamw(pack(wts), pack(reduced), pack(mom), pack(var), name="adamw_small")
    pos = 0
    for k in small_names:
        size = math.prod(wts[k].shape)
        for dst, src in ((delta, d_), (new_m, m_), (new_v, v_)):
            dst[k] = src.reshape(-1)[pos:pos + size].reshape(wts[k].shape)
        pos += size

    return (loss, dx[None], *[reduced[k] for k in WEIGHTS], *[delta[k] for k in WEIGHTS],
            *[new_m[k] for k in WEIGHTS], *[new_v[k] for k in WEIGHTS])
```

```python
import functools
import math

import jax
import jax.numpy as jnp
from jax import lax
from jax.experimental import pallas as pl
from jax.experimental.pallas import tpu as pltpu

F32, BF16 = jnp.float32, jnp.bfloat16
SDS = jax.ShapeDtypeStruct
BS = pl.BlockSpec
MESH = pl.DeviceIdType.MESH
HIGHEST = lax.Precision.HIGHEST

EPS = 1e-6
HEAD_DIM = 64
ATTN_BLOCK = 128
DILATIONS = (1, 4, 16)
N_GROUPS_ATTN = len(DILATIONS)
N_STEPS = 128
N_REL_BUCKETS = 32
REL_MAX_DISTANCE = 2048
SSM_GROUPS = 4
D_STATE = 128
CHUNK = 128
CONV_WIDTH = 4
HALO = 16
LANES = 128
N_CHIPS = 4
N_DEV = 8
VMEM_LIMIT_BYTES = 48 * 1024 * 1024

ADAM_LR, ADAM_B1, ADAM_B2, ADAM_EPS, ADAM_WD, ADAM_STEP = 0.001, 0.9, 0.999, 1e-08, 0.01, 10

NT = (((1,), (1,)), ((), ()))
TN = (((0,), (0,)), ((), ()))
NN = (((1,), (0,)), ((), ()))


def _cp(*sem):
    return pltpu.CompilerParams(dimension_semantics=sem if sem else None, vmem_limit_bytes=VMEM_LIMIT_BYTES)


def _pick(n, cands):
    for c in cands:
        if n % c == 0:
            return c
    raise ValueError(f"no block size of {cands} divides {n}")


def _divisors(n, cap):
    out = [c for c in range(LANES, min(n, cap) + 1, LANES) if n % c == 0]
    return out or [n]


def _wide(n, cap=2048):
    return _divisors(n, cap)[-1]


MXU_FLOPS = 9.0e14
HBM_BYTES_PER_S = 3.0e12
ACC_BYTES_PER_S = 4.0e12
GRID_STEP_S = 0.4e-6
TILE_VMEM_BYTES = 36 * 1024 * 1024


def _matmul_tiles(m, n, k, a_bytes, b_bytes, o_bytes, has_res):
    best = None
    for tm in _divisors(m, 2048):
        for tn in _divisors(n, 2048):
            for tk in _divisors(k, 4096):
                ni, nj, nk = m // tm, n // tn, k // tk
                vmem = 2 * (tm * tk * a_bytes + tk * tn * b_bytes + tm * tn * (o_bytes + (4 if has_res else 0)))
                vmem += tm * tn * 4 * (2 if nk > 1 else 1) + (tm * tk + tk * tn) * 2
                if vmem > TILE_VMEM_BYTES:
                    continue
                hbm = m * k * a_bytes * (nj if nk > 1 else 1) + k * n * b_bytes * (ni if nj * nk > 1 else 1)
                hbm += m * n * (o_bytes + (4 if has_res else 0))
                t = max(2.0 * m * n * k / MXU_FLOPS, hbm / HBM_BYTES_PER_S) + ni * nj * nk * GRID_STEP_S
                if nk > 1:
                    t += m * n * 8.0 * nk / ACC_BYTES_PER_S
                if best is None or t < best[0]:
                    best = (t, tm, tn, tk)
    assert best is not None, (m, n, k)
    return best[1:]


def _dot(a, b, dims=NN, precision=None):
    return lax.dot_general(a, b, dims, precision=precision, preferred_element_type=F32)


def _silu(x):
    return x / (1.0 + jnp.exp(-x))


def _sigmoid(x):
    return 1.0 / (1.0 + jnp.exp(-x))


def _dsilu(x):
    s = _sigmoid(x)
    return s * (1.0 + x * (1.0 - s))


def _matmul(a, b, *, name, ta=False, tb=False, out_dtype=F32, res=None):
    (kdim, m) = a.shape if ta else a.shape[::-1]
    (n, k2) = b.shape if tb else b.shape[::-1]
    assert kdim == k2, (a.shape, b.shape, ta, tb)
    tm, tn, tk = _matmul_tiles(m, n, kdim, a.dtype.itemsize, b.dtype.itemsize, jnp.dtype(out_dtype).itemsize, res is not None)
    nk = kdim // tk
    a_spec = BS((tk, tm), lambda i, j, k: (k, i)) if ta else BS((tm, tk), lambda i, j, k: (i, k))
    b_spec = BS((tn, tk), lambda i, j, k: (j, k)) if tb else BS((tk, tn), lambda i, j, k: (k, j))
    dims = (((0 if ta else 1,), (1 if tb else 0,)), ((), ()))
    has_res = res is not None

    def body(*refs):
        a_ref, b_ref = refs[:2]
        r_ref = refs[2] if has_res else None
        o_ref = refs[3] if has_res else refs[2]
        prod = _dot(a_ref[...].astype(BF16), b_ref[...].astype(BF16), dims)
        if nk == 1:
            o_ref[...] = (prod + r_ref[...] if has_res else prod).astype(o_ref.dtype)
            return
        acc = refs[-1]
        k = pl.program_id(2)

        @pl.when(k == 0)
        def _():
            acc[...] = prod

        @pl.when(k > 0)
        def _():
            acc[...] += prod

        @pl.when(k == nk - 1)
        def _():
            r = acc[...]
            if has_res:
                r = r + r_ref[...]
            o_ref[...] = r.astype(o_ref.dtype)

    in_specs = [a_spec, b_spec]
    args = [a, b]
    if has_res:
        in_specs.append(BS((tm, tn), lambda i, j, k: (i, j)))
        args.append(res)
    return pl.pallas_call(
        body, name=name, grid=(m // tm, n // tn, nk), in_specs=in_specs,
        out_specs=BS((tm, tn), lambda i, j, k: (i, j)), out_shape=SDS((m, n), out_dtype),
        scratch_shapes=[pltpu.VMEM((tm, tn), F32)] if nk > 1 else [],
        compiler_params=_cp("parallel", "parallel", "arbitrary"),
    )(*args)


def _rms_fwd(h, w, *, name):
    t, d = h.shape
    tm = _pick(t, (512, 256, 128))

    def body(h_ref, w_ref, o_ref):
        x = h_ref[...]
        r = lax.rsqrt(jnp.mean(x * x, axis=-1, keepdims=True) + EPS)
        o_ref[...] = (x * r * w_ref[...]).astype(BF16)

    return pl.pallas_call(
        body, name=name, grid=(t // tm,), in_specs=[BS((tm, d), lambda i: (i, 0)), BS((1, d), lambda i: (0, 0))],
        out_specs=BS((tm, d), lambda i: (i, 0)), out_shape=SDS((t, d), BF16), compiler_params=_cp("parallel"),
    )(h, w.reshape(1, d))


def _rms_bwd(h, w, dys, dres, *, name):
    t, d = h.shape
    tm = _pick(t, (512, 256, 128))
    n_dy = len(dys)

    def body(*refs):
        h_ref, w_ref = refs[:2]
        dy_refs = refs[2:2 + n_dy]
        dres_ref, dh_ref, dw_ref = refs[2 + n_dy:]
        x = h_ref[...]
        dy = dy_refs[0][...]
        for r_ in dy_refs[1:]:
            dy = dy + r_[...]
        r = lax.rsqrt(jnp.mean(x * x, axis=-1, keepdims=True) + EPS)
        g = dy * w_ref[...]
        proj = jnp.sum(g * x, axis=-1, keepdims=True) * (1.0 / d)
        dh_ref[...] = dres_ref[...] + r * g - x * (r * r * r) * proj

        @pl.when(pl.program_id(0) == 0)
        def _():
            dw_ref[...] = jnp.zeros_like(dw_ref)

        dw_ref[...] += jnp.sum(dy * x * r, axis=0, keepdims=True)

    row = BS((tm, d), lambda i: (i, 0))
    vec = BS((1, d), lambda i: (0, 0))
    dh, dw = pl.pallas_call(
        body, name=name, grid=(t // tm,), in_specs=[row, vec] + [row] * n_dy + [row],
        out_specs=[row, vec], out_shape=[SDS((t, d), F32), SDS((1, d), F32)], compiler_params=_cp("arbitrary"),
    )(h, w.reshape(1, d), *dys, dres)
    return dh, dw[0]


def _loss_head(h, w, tgt, *, name):
    t, d = h.shape
    tm = _pick(t, (512, 256, 128))

    def body(h_ref, w_ref, t_ref, loss_ref, dh_ref, dw_ref):
        x = h_ref[...]
        r = lax.rsqrt(jnp.mean(x * x, axis=-1, keepdims=True) + EPS)
        err = x * r * w_ref[...] - t_ref[...]
        loss_ref[...] = jnp.zeros(loss_ref.shape, F32) + 0.5 * jnp.sum(err * err) * (1.0 / d)
        dy = err * (1.0 / d)
        g = dy * w_ref[...]
        proj = jnp.sum(g * x, axis=-1, keepdims=True) * (1.0 / d)
        dh_ref[...] = r * g - x * (r * r * r) * proj

        @pl.when(pl.program_id(0) == 0)
        def _():
            dw_ref[...] = jnp.zeros_like(dw_ref)

        dw_ref[...] += jnp.sum(dy * x * r, axis=0, keepdims=True)

    row = BS((tm, d), lambda i: (i, 0))
    vec = BS((1, d), lambda i: (0, 0))
    loss, dh, dw = pl.pallas_call(
        body, name=name, grid=(t // tm,), in_specs=[row, vec, row],
        out_specs=[BS((1, 8, LANES), lambda i: (i, 0, 0)), row, vec],
        out_shape=[SDS((t // tm, 8, LANES), F32), SDS((t, d), F32), SDS((1, d), F32)], compiler_params=_cp("arbitrary"),
    )(h, w.reshape(1, d), tgt)
    return jnp.sum(loss[:, 0, 0]), dh, dw[0]


def _attn_masks(mb):
    qi = lax.broadcasted_iota(jnp.int32, (2 * ATTN_BLOCK, 2 * ATTN_BLOCK), 0) & (ATTN_BLOCK - 1)
    kj = lax.broadcasted_iota(jnp.int32, (2 * ATTN_BLOCK, 2 * ATTN_BLOCK), 1)
    steps = qi + ATTN_BLOCK - kj
    valid = (steps >= 0) & (steps <= N_STEPS) & ((kj >= ATTN_BLOCK) | (mb > 0))
    low = lax.broadcasted_iota(jnp.int32, (ATTN_BLOCK, LANES), 1) < HEAD_DIM
    return valid, low


def _stack_heads(x2, low):
    zero = jnp.zeros_like(x2)
    return jnp.concatenate([jnp.where(low, x2, zero), jnp.where(low, zero, x2)], axis=0)


def _head_cols(x2, low):
    return jnp.concatenate([jnp.max(jnp.where(low, x2, -jnp.inf), axis=-1, keepdims=True),
                            jnp.max(jnp.where(low, -jnp.inf, x2), axis=-1, keepdims=True)], axis=0)


def _to_residue_major(a, d):
    t, c = a.shape
    return a if d == 1 else a.reshape(t // d, d, c).transpose(1, 0, 2).reshape(t, c)


def _to_token_major(a, d):
    t, c = a.shape
    return a if d == 1 else a.reshape(d, t // d, c).transpose(1, 0, 2).reshape(t, c)


def _attn_specs(cols, nb, sub, awg, clamp):
    nsb = nb // sub

    def cur(col):
        return BS((sub * ATTN_BLOCK, awg), lambda r, sb: (r * nsb + clamp(sb), col))

    def prev(col):
        return BS((ATTN_BLOCK, awg), lambda r, sb: (r * nb + jnp.maximum(clamp(sb) * sub - 1, 0), col))

    return [cur(cols[0]), cur(cols[1]), prev(cols[1]), cur(cols[2]), prev(cols[2])]


def _sub_rows(s, n=1):
    return pl.ds(pl.multiple_of(s * ATTN_BLOCK, ATTN_BLOCK), n * ATTN_BLOCK)


def _attn_fwd(qkv, cols, bias, d, *, name):
    t = qkv.shape[0]
    hg = bias.shape[0]
    awg = hg * HEAD_DIM
    nb = t // d // ATTN_BLOCK
    sub = _pick(nb, (4, 2, 1))
    rows = sub * ATTN_BLOCK
    scale = HEAD_DIM ** -0.5

    def body(q_ref, kc_ref, kp_ref, vc_ref, vp_ref, b_ref, o_ref, l_ref, kext, vext):
        sb = pl.program_id(1)
        kext[pl.ds(0, ATTN_BLOCK), :] = kp_ref[...]
        kext[pl.ds(ATTN_BLOCK, rows), :] = kc_ref[...]
        vext[pl.ds(0, ATTN_BLOCK), :] = vp_ref[...]
        vext[pl.ds(ATTN_BLOCK, rows), :] = vc_ref[...]

        def one_block(s, carry):
            valid, low = _attn_masks(sb * sub + s)
            for pi in range(awg // LANES):
                sl = slice(pi * LANES, (pi + 1) * LANES)
                k2 = kext[_sub_rows(s, 2), sl]
                v2 = vext[_sub_rows(s, 2), sl]
                qs = _stack_heads(q_ref[_sub_rows(s), sl], low)
                bias2 = b_ref[pi * 2:pi * 2 + 2].reshape(2 * ATTN_BLOCK, 2 * ATTN_BLOCK)
                sc = jnp.where(valid, _dot(qs, k2, NT) * scale + bias2, -jnp.inf)
                m = jnp.max(sc, axis=-1, keepdims=True)
                p = jnp.exp(sc - m)
                den = jnp.sum(p, axis=-1, keepdims=True)
                o = _dot(p.astype(BF16), v2) / den
                lse = jnp.broadcast_to(m + jnp.log(den), (2 * ATTN_BLOCK, LANES))
                o_ref[_sub_rows(s), sl] = jnp.where(low, o[:ATTN_BLOCK], o[ATTN_BLOCK:])
                l_ref[_sub_rows(s), sl] = jnp.where(low, lse[:ATTN_BLOCK], lse[ATTN_BLOCK:])
            return carry

        lax.fori_loop(0, sub, one_block, 0)

    out_spec = BS((rows, awg), lambda r, sb: (r * (nb // sub) + sb, 0))
    return pl.pallas_call(
        body, name=name, grid=(d, nb // sub),
        in_specs=_attn_specs(cols, nb, sub, awg, lambda sb: sb) + [BS(bias.shape, lambda r, sb: (0, 0, 0))],
        out_specs=[out_spec, out_spec], out_shape=[SDS((t, awg), F32)] * 2,
        scratch_shapes=[pltpu.VMEM((rows + ATTN_BLOCK, awg), BF16)] * 2, compiler_params=_cp("parallel", "parallel"),
    )(*([qkv] * 5), bias)


def _attn_bwd(qkv, cols, bias, lse, do, corr, d, *, name):
    t = qkv.shape[0]
    hg = bias.shape[0]
    awg = hg * HEAD_DIM
    nb = t // d // ATTN_BLOCK
    sub = _pick(nb, (4, 2, 1))
    nsb = nb // sub
    rows = sub * ATTN_BLOCK
    scale = HEAD_DIM ** -0.5

    def body(q_ref, kc_ref, kp_ref, vc_ref, vp_ref, b_ref, l_ref, do_ref, c_ref, dq_ref, dk_ref, dv_ref, db_ref,
             kext, vext, dkext, dvext, ck, cv):
        r, sb = pl.program_id(0), pl.program_id(1)

        @pl.when((r == 0) & (sb == 0))
        def _():
            db_ref[...] = jnp.zeros_like(db_ref)

        @pl.when(sb == 0)
        def _():
            ck[...] = jnp.zeros_like(ck)
            cv[...] = jnp.zeros_like(cv)

        @pl.when(sb < nsb)
        def _():
            kext[pl.ds(0, ATTN_BLOCK), :] = kp_ref[...]
            kext[pl.ds(ATTN_BLOCK, rows), :] = kc_ref[...]
            vext[pl.ds(0, ATTN_BLOCK), :] = vp_ref[...]
            vext[pl.ds(ATTN_BLOCK, rows), :] = vc_ref[...]
            dkext[...] = jnp.zeros_like(dkext)
            dvext[...] = jnp.zeros_like(dvext)

            def one_block(s, carry):
                valid, low = _attn_masks(sb * sub + s)
                for pi in range(awg // LANES):
                    sl = slice(pi * LANES, (pi + 1) * LANES)
                    k2 = kext[_sub_rows(s, 2), sl]
                    v2 = vext[_sub_rows(s, 2), sl]
                    qs = _stack_heads(q_ref[_sub_rows(s), sl], low)
                    dos = _stack_heads(do_ref[_sub_rows(s), sl], low)
                    lse_c = _head_cols(l_ref[_sub_rows(s), sl], low)
                    corr_c = _head_cols(c_ref[_sub_rows(s), sl], low)
                    bias2 = b_ref[pi * 2:pi * 2 + 2].reshape(2 * ATTN_BLOCK, 2 * ATTN_BLOCK)
                    sc = _dot(qs, k2, NT) * scale + bias2
                    p = jnp.exp(jnp.where(valid, sc, -jnp.inf) - lse_c)
                    ds = p * (_dot(dos, v2, NT) + corr_c)
                    db_ref[pi * 2:pi * 2 + 2] += ds.reshape(2, ATTN_BLOCK, 2 * ATTN_BLOCK)
                    dsb = ds.astype(BF16)
                    dq = _dot(dsb, k2) * scale
                    dq_ref[_sub_rows(s), sl] = jnp.where(low, dq[:ATTN_BLOCK], dq[ATTN_BLOCK:]).astype(BF16)
                    dkext[_sub_rows(s, 2), sl] += _dot(dsb, qs, TN) * scale
                    dvext[_sub_rows(s, 2), sl] += _dot(p.astype(BF16), dos, TN)
                return carry

            lax.fori_loop(0, sub, one_block, 0)
            head, tail = pl.ds(0, rows - ATTN_BLOCK), pl.ds(rows - ATTN_BLOCK, ATTN_BLOCK)
            for out_ref, carry_ref, ext in ((dk_ref, ck, dkext), (dv_ref, cv, dvext)):
                if sub > 1:
                    out_ref[head, :] = carry_ref[head, :].astype(BF16)
                out_ref[tail, :] = (carry_ref[tail, :] + ext[pl.ds(0, ATTN_BLOCK), :]).astype(BF16)
                carry_ref[...] = ext[pl.ds(ATTN_BLOCK, rows), :]

        @pl.when(sb == nsb)
        def _():
            dk_ref[...] = ck[...].astype(BF16)
            dv_ref[...] = cv[...].astype(BF16)

    clamp = lambda sb: jnp.minimum(sb, nsb - 1)
    cur = BS((rows, awg), lambda r, sb: (r * nsb + clamp(sb), 0))
    prev = BS((rows, awg), lambda r, sb: (r * nsb + jnp.maximum(sb - 1, 0), 0))
    bias_spec = BS(bias.shape, lambda r, sb: (0, 0, 0))
    return pl.pallas_call(
        body, name=name, grid=(d, nsb + 1),
        in_specs=_attn_specs(cols, nb, sub, awg, clamp) + [bias_spec, cur, cur, cur],
        out_specs=[cur, prev, prev, bias_spec],
        out_shape=[SDS((t, awg), BF16)] * 3 + [SDS(bias.shape, F32)],
        scratch_shapes=[pltpu.VMEM((rows + ATTN_BLOCK, awg), BF16)] * 2 + [pltpu.VMEM((rows + ATTN_BLOCK, awg), F32)] * 2
        + [pltpu.VMEM((rows, awg), F32)] * 2,
        compiler_params=_cp("arbitrary", "arbitrary"),
    )(*([qkv] * 5), bias, lse, do, corr)


def _head_sum(x, low):
    a = jnp.sum(jnp.where(low, x, 0.0), axis=-1, keepdims=True)
    b = jnp.sum(jnp.where(low, 0.0, x), axis=-1, keepdims=True)
    return jnp.where(low, a, b)


def _combine_weights(lses):
    mx = jnp.maximum(jnp.maximum(lses[0], lses[1]), lses[2])
    es = [jnp.exp(l - mx) for l in lses]
    tot = es[0] + es[1] + es[2]
    return [e / tot for e in es]


def _combine_fwd(os_, lses, *, name):
    t, awg = os_[0].shape
    tm = _pick(t, (512, 256, 128))

    def body(o0, o1, o2, l0, l1, l2, out_ref):
        al = _combine_weights([l0[...], l1[...], l2[...]])
        out_ref[...] = (al[0] * o0[...] + al[1] * o1[...] + al[2] * o2[...]).astype(BF16)

    blk = BS((tm, awg), lambda i: (i, 0))
    return pl.pallas_call(
        body, name=name, grid=(t // tm,), in_specs=[blk] * 6, out_specs=blk,
        out_shape=SDS((t, awg), BF16), compiler_params=_cp("parallel"),
    )(*os_, *lses)


def _combine_bwd(dattn, os_, lses, *, name):
    t, awg = dattn.shape
    tm = _pick(t, (512, 256, 128))

    def body(da_ref, o0, o1, o2, l0, l1, l2, d0, d1, d2, c0, c1, c2):
        low = lax.broadcasted_iota(jnp.int32, (tm, LANES), 1) < HEAD_DIM
        for pi in range(awg // LANES):
            sl = slice(pi * LANES, (pi + 1) * LANES)
            da = da_ref[:, sl]
            al = _combine_weights([l0[:, sl], l1[:, sl], l2[:, sl]])
            tot = jnp.zeros((tm, LANES), F32)
            for a, o in zip(al, (o0, o1, o2)):
                tot = tot + a * _head_sum(da * o[:, sl], low)
            for a, d_ref, c_ref in zip(al, (d0, d1, d2), (c0, c1, c2)):
                d_ref[:, sl] = (a * da).astype(BF16)
                c_ref[:, sl] = -a * tot

    blk = BS((tm, awg), lambda i: (i, 0))
    outs = pl.pallas_call(
        body, name=name, grid=(t // tm,), in_specs=[blk] * 7, out_specs=[blk] * 6,
        out_shape=[SDS((t, awg), BF16)] * 3 + [SDS((t, awg), F32)] * 3, compiler_params=_cp("parallel"),
    )(dattn, *os_, *lses)
    return outs[:3], outs[3:]


def _conv_block(width, *offsets):
    for c in (512, 256, 128):
        if width % c == 0 and all(o % c == 0 for o in offsets):
            return c
    raise ValueError((width, offsets))


CONV_ROWS = 32


def _conv_pre(x_ref, halo_ref, w_ref, b_ref, ext, i, tm):
    ext[pl.ds(0, HALO), :] = jnp.where(i > 0, halo_ref[...].astype(F32), 0.0)
    ext[pl.ds(HALO, tm), :] = x_ref[...].astype(F32)
    taps = [w_ref[pl.ds(k, 1), :] for k in range(CONV_WIDTH)]
    bias = b_ref[...]
    for r0 in range(0, tm, CONV_ROWS):
        xs = [ext[pl.ds(HALO + r0 - (CONV_WIDTH - 1) + k, CONV_ROWS), :] for k in range(CONV_WIDTH)]
        pre = bias + taps[0] * xs[0]
        for k in range(1, CONV_WIDTH):
            pre = pre + taps[k] * xs[k]
        yield r0, pre, xs


def _fold8(v):
    return jnp.sum(v.reshape(v.shape[0] // 8, 8, v.shape[1]), axis=0)


def _conv_fwd(proj, off, w, b, *, name):
    t = proj.shape[0]
    c = w.shape[1]
    cw = _conv_block(c, off)
    tm = _pick(t, (512, 256, 128))
    ob = off // cw

    def body(x_ref, halo_ref, w_ref, b_ref, o_ref, ext):
        for r0, pre, _ in _conv_pre(x_ref, halo_ref, w_ref, b_ref, ext, pl.program_id(1), tm):
            o_ref[pl.ds(r0, CONV_ROWS), :] = _silu(pre).astype(BF16)

    return pl.pallas_call(
        body, name=name, grid=(c // cw, t // tm),
        in_specs=[BS((tm, cw), lambda j, i: (i, ob + j)),
                  BS((HALO, cw), lambda j, i: (jnp.maximum(i * (tm // HALO) - 1, 0), ob + j)),
                  BS((CONV_WIDTH, cw), lambda j, i: (0, j)), BS((1, cw), lambda j, i: (0, j))],
        out_specs=BS((tm, cw), lambda j, i: (i, j)), out_shape=SDS((t, c), BF16),
        scratch_shapes=[pltpu.VMEM((HALO + tm, cw), F32)], compiler_params=_cp("parallel", "arbitrary"),
    )(proj, proj, w, b.reshape(1, c))


def _conv_bwd(proj, off, w, b, dxc, *, name):
    t = proj.shape[0]
    c = w.shape[1]
    cw = _conv_block(c, off)
    tm = _pick(t, (512, 256, 128))
    ob = off // cw
    nt = t // tm

    def body_pre(x_ref, halo_ref, w_ref, b_ref, d_ref, dp_ref, dw_ref, db_ref, ext):
        i = pl.program_id(1)

        @pl.when(i == 0)
        def _():
            dw_ref[...] = jnp.zeros_like(dw_ref)
            db_ref[...] = jnp.zeros_like(db_ref)

        db_acc = jnp.zeros((8, cw), F32)
        dw_acc = [jnp.zeros((8, cw), F32) for _ in range(CONV_WIDTH)]
        for r0, pre, xs in _conv_pre(x_ref, halo_ref, w_ref, b_ref, ext, i, tm):
            dpre = d_ref[pl.ds(r0, CONV_ROWS), :] * _dsilu(pre)
            dp_ref[pl.ds(r0, CONV_ROWS), :] = dpre
            db_acc = db_acc + _fold8(dpre)
            dw_acc = [acc + _fold8(dpre * x) for acc, x in zip(dw_acc, xs)]
        db_ref[...] += jnp.sum(db_acc, axis=0, keepdims=True)
        for k in range(CONV_WIDTH):
            dw_ref[pl.ds(k, 1), :] += jnp.sum(dw_acc[k], axis=0, keepdims=True)

    dpre, dw, db = pl.pallas_call(
        body_pre, name=name + "_pre", grid=(c // cw, nt),
        in_specs=[BS((tm, cw), lambda j, i: (i, ob + j)),
                  BS((HALO, cw), lambda j, i: (jnp.maximum(i * (tm // HALO) - 1, 0), ob + j)),
                  BS((CONV_WIDTH, cw), lambda j, i: (0, j)), BS((1, cw), lambda j, i: (0, j)),
                  BS((tm, cw), lambda j, i: (i, j))],
        out_specs=[BS((tm, cw), lambda j, i: (i, j)), BS((CONV_WIDTH, cw), lambda j, i: (0, j)), BS((1, cw), lambda j, i: (0, j))],
        out_shape=[SDS((t, c), F32), SDS((CONV_WIDTH, c), F32), SDS((1, c), F32)],
        scratch_shapes=[pltpu.VMEM((HALO + tm, cw), F32)], compiler_params=_cp("parallel", "arbitrary"),
    )(proj, proj, w, b.reshape(1, c), dxc)

    def body_in(dp_ref, nxt_ref, w_ref, dx_ref, ext):
        i = pl.program_id(1)
        ext[pl.ds(0, tm), :] = dp_ref[...]
        ext[pl.ds(tm, 8), :] = jnp.where(i < nt - 1, nxt_ref[...], 0.0)
        taps = [w_ref[pl.ds(k, 1), :] for k in range(CONV_WIDTH)]
        for r0 in range(0, tm, CONV_ROWS):
            dx = taps[CONV_WIDTH - 1] * ext[pl.ds(r0, CONV_ROWS), :]
            for k in range(CONV_WIDTH - 1):
                dx = dx + taps[k] * ext[pl.ds(r0 + CONV_WIDTH - 1 - k, CONV_ROWS), :]
            dx_ref[pl.ds(r0, CONV_ROWS), :] = dx.astype(BF16)

    dx = pl.pallas_call(
        body_in, name=name + "_in", grid=(c // cw, nt),
        in_specs=[BS((tm, cw), lambda j, i: (i, j)),
                  BS((8, cw), lambda j, i: (jnp.minimum((i + 1) * (tm // 8), t // 8 - 1), j)),
                  BS((CONV_WIDTH, cw), lambda j, i: (0, j))],
        out_specs=BS((tm, cw), lambda j, i: (i, j)), out_shape=SDS((t, c), BF16),
        scratch_shapes=[pltpu.VMEM((tm + 8, cw), F32)], compiler_params=_cp("parallel", "arbitrary"),
    )(dpre, dpre, w)
    return dx, dw, db[0]


def _softplus(x):
    return jnp.maximum(x, 0.0) + jnp.log(1.0 + jnp.exp(-jnp.abs(x)))


def _tril():
    return lax.broadcasted_iota(jnp.int32, (CHUNK, CHUNK), 0) >= lax.broadcasted_iota(jnp.int32, (CHUNK, CHUNK), 1)


def _ssd_prep(dtr, dt_bias, a, *, name):
    t = dtr.shape[0]
    nc = t // CHUNK
    cb = _pick(nc, (4, 2, 1))

    def body(dtr_ref, bias_ref, a_ref, o_ref):
        tril = _tril().astype(F32)
        for ci in range(cb):
            rows = pl.ds(ci * CHUNK, CHUNK)
            pre = dtr_ref[rows, :] + bias_ref[...]
            dt = _softplus(pre)
            la = _dot(tril, dt * a_ref[...], precision=HIGHEST)
            for k, v in enumerate((dt, la, la.T, _sigmoid(pre))):
                o_ref[rows, k * LANES:(k + 1) * LANES] = v

    vec = BS((1, LANES), lambda i: (0, 0))
    return pl.pallas_call(
        body, name=name, grid=(nc // cb,), in_specs=[BS((cb * CHUNK, LANES), lambda i: (i, 0)), vec, vec],
        out_specs=BS((cb * CHUNK, 4 * LANES), lambda i: (i, 0)), out_shape=SDS((t, 4 * LANES), F32),
        compiler_params=_cp("parallel"),
    )(dtr, dt_bias, a)


def _ssd_common(time_ref):
    part = lambda k: time_ref[:, k * LANES:(k + 1) * LANES]
    return part(3), part(0), part(1), part(2), _tril()


def _lane_col(x, lane, h):
    return jnp.sum(jnp.where(lane == h, x, 0.0), axis=-1, keepdims=True)


def _ssd_specs(rows, di, gw, cidx):
    nbx = di // LANES
    return [BS((rows, gw), lambda g, c: (cidx(c), g)),
            BS((rows, D_STATE), lambda g, c: (cidx(c), nbx + g)),
            BS((rows, D_STATE), lambda g, c: (cidx(c), nbx + SSM_GROUPS + g)),
            BS((rows, 4 * LANES), lambda g, c: (cidx(c), 0)),
            BS((1, LANES), lambda g, c: (0, 0)),
            BS((1, gw), lambda g, c: (0, g))]


def _chunk_rows(ci):
    return pl.ds(pl.multiple_of(ci * CHUNK, CHUNK), CHUNK)


def _ssd_fwd(xc, prep, a, dskip, *, name):
    t = xc.shape[0]
    di = xc.shape[1] - 2 * SSM_GROUPS * D_STATE
    gw = di // SSM_GROUPS
    hpg = gw // HEAD_DIM
    npair = gw // LANES
    nc = t // CHUNK
    cb = _pick(nc, (4, 2, 1))

    def body(xb_ref, bb_ref, cb_ref, timeb_ref, a_ref, dsk_ref, yb_ref, st_ref, state):
        @pl.when(pl.program_id(1) == 0)
        def _():
            state[...] = jnp.zeros_like(state)

        g = pl.program_id(0)

        def one_chunk(ci, carry):
            rows = _chunk_rows(ci)
            ssd_chunk(xb_ref.at[rows], bb_ref.at[rows], cb_ref.at[rows], timeb_ref.at[rows], dsk_ref,
                      yb_ref.at[rows], st_ref, state, ci, g)
            return carry

        lax.fori_loop(0, cb, one_chunk, 0)

    def ssd_chunk(x_ref, b_ref, c_ref, time_ref, dsk_ref, y_ref, st_ref, state, ci, g):
        st_ref[ci, 0] = state[...]
        _, dt, la, la_t, tril = _ssd_common(time_ref)
        lane = lax.broadcasted_iota(jnp.int32, (CHUNK, LANES), 1)
        sub = lax.broadcasted_iota(jnp.int32, (LANES, CHUNK), 0)
        lane1 = lax.broadcasted_iota(jnp.int32, (1, LANES), 1)
        low, low1 = lane < HEAD_DIM, lane1 < HEAD_DIM
        last = lax.broadcasted_iota(jnp.int32, (CHUNK, LANES), 0) == CHUNK - 1
        lend = jnp.sum(jnp.where(last, la, 0.0), axis=0, keepdims=True)
        bm, cm = b_ref[...], c_ref[...]
        gmat = _dot(cm, bm, NT)
        for p in range(npair):
            sl = slice(p * LANES, (p + 1) * LANES)
            ps = slice(p * D_STATE, (p + 1) * D_STATE)
            x2 = x_ref[:, sl].astype(F32)
            cols, ms = [], []
            for hh in range(2):
                h = g * hpg + p * 2 + hh
                col_la = _lane_col(la, lane, h)
                row_la = jnp.sum(jnp.where(sub == h, la_t, 0.0), axis=0, keepdims=True)
                lend_h = _lane_col(lend, lane1, h)
                decay = jnp.exp(jnp.where(tril, col_la - row_la, -jnp.inf))
                ms.append((gmat * decay).astype(BF16))
                cols.append((_lane_col(dt, lane, h), jnp.exp(col_la), jnp.exp(lend_h - col_la), jnp.exp(lend_h)))
            pair = lambda k: jnp.where(low, cols[0][k], cols[1][k])
            xdt = x2 * pair(0)
            xdtb = xdt.astype(BF16)
            s2 = state[ps, :]
            yy = _dot(jnp.concatenate(ms, axis=0), xdtb)
            y = jnp.where(low, yy[:CHUNK], yy[CHUNK:])
            y = y + pair(1) * _dot(cm, s2.astype(BF16)) + x2 * dsk_ref[:, sl]
            y_ref[:, sl] = y
            state[ps, :] = s2 * jnp.where(low1, cols[0][3], cols[1][3]) + _dot(bm, (xdt * pair(2)).astype(BF16), TN)

    y, st = pl.pallas_call(
        body, name=name, grid=(SSM_GROUPS, nc // cb), in_specs=_ssd_specs(cb * CHUNK, di, gw, lambda c: c),
        out_specs=[BS((cb * CHUNK, gw), lambda g, c: (c, g)), BS((cb, 1, npair * D_STATE, LANES), lambda g, c: (c, g, 0, 0))],
        out_shape=[SDS((t, di), F32), SDS((nc, SSM_GROUPS, npair * D_STATE, LANES), F32)],
        scratch_shapes=[pltpu.VMEM((npair * D_STATE, LANES), F32)], compiler_params=_cp("parallel", "arbitrary"),
    )(xc, xc, xc, prep, a, dskip)
    return y, st


def _ssd_bwd(xc, prep, a, dskip, st, dy, *, name):
    t = xc.shape[0]
    di = xc.shape[1] - 2 * SSM_GROUPS * D_STATE
    gw = di // SSM_GROUPS
    hpg = gw // HEAD_DIM
    npair = gw // LANES
    nc = t // CHUNK
    cb = _pick(nc, (4, 2, 1))
    rev = lambda c: nc // cb - 1 - c

    def body(xb_ref, bb_ref, cb_ref, timeb_ref, a_ref, dsk_ref, st_ref, dyb_ref,
             dxb_ref, dbb_ref, dcb_ref, ddtrb_ref, da_ref, dbias_ref, ddsk_ref, dstate):
        @pl.when(pl.program_id(1) == 0)
        def _():
            dstate[...] = jnp.zeros_like(dstate)
            da_ref[...] = jnp.zeros_like(da_ref)
            dbias_ref[...] = jnp.zeros_like(dbias_ref)
            ddsk_ref[...] = jnp.zeros_like(ddsk_ref)

        g = pl.program_id(0)

        def one_chunk(j, carry):
            ci = cb - 1 - j
            rows = _chunk_rows(ci)
            ssd_chunk(xb_ref.at[rows], bb_ref.at[rows], cb_ref.at[rows], timeb_ref.at[rows], a_ref, dsk_ref,
                      st_ref.at[ci], dyb_ref.at[rows], dxb_ref.at[rows], dbb_ref.at[rows], dcb_ref.at[rows],
                      ddtrb_ref.at[:, rows], da_ref, dbias_ref, ddsk_ref, dstate, g)
            return carry

        lax.fori_loop(0, cb, one_chunk, 0)

    def ssd_chunk(x_ref, b_ref, c_ref, time_ref, a_ref, dsk_ref, st_ref, dy_ref,
                  dx_ref, db_ref, dc_ref, ddtr_ref, da_ref, dbias_ref, ddsk_ref, dstate, g):
        sig, dt, la, la_t, tril = _ssd_common(time_ref)
        lane = lax.broadcasted_iota(jnp.int32, (CHUNK, LANES), 1)
        sub = lax.broadcasted_iota(jnp.int32, (LANES, CHUNK), 0)
        lane1 = lax.broadcasted_iota(jnp.int32, (1, LANES), 1)
        low, low1 = lane < HEAD_DIM, lane1 < HEAD_DIM
        last = lax.broadcasted_iota(jnp.int32, (CHUNK, LANES), 0) == CHUNK - 1
        lend = jnp.sum(jnp.where(last, la, 0.0), axis=0, keepdims=True)
        bm, cm = b_ref[...], c_ref[...]
        gmat = _dot(cm, bm, NT)
        dg = jnp.zeros((CHUNK, CHUNK), F32)
        dla_cols = jnp.zeros((CHUNK, LANES), F32)
        dla_rows = jnp.zeros((LANES, CHUNK), F32)
        dtsum = jnp.zeros((CHUNK, LANES), F32)
        dbm = jnp.zeros((CHUNK, D_STATE), F32)
        dcm = jnp.zeros((CHUNK, D_STATE), F32)
        for p in range(npair):
            sl = slice(p * LANES, (p + 1) * LANES)
            ps = slice(p * D_STATE, (p + 1) * D_STATE)
            x2 = x_ref[:, sl].astype(F32)
            dy2 = dy_ref[:, sl]
            s2 = st_ref[0, ps, :]
            ds2 = dstate[ps, :]
            hs, cols, ms, decays = [], [], [], []
            for hh in range(2):
                h = g * hpg + p * 2 + hh
                col_la = _lane_col(la, lane, h)
                row_la = jnp.sum(jnp.where(sub == h, la_t, 0.0), axis=0, keepdims=True)
                lend_h = _lane_col(lend, lane1, h)
                decay = jnp.exp(jnp.where(tril, col_la - row_la, -jnp.inf))
                hs.append(h)
                decays.append(decay)
                ms.append(gmat * decay)
                cols.append((_lane_col(dt, lane, h), jnp.exp(col_la), jnp.exp(lend_h - col_la), jnp.exp(lend_h)))
            pair = lambda k: jnp.where(low, cols[0][k], cols[1][k])
            dtc, ec, eend = pair(0), pair(1), pair(2)
            eend_s = jnp.where(low1, cols[0][3], cols[1][3])
            xdt = x2 * dtc
            xdtb = xdt.astype(BF16)
            dys = dy2 * ec
            dysb = dys.astype(BF16)
            dxdt_state = eend * _dot(bm, ds2.astype(BF16))
            inter = dys * _dot(cm, s2.astype(BF16))
            u = dxdt_state * xdt
            sds = s2 * ds2
            dys2 = _stack_heads(dy2.astype(BF16), low)
            dxdt = dxdt_state + _dot(jnp.concatenate([ms[0].astype(BF16), ms[1].astype(BF16)], axis=0), dys2, TN)
            dms = _dot(dys2, xdtb, NT)
            for hh in range(2):
                h = hs[hh]
                mh = low if hh == 0 else jnp.logical_not(low)
                dm = dms[hh * CHUNK:(hh + 1) * CHUNK]
                w = dm * ms[hh]
                dg = dg + dm * decays[hh]
                u_col = jnp.sum(jnp.where(mh, u, 0.0), axis=-1, keepdims=True)
                dlend = jnp.sum(u_col, axis=0, keepdims=True) + cols[hh][3] * jnp.sum(jnp.where(low1 if hh == 0 else jnp.logical_not(low1), jnp.sum(sds, axis=0, keepdims=True), 0.0), axis=-1, keepdims=True)
                col = jnp.sum(w, axis=-1, keepdims=True) + jnp.sum(jnp.where(mh, inter, 0.0), axis=-1, keepdims=True) - u_col
                dla_cols = dla_cols + jnp.where(lane == h, col + jnp.where(last, dlend, 0.0), 0.0)
                dla_rows = dla_rows - jnp.where(sub == h, jnp.sum(w, axis=0, keepdims=True), 0.0)
            for hh in range(2):
                mh = low if hh == 0 else jnp.logical_not(low)
                dtsum = dtsum + jnp.where(lane == hs[hh], jnp.sum(jnp.where(mh, dxdt * x2, 0.0), axis=-1, keepdims=True), 0.0)
            dcm = dcm + _dot(dysb, s2.astype(BF16), NT)
            dbm = dbm + _dot((xdt * eend).astype(BF16), ds2.astype(BF16), NT)
            dstate[ps, :] = ds2 * eend_s + _dot(cm, dysb, TN)
            dx_ref[:, sl] = dxdt * dtc + dy2 * dsk_ref[:, sl]
            ddsk_ref[:, sl] += jnp.sum(dy2 * x2, axis=0, keepdims=True)
        dgb = dg.astype(BF16)
        dc_ref[...] = dcm + _dot(dgb, bm)
        db_ref[...] = dbm + _dot(dgb, cm, TN)
        dla = dla_cols + dla_rows.T
        triu = lax.broadcasted_iota(jnp.int32, (CHUNK, CHUNK), 0) <= lax.broadcasted_iota(jnp.int32, (CHUNK, CHUNK), 1)
        ddta = _dot(triu.astype(F32), dla, precision=HIGHEST)
        ddt = ddta * a_ref[...] + dtsum
        da_ref[0] += jnp.sum(ddta * dt, axis=0, keepdims=True)
        ddtr = ddt * sig
        ddtr_ref[0] = ddtr
        dbias_ref[0] += jnp.sum(ddtr, axis=0, keepdims=True)

    vec = BS((1, 1, LANES), lambda g, c: (g, 0, 0))
    outs = pl.pallas_call(
        body, name=name, grid=(SSM_GROUPS, nc // cb),
        in_specs=_ssd_specs(cb * CHUNK, di, gw, rev) + [BS((cb, 1, npair * D_STATE, LANES), lambda g, c: (rev(c), g, 0, 0)),
                                                        BS((cb * CHUNK, gw), lambda g, c: (rev(c), g))],
        out_specs=[BS((cb * CHUNK, gw), lambda g, c: (rev(c), g)), BS((cb * CHUNK, D_STATE), lambda g, c: (rev(c), g)),
                   BS((cb * CHUNK, D_STATE), lambda g, c: (rev(c), g)), BS((1, cb * CHUNK, LANES), lambda g, c: (g, rev(c), 0)),
                   vec, vec, BS((1, gw), lambda g, c: (0, g))],
        out_shape=[SDS((t, di), F32), SDS((t, SSM_GROUPS * D_STATE), F32), SDS((t, SSM_GROUPS * D_STATE), F32),
                   SDS((SSM_GROUPS, t, LANES), F32), SDS((SSM_GROUPS, 1, LANES), F32), SDS((SSM_GROUPS, 1, LANES), F32),
                   SDS((1, di), F32)],
        scratch_shapes=[pltpu.VMEM((npair * D_STATE, LANES), F32)], compiler_params=_cp("parallel", "arbitrary"),
    )(xc, xc, xc, prep, a, dskip, st, dy)
    return outs


def _gate_norm_fwd(y, proj, zoff, w, *, name):
    t, di = y.shape
    gw = di // SSM_GROUPS
    tm = _pick(t, (512, 256, 128))
    zb = zoff // gw

    def body(y_ref, z_ref, w_ref, o_ref):
        yg = y_ref[...] * _silu(z_ref[...].astype(F32))
        r = lax.rsqrt(jnp.mean(yg * yg, axis=-1, keepdims=True) + EPS)
        o_ref[...] = (yg * r * w_ref[...]).astype(BF16)

    return pl.pallas_call(
        body, name=name, grid=(t // tm, SSM_GROUPS),
        in_specs=[BS((tm, gw), lambda i, g: (i, g)), BS((tm, gw), lambda i, g: (i, zb + g)), BS((1, gw), lambda i, g: (0, g))],
        out_specs=BS((tm, gw), lambda i, g: (i, g)), out_shape=SDS((t, di), BF16), compiler_params=_cp("parallel", "parallel"),
    )(y, proj, w.reshape(1, di))


def _gate_norm_bwd(dssm, y, proj, zoff, w, *, name):
    t, di = y.shape
    gw = di // SSM_GROUPS
    tm = _pick(t, (512, 256, 128))
    zb = zoff // gw

    def body(d_ref, y_ref, z_ref, w_ref, dy_ref, dz_ref, dw_ref):
        z = z_ref[...].astype(F32)
        yv = y_ref[...]
        sz = _silu(z)
        yg = yv * sz
        r = lax.rsqrt(jnp.mean(yg * yg, axis=-1, keepdims=True) + EPS)
        n = yg * r
        d = d_ref[...]
        dn = d * w_ref[...]
        dyg = r * (dn - n * jnp.mean(dn * n, axis=-1, keepdims=True))
        dy_ref[...] = dyg * sz
        dz_ref[...] = (dyg * yv * _dsilu(z)).astype(BF16)

        @pl.when(pl.program_id(1) == 0)
        def _():
            dw_ref[...] = jnp.zeros_like(dw_ref)

        dw_ref[...] += jnp.sum(d * n, axis=0, keepdims=True)

    blk = BS((tm, gw), lambda g, i: (i, g))
    dy, dz, dw = pl.pallas_call(
        body, name=name, grid=(SSM_GROUPS, t // tm),
        in_specs=[blk, blk, BS((tm, gw), lambda g, i: (i, zb + g)), BS((1, gw), lambda g, i: (0, g))],
        out_specs=[blk, blk, BS((1, gw), lambda g, i: (0, g))],
        out_shape=[SDS((t, di), F32), SDS((t, di), BF16), SDS((1, di), F32)], compiler_params=_cp("parallel", "arbitrary"),
    )(dssm, y, proj, w.reshape(1, di))
    return dy, dz, dw[0]


def _merge_fwd(proj, goff, ga, gs, *, name):
    t, d = ga.shape
    cw = _conv_block(d, goff)
    tm = _pick(t, (512, 256, 128))
    gb = goff // cw

    def body(g0, g1, a_ref, s_ref, o_ref):
        o_ref[...] = (_sigmoid(g0[...].astype(F32)) * a_ref[...] + _sigmoid(g1[...].astype(F32)) * s_ref[...]).astype(BF16)

    blk = BS((tm, cw), lambda i, j: (i, j))
    return pl.pallas_call(
        body, name=name, grid=(t // tm, d // cw),
        in_specs=[BS((tm, cw), lambda i, j: (i, gb + j)), BS((tm, cw), lambda i, j: (i, gb + d // cw + j)), blk, blk],
        out_specs=blk, out_shape=SDS((t, d), BF16), compiler_params=_cp("parallel", "parallel"),
    )(proj, proj, ga, gs)


def _merge_bwd(proj, goff, ga, gs, dm, *, name):
    t, d = ga.shape
    cw = _conv_block(d, goff)
    tm = _pick(t, (512, 256, 128))
    gb = goff // cw

    def body(g0, g1, a_ref, s_ref, dm_ref, da_ref, ds_ref, dg0_ref, dg1_ref):
        dmv = dm_ref[...]
        s0 = _sigmoid(g0[...].astype(F32))
        s1 = _sigmoid(g1[...].astype(F32))
        da_ref[...] = (s0 * dmv).astype(BF16)
        ds_ref[...] = (s1 * dmv).astype(BF16)
        dg0_ref[...] = (dmv * a_ref[...] * s0 * (1.0 - s0)).astype(BF16)
        dg1_ref[...] = (dmv * s_ref[...] * s1 * (1.0 - s1)).astype(BF16)

    blk = BS((tm, cw), lambda i, j: (i, j))
    return pl.pallas_call(
        body, name=name, grid=(t // tm, d // cw),
        in_specs=[BS((tm, cw), lambda i, j: (i, gb + j)), BS((tm, cw), lambda i, j: (i, gb + d // cw + j)), blk, blk, blk],
        out_specs=[blk] * 4, out_shape=[SDS((t, d), BF16)] * 4, compiler_params=_cp("parallel", "parallel"),
    )(proj, proj, ga, gs, dm)


def _swiglu_fwd(u, *, name):
    t, two_f = u.shape
    f = two_f // 2
    cw = _wide(f)
    tm = _pick(t, (512, 256, 128))

    def body(g_ref, u_ref, o_ref):
        o_ref[...] = (_silu(g_ref[...].astype(F32)) * u_ref[...].astype(F32)).astype(BF16)

    return pl.pallas_call(
        body, name=name, grid=(t // tm, f // cw),
        in_specs=[BS((tm, cw), lambda i, j: (i, j)), BS((tm, cw), lambda i, j: (i, f // cw + j))],
        out_specs=BS((tm, cw), lambda i, j: (i, j)), out_shape=SDS((t, f), BF16), compiler_params=_cp("parallel", "parallel"),
    )(u, u)


def _swiglu_bwd(u, df, *, name):
    t, two_f = u.shape
    f = two_f // 2
    cw = _wide(f)
    tm = _pick(t, (256, 128))

    def body(u_ref, d_ref, o_ref):
        for j in range(f // cw):
            gate, up = slice(j * cw, (j + 1) * cw), slice(f + j * cw, f + (j + 1) * cw)
            gt = u_ref[:, gate].astype(F32)
            d = d_ref[:, gate].astype(F32)
            o_ref[:, gate] = (d * u_ref[:, up].astype(F32) * _dsilu(gt)).astype(BF16)
            o_ref[:, up] = (d * _silu(gt)).astype(BF16)

    return pl.pallas_call(
        body, name=name, grid=(t // tm,), in_specs=[BS((tm, two_f), lambda i: (i, 0)), BS((tm, f), lambda i: (i, 0))],
        out_specs=BS((tm, two_f), lambda i: (i, 0)), out_shape=SDS((t, two_f), BF16), compiler_params=_cp("parallel"),
    )(u, df)


def _row_block(rows, cols, n_arrays):
    budget = VMEM_LIMIT_BYTES // 3
    for tr in (512, 256, 128, 64, 32, 16, 8):
        if rows % tr == 0 and tr * cols * 4 * n_arrays * 2 <= budget:
            return tr
    raise ValueError((rows, cols))


def _concat_cols(pieces, *, name):
    pieces = [p if isinstance(p, tuple) else (p, p.shape[1], 0) for p in pieces]
    rows, dtype = pieces[0][0].shape[0], pieces[0][0].dtype
    widths = [w for _, w, _ in pieces]
    total = sum(widths)
    assert all(w % LANES == 0 for w in widths) and all(a.dtype == dtype and a.shape[0] == rows for a, _, _ in pieces)
    tr = next(c for c in (512, 256, 128, 64, 32, 16) if rows % c == 0 and 4 * c * total * dtype.itemsize <= VMEM_LIMIT_BYTES // 2)

    def body(*refs):
        o_ref = refs[-1]
        off = 0
        for p_ref, w in zip(refs[:-1], widths):
            o_ref[:, off:off + w] = p_ref[...]
            off += w

    return pl.pallas_call(
        body, name=name, grid=(rows // tr,), in_specs=[BS((tr, w), lambda i, j=j: (i, j)) for _, w, j in pieces],
        out_specs=BS((tr, total), lambda i: (i, 0)), out_shape=SDS((rows, total), dtype), compiler_params=_cp("parallel"),
    )(*[a for a, _, _ in pieces])


def _add_own_layer(g0, g1, got, core, *, name):
    rows, cols = got.shape
    tr = _row_block(rows, cols, 4)

    def body(core_ref, g0_ref, g1_ref, got_ref, o_ref):
        o_ref[...] = (jnp.where(core_ref[0] == 0, g0_ref[...], g1_ref[...]) + got_ref[...]).astype(o_ref.dtype)

    blk = BS((tr, cols), lambda i, cr: (i, 0))
    grid_spec = pltpu.PrefetchScalarGridSpec(
        num_scalar_prefetch=1, grid=(rows // tr,),
        in_specs=[BS((tr, cols), lambda i, cr: (i * (1 - cr[0]), 0)), BS((tr, cols), lambda i, cr: (i * cr[0], 0)), blk],
        out_specs=blk)
    return pl.pallas_call(body, name=name, grid_spec=grid_spec, out_shape=SDS((rows, cols), BF16),
                          compiler_params=_cp("arbitrary"))(core, g0, g1, got)


def _sum_chips(a, *, name):
    _, rows, cols = a.shape
    tr = _row_block(rows, cols, 5)

    def body(a_ref, o_ref):
        o_ref[...] = ((a_ref[0].astype(F32) + a_ref[1].astype(F32)) + a_ref[2].astype(F32)) + a_ref[3].astype(F32)

    return pl.pallas_call(body, name=name, grid=(rows // tr,), in_specs=[BS((N_CHIPS, tr, cols), lambda i: (0, i, 0))],
                          out_specs=BS((tr, cols), lambda i: (i, 0)), out_shape=SDS((rows, cols), F32),
                          compiler_params=_cp("parallel"))(a)


def _adamw(w, g, m, v, *, name):
    rows, cols = w.shape
    tr = _row_block(rows, cols, 7) if rows % 8 == 0 else rows
    c1 = 1.0 - ADAM_B1 ** ADAM_STEP
    c2 = 1.0 - ADAM_B2 ** ADAM_STEP

    def body(w_ref, g_ref, m_ref, v_ref, d_ref, nm_ref, nv_ref):
        gv = g_ref[...]
        nm = ADAM_B1 * m_ref[...] + (1.0 - ADAM_B1) * gv
        nv = ADAM_B2 * v_ref[...] + (1.0 - ADAM_B2) * (gv * gv)
        d_ref[...] = -ADAM_LR * ((nm / c1) / (jnp.sqrt(nv / c2) + ADAM_EPS) + ADAM_WD * w_ref[...])
        nm_ref[...] = nm
        nv_ref[...] = nv

    blk = BS((tr, cols), lambda i: (i, 0))
    return pl.pallas_call(body, name=name, grid=(rows // tr,), in_specs=[blk] * 4, out_specs=[blk] * 3,
                          out_shape=[SDS((rows, cols), F32)] * 3, compiler_params=_cp("parallel"))(w, g, m, v)


def _adamw_layers(w, g_own, g_other, m, v, core, *, name):
    _, rows, cols = w.shape
    tr = _row_block(rows, cols, 9)
    c1 = 1.0 - ADAM_B1 ** ADAM_STEP
    c2 = 1.0 - ADAM_B2 ** ADAM_STEP

    def body(core_ref, w_ref, own_ref, oth_ref, m_ref, v_ref, g_ref, d_ref, nm_ref, nv_ref):
        gv = jnp.where(pl.program_id(0) == core_ref[0], own_ref[...], oth_ref[...])
        nm = ADAM_B1 * m_ref[0] + (1.0 - ADAM_B1) * gv
        nv = ADAM_B2 * v_ref[0] + (1.0 - ADAM_B2) * (gv * gv)
        g_ref[0] = gv
        d_ref[0] = -ADAM_LR * ((nm / c1) / (jnp.sqrt(nv / c2) + ADAM_EPS) + ADAM_WD * w_ref[0])
        nm_ref[0] = nm
        nv_ref[0] = nv

    own_here = lambda l, cr: 1 - (l - cr[0]) * (l - cr[0])
    slab = BS((1, tr, cols), lambda l, i, cr: (l, i, 0))
    grid_spec = pltpu.PrefetchScalarGridSpec(
        num_scalar_prefetch=1, grid=(2, rows // tr),
        in_specs=[slab, BS((tr, cols), lambda l, i, cr: (i * own_here(l, cr), 0)),
                  BS((tr, cols), lambda l, i, cr: (i * (1 - own_here(l, cr)), 0)), slab, slab],
        out_specs=[slab] * 4)
    return pl.pallas_call(body, name=name, grid_spec=grid_spec, out_shape=[SDS(w.shape, F32)] * 4,
                          compiler_params=_cp("arbitrary", "arbitrary"))(core, w, g_own, g_other, m, v)


ANY = BS(memory_space=pl.ANY)


def _place():
    x, y, c = lax.axis_index("x"), lax.axis_index("y"), lax.axis_index("c")
    return x, y, c, [(1 - x, y), (x, 1 - y), (1 - x, 1 - y)]


def _gather_shards(arrs, *, name):
    n = len(arrs)

    def body(*refs):
        ins, outs = refs[:n], refs[n:2 * n]
        send_sems, recv_sems, pass_send_sems, pass_recv_sems = refs[2 * n:]
        x, y, c, chips = _place()
        s = 2 * x + y

        def ici(i, j, src_chip, to):
            src = ins[i].at[c] if src_chip is None else outs[i].at[src_chip, c]
            return pltpu.make_async_remote_copy(
                src_ref=src, dst_ref=outs[i].at[s if src_chip is None else src_chip, c], send_sem=send_sems.at[i * 3 + j],
                recv_sem=recv_sems.at[i * 3 + j], device_id=to, device_id_type=MESH)

        def d2d(i, j, src_chip, layer):
            slab = outs[i].at[src_chip, layer]
            return pltpu.make_async_remote_copy(
                src_ref=slab, dst_ref=slab, send_sem=pass_send_sems.at[i * 3 + j], recv_sem=pass_recv_sems.at[i * 3 + j],
                device_id=(x, y, 1 - c), device_id_type=MESH)

        sent = []
        for i in range(n):
            for j, (px, py) in enumerate(chips):
                cp = ici(i, j, None, (px, py, c))
                cp.start()
                sent.append(cp)
        passed = []
        for i in range(n):
            for j, (px, py) in enumerate(chips):
                ici(i, j, 2 * px + py, (x, y, c)).wait_recv()
                cp = d2d(i, j, 2 * px + py, c)
                cp.start()
                passed.append(cp)
        for i in range(n):
            for j, (px, py) in enumerate(chips):
                d2d(i, j, 2 * px + py, 1 - c).wait_recv()
        for cp in sent + passed:
            cp.wait_send()

    return pl.pallas_call(
        body, name=name, in_specs=[ANY] * n, out_specs=[ANY] * n,
        out_shape=[SDS((N_CHIPS,) + a.shape, a.dtype) for a in arrs],
        scratch_shapes=[pltpu.SemaphoreType.DMA((3 * n,))] * 4,
    )(*arrs)


def _pair_swap_layers(layer0, layer1, *, name):
    n = len(layer0)

    def body(*refs):
        in0, in1, outs = refs[:n], refs[n:2 * n], refs[2 * n:3 * n]
        send_sems, recv_sems = refs[3 * n:]
        x, y, c, _ = _place()

        def copy(src, i):
            return pltpu.make_async_remote_copy(
                src_ref=src[i], dst_ref=outs[i], send_sem=send_sems.at[i], recv_sem=recv_sems.at[i],
                device_id=(x, y, 1 - c), device_id_type=MESH)

        @pl.when(c == 0)
        def _():
            for i in range(n):
                copy(in1, i).start()

        @pl.when(c == 1)
        def _():
            for i in range(n):
                copy(in0, i).start()

        for i in range(n):
            copy(in0, i).wait()

    return pl.pallas_call(
        body, name=name, in_specs=[ANY] * (2 * n), out_specs=[ANY] * n, out_shape=[SDS(a.shape, a.dtype) for a in layer0],
        scratch_shapes=[pltpu.SemaphoreType.DMA((n,)), pltpu.SemaphoreType.DMA((n,))],
    )(*layer0, *layer1)


def _scatter_to_chips(arrs, *, name):
    n = len(arrs)

    def body(*refs):
        ins, outs = refs[:n], refs[n:2 * n]
        send_sems, recv_sems = refs[2 * n:]
        x, y, c, chips = _place()
        s = 2 * x + y
        copies = []
        for i in range(n):
            for j, (px, py) in enumerate(chips):
                cp = pltpu.make_async_remote_copy(
                    src_ref=ins[i].at[2 * px + py], dst_ref=outs[i].at[s], send_sem=send_sems.at[i * 3 + j],
                    recv_sem=recv_sems.at[i * 3 + j], device_id=(px, py, c), device_id_type=MESH)
                cp.start()
                copies.append(cp)
        for cp in copies:
            cp.wait()

    return pl.pallas_call(
        body, name=name, in_specs=[ANY] * n, out_specs=[ANY] * n, out_shape=[SDS(a.shape, a.dtype) for a in arrs],
        scratch_shapes=[pltpu.SemaphoreType.DMA((3 * n,)), pltpu.SemaphoreType.DMA((3 * n,))],
    )(*arrs)


def _pair_swap(arrs, *, name):
    n = len(arrs)

    def body(*refs):
        ins, outs = refs[:n], refs[n:2 * n]
        send_sems, recv_sems = refs[2 * n:]
        x, y, c, _ = _place()
        copies = []
        for i in range(n):
            cp = pltpu.make_async_remote_copy(
                src_ref=ins[i], dst_ref=outs[i], send_sem=send_sems.at[i], recv_sem=recv_sems.at[i],
                device_id=(x, y, 1 - c), device_id_type=MESH)
            cp.start()
            copies.append(cp)
        for cp in copies:
            cp.wait()

    return pl.pallas_call(
        body, name=name, in_specs=[ANY] * n, out_specs=[ANY] * n, out_shape=[SDS(a.shape, a.dtype) for a in arrs],
        scratch_shapes=[pltpu.SemaphoreType.DMA((n,)), pltpu.SemaphoreType.DMA((n,))],
    )(*arrs)


def _allreduce_small(v, *, name):
    rows, cols = v.shape

    def body(v_ref, o_ref, gath, send_sems, recv_sems):
        x, y, c, _ = _place()
        me = 4 * x + 2 * y + c
        gath[me] = v_ref[...]
        copies = []
        for k in range(1, N_DEV):
            fx, fy, fc = (k >> 2) & 1, (k >> 1) & 1, k & 1
            peer = (1 - x if fx else x, 1 - y if fy else y, 1 - c if fc else c)
            cp = pltpu.make_async_remote_copy(
                src_ref=v_ref, dst_ref=gath.at[me], send_sem=send_sems.at[k - 1], recv_sem=recv_sems.at[k - 1],
                device_id=peer, device_id_type=MESH)
            cp.start()
            copies.append(cp)
        for cp in copies:
            cp.wait()
        acc = gath[0]
        for k in range(1, N_DEV):
            acc = acc + gath[k]
        o_ref[...] = acc

    vm = BS(memory_space=pltpu.VMEM)
    return pl.pallas_call(
        body, name=name, in_specs=[vm], out_specs=vm, out_shape=SDS((rows, cols), F32),
        scratch_shapes=[pltpu.VMEM((N_DEV, rows, cols), F32), pltpu.SemaphoreType.DMA((N_DEV - 1,)), pltpu.SemaphoreType.DMA((N_DEV - 1,))],
    )(v)


def _t5_bucket(dist):
    max_exact = N_REL_BUCKETS // 2
    d_f = jnp.maximum(dist, 1).astype(F32)
    large = max_exact + (jnp.log(d_f / max_exact) / math.log(REL_MAX_DISTANCE / max_exact) * (N_REL_BUCKETS - max_exact)).astype(jnp.int32)
    return jnp.where(dist < max_exact, dist, jnp.minimum(large, N_REL_BUCKETS - 1))


def _rel_buckets(dilation):
    qi = jnp.arange(ATTN_BLOCK)[:, None]
    kj = jnp.arange(2 * ATTN_BLOCK)[None, :]
    return _t5_bucket(jnp.clip(qi + ATTN_BLOCK - kj, 0, N_STEPS) * dilation)


def _layer_fwd(h, p, biases, lname):
    sv = {"h": h}
    xn1 = _rms_fwd(h, p["norm1_w"], name=lname + "norm1")
    proj = _matmul(xn1, p["w_main"], out_dtype=BF16, name=lname + "in_proj")
    dtr = _matmul(xn1, p["w_dt"], out_dtype=F32, name=lname + "in_proj_dt")
    xn1_rm, qkvs = [xn1], [proj]
    for g in range(1, N_GROUPS_ATTN):
        xn1_rm.append(_to_residue_major(xn1, DILATIONS[g]))
        qkvs.append(_matmul(xn1_rm[g], p["w_qkv"][g], out_dtype=BF16, name=f"{lname}in_proj_qkv{g}"))
    os_, lses, lses_rm = [], [], []
    for g, d in enumerate(DILATIONS):
        o, lse = _attn_fwd(qkvs[g], (0, 1, 2), biases[g], d, name=f"{lname}attn{g}")
        os_.append(_to_token_major(o, d))
        lses.append(_to_token_major(lse, d))
        lses_rm.append(lse)
    sv.update(xn1_rm=xn1_rm, qkvs=qkvs, lses_rm=lses_rm)
    attn = _combine_fwd(os_, lses, name=lname + "attn_combine")
    xc = _conv_fwd(proj, p["off_xbc"], p["conv_w"], p["conv_b"], name=lname + "conv")
    prep = _ssd_prep(dtr, p["dt_bias"], p["a"], name=lname + "ssd_prep")
    sv["prep"] = prep
    y, st = _ssd_fwd(xc, prep, p["a"], p["dskip"], name=lname + "ssd")
    ssm = _gate_norm_fwd(y, proj, p["off_z"], p["ssm_norm_w"], name=lname + "gate_norm")
    ga = _matmul(attn, p["w_attn_branch"], name=lname + "attn_branch")
    gs = _matmul(ssm, p["w_ssm_branch"], name=lname + "ssm_branch")
    merged = _merge_fwd(proj, p["off_gate"], ga, gs, name=lname + "merge")
    h1 = _matmul(merged, p["w_out"], res=h, name=lname + "out_proj")
    xn2 = _rms_fwd(h1, p["norm2_w"], name=lname + "norm2")
    u = _matmul(xn2, p["w_ffn_in"], out_dtype=BF16, name=lname + "ffn_in")
    f = _swiglu_fwd(u, name=lname + "swiglu")
    h2 = _matmul(f, p["w_ffn_out"], res=h1, name=lname + "ffn_out")
    sv.update(xn1=xn1, proj=proj, dtr=dtr, os=os_, lses=lses, attn=attn, xc=xc, y=y, st=st, ssm=ssm, ga=ga, gs=gs,
              merged=merged, h1=h1, xn2=xn2, u=u, f=f)
    return h2, sv


def _layer_bwd(dh2, p, sv, biases, lname):
    gr = {}
    lname = lname + "bwd_"
    df = _matmul(dh2, p["w_ffn_out"], tb=True, out_dtype=BF16, name=lname + "ffn_out_dx")
    gr["w_ffn_out"] = _matmul(sv["f"], dh2, ta=True, name=lname + "ffn_out_dw")
    du = _swiglu_bwd(sv["u"], df, name=lname + "swiglu")
    dxn2 = _matmul(du, p["w_ffn_in"], tb=True, name=lname + "ffn_in_dx")
    gr["w_ffn_in"] = _matmul(sv["xn2"].T, du, name=lname + "ffn_in_dw")
    dh1, gr["norm2_w"] = _rms_bwd(sv["h1"], p["norm2_w"], [dxn2], dh2, name=lname + "norm2")
    dmerged = _matmul(dh1, p["w_out"], tb=True, name=lname + "out_proj_dx")
    gr["w_out"] = _matmul(sv["merged"], dh1, ta=True, name=lname + "out_proj_dw")
    dga, dgs, dg0, dg1 = _merge_bwd(sv["proj"], p["off_gate"], sv["ga"], sv["gs"], dmerged, name=lname + "merge")
    dattn = _matmul(dga, p["w_attn_branch"], tb=True, name=lname + "attn_branch_dx")
    gr["w_attn_branch"] = _matmul(sv["attn"], dga, ta=True, name=lname + "attn_branch_dw")
    dssm = _matmul(dgs, p["w_ssm_branch"], tb=True, name=lname + "ssm_branch_dx")
    gr["w_ssm_branch"] = _matmul(sv["ssm"], dgs, ta=True, name=lname + "ssm_branch_dw")
    dy, dz, gr["ssm_norm_w"] = _gate_norm_bwd(dssm, sv["y"], sv["proj"], p["off_z"], p["ssm_norm_w"], name=lname + "gate_norm")
    dxs, dbm, dcm, ddtr4, da4, dbias4, ddsk = _ssd_bwd(sv["xc"], sv["prep"], p["a"], p["dskip"], sv["st"], dy,
                                                       name=lname + "ssd")
    nsh = p["n_ssm_heads"]
    ddtr = jnp.sum(ddtr4, axis=0)
    gr["a_log"] = jnp.sum(da4, axis=(0, 1))[:nsh] * p["a"][0, :nsh]
    gr["dt_bias"] = jnp.sum(dbias4, axis=(0, 1))[:nsh]
    gr["d_skip"] = jnp.sum(ddsk.reshape(nsh, HEAD_DIM), axis=1)
    di = dxs.shape[1]
    dxbc, dcw, dcb = [], [], []
    for part, (lo, hi) in zip((dxs, dbm, dcm), ((0, di), (di, di + dbm.shape[1]), (di + dbm.shape[1], di + 2 * dbm.shape[1]))):
        dx_, dw_, db_ = _conv_bwd(sv["proj"], p["off_xbc"] + lo, p["conv_w"][:, lo:hi], p["conv_b"][lo:hi], part,
                                  name=f"{lname}conv{lo}")
        dxbc.append(dx_)
        dcw.append(dw_)
        dcb.append(db_)
    gr["conv_w"] = jnp.concatenate(dcw, axis=1)
    gr["conv_b"] = jnp.concatenate(dcb, axis=0)
    dos, corrs = _combine_bwd(dattn, sv["os"], sv["lses"], name=lname + "attn_combine")
    dqkvs, dbiases = [], []
    for g, d in enumerate(DILATIONS):
        dq, dk, dv, dbias = _attn_bwd(sv["qkvs"][g], (0, 1, 2), biases[g], sv["lses_rm"][g], _to_residue_major(dos[g], d),
                                      _to_residue_major(corrs[g], d), d, name=f"{lname}attn{g}")
        dqkvs.append([dq, dk, dv])
        dbiases.append(dbias)
    dmain = _concat_cols(dqkvs[0] + [dz] + dxbc + [dg0, dg1], name=lname + "in_proj_join")
    dxn1 = [_matmul(dmain, p["w_main"], tb=True, name=lname + "in_proj_dx"),
            _matmul(ddtr, p["w_dt"], tb=True, name=lname + "in_proj_dt_dx")]
    xn1_t = sv["xn1"].T
    dw_main = _matmul(xn1_t, dmain, name=lname + "in_proj_dw")
    dw_dt = _matmul(xn1_t, ddtr, name=lname + "in_proj_dt_dw")
    dw_qkv = [dw_main]
    for g in range(1, N_GROUPS_ATTN):
        dqkv = _concat_cols(dqkvs[g], name=f"{lname}in_proj_qkv{g}_join")
        dxn1.append(_to_token_major(_matmul(dqkv, p["w_qkv"][g], tb=True, name=f"{lname}in_proj_qkv{g}_dx"), DILATIONS[g]))
        dw_qkv.append(_matmul(sv["xn1_rm"][g], dqkv, ta=True, name=f"{lname}in_proj_qkv{g}_dw"))
    awg, og = dqkvs[0][0].shape[1], p["off_gate"]
    cols = [dw[:, i * awg:(i + 1) * awg] for i in range(3) for dw in dw_qkv]
    gr["w_in"] = jnp.concatenate(cols + [dw_main[:, 3 * awg:og], dw_dt[:, :nsh], dw_main[:, og:]], axis=1)
    dh, gr["norm1_w"] = _rms_bwd(sv["h"], p["norm1_w"], dxn1, dh1, name=lname + "norm1")
    return dh, gr, dbiases


def _layer_params(l, w, n_ssm_heads, hg):
    awg = hg * HEAD_DIM
    aw = N_GROUPS_ATTN * awg
    di = n_ssm_heads * HEAD_DIM
    xbc = di + 2 * SSM_GROUPS * D_STATE
    in_dt = 3 * aw + di + xbc
    w_in = w["w_in"][l]
    qkv_cols = lambda g: [(w_in, awg, i * N_GROUPS_ATTN + g) for i in range(3)]
    z_xbc_cols = [(w_in, awg, j) for j in range(3 * N_GROUPS_ATTN, in_dt // awg)]
    assert in_dt % awg == 0
    pad = lambda v: jnp.pad(v.astype(F32), (0, LANES - n_ssm_heads)).reshape(1, LANES)
    return dict(
        n_ssm_heads=n_ssm_heads, off_z=3 * awg, off_xbc=3 * awg + di, off_gate=3 * awg + di + xbc,
        w_main=_concat_cols(qkv_cols(0) + z_xbc_cols + [w_in[:, in_dt + n_ssm_heads:]], name=f"l{l}_w_main"),
        w_qkv=[None] + [_concat_cols(qkv_cols(g), name=f"l{l}_w_qkv{g}") for g in range(1, N_GROUPS_ATTN)],
        w_dt=jnp.pad(w_in[:, in_dt:in_dt + n_ssm_heads], ((0, 0), (0, LANES - n_ssm_heads))),
        norm1_w=w["norm1_w"][l], norm2_w=w["norm2_w"][l], conv_w=w["conv_w"][l], conv_b=w["conv_b"][l],
        dt_bias=pad(w["dt_bias"][l]), a=pad(-jnp.exp(w["a_log"][l])),
        dskip=jnp.repeat(w["d_skip"][l], HEAD_DIM).reshape(1, di), ssm_norm_w=w["ssm_norm_w"][l],
        w_attn_branch=w["w_attn_branch"][l], w_ssm_branch=w["w_ssm_branch"][l], w_out=w["w_out"][l],
        w_ffn_in=w["w_ffn_in"][l], w_ffn_out=w["w_ffn_out"][l],
    )


def _local_step(x, tgt, w):
    depth = w["norm1_w"].shape[0]
    n_ssm_heads = w["dt_bias"].shape[1]
    hg = w["rel_bias"].shape[1] // N_GROUPS_ATTN
    onehots = [(_rel_buckets(dil)[:, :, None] == jnp.arange(N_REL_BUCKETS)[None, None, :]).astype(F32) for dil in DILATIONS]
    biases = [jnp.einsum("qkb,bh->hqk", oh, w["rel_bias"][:, g * hg:(g + 1) * hg].astype(F32), precision=HIGHEST)
              for g, oh in enumerate(onehots)]
    params = [_layer_params(l, w, n_ssm_heads, hg) for l in range(depth)]
    h = x
    saved = []
    for l in range(depth):
        h, sv = _layer_fwd(h, params[l], biases, f"l{l}_")
        saved.append(sv)
    loss, dh, g_final = _loss_head(h, w["final_norm_w"], tgt, name="loss_head")
    grads = [None] * depth
    dbias_tot = [jnp.zeros(b.shape, F32) for b in biases]
    for l in reversed(range(depth)):
        dh, grads[l], dbiases = _layer_bwd(dh, params[l], saved[l], biases, f"l{l}_")
        dbias_tot = [a + b for a, b in zip(dbias_tot, dbiases)]
    out = {k: [gl[k] for gl in grads] if k in MATRICES else jnp.stack([gl[k] for gl in grads]) for k in grads[0]}
    out["final_norm_w"] = g_final
    drel = []
    for g, (oh, db) in enumerate(zip(onehots, dbias_tot)):
        oh_t = jnp.pad(oh.reshape(-1, N_REL_BUCKETS).T, ((0, LANES - N_REL_BUCKETS), (0, 0)))
        db_rows = jnp.pad(db.reshape(hg, -1), ((0, LANES - hg), (0, 0)))
        drel.append(_matmul(oh_t, db_rows, tb=True, name=f"rel_bias_fold{g}")[:N_REL_BUCKETS, :hg])
    out["rel_bias"] = jnp.concatenate(drel, axis=1)
    return loss, dh, out


MATRICES = ("w_in", "w_attn_branch", "w_ssm_branch", "w_out", "w_ffn_in", "w_ffn_out")
COL_SHARDED = ("w_in", "w_attn_branch", "w_ffn_in")
SMALL = ("norm1_w", "conv_b", "dt_bias", "a_log", "d_skip", "ssm_norm_w", "norm2_w", "rel_bias", "final_norm_w")
WEIGHTS = ("norm1_w", "w_in", "conv_w", "conv_b", "dt_bias", "a_log", "d_skip", "ssm_norm_w", "w_attn_branch",
           "w_ssm_branch", "w_out", "norm2_w", "w_ffn_in", "w_ffn_out", "rel_bias", "final_norm_w")
SMALL_COLS = 1024


def _unshard(name, g):
    _, depth, r, c = g.shape
    if name in COL_SHARDED or name == "conv_w":
        return jnp.transpose(g, (1, 2, 0, 3)).reshape(depth, r, N_CHIPS * c)
    return jnp.transpose(g, (1, 0, 2, 3)).reshape(depth, N_CHIPS * r, c)


def _to_shards(name, g):
    r, c = g.shape
    if name in COL_SHARDED:
        return jnp.transpose(g.reshape(r, N_CHIPS, c // N_CHIPS), (1, 0, 2))
    return g.reshape(N_CHIPS, r // N_CHIPS, c)


def kernel(x, norm1_w, w_in, conv_w, conv_b, dt_bias, a_log, d_skip, ssm_norm_w, w_attn_branch, w_ssm_branch, w_out, norm2_w, w_ffn_in, w_ffn_out, rel_bias, final_norm_w, loss_target, m_norm1_w, m_w_in, m_conv_w, m_conv_b, m_dt_bias, m_a_log, m_d_skip, m_ssm_norm_w, m_w_attn_branch, m_w_ssm_branch, m_w_out, m_norm2_w, m_w_ffn_in, m_w_ffn_out, m_rel_bias, m_final_norm_w, v_norm1_w, v_w_in, v_conv_w, v_conv_b, v_dt_bias, v_a_log, v_d_skip, v_ssm_norm_w, v_w_attn_branch, v_w_ssm_branch, v_w_out, v_norm2_w, v_w_ffn_in, v_w_ffn_out, v_rel_bias, v_final_norm_w):
    env = dict(locals())
    wts = {k: env[k] for k in WEIGHTS}
    mom = {k: env["m_" + k] for k in WEIGHTS}
    var = {k: env["v_" + k] for k in WEIGHTS}
    chip = 2 * lax.axis_index("x") + lax.axis_index("y")
    core = lax.axis_index("c")

    shards = [wts[k].astype(BF16) for k in MATRICES] + [conv_w]
    gathered = _gather_shards(shards, name="gather_weights")
    full = {k: wts[k] for k in SMALL}
    for k, own, g in zip(MATRICES + ("conv_w",), shards, gathered):
        full[k] = _unshard(k, lax.dynamic_update_index_in_dim(g, own, chip, axis=0))

    loss, dx, grads = _local_step(x[0], loss_target[0], full)
    loss = lax.psum(loss, ("x", "y", "c"))

    core1 = core.reshape(1).astype(jnp.int32)
    from_pair = _pair_swap_layers([grads[k][0] for k in MATRICES], [grads[k][1] for k in MATRICES], name="reduce_pair_swap")
    scatter_in = [_to_shards(k, _add_own_layer(grads[k][0], grads[k][1], got, core1, name="reduce_pair_add_" + k))
                  for k, got in zip(MATRICES, from_pair)]
    scattered = _scatter_to_chips(scatter_in, name="reduce_scatter")
    own_layer = []
    for k, sent, got in zip(MATRICES, scatter_in, scattered):
        got = lax.dynamic_update_index_in_dim(got, lax.dynamic_index_in_dim(sent, chip, axis=0, keepdims=False), chip, axis=0)
        own_layer.append(_sum_chips(got, name="reduce_sum_" + k))
    other_layer = _pair_swap(own_layer, name="reduce_pair_exchange")
    reduced = {}

    small_names = SMALL + ("conv_w",)
    flat = jnp.concatenate([grads[k].reshape(-1) for k in small_names])
    n_small = flat.shape[0]
    rows = -(-n_small // SMALL_COLS)
    rows = -(-rows // 8) * 8
    flat = jnp.pad(flat, (0, rows * SMALL_COLS - n_small)).reshape(rows, SMALL_COLS)
    flat = _allreduce_small(flat, name="allreduce_small").reshape(-1)
    pos = 0
    for k in small_names:
        size = math.prod(grads[k].shape)
        reduced[k] = flat[pos:pos + size].reshape(grads[k].shape)
        pos += size
    cs = conv_w.shape[2]
    reduced["conv_w"] = lax.dynamic_slice_in_dim(reduced["conv_w"], chip * cs, cs, axis=2)

    delta, new_m, new_v = {}, {}, {}
    for k, own, other in zip(MATRICES, own_layer, other_layer):
        reduced[k], delta[k], new_m[k], new_v[k] = _adamw_layers(wts[k], own, other, mom[k], var[k], core1, name="adamw_" + k)
    pack = lambda src: jnp.pad(jnp.concatenate([src[k].reshape(-1) for k in small_names]),
                               (0, rows * SMALL_COLS - n_shard)).reshape(rows, SMALL_COLS)
    n_shard = sum(math.prod(wts[k].shape) for k in small_names)
    d_, m_, v_ = _adamw(pack(wts), pack(reduced), pack(mom), pack(var), name="adamw_small")
    pos = 0
    for k in small_names:
        size = math.prod(wts[k].shape)
        for dst, src in ((delta, d_), (new_m, m_), (new_v, v_)):
            dst[k] = src.reshape(-1)[pos:pos + size].reshape(wts[k].shape)
        pos += size

    return (loss, dx[None], *[reduced[k] for k in WEIGHTS], *[delta[k] for k in WEIGHTS],
            *[new_m[k] for k in WEIGHTS], *[new_v[k] for k in WEIGHTS])
```

```python
import functools
import math

import jax
import jax.numpy as jnp
from jax import lax
from jax.experimental import pallas as pl
from jax.experimental.pallas import tpu as pltpu

F32, BF16 = jnp.float32, jnp.bfloat16
SDS = jax.ShapeDtypeStruct
BS = pl.BlockSpec
MESH = pl.DeviceIdType.MESH
HIGHEST = lax.Precision.HIGHEST

EPS = 1e-6
HEAD_DIM = 64
ATTN_BLOCK = 128
DILATIONS = (1, 4, 16)
N_GROUPS_ATTN = len(DILATIONS)
N_STEPS = 128
N_REL_BUCKETS = 32
REL_MAX_DISTANCE = 2048
SSM_GROUPS = 4
D_STATE = 128
CHUNK = 128
CONV_WIDTH = 4
HALO = 16
LANES = 128
N_CHIPS = 4
N_DEV = 8
VMEM_LIMIT_BYTES = 48 * 1024 * 1024

ADAM_LR, ADAM_B1, ADAM_B2, ADAM_EPS, ADAM_WD, ADAM_STEP = 0.001, 0.9, 0.999, 1e-08, 0.01, 10

NT = (((1,), (1,)), ((), ()))
TN = (((0,), (0,)), ((), ()))
NN = (((1,), (0,)), ((), ()))


def _cp(*sem):
    return pltpu.CompilerParams(dimension_semantics=sem if sem else None, vmem_limit_bytes=VMEM_LIMIT_BYTES)


def _pick(n, cands):
    for c in cands:
        if n % c == 0:
            return c
    raise ValueError(f"no block size of {cands} divides {n}")


def _divisors(n, cap):
    out = [c for c in range(LANES, min(n, cap) + 1, LANES) if n % c == 0]
    return out or [n]


def _wide(n, cap=2048):
    return _divisors(n, cap)[-1]


MXU_FLOPS = 9.0e14
HBM_BYTES_PER_S = 3.0e12
ACC_BYTES_PER_S = 4.0e12
GRID_STEP_S = 0.4e-6
TILE_VMEM_BYTES = 36 * 1024 * 1024


def _matmul_tiles(m, n, k, a_bytes, b_bytes, o_bytes, has_res):
    best = None
    for tm in _divisors(m, 2048):
        for tn in _divisors(n, 2048):
            for tk in _divisors(k, 4096):
                ni, nj, nk = m // tm, n // tn, k // tk
                vmem = 2 * (tm * tk * a_bytes + tk * tn * b_bytes + tm * tn * (o_bytes + (4 if has_res else 0)))
                vmem += tm * tn * 4 * (2 if nk > 1 else 1) + (tm * tk + tk * tn) * 2
                if vmem > TILE_VMEM_BYTES:
                    continue
                hbm = m * k * a_bytes * (nj if nk > 1 else 1) + k * n * b_bytes * (ni if nj * nk > 1 else 1)
                hbm += m * n * (o_bytes + (4 if has_res else 0))
                t = max(2.0 * m * n * k / MXU_FLOPS, hbm / HBM_BYTES_PER_S) + ni * nj * nk * GRID_STEP_S
                if nk > 1:
                    t += m * n * 8.0 * nk / ACC_BYTES_PER_S
                if best is None or t < best[0]:
                    best = (t, tm, tn, tk)
    assert best is not None, (m, n, k)
    return best[1:]


def _dot(a, b, dims=NN, precision=None):
    return lax.dot_general(a, b, dims, precision=precision, preferred_element_type=F32)


def _silu(x):
    return x / (1.0 + jnp.exp(-x))


def _sigmoid(x):
    return 1.0 / (1.0 + jnp.exp(-x))


def _dsilu(x):
    s = _sigmoid(x)
    return s * (1.0 + x * (1.0 - s))


def _matmul(a, b, *, name, ta=False, tb=False, out_dtype=F32, res=None):
    (kdim, m) = a.shape if ta else a.shape[::-1]
    (n, k2) = b.shape if tb else b.shape[::-1]
    assert kdim == k2, (a.shape, b.shape, ta, tb)
    tm, tn, tk = _matmul_tiles(m, n, kdim, a.dtype.itemsize, b.dtype.itemsize, jnp.dtype(out_dtype).itemsize, res is not None)
    nk = kdim // tk
    a_spec = BS((tk, tm), lambda i, j, k: (k, i)) if ta else BS((tm, tk), lambda i, j, k: (i, k))
    b_spec = BS((tn, tk), lambda i, j, k: (j, k)) if tb else BS((tk, tn), lambda i, j, k: (k, j))
    dims = (((0 if ta else 1,), (1 if tb else 0,)), ((), ()))
    has_res = res is not None

    def body(*refs):
        a_ref, b_ref = refs[:2]
        r_ref = refs[2] if has_res else None
        o_ref = refs[3] if has_res else refs[2]
        prod = _dot(a_ref[...].astype(BF16), b_ref[...].astype(BF16), dims)
        if nk == 1:
            o_ref[...] = (prod + r_ref[...] if has_res else prod).astype(o_ref.dtype)
            return
        acc = refs[-1]
        k = pl.program_id(2)

        @pl.when(k == 0)
        def _():
            acc[...] = prod

        @pl.when(k > 0)
        def _():
            acc[...] += prod

        @pl.when(k == nk - 1)
        def _():
            r = acc[...]
            if has_res:
                r = r + r_ref[...]
            o_ref[...] = r.astype(o_ref.dtype)

    in_specs = [a_spec, b_spec]
    args = [a, b]
    if has_res:
        in_specs.append(BS((tm, tn), lambda i, j, k: (i, j)))
        args.append(res)
    return pl.pallas_call(
        body, name=name, grid=(m // tm, n // tn, nk), in_specs=in_specs,
        out_specs=BS((tm, tn), lambda i, j, k: (i, j)), out_shape=SDS((m, n), out_dtype),
        scratch_shapes=[pltpu.VMEM((tm, tn), F32)] if nk > 1 else [],
        compiler_params=_cp("parallel", "parallel", "arbitrary"),
    )(*args)


def _rms_fwd(h, w, *, name):
    t, d = h.shape
    tm = _pick(t, (512, 256, 128))

    def body(h_ref, w_ref, o_ref):
        x = h_ref[...]
        r = lax.rsqrt(jnp.mean(x * x, axis=-1, keepdims=True) + EPS)
        o_ref[...] = (x * r * w_ref[...]).astype(BF16)

    return pl.pallas_call(
        body, name=name, grid=(t // tm,), in_specs=[BS((tm, d), lambda i: (i, 0)), BS((1, d), lambda i: (0, 0))],
        out_specs=BS((tm, d), lambda i: (i, 0)), out_shape=SDS((t, d), BF16), compiler_params=_cp("parallel"),
    )(h, w.reshape(1, d))


def _rms_bwd(h, w, dys, dres, *, name):
    t, d = h.shape
    tm = _pick(t, (512, 256, 128))
    n_dy = len(dys)

    def body(*refs):
        h_ref, w_ref = refs[:2]
        dy_refs = refs[2:2 + n_dy]
        dres_ref, dh_ref, dhb_ref, dw_ref = refs[2 + n_dy:]
        x = h_ref[...]
        dy = dy_refs[0][...]
        for r_ in dy_refs[1:]:
            dy = dy + r_[...]
        r = lax.rsqrt(jnp.mean(x * x, axis=-1, keepdims=True) + EPS)
        g = dy * w_ref[...]
        proj = jnp.sum(g * x, axis=-1, keepdims=True) * (1.0 / d)
        dh = dres_ref[...] + r * g - x * (r * r * r) * proj
        dh_ref[...] = dh
        dhb_ref[...] = dh.astype(BF16)

        @pl.when(pl.program_id(0) == 0)
        def _():
            dw_ref[...] = jnp.zeros_like(dw_ref)

        dw_ref[...] += jnp.sum(dy * x * r, axis=0, keepdims=True)

    row = BS((tm, d), lambda i: (i, 0))
    vec = BS((1, d), lambda i: (0, 0))
    dh, dhb, dw = pl.pallas_call(
        body, name=name, grid=(t // tm,), in_specs=[row, vec] + [row] * n_dy + [row],
        out_specs=[row, row, vec], out_shape=[SDS((t, d), F32), SDS((t, d), BF16), SDS((1, d), F32)],
        compiler_params=_cp("arbitrary"),
    )(h, w.reshape(1, d), *dys, dres)
    return dh, dhb, dw[0]


def _loss_head(h, w, tgt, *, name):
    t, d = h.shape
    tm = _pick(t, (512, 256, 128))

    def body(h_ref, w_ref, t_ref, loss_ref, dh_ref, dhb_ref, dw_ref):
        x = h_ref[...]
        r = lax.rsqrt(jnp.mean(x * x, axis=-1, keepdims=True) + EPS)
        err = x * r * w_ref[...] - t_ref[...]
        loss_ref[...] = jnp.zeros(loss_ref.shape, F32) + 0.5 * jnp.sum(err * err) * (1.0 / d)
        dy = err * (1.0 / d)
        g = dy * w_ref[...]
        proj = jnp.sum(g * x, axis=-1, keepdims=True) * (1.0 / d)
        dh = r * g - x * (r * r * r) * proj
        dh_ref[...] = dh
        dhb_ref[...] = dh.astype(BF16)

        @pl.when(pl.program_id(0) == 0)
        def _():
            dw_ref[...] = jnp.zeros_like(dw_ref)

        dw_ref[...] += jnp.sum(dy * x * r, axis=0, keepdims=True)

    row = BS((tm, d), lambda i: (i, 0))
    vec = BS((1, d), lambda i: (0, 0))
    loss, dh, dhb, dw = pl.pallas_call(
        body, name=name, grid=(t // tm,), in_specs=[row, vec, row],
        out_specs=[BS((1, 8, LANES), lambda i: (i, 0, 0)), row, row, vec],
        out_shape=[SDS((t // tm, 8, LANES), F32), SDS((t, d), F32), SDS((t, d), BF16), SDS((1, d), F32)],
        compiler_params=_cp("arbitrary"),
    )(h, w.reshape(1, d), tgt)
    return jnp.sum(loss[:, 0, 0]), dh, dhb, dw[0]


def _attn_masks(mb):
    qi = lax.broadcasted_iota(jnp.int32, (2 * ATTN_BLOCK, 2 * ATTN_BLOCK), 0) & (ATTN_BLOCK - 1)
    kj = lax.broadcasted_iota(jnp.int32, (2 * ATTN_BLOCK, 2 * ATTN_BLOCK), 1)
    steps = qi + ATTN_BLOCK - kj
    valid = (steps >= 0) & (steps <= N_STEPS) & ((kj >= ATTN_BLOCK) | (mb > 0))
    low = lax.broadcasted_iota(jnp.int32, (ATTN_BLOCK, LANES), 1) < HEAD_DIM
    return valid, low


def _stack_heads(x2, low):
    zero = jnp.zeros_like(x2)
    return jnp.concatenate([jnp.where(low, x2, zero), jnp.where(low, zero, x2)], axis=0)


def _head_cols(x2, low):
    return jnp.concatenate([jnp.max(jnp.where(low, x2, -jnp.inf), axis=-1, keepdims=True),
                            jnp.max(jnp.where(low, -jnp.inf, x2), axis=-1, keepdims=True)], axis=0)


def _to_residue_major(a, d):
    t, c = a.shape
    return a if d == 1 else a.reshape(t // d, d, c).transpose(1, 0, 2).reshape(t, c)


def _to_token_major(a, d):
    t, c = a.shape
    return a if d == 1 else a.reshape(d, t // d, c).transpose(1, 0, 2).reshape(t, c)


def _attn_specs(cols, nb, sub, awg, clamp):
    nsb = nb // sub

    def cur(col):
        return BS((sub * ATTN_BLOCK, awg), lambda r, sb: (r * nsb + clamp(sb), col))

    def prev(col):
        return BS((ATTN_BLOCK, awg), lambda r, sb: (r * nb + jnp.maximum(clamp(sb) * sub - 1, 0), col))

    return [cur(cols[0]), cur(cols[1]), prev(cols[1]), cur(cols[2]), prev(cols[2])]


def _sub_rows(s, n=1):
    return pl.ds(pl.multiple_of(s * ATTN_BLOCK, ATTN_BLOCK), n * ATTN_BLOCK)


def _attn_fwd(qkv, cols, bias, d, *, name):
    t = qkv.shape[0]
    hg = bias.shape[0]
    awg = hg * HEAD_DIM
    nb = t // d // ATTN_BLOCK
    sub = _pick(nb, (4, 2, 1))
    rows = sub * ATTN_BLOCK
    scale = HEAD_DIM ** -0.5

    def body(q_ref, kc_ref, kp_ref, vc_ref, vp_ref, b_ref, o_ref, l_ref, kext, vext):
        sb = pl.program_id(1)
        kext[pl.ds(0, ATTN_BLOCK), :] = kp_ref[...]
        kext[pl.ds(ATTN_BLOCK, rows), :] = kc_ref[...]
        vext[pl.ds(0, ATTN_BLOCK), :] = vp_ref[...]
        vext[pl.ds(ATTN_BLOCK, rows), :] = vc_ref[...]

        def one_block(s, carry):
            valid, low = _attn_masks(sb * sub + s)
            for pi in range(awg // LANES):
                sl = slice(pi * LANES, (pi + 1) * LANES)
                k2 = kext[_sub_rows(s, 2), sl]
                v2 = vext[_sub_rows(s, 2), sl]
                qs = _stack_heads(q_ref[_sub_rows(s), sl], low)
                bias2 = b_ref[pi * 2:pi * 2 + 2].reshape(2 * ATTN_BLOCK, 2 * ATTN_BLOCK)
                sc = jnp.where(valid, _dot(qs, k2, NT) * scale + bias2, -jnp.inf)
                m = jnp.max(sc, axis=-1, keepdims=True)
                p = jnp.exp(sc - m)
                den = jnp.sum(p, axis=-1, keepdims=True)
                o = _dot(p.astype(BF16), v2) / den
                lse = jnp.broadcast_to(m + jnp.log(den), (2 * ATTN_BLOCK, LANES))
                o_ref[_sub_rows(s), sl] = jnp.where(low, o[:ATTN_BLOCK], o[ATTN_BLOCK:])
                l_ref[_sub_rows(s), sl] = jnp.where(low, lse[:ATTN_BLOCK], lse[ATTN_BLOCK:])
            return carry

        lax.fori_loop(0, sub, one_block, 0)

    out_spec = BS((rows, awg), lambda r, sb: (r * (nb // sub) + sb, 0))
    return pl.pallas_call(
        body, name=name, grid=(d, nb // sub),
        in_specs=_attn_specs(cols, nb, sub, awg, lambda sb: sb) + [BS(bias.shape, lambda r, sb: (0, 0, 0))],
        out_specs=[out_spec, out_spec], out_shape=[SDS((t, awg), F32)] * 2,
        scratch_shapes=[pltpu.VMEM((rows + ATTN_BLOCK, awg), BF16)] * 2, compiler_params=_cp("parallel", "parallel"),
    )(*([qkv] * 5), bias)


def _attn_bwd(qkv, cols, bias, lse, do, corr, d, *, name):
    t = qkv.shape[0]
    hg = bias.shape[0]
    awg = hg * HEAD_DIM
    nb = t // d // ATTN_BLOCK
    sub = _pick(nb, (4, 2, 1))
    nsb = nb // sub
    rows = sub * ATTN_BLOCK
    scale = HEAD_DIM ** -0.5

    def body(q_ref, kc_ref, kp_ref, vc_ref, vp_ref, b_ref, l_ref, do_ref, c_ref, dq_ref, dk_ref, dv_ref, db_ref,
             kext, vext, dkext, dvext, ck, cv):
        r, sb = pl.program_id(0), pl.program_id(1)

        @pl.when((r == 0) & (sb == 0))
        def _():
            db_ref[...] = jnp.zeros_like(db_ref)

        @pl.when(sb == 0)
        def _():
            ck[...] = jnp.zeros_like(ck)
            cv[...] = jnp.zeros_like(cv)

        @pl.when(sb < nsb)
        def _():
            kext[pl.ds(0, ATTN_BLOCK), :] = kp_ref[...]
            kext[pl.ds(ATTN_BLOCK, rows), :] = kc_ref[...]
            vext[pl.ds(0, ATTN_BLOCK), :] = vp_ref[...]
            vext[pl.ds(ATTN_BLOCK, rows), :] = vc_ref[...]
            dkext[...] = jnp.zeros_like(dkext)
            dvext[...] = jnp.zeros_like(dvext)

            def one_block(s, carry):
                valid, low = _attn_masks(sb * sub + s)
                for pi in range(awg // LANES):
                    sl = slice(pi * LANES, (pi + 1) * LANES)
                    k2 = kext[_sub_rows(s, 2), sl]
                    v2 = vext[_sub_rows(s, 2), sl]
                    qs = _stack_heads(q_ref[_sub_rows(s), sl], low)
                    dos = _stack_heads(do_ref[_sub_rows(s), sl], low)
                    lse_c = _head_cols(l_ref[_sub_rows(s), sl], low)
                    corr_c = _head_cols(c_ref[_sub_rows(s), sl], low)
                    bias2 = b_ref[pi * 2:pi * 2 + 2].reshape(2 * ATTN_BLOCK, 2 * ATTN_BLOCK)
                    sc = _dot(qs, k2, NT) * scale + bias2
                    p = jnp.exp(jnp.where(valid, sc, -jnp.inf) - lse_c)
                    ds = p * (_dot(dos, v2, NT) + corr_c)
                    db_ref[pi * 2:pi * 2 + 2] += ds.reshape(2, ATTN_BLOCK, 2 * ATTN_BLOCK)
                    dsb = ds.astype(BF16)
                    dq = _dot(dsb, k2) * scale
                    dq_ref[_sub_rows(s), sl] = jnp.where(low, dq[:ATTN_BLOCK], dq[ATTN_BLOCK:]).astype(BF16)
                    dkext[_sub_rows(s, 2), sl] += _dot(dsb, qs, TN) * scale
                    dvext[_sub_rows(s, 2), sl] += _dot(p.astype(BF16), dos, TN)
                return carry

            lax.fori_loop(0, sub, one_block, 0)
            head, tail = pl.ds(0, rows - ATTN_BLOCK), pl.ds(rows - ATTN_BLOCK, ATTN_BLOCK)
            for out_ref, carry_ref, ext in ((dk_ref, ck, dkext), (dv_ref, cv, dvext)):
                if sub > 1:
                    out_ref[head, :] = carry_ref[head, :].astype(BF16)
                out_ref[tail, :] = (carry_ref[tail, :] + ext[pl.ds(0, ATTN_BLOCK), :]).astype(BF16)
                carry_ref[...] = ext[pl.ds(ATTN_BLOCK, rows), :]

        @pl.when(sb == nsb)
        def _():
            dk_ref[...] = ck[...].astype(BF16)
            dv_ref[...] = cv[...].astype(BF16)

    clamp = lambda sb: jnp.minimum(sb, nsb - 1)
    cur = BS((rows, awg), lambda r, sb: (r * nsb + clamp(sb), 0))
    prev = BS((rows, awg), lambda r, sb: (r * nsb + jnp.maximum(sb - 1, 0), 0))
    bias_spec = BS(bias.shape, lambda r, sb: (0, 0, 0))
    return pl.pallas_call(
        body, name=name, grid=(d, nsb + 1),
        in_specs=_attn_specs(cols, nb, sub, awg, clamp) + [bias_spec, cur, cur, cur],
        out_specs=[cur, prev, prev, bias_spec],
        out_shape=[SDS((t, awg), BF16)] * 3 + [SDS(bias.shape, F32)],
        scratch_shapes=[pltpu.VMEM((rows + ATTN_BLOCK, awg), BF16)] * 2 + [pltpu.VMEM((rows + ATTN_BLOCK, awg), F32)] * 2
        + [pltpu.VMEM((rows, awg), F32)] * 2,
        compiler_params=_cp("arbitrary", "arbitrary"),
    )(*([qkv] * 5), bias, lse, do, corr)


def _head_sum(x, low):
    a = jnp.sum(jnp.where(low, x, 0.0), axis=-1, keepdims=True)
    b = jnp.sum(jnp.where(low, 0.0, x), axis=-1, keepdims=True)
    return jnp.where(low, a, b)


def _combine_weights(lses):
    mx = jnp.maximum(jnp.maximum(lses[0], lses[1]), lses[2])
    es = [jnp.exp(l - mx) for l in lses]
    tot = es[0] + es[1] + es[2]
    return [e / tot for e in es]


def _combine_fwd(os_, lses, *, name):
    t, awg = os_[0].shape
    tm = _pick(t, (512, 256, 128))

    def body(o0, o1, o2, l0, l1, l2, out_ref):
        al = _combine_weights([l0[...], l1[...], l2[...]])
        out_ref[...] = (al[0] * o0[...] + al[1] * o1[...] + al[2] * o2[...]).astype(BF16)

    blk = BS((tm, awg), lambda i: (i, 0))
    return pl.pallas_call(
        body, name=name, grid=(t // tm,), in_specs=[blk] * 6, out_specs=blk,
        out_shape=SDS((t, awg), BF16), compiler_params=_cp("parallel"),
    )(*os_, *lses)


def _combine_bwd(dattn, os_, lses, *, name):
    t, awg = dattn.shape
    tm = _pick(t, (512, 256, 128))

    def body(da_ref, o0, o1, o2, l0, l1, l2, d0, d1, d2, c0, c1, c2):
        low = lax.broadcasted_iota(jnp.int32, (tm, LANES), 1) < HEAD_DIM
        for pi in range(awg // LANES):
            sl = slice(pi * LANES, (pi + 1) * LANES)
            da = da_ref[:, sl]
            al = _combine_weights([l0[:, sl], l1[:, sl], l2[:, sl]])
            tot = jnp.zeros((tm, LANES), F32)
            for a, o in zip(al, (o0, o1, o2)):
                tot = tot + a * _head_sum(da * o[:, sl], low)
            for a, d_ref, c_ref in zip(al, (d0, d1, d2), (c0, c1, c2)):
                d_ref[:, sl] = (a * da).astype(BF16)
                c_ref[:, sl] = -a * tot

    blk = BS((tm, awg), lambda i: (i, 0))
    outs = pl.pallas_call(
        body, name=name, grid=(t // tm,), in_specs=[blk] * 7, out_specs=[blk] * 6,
        out_shape=[SDS((t, awg), BF16)] * 3 + [SDS((t, awg), F32)] * 3, compiler_params=_cp("parallel"),
    )(dattn, *os_, *lses)
    return outs[:3], outs[3:]


def _conv_block(width, *offsets):
    for c in (512, 256, 128):
        if width % c == 0 and all(o % c == 0 for o in offsets):
            return c
    raise ValueError((width, offsets))


CONV_ROWS = 32


def _conv_pre(x_ref, halo_ref, w_ref, b_ref, ext, i, tm):
    ext[pl.ds(0, HALO), :] = jnp.where(i > 0, halo_ref[...].astype(F32), 0.0)
    ext[pl.ds(HALO, tm), :] = x_ref[...].astype(F32)
    taps = [w_ref[pl.ds(k, 1), :] for k in range(CONV_WIDTH)]
    bias = b_ref[...]
    for r0 in range(0, tm, CONV_ROWS):
        xs = [ext[pl.ds(HALO + r0 - (CONV_WIDTH - 1) + k, CONV_ROWS), :] for k in range(CONV_WIDTH)]
        pre = bias + taps[0] * xs[0]
        for k in range(1, CONV_WIDTH):
            pre = pre + taps[k] * xs[k]
        yield r0, pre, xs


def _fold8(v):
    return jnp.sum(v.reshape(v.shape[0] // 8, 8, v.shape[1]), axis=0)


def _conv_fwd(proj, off, w, b, *, name):
    t = proj.shape[0]
    c = w.shape[1]
    cw = _conv_block(c, off)
    tm = _pick(t, (512, 256, 128))
    ob = off // cw

    def body(x_ref, halo_ref, w_ref, b_ref, o_ref, ext):
        for r0, pre, _ in _conv_pre(x_ref, halo_ref, w_ref, b_ref, ext, pl.program_id(1), tm):
            o_ref[pl.ds(r0, CONV_ROWS), :] = _silu(pre).astype(BF16)

    return pl.pallas_call(
        body, name=name, grid=(c // cw, t // tm),
        in_specs=[BS((tm, cw), lambda j, i: (i, ob + j)),
                  BS((HALO, cw), lambda j, i: (jnp.maximum(i * (tm // HALO) - 1, 0), ob + j)),
                  BS((CONV_WIDTH, cw), lambda j, i: (0, j)), BS((1, cw), lambda j, i: (0, j))],
        out_specs=BS((tm, cw), lambda j, i: (i, j)), out_shape=SDS((t, c), BF16),
        scratch_shapes=[pltpu.VMEM((HALO + tm, cw), F32)], compiler_params=_cp("parallel", "arbitrary"),
    )(proj, proj, w, b.reshape(1, c))


def _conv_bwd(proj, off, w, b, dxc, *, name):
    t = proj.shape[0]
    c = w.shape[1]
    cw = _conv_block(c, off)
    tm = _pick(t, (512, 256, 128))
    ob = off // cw
    nt = t // tm

    def body_pre(x_ref, halo_ref, w_ref, b_ref, d_ref, dp_ref, dw_ref, db_ref, ext):
        i = pl.program_id(1)

        @pl.when(i == 0)
        def _():
            dw_ref[...] = jnp.zeros_like(dw_ref)
            db_ref[...] = jnp.zeros_like(db_ref)

        db_acc = jnp.zeros((8, cw), F32)
        dw_acc = [jnp.zeros((8, cw), F32) for _ in range(CONV_WIDTH)]
        for r0, pre, xs in _conv_pre(x_ref, halo_ref, w_ref, b_ref, ext, i, tm):
            dpre = d_ref[pl.ds(r0, CONV_ROWS), :] * _dsilu(pre)
            dp_ref[pl.ds(r0, CONV_ROWS), :] = dpre
            db_acc = db_acc + _fold8(dpre)
            dw_acc = [acc + _fold8(dpre * x) for acc, x in zip(dw_acc, xs)]
        db_ref[...] += jnp.sum(db_acc, axis=0, keepdims=True)
        for k in range(CONV_WIDTH):
            dw_ref[pl.ds(k, 1), :] += jnp.sum(dw_acc[k], axis=0, keepdims=True)

    dpre, dw, db = pl.pallas_call(
        body_pre, name=name + "_pre", grid=(c // cw, nt),
        in_specs=[BS((tm, cw), lambda j, i: (i, ob + j)),
                  BS((HALO, cw), lambda j, i: (jnp.maximum(i * (tm // HALO) - 1, 0), ob + j)),
                  BS((CONV_WIDTH, cw), lambda j, i: (0, j)), BS((1, cw), lambda j, i: (0, j)),
                  BS((tm, cw), lambda j, i: (i, j))],
        out_specs=[BS((tm, cw), lambda j, i: (i, j)), BS((CONV_WIDTH, cw), lambda j, i: (0, j)), BS((1, cw), lambda j, i: (0, j))],
        out_shape=[SDS((t, c), F32), SDS((CONV_WIDTH, c), F32), SDS((1, c), F32)],
        scratch_shapes=[pltpu.VMEM((HALO + tm, cw), F32)], compiler_params=_cp("parallel", "arbitrary"),
    )(proj, proj, w, b.reshape(1, c), dxc)

    def body_in(dp_ref, nxt_ref, w_ref, dx_ref, ext):
        i = pl.program_id(1)
        ext[pl.ds(0, tm), :] = dp_ref[...]
        ext[pl.ds(tm, 8), :] = jnp.where(i < nt - 1, nxt_ref[...], 0.0)
        taps = [w_ref[pl.ds(k, 1), :] for k in range(CONV_WIDTH)]
        for r0 in range(0, tm, CONV_ROWS):
            dx = taps[CONV_WIDTH - 1] * ext[pl.ds(r0, CONV_ROWS), :]
            for k in range(CONV_WIDTH - 1):
                dx = dx + taps[k] * ext[pl.ds(r0 + CONV_WIDTH - 1 - k, CONV_ROWS), :]
            dx_ref[pl.ds(r0, CONV_ROWS), :] = dx.astype(BF16)

    dx = pl.pallas_call(
        body_in, name=name + "_in", grid=(c // cw, nt),
        in_specs=[BS((tm, cw), lambda j, i: (i, j)),
                  BS((8, cw), lambda j, i: (jnp.minimum((i + 1) * (tm // 8), t // 8 - 1), j)),
                  BS((CONV_WIDTH, cw), lambda j, i: (0, j))],
        out_specs=BS((tm, cw), lambda j, i: (i, j)), out_shape=SDS((t, c), BF16),
        scratch_shapes=[pltpu.VMEM((tm + 8, cw), F32)], compiler_params=_cp("parallel", "arbitrary"),
    )(dpre, dpre, w)
    return dx, dw, db[0]


def _softplus(x):
    return jnp.maximum(x, 0.0) + jnp.log(1.0 + jnp.exp(-jnp.abs(x)))


def _tril():
    return lax.broadcasted_iota(jnp.int32, (CHUNK, CHUNK), 0) >= lax.broadcasted_iota(jnp.int32, (CHUNK, CHUNK), 1)


def _ssd_prep(dtr, dt_bias, a, *, name):
    t = dtr.shape[0]
    nc = t // CHUNK
    cb = _pick(nc, (4, 2, 1))

    def body(dtr_ref, bias_ref, a_ref, o_ref):
        tril = _tril().astype(F32)
        for ci in range(cb):
            rows = pl.ds(ci * CHUNK, CHUNK)
            pre = dtr_ref[rows, :] + bias_ref[...]
            dt = _softplus(pre)
            la = _dot(tril, dt * a_ref[...], precision=HIGHEST)
            for k, v in enumerate((dt, la, la.T, _sigmoid(pre))):
                o_ref[rows, k * LANES:(k + 1) * LANES] = v

    vec = BS((1, LANES), lambda i: (0, 0))
    return pl.pallas_call(
        body, name=name, grid=(nc // cb,), in_specs=[BS((cb * CHUNK, LANES), lambda i: (i, 0)), vec, vec],
        out_specs=BS((cb * CHUNK, 4 * LANES), lambda i: (i, 0)), out_shape=SDS((t, 4 * LANES), F32),
        compiler_params=_cp("parallel"),
    )(dtr, dt_bias, a)


def _ssd_common(time_ref):
    part = lambda k: time_ref[:, k * LANES:(k + 1) * LANES]
    return part(3), part(0), part(1), part(2), _tril()


def _lane_col(x, lane, h):
    return jnp.sum(jnp.where(lane == h, x, 0.0), axis=-1, keepdims=True)


def _ssd_specs(rows, di, gw, cidx):
    nbx = di // LANES
    return [BS((rows, gw), lambda g, c: (cidx(c), g)),
            BS((rows, D_STATE), lambda g, c: (cidx(c), nbx + g)),
            BS((rows, D_STATE), lambda g, c: (cidx(c), nbx + SSM_GROUPS + g)),
            BS((rows, 4 * LANES), lambda g, c: (cidx(c), 0)),
            BS((1, LANES), lambda g, c: (0, 0)),
            BS((1, gw), lambda g, c: (0, g))]


def _chunk_rows(ci):
    return pl.ds(pl.multiple_of(ci * CHUNK, CHUNK), CHUNK)


def _ssd_fwd(xc, prep, a, dskip, *, name):
    t = xc.shape[0]
    di = xc.shape[1] - 2 * SSM_GROUPS * D_STATE
    gw = di // SSM_GROUPS
    hpg = gw // HEAD_DIM
    npair = gw // LANES
    nc = t // CHUNK
    cb = _pick(nc, (4, 2, 1))

    def body(xb_ref, bb_ref, cb_ref, timeb_ref, a_ref, dsk_ref, yb_ref, st_ref, state):
        @pl.when(pl.program_id(1) == 0)
        def _():
            state[...] = jnp.zeros_like(state)

        g = pl.program_id(0)

        def one_chunk(ci, carry):
            rows = _chunk_rows(ci)
            ssd_chunk(xb_ref.at[rows], bb_ref.at[rows], cb_ref.at[rows], timeb_ref.at[rows], dsk_ref,
                      yb_ref.at[rows], st_ref, state, ci, g)
            return carry

        lax.fori_loop(0, cb, one_chunk, 0)

    def ssd_chunk(x_ref, b_ref, c_ref, time_ref, dsk_ref, y_ref, st_ref, state, ci, g):
        st_ref[ci, 0] = state[...]
        _, dt, la, la_t, tril = _ssd_common(time_ref)
        lane = lax.broadcasted_iota(jnp.int32, (CHUNK, LANES), 1)
        sub = lax.broadcasted_iota(jnp.int32, (LANES, CHUNK), 0)
        lane1 = lax.broadcasted_iota(jnp.int32, (1, LANES), 1)
        low, low1 = lane < HEAD_DIM, lane1 < HEAD_DIM
        last = lax.broadcasted_iota(jnp.int32, (CHUNK, LANES), 0) == CHUNK - 1
        lend = jnp.sum(jnp.where(last, la, 0.0), axis=0, keepdims=True)
        bm, cm = b_ref[...], c_ref[...]
        gmat = _dot(cm, bm, NT)
        for p in range(npair):
            sl = slice(p * LANES, (p + 1) * LANES)
            ps = slice(p * D_STATE, (p + 1) * D_STATE)
            x2 = x_ref[:, sl].astype(F32)
            cols, ms = [], []
            for hh in range(2):
                h = g * hpg + p * 2 + hh
                col_la = _lane_col(la, lane, h)
                row_la = jnp.sum(jnp.where(sub == h, la_t, 0.0), axis=0, keepdims=True)
                lend_h = _lane_col(lend, lane1, h)
                decay = jnp.exp(jnp.where(tril, col_la - row_la, -jnp.inf))
                ms.append((gmat * decay).astype(BF16))
                cols.append((_lane_col(dt, lane, h), jnp.exp(col_la), jnp.exp(lend_h - col_la), jnp.exp(lend_h)))
            pair = lambda k: jnp.where(low, cols[0][k], cols[1][k])
            xdt = x2 * pair(0)
            xdtb = xdt.astype(BF16)
            s2 = state[ps, :]
            yy = _dot(jnp.concatenate(ms, axis=0), xdtb)
            y = jnp.where(low, yy[:CHUNK], yy[CHUNK:])
            y = y + pair(1) * _dot(cm, s2.astype(BF16)) + x2 * dsk_ref[:, sl]
            y_ref[:, sl] = y
            state[ps, :] = s2 * jnp.where(low1, cols[0][3], cols[1][3]) + _dot(bm, (xdt * pair(2)).astype(BF16), TN)

    y, st = pl.pallas_call(
        body, name=name, grid=(SSM_GROUPS, nc // cb), in_specs=_ssd_specs(cb * CHUNK, di, gw, lambda c: c),
        out_specs=[BS((cb * CHUNK, gw), lambda g, c: (c, g)), BS((cb, 1, npair * D_STATE, LANES), lambda g, c: (c, g, 0, 0))],
        out_shape=[SDS((t, di), F32), SDS((nc, SSM_GROUPS, npair * D_STATE, LANES), F32)],
        scratch_shapes=[pltpu.VMEM((npair * D_STATE, LANES), F32)], compiler_params=_cp("parallel", "arbitrary"),
    )(xc, xc, xc, prep, a, dskip)
    return y, st


def _ssd_bwd(xc, prep, a, dskip, st, dy, *, name):
    t = xc.shape[0]
    di = xc.shape[1] - 2 * SSM_GROUPS * D_STATE
    gw = di // SSM_GROUPS
    hpg = gw // HEAD_DIM
    npair = gw // LANES
    nc = t // CHUNK
    cb = _pick(nc, (4, 2, 1))
    rev = lambda c: nc // cb - 1 - c

    def body(xb_ref, bb_ref, cb_ref, timeb_ref, a_ref, dsk_ref, st_ref, dyb_ref,
             dxb_ref, dbb_ref, dcb_ref, ddtrb_ref, da_ref, dbias_ref, ddsk_ref, dstate):
        @pl.when(pl.program_id(1) == 0)
        def _():
            dstate[...] = jnp.zeros_like(dstate)
            da_ref[...] = jnp.zeros_like(da_ref)
            dbias_ref[...] = jnp.zeros_like(dbias_ref)
            ddsk_ref[...] = jnp.zeros_like(ddsk_ref)

        g = pl.program_id(0)

        def one_chunk(j, carry):
            ci = cb - 1 - j
            rows = _chunk_rows(ci)
            ssd_chunk(xb_ref.at[rows], bb_ref.at[rows], cb_ref.at[rows], timeb_ref.at[rows], a_ref, dsk_ref,
                      st_ref.at[ci], dyb_ref.at[rows], dxb_ref.at[rows], dbb_ref.at[rows], dcb_ref.at[rows],
                      ddtrb_ref.at[:, rows], da_ref, dbias_ref, ddsk_ref, dstate, g)
            return carry

        lax.fori_loop(0, cb, one_chunk, 0)

    def ssd_chunk(x_ref, b_ref, c_ref, time_ref, a_ref, dsk_ref, st_ref, dy_ref,
                  dx_ref, db_ref, dc_ref, ddtr_ref, da_ref, dbias_ref, ddsk_ref, dstate, g):
        sig, dt, la, la_t, tril = _ssd_common(time_ref)
        lane = lax.broadcasted_iota(jnp.int32, (CHUNK, LANES), 1)
        sub = lax.broadcasted_iota(jnp.int32, (LANES, CHUNK), 0)
        lane1 = lax.broadcasted_iota(jnp.int32, (1, LANES), 1)
        low, low1 = lane < HEAD_DIM, lane1 < HEAD_DIM
        last = lax.broadcasted_iota(jnp.int32, (CHUNK, LANES), 0) == CHUNK - 1
        lend = jnp.sum(jnp.where(last, la, 0.0), axis=0, keepdims=True)
        bm, cm = b_ref[...], c_ref[...]
        gmat = _dot(cm, bm, NT)
        dg = jnp.zeros((CHUNK, CHUNK), F32)
        dla_cols = jnp.zeros((CHUNK, LANES), F32)
        dla_rows = jnp.zeros((LANES, CHUNK), F32)
        dtsum = jnp.zeros((CHUNK, LANES), F32)
        dbm = jnp.zeros((CHUNK, D_STATE), F32)
        dcm = jnp.zeros((CHUNK, D_STATE), F32)
        for p in range(npair):
            sl = slice(p * LANES, (p + 1) * LANES)
            ps = slice(p * D_STATE, (p + 1) * D_STATE)
            x2 = x_ref[:, sl].astype(F32)
            dy2 = dy_ref[:, sl]
            s2 = st_ref[0, ps, :]
            ds2 = dstate[ps, :]
            hs, cols, ms, decays = [], [], [], []
            for hh in range(2):
                h = g * hpg + p * 2 + hh
                col_la = _lane_col(la, lane, h)
                row_la = jnp.sum(jnp.where(sub == h, la_t, 0.0), axis=0, keepdims=True)
                lend_h = _lane_col(lend, lane1, h)
                decay = jnp.exp(jnp.where(tril, col_la - row_la, -jnp.inf))
                hs.append(h)
                decays.append(decay)
                ms.append(gmat * decay)
                cols.append((_lane_col(dt, lane, h), jnp.exp(col_la), jnp.exp(lend_h - col_la), jnp.exp(lend_h)))
            pair = lambda k: jnp.where(low, cols[0][k], cols[1][k])
            dtc, ec, eend = pair(0), pair(1), pair(2)
            eend_s = jnp.where(low1, cols[0][3], cols[1][3])
            xdt = x2 * dtc
            xdtb = xdt.astype(BF16)
            dys = dy2 * ec
            dysb = dys.astype(BF16)
            dxdt_state = eend * _dot(bm, ds2.astype(BF16))
            inter = dys * _dot(cm, s2.astype(BF16))
            u = dxdt_state * xdt
            sds = s2 * ds2
            dys2 = _stack_heads(dy2.astype(BF16), low)
            dxdt = dxdt_state + _dot(jnp.concatenate([ms[0].astype(BF16), ms[1].astype(BF16)], axis=0), dys2, TN)
            dms = _dot(dys2, xdtb, NT)
            for hh in range(2):
                h = hs[hh]
                mh = low if hh == 0 else jnp.logical_not(low)
                dm = dms[hh * CHUNK:(hh + 1) * CHUNK]
                w = dm * ms[hh]
                dg = dg + dm * decays[hh]
                u_col = jnp.sum(jnp.where(mh, u, 0.0), axis=-1, keepdims=True)
                dlend = jnp.sum(u_col, axis=0, keepdims=True) + cols[hh][3] * jnp.sum(jnp.where(low1 if hh == 0 else jnp.logical_not(low1), jnp.sum(sds, axis=0, keepdims=True), 0.0), axis=-1, keepdims=True)
                col = jnp.sum(w, axis=-1, keepdims=True) + jnp.sum(jnp.where(mh, inter, 0.0), axis=-1, keepdims=True) - u_col
                dla_cols = dla_cols + jnp.where(lane == h, col + jnp.where(last, dlend, 0.0), 0.0)
                dla_rows = dla_rows - jnp.where(sub == h, jnp.sum(w, axis=0, keepdims=True), 0.0)
            for hh in range(2):
                mh = low if hh == 0 else jnp.logical_not(low)
                dtsum = dtsum + jnp.where(lane == hs[hh], jnp.sum(jnp.where(mh, dxdt * x2, 0.0), axis=-1, keepdims=True), 0.0)
            dcm = dcm + _dot(dysb, s2.astype(BF16), NT)
            dbm = dbm + _dot((xdt * eend).astype(BF16), ds2.astype(BF16), NT)
            dstate[ps, :] = ds2 * eend_s + _dot(cm, dysb, TN)
            dx_ref[:, sl] = dxdt * dtc + dy2 * dsk_ref[:, sl]
            ddsk_ref[:, sl] += jnp.sum(dy2 * x2, axis=0, keepdims=True)
        dgb = dg.astype(BF16)
        dc_ref[...] = dcm + _dot(dgb, bm)
        db_ref[...] = dbm + _dot(dgb, cm, TN)
        dla = dla_cols + dla_rows.T
        triu = lax.broadcasted_iota(jnp.int32, (CHUNK, CHUNK), 0) <= lax.broadcasted_iota(jnp.int32, (CHUNK, CHUNK), 1)
        ddta = _dot(triu.astype(F32), dla, precision=HIGHEST)
        ddt = ddta * a_ref[...] + dtsum
        da_ref[0] += jnp.sum(ddta * dt, axis=0, keepdims=True)
        ddtr = ddt * sig
        ddtr_ref[0] = ddtr
        dbias_ref[0] += jnp.sum(ddtr, axis=0, keepdims=True)

    vec = BS((1, 1, LANES), lambda g, c: (g, 0, 0))
    outs = pl.pallas_call(
        body, name=name, grid=(SSM_GROUPS, nc // cb),
        in_specs=_ssd_specs(cb * CHUNK, di, gw, rev) + [BS((cb, 1, npair * D_STATE, LANES), lambda g, c: (rev(c), g, 0, 0)),
                                                        BS((cb * CHUNK, gw), lambda g, c: (rev(c), g))],
        out_specs=[BS((cb * CHUNK, gw), lambda g, c: (rev(c), g)), BS((cb * CHUNK, D_STATE), lambda g, c: (rev(c), g)),
                   BS((cb * CHUNK, D_STATE), lambda g, c: (rev(c), g)), BS((1, cb * CHUNK, LANES), lambda g, c: (g, rev(c), 0)),
                   vec, vec, BS((1, gw), lambda g, c: (0, g))],
        out_shape=[SDS((t, di), F32), SDS((t, SSM_GROUPS * D_STATE), F32), SDS((t, SSM_GROUPS * D_STATE), F32),
                   SDS((SSM_GROUPS, t, LANES), F32), SDS((SSM_GROUPS, 1, LANES), F32), SDS((SSM_GROUPS, 1, LANES), F32),
                   SDS((1, di), F32)],
        scratch_shapes=[pltpu.VMEM((npair * D_STATE, LANES), F32)], compiler_params=_cp("parallel", "arbitrary"),
    )(xc, xc, xc, prep, a, dskip, st, dy)
    return outs


def _gate_norm_fwd(y, proj, zoff, w, *, name):
    t, di = y.shape
    gw = di // SSM_GROUPS
    tm = _pick(t, (512, 256, 128))
    zb = zoff // gw

    def body(y_ref, z_ref, w_ref, o_ref):
        yg = y_ref[...] * _silu(z_ref[...].astype(F32))
        r = lax.rsqrt(jnp.mean(yg * yg, axis=-1, keepdims=True) + EPS)
        o_ref[...] = (yg * r * w_ref[...]).astype(BF16)

    return pl.pallas_call(
        body, name=name, grid=(t // tm, SSM_GROUPS),
        in_specs=[BS((tm, gw), lambda i, g: (i, g)), BS((tm, gw), lambda i, g: (i, zb + g)), BS((1, gw), lambda i, g: (0, g))],
        out_specs=BS((tm, gw), lambda i, g: (i, g)), out_shape=SDS((t, di), BF16), compiler_params=_cp("parallel", "parallel"),
    )(y, proj, w.reshape(1, di))


def _gate_norm_bwd(dssm, y, proj, zoff, w, *, name):
    t, di = y.shape
    gw = di // SSM_GROUPS
    tm = _pick(t, (512, 256, 128))
    zb = zoff // gw

    def body(d_ref, y_ref, z_ref, w_ref, dy_ref, dz_ref, dw_ref):
        z = z_ref[...].astype(F32)
        yv = y_ref[...]
        sz = _silu(z)
        yg = yv * sz
        r = lax.rsqrt(jnp.mean(yg * yg, axis=-1, keepdims=True) + EPS)
        n = yg * r
        d = d_ref[...]
        dn = d * w_ref[...]
        dyg = r * (dn - n * jnp.mean(dn * n, axis=-1, keepdims=True))
        dy_ref[...] = dyg * sz
        dz_ref[...] = (dyg * yv * _dsilu(z)).astype(BF16)

        @pl.when(pl.program_id(1) == 0)
        def _():
            dw_ref[...] = jnp.zeros_like(dw_ref)

        dw_ref[...] += jnp.sum(d * n, axis=0, keepdims=True)

    blk = BS((tm, gw), lambda g, i: (i, g))
    dy, dz, dw = pl.pallas_call(
        body, name=name, grid=(SSM_GROUPS, t // tm),
        in_specs=[blk, blk, BS((tm, gw), lambda g, i: (i, zb + g)), BS((1, gw), lambda g, i: (0, g))],
        out_specs=[blk, blk, BS((1, gw), lambda g, i: (0, g))],
        out_shape=[SDS((t, di), F32), SDS((t, di), BF16), SDS((1, di), F32)], compiler_params=_cp("parallel", "arbitrary"),
    )(dssm, y, proj, w.reshape(1, di))
    return dy, dz, dw[0]


def _merge_fwd(proj, goff, ga, gs, *, name):
    t, d = ga.shape
    cw = _conv_block(d, goff)
    tm = _pick(t, (512, 256, 128))
    gb = goff // cw

    def body(g0, g1, a_ref, s_ref, o_ref):
        o_ref[...] = (_sigmoid(g0[...].astype(F32)) * a_ref[...] + _sigmoid(g1[...].astype(F32)) * s_ref[...]).astype(BF16)

    blk = BS((tm, cw), lambda i, j: (i, j))
    return pl.pallas_call(
        body, name=name, grid=(t // tm, d // cw),
        in_specs=[BS((tm, cw), lambda i, j: (i, gb + j)), BS((tm, cw), lambda i, j: (i, gb + d // cw + j)), blk, blk],
        out_specs=blk, out_shape=SDS((t, d), BF16), compiler_params=_cp("parallel", "parallel"),
    )(proj, proj, ga, gs)


def _merge_bwd(proj, goff, ga, gs, dm, *, name):
    t, d = ga.shape
    cw = _conv_block(d, goff)
    tm = _pick(t, (512, 256, 128))
    gb = goff // cw

    def body(g0, g1, a_ref, s_ref, dm_ref, da_ref, ds_ref, dg0_ref, dg1_ref):
        dmv = dm_ref[...]
        s0 = _sigmoid(g0[...].astype(F32))
        s1 = _sigmoid(g1[...].astype(F32))
        da_ref[...] = (s0 * dmv).astype(BF16)
        ds_ref[...] = (s1 * dmv).astype(BF16)
        dg0_ref[...] = (dmv * a_ref[...] * s0 * (1.0 - s0)).astype(BF16)
        dg1_ref[...] = (dmv * s_ref[...] * s1 * (1.0 - s1)).astype(BF16)

    blk = BS((tm, cw), lambda i, j: (i, j))
    return pl.pallas_call(
        body, name=name, grid=(t // tm, d // cw),
        in_specs=[BS((tm, cw), lambda i, j: (i, gb + j)), BS((tm, cw), lambda i, j: (i, gb + d // cw + j)), blk, blk, blk],
        out_specs=[blk] * 4, out_shape=[SDS((t, d), BF16)] * 4, compiler_params=_cp("parallel", "parallel"),
    )(proj, proj, ga, gs, dm)


def _swiglu_fwd(u, *, name):
    t, two_f = u.shape
    f = two_f // 2
    cw = _wide(f)
    tm = _pick(t, (512, 256, 128))

    def body(g_ref, u_ref, o_ref):
        o_ref[...] = (_silu(g_ref[...].astype(F32)) * u_ref[...].astype(F32)).astype(BF16)

    return pl.pallas_call(
        body, name=name, grid=(t // tm, f // cw),
        in_specs=[BS((tm, cw), lambda i, j: (i, j)), BS((tm, cw), lambda i, j: (i, f // cw + j))],
        out_specs=BS((tm, cw), lambda i, j: (i, j)), out_shape=SDS((t, f), BF16), compiler_params=_cp("parallel", "parallel"),
    )(u, u)


def _swiglu_bwd(u, df, *, name):
    t, two_f = u.shape
    f = two_f // 2
    cw = _wide(f)
    tm = _pick(t, (256, 128))

    def body(u_ref, d_ref, o_ref):
        for j in range(f // cw):
            gate, up = slice(j * cw, (j + 1) * cw), slice(f + j * cw, f + (j + 1) * cw)
            gt = u_ref[:, gate].astype(F32)
            d = d_ref[:, gate].astype(F32)
            o_ref[:, gate] = (d * u_ref[:, up].astype(F32) * _dsilu(gt)).astype(BF16)
            o_ref[:, up] = (d * _silu(gt)).astype(BF16)

    return pl.pallas_call(
        body, name=name, grid=(t // tm,), in_specs=[BS((tm, two_f), lambda i: (i, 0)), BS((tm, f), lambda i: (i, 0))],
        out_specs=BS((tm, two_f), lambda i: (i, 0)), out_shape=SDS((t, two_f), BF16), compiler_params=_cp("parallel"),
    )(u, df)


def _row_block(rows, cols, n_arrays):
    budget = VMEM_LIMIT_BYTES // 3
    for tr in (512, 256, 128, 64, 32, 16, 8):
        if rows % tr == 0 and tr * cols * 4 * n_arrays * 2 <= budget:
            return tr
    raise ValueError((rows, cols))


def _concat_cols(pieces, *, name):
    pieces = [p if isinstance(p, tuple) else (p, p.shape[1], 0) for p in pieces]
    rows, dtype = pieces[0][0].shape[0], pieces[0][0].dtype
    widths = [w for _, w, _ in pieces]
    total = sum(widths)
    assert all(w % LANES == 0 for w in widths) and all(a.dtype == dtype and a.shape[0] == rows for a, _, _ in pieces)
    tr = next(c for c in (512, 256, 128, 64, 32, 16) if rows % c == 0 and 4 * c * total * dtype.itemsize <= VMEM_LIMIT_BYTES // 2)

    def body(*refs):
        o_ref = refs[-1]
        off = 0
        for p_ref, w in zip(refs[:-1], widths):
            o_ref[:, off:off + w] = p_ref[...]
            off += w

    return pl.pallas_call(
        body, name=name, grid=(rows // tr,), in_specs=[BS((tr, w), lambda i, j=j: (i, j)) for _, w, j in pieces],
        out_specs=BS((tr, total), lambda i: (i, 0)), out_shape=SDS((rows, total), dtype), compiler_params=_cp("parallel"),
    )(*[a for a, _, _ in pieces])


def _add_own_layer(g0, g1, got, core, *, name):
    rows, cols = got.shape
    tr = _row_block(rows, cols, 4)

    def body(core_ref, g0_ref, g1_ref, got_ref, o_ref):
        o_ref[...] = (jnp.where(core_ref[0] == 0, g0_ref[...], g1_ref[...]) + got_ref[...]).astype(o_ref.dtype)

    blk = BS((tr, cols), lambda i, cr: (i, 0))
    grid_spec = pltpu.PrefetchScalarGridSpec(
        num_scalar_prefetch=1, grid=(rows // tr,),
        in_specs=[BS((tr, cols), lambda i, cr: (i * (1 - cr[0]), 0)), BS((tr, cols), lambda i, cr: (i * cr[0], 0)), blk],
        out_specs=blk)
    return pl.pallas_call(body, name=name, grid_spec=grid_spec, out_shape=SDS((rows, cols), BF16),
                          compiler_params=_cp("arbitrary"))(core, g0, g1, got)


def _sum_chips(a, *, name):
    _, rows, cols = a.shape
    tr = _row_block(rows, cols, 5)

    def body(a_ref, o_ref):
        o_ref[...] = ((a_ref[0].astype(F32) + a_ref[1].astype(F32)) + a_ref[2].astype(F32)) + a_ref[3].astype(F32)

    return pl.pallas_call(body, name=name, grid=(rows // tr,), in_specs=[BS((N_CHIPS, tr, cols), lambda i: (0, i, 0))],
                          out_specs=BS((tr, cols), lambda i: (i, 0)), out_shape=SDS((rows, cols), F32),
                          compiler_params=_cp("parallel"))(a)


def _adamw(w, g, m, v, *, name):
    rows, cols = w.shape
    tr = _row_block(rows, cols, 7) if rows % 8 == 0 else rows
    c1 = 1.0 - ADAM_B1 ** ADAM_STEP
    c2 = 1.0 - ADAM_B2 ** ADAM_STEP

    def body(w_ref, g_ref, m_ref, v_ref, d_ref, nm_ref, nv_ref):
        gv = g_ref[...]
        nm = ADAM_B1 * m_ref[...] + (1.0 - ADAM_B1) * gv
        nv = ADAM_B2 * v_ref[...] + (1.0 - ADAM_B2) * (gv * gv)
        d_ref[...] = -ADAM_LR * ((nm / c1) / (jnp.sqrt(nv / c2) + ADAM_EPS) + ADAM_WD * w_ref[...])
        nm_ref[...] = nm
        nv_ref[...] = nv

    blk = BS((tr, cols), lambda i: (i, 0))
    return pl.pallas_call(body, name=name, grid=(rows // tr,), in_specs=[blk] * 4, out_specs=[blk] * 3,
                          out_shape=[SDS((rows, cols), F32)] * 3, compiler_params=_cp("parallel"))(w, g, m, v)


def _adamw_layers(w, g_own, g_other, m, v, core, *, name):
    _, rows, cols = w.shape
    tr = _row_block(rows, cols, 9)
    c1 = 1.0 - ADAM_B1 ** ADAM_STEP
    c2 = 1.0 - ADAM_B2 ** ADAM_STEP

    def body(core_ref, w_ref, own_ref, oth_ref, m_ref, v_ref, g_ref, d_ref, nm_ref, nv_ref):
        gv = jnp.where(pl.program_id(0) == core_ref[0], own_ref[...], oth_ref[...])
        nm = ADAM_B1 * m_ref[0] + (1.0 - ADAM_B1) * gv
        nv = ADAM_B2 * v_ref[0] + (1.0 - ADAM_B2) * (gv * gv)
        g_ref[0] = gv
        d_ref[0] = -ADAM_LR * ((nm / c1) / (jnp.sqrt(nv / c2) + ADAM_EPS) + ADAM_WD * w_ref[0])
        nm_ref[0] = nm
        nv_ref[0] = nv

    own_here = lambda l, cr: 1 - (l - cr[0]) * (l - cr[0])
    slab = BS((1, tr, cols), lambda l, i, cr: (l, i, 0))
    grid_spec = pltpu.PrefetchScalarGridSpec(
        num_scalar_prefetch=1, grid=(2, rows // tr),
        in_specs=[slab, BS((tr, cols), lambda l, i, cr: (i * own_here(l, cr), 0)),
                  BS((tr, cols), lambda l, i, cr: (i * (1 - own_here(l, cr)), 0)), slab, slab],
        out_specs=[slab] * 4)
    return pl.pallas_call(body, name=name, grid_spec=grid_spec, out_shape=[SDS(w.shape, F32)] * 4,
                          compiler_params=_cp("arbitrary", "arbitrary"))(core, w, g_own, g_other, m, v)


ANY = BS(memory_space=pl.ANY)


def _place():
    x, y, c = lax.axis_index("x"), lax.axis_index("y"), lax.axis_index("c")
    return x, y, c, [(1 - x, y), (x, 1 - y), (1 - x, 1 - y)]


def _gather_shards(arrs, *, name):
    n = len(arrs)

    def body(*refs):
        ins, outs = refs[:n], refs[n:2 * n]
        send_sems, recv_sems, pass_send_sems, pass_recv_sems = refs[2 * n:]
        x, y, c, chips = _place()
        s = 2 * x + y

        def ici(i, j, src_chip, to):
            src = ins[i].at[c] if src_chip is None else outs[i].at[src_chip, c]
            return pltpu.make_async_remote_copy(
                src_ref=src, dst_ref=outs[i].at[s if src_chip is None else src_chip, c], send_sem=send_sems.at[i * 3 + j],
                recv_sem=recv_sems.at[i * 3 + j], device_id=to, device_id_type=MESH)

        def d2d(i, j, src_chip, layer):
            slab = outs[i].at[src_chip, layer]
            return pltpu.make_async_remote_copy(
                src_ref=slab, dst_ref=slab, send_sem=pass_send_sems.at[i * 3 + j], recv_sem=pass_recv_sems.at[i * 3 + j],
                device_id=(x, y, 1 - c), device_id_type=MESH)

        sent = []
        for i in range(n):
            for j, (px, py) in enumerate(chips):
                cp = ici(i, j, None, (px, py, c))
                cp.start()
                sent.append(cp)
        passed = []
        for i in range(n):
            for j, (px, py) in enumerate(chips):
                ici(i, j, 2 * px + py, (x, y, c)).wait_recv()
                cp = d2d(i, j, 2 * px + py, c)
                cp.start()
                passed.append(cp)
        for i in range(n):
            for j, (px, py) in enumerate(chips):
                d2d(i, j, 2 * px + py, 1 - c).wait_recv()
        for cp in sent + passed:
            cp.wait_send()

    return pl.pallas_call(
        body, name=name, in_specs=[ANY] * n, out_specs=[ANY] * n,
        out_shape=[SDS((N_CHIPS,) + a.shape, a.dtype) for a in arrs],
        scratch_shapes=[pltpu.SemaphoreType.DMA((3 * n,))] * 4,
    )(*arrs)


def _pair_swap_layers(layer0, layer1, *, name):
    n = len(layer0)

    def body(*refs):
        in0, in1, outs = refs[:n], refs[n:2 * n], refs[2 * n:3 * n]
        send_sems, recv_sems = refs[3 * n:]
        x, y, c, _ = _place()

        def copy(src, i):
            return pltpu.make_async_remote_copy(
                src_ref=src[i], dst_ref=outs[i], send_sem=send_sems.at[i], recv_sem=recv_sems.at[i],
                device_id=(x, y, 1 - c), device_id_type=MESH)

        @pl.when(c == 0)
        def _():
            for i in range(n):
                copy(in1, i).start()

        @pl.when(c == 1)
        def _():
            for i in range(n):
                copy(in0, i).start()

        for i in range(n):
            copy(in0, i).wait()

    return pl.pallas_call(
        body, name=name, in_specs=[ANY] * (2 * n), out_specs=[ANY] * n, out_shape=[SDS(a.shape, a.dtype) for a in layer0],
        scratch_shapes=[pltpu.SemaphoreType.DMA((n,)), pltpu.SemaphoreType.DMA((n,))],
    )(*layer0, *layer1)


def _scatter_to_chips(arrs, *, name):
    n = len(arrs)

    def body(*refs):
        ins, outs = refs[:n], refs[n:2 * n]
        send_sems, recv_sems = refs[2 * n:]
        x, y, c, chips = _place()
        s = 2 * x + y
        copies = []
        for i in range(n):
            for j, (px, py) in enumerate(chips):
                cp = pltpu.make_async_remote_copy(
                    src_ref=ins[i].at[2 * px + py], dst_ref=outs[i].at[s], send_sem=send_sems.at[i * 3 + j],
                    recv_sem=recv_sems.at[i * 3 + j], device_id=(px, py, c), device_id_type=MESH)
                cp.start()
                copies.append(cp)
        for cp in copies:
            cp.wait()

    return pl.pallas_call(
        body, name=name, in_specs=[ANY] * n, out_specs=[ANY] * n, out_shape=[SDS(a.shape, a.dtype) for a in arrs],
        scratch_shapes=[pltpu.SemaphoreType.DMA((3 * n,)), pltpu.SemaphoreType.DMA((3 * n,))],
    )(*arrs)


def _pair_swap(arrs, *, name):
    n = len(arrs)

    def body(*refs):
        ins, outs = refs[:n], refs[n:2 * n]
        send_sems, recv_sems = refs[2 * n:]
        x, y, c, _ = _place()
        copies = []
        for i in range(n):
            cp = pltpu.make_async_remote_copy(
                src_ref=ins[i], dst_ref=outs[i], send_sem=send_sems.at[i], recv_sem=recv_sems.at[i],
                device_id=(x, y, 1 - c), device_id_type=MESH)
            cp.start()
            copies.append(cp)
        for cp in copies:
            cp.wait()

    return pl.pallas_call(
        body, name=name, in_specs=[ANY] * n, out_specs=[ANY] * n, out_shape=[SDS(a.shape, a.dtype) for a in arrs],
        scratch_shapes=[pltpu.SemaphoreType.DMA((n,)), pltpu.SemaphoreType.DMA((n,))],
    )(*arrs)


def _allreduce_small(v, *, name):
    rows, cols = v.shape

    def body(v_ref, o_ref, gath, send_sems, recv_sems):
        x, y, c, _ = _place()
        me = 4 * x + 2 * y + c
        gath[me] = v_ref[...]
        copies = []
        for k in range(1, N_DEV):
            fx, fy, fc = (k >> 2) & 1, (k >> 1) & 1, k & 1
            peer = (1 - x if fx else x, 1 - y if fy else y, 1 - c if fc else c)
            cp = pltpu.make_async_remote_copy(
                src_ref=v_ref, dst_ref=gath.at[me], send_sem=send_sems.at[k - 1], recv_sem=recv_sems.at[k - 1],
                device_id=peer, device_id_type=MESH)
            cp.start()
            copies.append(cp)
        for cp in copies:
            cp.wait()
        acc = gath[0]
        for k in range(1, N_DEV):
            acc = acc + gath[k]
        o_ref[...] = acc

    vm = BS(memory_space=pltpu.VMEM)
    return pl.pallas_call(
        body, name=name, in_specs=[vm], out_specs=vm, out_shape=SDS((rows, cols), F32),
        scratch_shapes=[pltpu.VMEM((N_DEV, rows, cols), F32), pltpu.SemaphoreType.DMA((N_DEV - 1,)), pltpu.SemaphoreType.DMA((N_DEV - 1,))],
    )(v)


def _t5_bucket(dist):
    max_exact = N_REL_BUCKETS // 2
    d_f = jnp.maximum(dist, 1).astype(F32)
    large = max_exact + (jnp.log(d_f / max_exact) / math.log(REL_MAX_DISTANCE / max_exact) * (N_REL_BUCKETS - max_exact)).astype(jnp.int32)
    return jnp.where(dist < max_exact, dist, jnp.minimum(large, N_REL_BUCKETS - 1))


def _rel_buckets(dilation):
    qi = jnp.arange(ATTN_BLOCK)[:, None]
    kj = jnp.arange(2 * ATTN_BLOCK)[None, :]
    return _t5_bucket(jnp.clip(qi + ATTN_BLOCK - kj, 0, N_STEPS) * dilation)


def _layer_fwd(h, p, biases, lname):
    sv = {"h": h}
    xn1 = _rms_fwd(h, p["norm1_w"], name=lname + "norm1")
    proj = _matmul(xn1, p["w_main"], out_dtype=BF16, name=lname + "in_proj")
    dtr = _matmul(xn1, p["w_dt"], out_dtype=F32, name=lname + "in_proj_dt")
    xn1_rm, qkvs = [xn1], [proj]
    for g in range(1, N_GROUPS_ATTN):
        xn1_rm.append(_to_residue_major(xn1, DILATIONS[g]))
        qkvs.append(_matmul(xn1_rm[g], p["w_qkv"][g], out_dtype=BF16, name=f"{lname}in_proj_qkv{g}"))
    os_, lses, lses_rm = [], [], []
    for g, d in enumerate(DILATIONS):
        o, lse = _attn_fwd(qkvs[g], (0, 1, 2), biases[g], d, name=f"{lname}attn{g}")
        os_.append(_to_token_major(o, d))
        lses.append(_to_token_major(lse, d))
        lses_rm.append(lse)
    sv.update(xn1_rm=xn1_rm, qkvs=qkvs, lses_rm=lses_rm)
    attn = _combine_fwd(os_, lses, name=lname + "attn_combine")
    xc = _conv_fwd(proj, p["off_xbc"], p["conv_w"], p["conv_b"], name=lname + "conv")
    prep = _ssd_prep(dtr, p["dt_bias"], p["a"], name=lname + "ssd_prep")
    sv["prep"] = prep
    y, st = _ssd_fwd(xc, prep, p["a"], p["dskip"], name=lname + "ssd")
    ssm = _gate_norm_fwd(y, proj, p["off_z"], p["ssm_norm_w"], name=lname + "gate_norm")
    ga = _matmul(attn, p["w_attn_branch"], name=lname + "attn_branch")
    gs = _matmul(ssm, p["w_ssm_branch"], name=lname + "ssm_branch")
    merged = _merge_fwd(proj, p["off_gate"], ga, gs, name=lname + "merge")
    h1 = _matmul(merged, p["w_out"], res=h, name=lname + "out_proj")
    xn2 = _rms_fwd(h1, p["norm2_w"], name=lname + "norm2")
    u = _matmul(xn2, p["w_ffn_in"], out_dtype=BF16, name=lname + "ffn_in")
    f = _swiglu_fwd(u, name=lname + "swiglu")
    h2 = _matmul(f, p["w_ffn_out"], res=h1, name=lname + "ffn_out")
    sv.update(xn1=xn1, proj=proj, dtr=dtr, os=os_, lses=lses, attn=attn, xc=xc, y=y, st=st, ssm=ssm, ga=ga, gs=gs,
              merged=merged, h1=h1, xn2=xn2, u=u, f=f)
    return h2, sv


def _layer_bwd(dh2, dh2b, p, sv, biases, lname):
    gr = {}
    lname = lname + "bwd_"
    df = _matmul(dh2b, p["w_ffn_out"], tb=True, out_dtype=BF16, name=lname + "ffn_out_dx")
    gr["w_ffn_out"] = _matmul(sv["f"], dh2b, ta=True, name=lname + "ffn_out_dw")
    du = _swiglu_bwd(sv["u"], df, name=lname + "swiglu")
    dxn2 = _matmul(du, p["w_ffn_in"], tb=True, name=lname + "ffn_in_dx")
    gr["w_ffn_in"] = _matmul(sv["xn2"], du, ta=True, name=lname + "ffn_in_dw")
    dh1, dh1b, gr["norm2_w"] = _rms_bwd(sv["h1"], p["norm2_w"], [dxn2], dh2, name=lname + "norm2")
    dmerged = _matmul(dh1b, p["w_out"], tb=True, name=lname + "out_proj_dx")
    gr["w_out"] = _matmul(sv["merged"], dh1b, ta=True, name=lname + "out_proj_dw")
    dga, dgs, dg0, dg1 = _merge_bwd(sv["proj"], p["off_gate"], sv["ga"], sv["gs"], dmerged, name=lname + "merge")
    dattn = _matmul(dga, p["w_attn_branch"], tb=True, name=lname + "attn_branch_dx")
    gr["w_attn_branch"] = _matmul(sv["attn"], dga, ta=True, name=lname + "attn_branch_dw")
    dssm = _matmul(dgs, p["w_ssm_branch"], tb=True, name=lname + "ssm_branch_dx")
    gr["w_ssm_branch"] = _matmul(sv["ssm"], dgs, ta=True, name=lname + "ssm_branch_dw")
    dy, dz, gr["ssm_norm_w"] = _gate_norm_bwd(dssm, sv["y"], sv["proj"], p["off_z"], p["ssm_norm_w"], name=lname + "gate_norm")
    dxs, dbm, dcm, ddtr4, da4, dbias4, ddsk = _ssd_bwd(sv["xc"], sv["prep"], p["a"], p["dskip"], sv["st"], dy,
                                                       name=lname + "ssd")
    nsh = p["n_ssm_heads"]
    ddtr = jnp.sum(ddtr4, axis=0)
    gr["a_log"] = jnp.sum(da4, axis=(0, 1))[:nsh] * p["a"][0, :nsh]
    gr["dt_bias"] = jnp.sum(dbias4, axis=(0, 1))[:nsh]
    gr["d_skip"] = jnp.sum(ddsk.reshape(nsh, HEAD_DIM), axis=1)
    di = dxs.shape[1]
    dxbc, dcw, dcb = [], [], []
    for part, (lo, hi) in zip((dxs, dbm, dcm), ((0, di), (di, di + dbm.shape[1]), (di + dbm.shape[1], di + 2 * dbm.shape[1]))):
        dx_, dw_, db_ = _conv_bwd(sv["proj"], p["off_xbc"] + lo, p["conv_w"][:, lo:hi], p["conv_b"][lo:hi], part,
                                  name=f"{lname}conv{lo}")
        dxbc.append(dx_)
        dcw.append(dw_)
        dcb.append(db_)
    gr["conv_w"] = jnp.concatenate(dcw, axis=1)
    gr["conv_b"] = jnp.concatenate(dcb, axis=0)
    dos, corrs = _combine_bwd(dattn, sv["os"], sv["lses"], name=lname + "attn_combine")
    dqkvs, dbiases = [], []
    for g, d in enumerate(DILATIONS):
        dq, dk, dv, dbias = _attn_bwd(sv["qkvs"][g], (0, 1, 2), biases[g], sv["lses_rm"][g], _to_residue_major(dos[g], d),
                                      _to_residue_major(corrs[g], d), d, name=f"{lname}attn{g}")
        dqkvs.append([dq, dk, dv])
        dbiases.append(dbias)
    dmain = _concat_cols(dqkvs[0] + [dz] + dxbc + [dg0, dg1], name=lname + "in_proj_join")
    dxn1 = [_matmul(dmain, p["w_main"], tb=True, name=lname + "in_proj_dx"),
            _matmul(ddtr, p["w_dt"], tb=True, name=lname + "in_proj_dt_dx")]
    dw_main = _matmul(sv["xn1"], dmain, ta=True, name=lname + "in_proj_dw")
    dw_dt = _matmul(sv["xn1"], ddtr, ta=True, name=lname + "in_proj_dt_dw")
    dw_qkv = [dw_main]
    for g in range(1, N_GROUPS_ATTN):
        dqkv = _concat_cols(dqkvs[g], name=f"{lname}in_proj_qkv{g}_join")
        dxn1.append(_to_token_major(_matmul(dqkv, p["w_qkv"][g], tb=True, name=f"{lname}in_proj_qkv{g}_dx"), DILATIONS[g]))
        dw_qkv.append(_matmul(sv["xn1_rm"][g], dqkv, ta=True, name=f"{lname}in_proj_qkv{g}_dw"))
    awg, og = dqkvs[0][0].shape[1], p["off_gate"]
    cols = [dw[:, i * awg:(i + 1) * awg] for i in range(3) for dw in dw_qkv]
    gr["w_in"] = jnp.concatenate(cols + [dw_main[:, 3 * awg:og], dw_dt[:, :nsh], dw_main[:, og:]], axis=1)
    dh, dhb, gr["norm1_w"] = _rms_bwd(sv["h"], p["norm1_w"], dxn1, dh1, name=lname + "norm1")
    return dh, dhb, gr, dbiases


def _layer_params(l, w, n_ssm_heads, hg):
    awg = hg * HEAD_DIM
    aw = N_GROUPS_ATTN * awg
    di = n_ssm_heads * HEAD_DIM
    xbc = di + 2 * SSM_GROUPS * D_STATE
    in_dt = 3 * aw + di + xbc
    w_in = w["w_in"][l]
    qkv_cols = lambda g: [(w_in, awg, i * N_GROUPS_ATTN + g) for i in range(3)]
    z_xbc_cols = [(w_in, awg, j) for j in range(3 * N_GROUPS_ATTN, in_dt // awg)]
    assert in_dt % awg == 0
    pad = lambda v: jnp.pad(v.astype(F32), (0, LANES - n_ssm_heads)).reshape(1, LANES)
    return dict(
        n_ssm_heads=n_ssm_heads, off_z=3 * awg, off_xbc=3 * awg + di, off_gate=3 * awg + di + xbc,
        w_main=_concat_cols(qkv_cols(0) + z_xbc_cols + [w_in[:, in_dt + n_ssm_heads:]], name=f"l{l}_w_main"),
        w_qkv=[None] + [_concat_cols(qkv_cols(g), name=f"l{l}_w_qkv{g}") for g in range(1, N_GROUPS_ATTN)],
        w_dt=jnp.pad(w_in[:, in_dt:in_dt + n_ssm_heads], ((0, 0), (0, LANES - n_ssm_heads))),
        norm1_w=w["norm1_w"][l], norm2_w=w["norm2_w"][l], conv_w=w["conv_w"][l], conv_b=w["conv_b"][l],
        dt_bias=pad(w["dt_bias"][l]), a=pad(-jnp.exp(w["a_log"][l])),
        dskip=jnp.repeat(w["d_skip"][l], HEAD_DIM).reshape(1, di), ssm_norm_w=w["ssm_norm_w"][l],
        w_attn_branch=w["w_attn_branch"][l], w_ssm_branch=w["w_ssm_branch"][l], w_out=w["w_out"][l],
        w_ffn_in=w["w_ffn_in"][l], w_ffn_out=w["w_ffn_out"][l],
    )


def _local_step(x, tgt, w):
    depth = w["norm1_w"].shape[0]
    n_ssm_heads = w["dt_bias"].shape[1]
    hg = w["rel_bias"].shape[1] // N_GROUPS_ATTN
    onehots = [(_rel_buckets(dil)[:, :, None] == jnp.arange(N_REL_BUCKETS)[None, None, :]).astype(F32) for dil in DILATIONS]
    biases = [jnp.einsum("qkb,bh->hqk", oh, w["rel_bias"][:, g * hg:(g + 1) * hg].astype(F32), precision=HIGHEST)
              for g, oh in enumerate(onehots)]
    params = [_layer_params(l, w, n_ssm_heads, hg) for l in range(depth)]
    h = x
    saved = []
    for l in range(depth):
        h, sv = _layer_fwd(h, params[l], biases, f"l{l}_")
        saved.append(sv)
    loss, dh, dhb, g_final = _loss_head(h, w["final_norm_w"], tgt, name="loss_head")
    grads = [None] * depth
    dbias_tot = [jnp.zeros(b.shape, F32) for b in biases]
    for l in reversed(range(depth)):
        dh, dhb, grads[l], dbiases = _layer_bwd(dh, dhb, params[l], saved[l], biases, f"l{l}_")
        dbias_tot = [a + b for a, b in zip(dbias_tot, dbiases)]
    out = {k: [gl[k] for gl in grads] if k in MATRICES else jnp.stack([gl[k] for gl in grads]) for k in grads[0]}
    out["final_norm_w"] = g_final
    drel = []
    for g, (oh, db) in enumerate(zip(onehots, dbias_tot)):
        oh_t = jnp.pad(oh.reshape(-1, N_REL_BUCKETS).T, ((0, LANES - N_REL_BUCKETS), (0, 0)))
        db_rows = jnp.pad(db.reshape(hg, -1), ((0, LANES - hg), (0, 0)))
        drel.append(_matmul(oh_t, db_rows, tb=True, name=f"rel_bias_fold{g}")[:N_REL_BUCKETS, :hg])
    out["rel_bias"] = jnp.concatenate(drel, axis=1)
    return loss, dh, out


MATRICES = ("w_in", "w_attn_branch", "w_ssm_branch", "w_out", "w_ffn_in", "w_ffn_out")
COL_SHARDED = ("w_in", "w_attn_branch", "w_ffn_in")
SMALL = ("norm1_w", "conv_b", "dt_bias", "a_log", "d_skip", "ssm_norm_w", "norm2_w", "rel_bias", "final_norm_w")
WEIGHTS = ("norm1_w", "w_in", "conv_w", "conv_b", "dt_bias", "a_log", "d_skip", "ssm_norm_w", "w_attn_branch",
           "w_ssm_branch", "w_out", "norm2_w", "w_ffn_in", "w_ffn_out", "rel_bias", "final_norm_w")
SMALL_COLS = 1024


def _unshard(name, g):
    _, depth, r, c = g.shape
    if name in COL_SHARDED or name == "conv_w":
        return jnp.transpose(g, (1, 2, 0, 3)).reshape(depth, r, N_CHIPS * c)
    return jnp.transpose(g, (1, 0, 2, 3)).reshape(depth, N_CHIPS * r, c)


def _to_shards(name, g):
    r, c = g.shape
    if name in COL_SHARDED:
        return jnp.transpose(g.reshape(r, N_CHIPS, c // N_CHIPS), (1, 0, 2))
    return g.reshape(N_CHIPS, r // N_CHIPS, c)


def kernel(x, norm1_w, w_in, conv_w, conv_b, dt_bias, a_log, d_skip, ssm_norm_w, w_attn_branch, w_ssm_branch, w_out, norm2_w, w_ffn_in, w_ffn_out, rel_bias, final_norm_w, loss_target, m_norm1_w, m_w_in, m_conv_w, m_conv_b, m_dt_bias, m_a_log, m_d_skip, m_ssm_norm_w, m_w_attn_branch, m_w_ssm_branch, m_w_out, m_norm2_w, m_w_ffn_in, m_w_ffn_out, m_rel_bias, m_final_norm_w, v_norm1_w, v_w_in, v_conv_w, v_conv_b, v_dt_bias, v_a_log, v_d_skip, v_ssm_norm_w, v_w_attn_branch, v_w_ssm_branch, v_w_out, v_norm2_w, v_w_ffn_in, v_w_ffn_out, v_rel_bias, v_final_norm_w):
    env = dict(locals())
    wts = {k: env[k] for k in WEIGHTS}
    mom = {k: env["m_" + k] for k in WEIGHTS}
    var = {k: env["v_" + k] for k in WEIGHTS}
    chip = 2 * lax.axis_index("x") + lax.axis_index("y")
    core = lax.axis_index("c")

    shards = [wts[k].astype(BF16) for k in MATRICES] + [conv_w]
    gathered = _gather_shards(shards, name="gather_weights")
    full = {k: wts[k] for k in SMALL}
    for k, own, g in zip(MATRICES + ("conv_w",), shards, gathered):
        full[k] = _unshard(k, lax.dynamic_update_index_in_dim(g, own, chip, axis=0))

    loss, dx, grads = _local_step(x[0], loss_target[0], full)
    loss = lax.psum(loss, ("x", "y", "c"))

    core1 = core.reshape(1).astype(jnp.int32)
    from_pair = _pair_swap_layers([grads[k][0] for k in MATRICES], [grads[k][1] for k in MATRICES], name="reduce_pair_swap")
    scatter_in = [_to_shards(k, _add_own_layer(grads[k][0], grads[k][1], got, core1, name="reduce_pair_add_" + k))
                  for k, got in zip(MATRICES, from_pair)]
    scattered = _scatter_to_chips(scatter_in, name="reduce_scatter")
    own_layer = []
    for k, sent, got in zip(MATRICES, scatter_in, scattered):
        got = lax.dynamic_update_index_in_dim(got, lax.dynamic_index_in_dim(sent, chip, axis=0, keepdims=False), chip, axis=0)
        own_layer.append(_sum_chips(got, name="reduce_sum_" + k))
    other_layer = _pair_swap(own_layer, name="reduce_pair_exchange")
    reduced = {}

    small_names = SMALL + ("conv_w",)
    flat = jnp.concatenate([grads[k].reshape(-1) for k in small_names])
    n_small = flat.shape[0]
    rows = -(-n_small // SMALL_COLS)
    rows = -(-rows // 8) * 8
    flat = jnp.pad(flat, (0, rows * SMALL_COLS - n_small)).reshape(rows, SMALL_COLS)
    flat = _allreduce_small(flat, name="allreduce_small").reshape(-1)
    pos = 0
    for k in small_names:
        size = math.prod(grads[k].shape)
        reduced[k] = flat[pos:pos + size].reshape(grads[k].shape)
        pos += size
    cs = conv_w.shape[2]
    reduced["conv_w"] = lax.dynamic_slice_in_dim(reduced["conv_w"], chip * cs, cs, axis=2)

    delta, new_m, new_v = {}, {}, {}
    for k, own, other in zip(MATRICES, own_layer, other_layer):
        reduced[k], delta[k], new_m[k], new_v[k] = _adamw_layers(wts[k], own, other, mom[k], var[k], core1, name="adamw_" + k)
    pack = lambda src: jnp.pad(jnp.concatenate([src[k].reshape(-1) for k in small_names]),
                               (0, rows * SMALL_COLS - n_shard)).reshape(rows, SMALL_COLS)
    n_shard = sum(math.prod(wts[k].shape) for k in small_names)
    d_, m_, v_ = _adamw(pack(wts), pack(reduced), pack(mom), pack(var), name="adamw_small")
    pos = 0
    for k in small_names:
        size = math.prod(wts[k].shape)
        for dst, src in ((delta, d_), (new_m, m_), (new_v, v_)):
            dst[k] = src.reshape(-1)[pos:pos + size].reshape(wts[k].shape)
        pos += size

    return (loss, dx[None], *[reduced[k] for k in WEIGHTS], *[delta[k] for k in WEIGHTS],
            *[new_m[k] for k in WEIGHTS], *[new_v[k] for k in WEIGHTS])
```
